```python
import math
import jax
import jax.numpy as jnp
from jax import lax
import numpy as np

D_MODEL = 1024
BATCH = 16
SEQ = 4096
DEPTH = 2

CTX_LEN = 256
GRID_W = 64

SWA_HEADS = 4
SWA_KV_HEADS = 2
SWA_HEAD_DIM = 64
SWA_WINDOW = 128
SWA_BLOCK = 128
DN_HEADS = 4
DN_HEAD_DIM = 64
DN_CONV = 5
DN_CHUNK = 64
RET_HEADS = 4
RET_QK_DIM = 32
RET_V_DIM = 64
RET_CHUNK = 64
MLA_HEADS = 4
MLA_Q_RANK = 256
MLA_KV_RANK = 128
MLA_NOPE_DIM = 64
MLA_ROPE_DIM = 32
MLA_V_DIM = 64
MLA_BLOCK = 128

D_FF = 4 * D_MODEL
ROPE_BASE = 10000.0
NORM_EPS = 1e-6
LN_EPS = 1e-5
DEEPNORM_ALPHA = (2 * DEPTH) ** 0.25
DEEPNORM_BETA = (8 * DEPTH) ** -0.25

SWA_Q = SWA_HEADS * SWA_HEAD_DIM
SWA_KV = SWA_KV_HEADS * SWA_HEAD_DIM
DN_W = DN_HEADS * DN_HEAD_DIM
RET_QK = RET_HEADS * RET_QK_DIM
RET_V = RET_HEADS * RET_V_DIM
MLA_OUT = MLA_HEADS * MLA_V_DIM
MIX_WIDTH = SWA_Q + DN_W + RET_V + MLA_OUT
IN_SPLITS = (SWA_Q, SWA_KV, SWA_KV, 3 * DN_W, DN_W, 4 * DN_HEADS, RET_QK, RET_QK, RET_V, RET_V,
             MLA_Q_RANK, MLA_KV_RANK, MLA_ROPE_DIM)
IN_WIDTH = sum(IN_SPLITS)

kernel_name = 'hybrid_parallel_group_dit_block'

F32 = jnp.float32


def layer_norm(x, g, b):
    xf = x.astype(F32)
    mu = jnp.mean(xf, axis=-1, keepdims=True)
    var = jnp.mean(jnp.square(xf - mu), axis=-1, keepdims=True)
    return ((xf - mu) * lax.rsqrt(var + LN_EPS) * g.astype(F32) + b.astype(F32)).astype(x.dtype)


def rms_norm(x, g):
    xf = x.astype(F32)
    return (xf * lax.rsqrt(jnp.mean(xf * xf, axis=-1, keepdims=True) + NORM_EPS) * g.astype(F32)).astype(x.dtype)


def head_layer_norm(o, g):
    b_, t_, h_, d_ = o.shape
    mu = jnp.mean(o, axis=-1, keepdims=True)
    var = jnp.mean(jnp.square(o - mu), axis=-1, keepdims=True)
    return ((o - mu) * lax.rsqrt(var + NORM_EPS)).reshape(b_, t_, h_ * d_) * g.astype(F32)


def l2norm(t):
    return t * lax.rsqrt(jnp.sum(t * t, axis=-1, keepdims=True) + NORM_EPS)


def rope_freqs(dim):
    return ROPE_BASE ** (-jnp.arange(0, dim, 2, dtype=F32) / dim)


def axial_rope(rows, rot_dim):
    row = jnp.broadcast_to(jnp.arange(rows, dtype=F32)[:, None], (rows, GRID_W)).reshape(-1)
    col = jnp.broadcast_to(jnp.arange(GRID_W, dtype=F32)[None, :], (rows, GRID_W)).reshape(-1)
    inv = rope_freqs(rot_dim // 2)
    ang = jnp.concatenate([row[:, None] * inv, col[:, None] * inv], axis=-1)
    return jnp.cos(ang), jnp.sin(ang)


def sequence_rope(n_tok, rot_dim):
    ang = jnp.arange(n_tok, dtype=F32)[:, None] * rope_freqs(rot_dim)
    return jnp.cos(ang), jnp.sin(ang)


def apply_rope(x, cos, sin):
    xf = x.astype(F32)
    x1, x2 = jnp.split(xf, 2, axis=-1)
    c = cos[:, None, :]
    s = sin[:, None, :]
    return jnp.concatenate([x1 * c - x2 * s, x1 * s + x2 * c], axis=-1).astype(x.dtype)


def _flip_t(t):
    return jnp.flip(t, axis=2)


def _split_columns(z):
    idx = np.cumsum(np.array(IN_SPLITS))[:-1].tolist()
    return jnp.split(z, idx, axis=-1)


def short_conv(x, w):
    k_width, ch = w.shape
    pad = k_width // 2
    return lax.conv_general_dilated(x, w[:, None, :].astype(x.dtype), window_strides=(1,),
                                    padding=[(pad, pad)], dimension_numbers=('NWC', 'WIO', 'NWC'),
                                    feature_group_count=ch)


def swa_group(q, k, v, qc, kc, vc, sink, cos, sin, with_ctx_out):
    b_, s_, _ = q.shape
    l_ = kc.shape[1]
    grp = SWA_HEADS // SWA_KV_HEADS
    w_ = SWA_BLOCK
    nb = s_ // w_
    d = SWA_HEAD_DIM
    scale = d ** -0.5
    q = apply_rope(q.reshape(b_, s_, SWA_HEADS, d), cos, sin).reshape(b_, nb, w_, SWA_KV_HEADS, grp, d)
    k = apply_rope(k.reshape(b_, s_, SWA_KV_HEADS, d), cos, sin)
    v = v.reshape(b_, s_, SWA_KV_HEADS, d)
    kc = kc.reshape(b_, l_, SWA_KV_HEADS, d)
    vc = vc.reshape(b_, l_, SWA_KV_HEADS, d)

    def band(t):
        tp = jnp.pad(t, ((0, 0), (w_, w_), (0, 0), (0, 0))).reshape(b_, nb + 2, w_, SWA_KV_HEADS, d)
        return jnp.concatenate([tp[:, :-2], tp[:, 1:-1], tp[:, 2:]], axis=2)

    kb, vb = band(k), band(v)
    qpos = jnp.arange(s_).reshape(nb, w_)
    kpos = (jnp.arange(nb) * w_ - w_)[:, None] + jnp.arange(3 * w_)[None, :]
    rel = kpos[:, None, :] - qpos[:, :, None]
    valid = (jnp.abs(rel) <= SWA_WINDOW) & (kpos >= 0)[:, None, :] & (kpos < s_)[:, None, :]
    sink_hg = sink.astype(F32).reshape(SWA_KV_HEADS, grp)
    s_loc = jnp.where(valid, jnp.einsum('bnqhgd,bnkhd->bhgnqk', q, kb).astype(F32) * scale, -jnp.inf)
    s_ctx = jnp.einsum('bnqhgd,bkhd->bhgnqk', q, kc).astype(F32) * scale
    s_sink = jnp.broadcast_to(sink_hg[None, :, :, None, None, None], s_ctx.shape[:-1] + (1,))
    p = jax.nn.softmax(jnp.concatenate([s_loc, s_ctx, s_sink], axis=-1), axis=-1).astype(v.dtype)
    y = (jnp.einsum('bhgnqk,bnkhd->bnqhgd', p[..., :3 * w_], vb)
         + jnp.einsum('bhgnqk,bkhd->bnqhgd', p[..., 3 * w_:3 * w_ + l_], vc)).reshape(b_, s_, SWA_Q)
    yc = None
    if with_ctx_out:
        qcg = qc.reshape(b_, l_, SWA_KV_HEADS, grp, d)
        sc = jnp.einsum('bqhgd,bkhd->bhgqk', qcg, kc).astype(F32) * scale
        ss = jnp.broadcast_to(sink_hg[None, :, :, None, None], sc.shape[:-1] + (1,))
        pc = jax.nn.softmax(jnp.concatenate([sc, ss], axis=-1), axis=-1).astype(vc.dtype)
        yc = jnp.einsum('bhgqk,bkhd->bqhgd', pc[..., :l_], vc).reshape(b_, l_, SWA_Q)
    return y, yc


def _delta_update(s, w_i, u_i, kt_i, gl_i):
    v_new = u_i - jnp.einsum('bhcd,bhde->bhce', w_i, s)
    s_new = s * jnp.exp(gl_i)[..., None, None] + jnp.einsum('bhcd,bhce->bhde', kt_i, v_new)
    return s_new, v_new


def gated_delta_chunked(q, k, v, log_g, beta, state0):
    b_, h_, t_, dk = k.shape
    dv = v.shape[-1]
    c_ = DN_CHUNK
    n = t_ // c_
    k = k.reshape(b_, h_, n, c_, dk)
    v = v.reshape(b_, h_, n, c_, dv)
    g_cum = jnp.cumsum(log_g.reshape(b_, h_, n, c_), axis=-1)
    beta = beta.reshape(b_, h_, n, c_, 1)
    incl = jnp.tril(jnp.ones((c_, c_), bool))
    strict = jnp.tril(jnp.ones((c_, c_), bool), -1)
    decay = jnp.exp(jnp.where(incl, g_cum[..., :, None] - g_cum[..., None, :], -jnp.inf))
    kb = k * beta
    a_mat = jnp.where(strict, jnp.einsum('bhncd,bhnsd->bhncs', kb, k) * decay, 0.0)
    lhs = a_mat + jnp.eye(c_, dtype=a_mat.dtype)
    rhs = jnp.concatenate([kb * jnp.exp(g_cum)[..., None], v * beta], axis=-1)
    wu = lax.linalg.triangular_solve(lhs, rhs, left_side=True, lower=True)
    w, u = wu[..., :dk], wu[..., dk:]
    g_last = g_cum[..., -1]
    k_tail = k * jnp.exp(g_last[..., None] - g_cum)[..., None]
    chunks = lambda t: jnp.moveaxis(t, 2, 0)
    if q is None:
        def step_state(s, inp):
            s_new, _ = _delta_update(s, *inp)
            return s_new, None
        s_last, _ = lax.scan(step_state, state0, (chunks(w), chunks(u), chunks(k_tail), chunks(g_last)))
        return None, s_last
    q = q.reshape(b_, h_, n, c_, dk)
    qk = jnp.einsum('bhncd,bhnsd->bhncs', q, k) * decay
    q_dec = q * jnp.exp(g_cum)[..., None]

    def step(s, inp):
        w_i, u_i, kt_i, gl_i, qd_i, qk_i = inp
        s_new, v_new = _delta_update(s, w_i, u_i, kt_i, gl_i)
        o_i = jnp.einsum('bhcd,bhde->bhce', qd_i, s) + jnp.einsum('bhcs,bhse->bhce', qk_i, v_new)
        return s_new, o_i

    s_last, o = lax.scan(step, state0, (chunks(w), chunks(u), chunks(k_tail), chunks(g_last),
                                        chunks(q_dec), chunks(qk)))
    return jnp.moveaxis(o, 0, 2).reshape(b_, h_, t_, dv), s_last


def deltanet_group(qkv, z, ab, qkv_c, z_c, ab_c, conv_w, a_log, dt_bias, norm_g, with_ctx_out):
    def prep(qkv_, ab_):
        b_, t_, _ = qkv_.shape
        y = jax.nn.silu(short_conv(qkv_, conv_w)).astype(F32)
        q, k, v = [t.reshape(b_, t_, DN_HEADS, DN_HEAD_DIM).transpose(0, 2, 1, 3) for t in jnp.split(y, 3, axis=-1)]
        q = l2norm(q) * DN_HEAD_DIM ** -0.5
        k = l2norm(k)
        ab_ = ab_.astype(F32).reshape(b_, t_, 2, 2, DN_HEADS)
        log_g = -jnp.exp(a_log.astype(F32)) * jax.nn.softplus(ab_[:, :, :, 0] + dt_bias.astype(F32))
        beta = jax.nn.sigmoid(ab_[:, :, :, 1])
        return q, k, v, log_g.transpose(2, 0, 3, 1), beta.transpose(2, 0, 3, 1)

    def out(o, z_):
        b_, t_, _ = z_.shape
        o = rms_norm(o.transpose(0, 2, 1, 3), norm_g) * jax.nn.silu(z_.astype(F32)).reshape(b_, t_, DN_HEADS, DN_HEAD_DIM)
        return o.reshape(b_, t_, DN_W).astype(z_.dtype)

    qc, kc, vc, lgc, bc = prep(qkv_c, ab_c)
    q, k, v, lg, bt = prep(qkv, ab)
    zero = jnp.zeros((qkv.shape[0], DN_HEADS, DN_HEAD_DIM, DN_HEAD_DIM), F32)
    oc_f, s_f = gated_delta_chunked(qc if with_ctx_out else None, kc, vc, lgc[0], bc[0], zero)
    oc_b, s_b = gated_delta_chunked(_flip_t(qc) if with_ctx_out else None, _flip_t(kc), _flip_t(vc),
                                    _flip_t(lgc[1]), _flip_t(bc[1]), zero)
    o_f, _ = gated_delta_chunked(q, k, v, lg[0], bt[0], s_f)
    o_b, _ = gated_delta_chunked(_flip_t(q), _flip_t(k), _flip_t(v), _flip_t(lg[1]), _flip_t(bt[1]), s_b)
    y = out(o_f + _flip_t(o_b), z)
    yc = out(oc_f + _flip_t(oc_b), z_c) if with_ctx_out else None
    return y, yc


def retention_scan(k, v, log_gamma, state0, emit_starts):
    b_, h_, t_, dk = k.shape
    c_ = RET_CHUNK
    n = t_ // c_
    pos = jnp.arange(c_, dtype=F32)
    zeta = jnp.exp((c_ - 1 - pos)[None, :] * log_gamma[:, None])
    kv = jnp.einsum('bhncd,bhnce->bhnde', k.reshape(b_, h_, n, c_, dk) * zeta[None, :, None, :, None],
                    v.reshape(b_, h_, n, c_, v.shape[-1]))
    g_chunk = jnp.exp(c_ * log_gamma)[None, :, None, None]

    def step(r, kv_i):
        return r * g_chunk + kv_i, (r if emit_starts else None)

    r_last, starts = lax.scan(step, state0, jnp.moveaxis(kv, 2, 0))
    return (jnp.moveaxis(starts, 0, 2) if emit_starts else None), r_last


def retention_readout(q, k, v, log_gamma, starts):
    b_, h_, t_, dk = q.shape
    dv = v.shape[-1]
    c_ = RET_CHUNK
    n = t_ // c_
    pos = jnp.arange(c_, dtype=F32)
    rel = pos[:, None] - pos[None, :]
    dmat = jnp.where(rel >= 0, jnp.exp(jnp.maximum(rel, 0.0)[None] * log_gamma[:, None, None]), 0.0)
    qr = q.reshape(b_, h_, n, c_, dk)
    kr = k.reshape(b_, h_, n, c_, dk)
    vr = v.reshape(b_, h_, n, c_, dv)
    inner = jnp.einsum('bhncs,bhnse->bhnce', jnp.einsum('bhncd,bhnsd->bhncs', qr, kr) * dmat[None, :, None], vr)
    xi = jnp.exp((pos + 1.0)[None, :] * log_gamma[:, None])
    cross = jnp.einsum('bhncd,bhnde->bhnce', qr * xi[None, :, None, :, None], starts)
    return (inner + cross).reshape(b_, h_, t_, dv)


def retention_group(q, k, v, g, qc, kc, vc, gc, log1m_gamma, norm_g, cos, sin, with_ctx_out):
    log_gamma = jnp.log1p(-jnp.exp(log1m_gamma.astype(F32)))
    heads = lambda t, dh: t.reshape(t.shape[0], t.shape[1], RET_HEADS, dh)
    bhtd = lambda t: t.astype(F32).transpose(0, 2, 1, 3)
    sc = RET_QK_DIM ** -0.5
    q = bhtd(apply_rope(heads(q, RET_QK_DIM), cos, sin)) * sc
    k = bhtd(apply_rope(heads(k, RET_QK_DIM), cos, sin))
    v = bhtd(heads(v, RET_V_DIM))
    kc = bhtd(heads(kc, RET_QK_DIM))
    vc = bhtd(heads(vc, RET_V_DIM))
    zero = jnp.zeros((q.shape[0], RET_HEADS, RET_QK_DIM, RET_V_DIM), F32)

    def out(o, g_):
        y = head_layer_norm(o.transpose(0, 2, 1, 3), norm_g) * jax.nn.silu(g_.astype(F32))
        return y.astype(g_.dtype)

    st_cf, r_f = retention_scan(kc, vc, log_gamma[0], zero, with_ctx_out)
    st_cb, r_b = retention_scan(_flip_t(kc), _flip_t(vc), log_gamma[1], zero, with_ctx_out)
    st_f, _ = retention_scan(k, v, log_gamma[0], r_f, True)
    st_b, _ = retention_scan(_flip_t(k), _flip_t(v), log_gamma[1], r_b, True)
    o = (retention_readout(q, k, v, log_gamma[0], st_f)
         + _flip_t(retention_readout(_flip_t(q), _flip_t(k), _flip_t(v), log_gamma[1], st_b)))
    y = out(o, g)
    yc = None
    if with_ctx_out:
        qcs = bhtd(heads(qc, RET_QK_DIM)) * sc
        oc = (retention_readout(qcs, kc, vc, log_gamma[0], st_cf)
              + _flip_t(retention_readout(_flip_t(qcs), _flip_t(kc), _flip_t(vc), log_gamma[1], st_cb)))
        yc = out(oc, gc)
    return y, yc


def mla_queries(cq, q_norm, w_uq, cos, sin):
    b_, t_, _ = cq.shape
    q = (rms_norm(cq, q_norm) @ w_uq).reshape(b_, t_, MLA_HEADS, MLA_NOPE_DIM + MLA_ROPE_DIM)
    if cos is None:
        return q
    return jnp.concatenate([q[..., :MLA_NOPE_DIM], apply_rope(q[..., MLA_NOPE_DIM:], cos, sin)], axis=-1)


def mla_keys_values(ckv, kr, kv_norm, w_ukv, cos, sin):
    b_, t_, _ = ckv.shape
    kv = (rms_norm(ckv, kv_norm) @ w_ukv).reshape(b_, t_, MLA_HEADS, MLA_NOPE_DIM + MLA_V_DIM)
    kr = kr[:, :, None, :]
    if cos is not None:
        kr = apply_rope(kr, cos, sin)
    k = jnp.concatenate([kv[..., :MLA_NOPE_DIM], jnp.broadcast_to(kr, (b_, t_, MLA_HEADS, MLA_ROPE_DIM))], axis=-1)
    return k, kv[..., MLA_NOPE_DIM:]


def mla_attend(q, k_all, v_all):
    b_, t_, h_, dq = q.shape
    nb = t_ // MLA_BLOCK
    scale = dq ** -0.5
    qb = q.reshape(b_, nb, MLA_BLOCK, h_, dq).transpose(1, 0, 2, 3, 4)

    def one_block(qi):
        s = jnp.einsum('bqhd,bkhd->bhqk', qi, k_all).astype(F32) * scale
        p = jax.nn.softmax(s, axis=-1).astype(v_all.dtype)
        return jnp.einsum('bhqk,bkhd->bqhd', p, v_all)

    o = lax.map(one_block, qb)
    return o.transpose(1, 0, 2, 3, 4).reshape(b_, t_, h_ * v_all.shape[-1])


def mla_group(cq, ckv, kr, cq_c, ckv_c, kr_c, q_norm, w_uq, kv_norm, w_ukv, cos, sin, with_ctx_out):
    q = mla_queries(cq, q_norm, w_uq, cos, sin)
    k, v = mla_keys_values(ckv, kr, kv_norm, w_ukv, cos, sin)
    kc, vc = mla_keys_values(ckv_c, kr_c, kv_norm, w_ukv, None, None)
    y = mla_attend(q, jnp.concatenate([k, kc], axis=1), jnp.concatenate([v, vc], axis=1))
    yc = mla_attend(mla_queries(cq_c, q_norm, w_uq, None, None), kc, vc) if with_ctx_out else None
    return y, yc


def token_mixers(h, hc, w_in, swa_sink, dn_conv_w, dn_a_log, dn_dt_bias, dn_norm_g, ret_log1m_gamma,
                 ret_norm_g, mla_q_norm, mla_w_uq, mla_kv_norm, mla_w_ukv, rope, with_ctx_out):
    (a_q, a_k, a_v, b_qkv, b_z, b_ab, c_q, c_k, c_v, c_g, d_cq, d_ckv, d_kr) = _split_columns(h @ w_in)
    (a_qc, a_kc, a_vc, b_qkvc, b_zc, b_abc, c_qc, c_kc, c_vc, c_gc, d_cqc, d_ckvc, d_krc) = _split_columns(hc @ w_in)
    swa_cos, swa_sin, ret_cos, ret_sin, mla_cos, mla_sin = rope
    ya, yac = swa_group(a_q, a_k, a_v, a_qc, a_kc, a_vc, swa_sink, swa_cos, swa_sin, with_ctx_out)
    yb, ybc = deltanet_group(b_qkv, b_z, b_ab, b_qkvc, b_zc, b_abc, dn_conv_w, dn_a_log, dn_dt_bias,
                             dn_norm_g, with_ctx_out)
    yr, yrc = retention_group(c_q, c_k, c_v, c_g, c_qc, c_kc, c_vc, c_gc, ret_log1m_gamma, ret_norm_g,
                              ret_cos, ret_sin, with_ctx_out)
    yd, ydc = mla_group(d_cq, d_ckv, d_kr, d_cqc, d_ckvc, d_krc, mla_q_norm, mla_w_uq, mla_kv_norm,
                        mla_w_ukv, mla_cos, mla_sin, with_ctx_out)
    y = jnp.concatenate([ya, yb, yr, yd], axis=-1)
    yc = jnp.concatenate([yac, ybc, yrc, ydc], axis=-1) if with_ctx_out else None
    return y, yc


def squared_relu_mlp(h, w1, w2):
    return jnp.square(jax.nn.relu(h @ w1)) @ w2


def _fwd_setup_inputs(seed: int = 0) -> dict:
    key = jax.random.key(seed)
    ks = jax.random.split(key, 32)
    nrm = lambda k, shape, scale: jax.random.normal(k, shape, F32) * scale
    dt = jnp.exp(jax.random.uniform(ks[10], (DEPTH, 2, DN_HEADS), F32, math.log(1e-3), math.log(1e-1)))
    return {
        'x': nrm(ks[0], (BATCH, SEQ, D_MODEL), 1.0),
        'c': nrm(ks[1], (BATCH, D_MODEL), 1.0),
        'ctx': nrm(ks[2], (BATCH, CTX_LEN, D_MODEL), 1.0),
        'c_ctx': nrm(ks[3], (D_MODEL,), 1.0),
        'ada_w': nrm(ks[4], (DEPTH, D_MODEL, 6 * D_MODEL), D_MODEL ** -0.5),
        'ada_b': nrm(ks[5], (DEPTH, 6 * D_MODEL), 0.02),
        'w_in': nrm(ks[6], (DEPTH, D_MODEL, IN_WIDTH), D_MODEL ** -0.5),
        'swa_sink': nrm(ks[7], (DEPTH, SWA_HEADS), 0.5),
        'dn_conv_w': nrm(ks[8], (DEPTH, DN_CONV, 3 * DN_W), DN_CONV ** -0.5),
        'dn_a_log': jnp.log(jax.random.uniform(ks[9], (DEPTH, 2, DN_HEADS), F32, 1.0, 16.0)),
        'dn_dt_bias': dt + jnp.log(-jnp.expm1(-dt)),
        'dn_norm_g': 1.0 + nrm(ks[11], (DEPTH, DN_HEAD_DIM), 0.02),
        'ret_log1m_gamma': (-(5.0 + jnp.arange(RET_HEADS, dtype=F32)) * math.log(2.0)
                            + nrm(ks[12], (DEPTH, 2, RET_HEADS), 0.05)),
        'ret_norm_g': 1.0 + nrm(ks[13], (DEPTH, RET_V), 0.02),
        'mla_q_norm': 1.0 + nrm(ks[14], (DEPTH, MLA_Q_RANK), 0.02),
        'mla_w_uq': nrm(ks[15], (DEPTH, MLA_Q_RANK, MLA_HEADS * (MLA_NOPE_DIM + MLA_ROPE_DIM)), MLA_Q_RANK ** -0.5),
        'mla_kv_norm': 1.0 + nrm(ks[16], (DEPTH, MLA_KV_RANK), 0.02),
        'mla_w_ukv': nrm(ks[17], (DEPTH, MLA_KV_RANK, MLA_HEADS * (MLA_NOPE_DIM + MLA_V_DIM)), MLA_KV_RANK ** -0.5),
        'w_out': nrm(ks[18], (DEPTH, MIX_WIDTH, D_MODEL), MIX_WIDTH ** -0.5 * DEEPNORM_BETA),
        'ln1_g': 1.0 + nrm(ks[19], (DEPTH, D_MODEL), 0.02),
        'ln1_b': nrm(ks[20], (DEPTH, D_MODEL), 0.02),
        'w_ff1': nrm(ks[21], (DEPTH, D_MODEL, D_FF), D_MODEL ** -0.5),
        'w_ff2': nrm(ks[22], (DEPTH, D_FF, D_MODEL), D_FF ** -0.5 * DEEPNORM_BETA),
        'ln2_g': 1.0 + nrm(ks[23], (DEPTH, D_MODEL), 0.02),
        'ln2_b': nrm(ks[24], (DEPTH, D_MODEL), 0.02),
    }


def _fwd_reference(x, c, ctx, c_ctx, ada_w, ada_b, w_in, swa_sink, dn_conv_w, dn_a_log, dn_dt_bias, dn_norm_g,
              ret_log1m_gamma, ret_norm_g, mla_q_norm, mla_w_uq, mla_kv_norm, mla_w_ukv, w_out, ln1_g, ln1_b,
              w_ff1, w_ff2, ln2_g, ln2_b):
    n_tok = x.shape[1]
    rows = n_tok // GRID_W
    rope = (*axial_rope(rows, SWA_HEAD_DIM), *sequence_rope(n_tok, RET_QK_DIM), *axial_rope(rows, MLA_ROPE_DIM))
    silu_c = jax.nn.silu(c)
    silu_cc = jax.nn.silu(c_ctx)
    xc = ctx
    for layer in range(DEPTH):
        with_ctx_out = layer < DEPTH - 1
        sh1, sc1, g1, sh2, sc2, g2 = jnp.split((silu_c @ ada_w[layer] + ada_b[layer])[:, None, :], 6, axis=-1)
        csh1, csc1, cg1, csh2, csc2, cg2 = jnp.split((silu_cc @ ada_w[layer] + ada_b[layer])[None, None, :], 6, axis=-1)
        y, yc = token_mixers(x * (1 + sc1) + sh1, xc * (1 + csc1) + csh1, w_in[layer], swa_sink[layer],
                             dn_conv_w[layer], dn_a_log[layer], dn_dt_bias[layer], dn_norm_g[layer],
                             ret_log1m_gamma[layer], ret_norm_g[layer], mla_q_norm[layer], mla_w_uq[layer],
                             mla_kv_norm[layer], mla_w_ukv[layer], rope, with_ctx_out)
        x = layer_norm(DEEPNORM_ALPHA * x + g1 * (y @ w_out[layer]), ln1_g[layer], ln1_b[layer])
        x = layer_norm(DEEPNORM_ALPHA * x + g2 * squared_relu_mlp(x * (1 + sc2) + sh2, w_ff1[layer], w_ff2[layer]),
                       ln2_g[layer], ln2_b[layer])
        if with_ctx_out:
            xc = layer_norm(DEEPNORM_ALPHA * xc + cg1 * (yc @ w_out[layer]), ln1_g[layer], ln1_b[layer])
            xc = layer_norm(DEEPNORM_ALPHA * xc + cg2 * squared_relu_mlp(xc * (1 + csc2) + csh2, w_ff1[layer], w_ff2[layer]),
                            ln2_g[layer], ln2_b[layer])
    return x


import jax as _jax
import jax.numpy as _jnp

TWIN_FORMAT = 'train_step'
FWD_PARAMS = ['x', 'c', 'ctx', 'c_ctx', 'ada_w', 'ada_b', 'w_in', 'swa_sink', 'dn_conv_w', 'dn_a_log', 'dn_dt_bias', 'dn_norm_g', 'ret_log1m_gamma', 'ret_norm_g', 'mla_q_norm', 'mla_w_uq', 'mla_kv_norm', 'mla_w_ukv', 'w_out', 'ln1_g', 'ln1_b', 'w_ff1', 'w_ff2', 'ln2_g', 'ln2_b']
TWIN_WEIGHTS = ['c_ctx', 'ada_w', 'ada_b', 'w_in', 'swa_sink', 'dn_conv_w', 'dn_a_log', 'dn_dt_bias', 'dn_norm_g', 'ret_log1m_gamma', 'ret_norm_g', 'mla_q_norm', 'mla_w_uq', 'mla_kv_norm', 'mla_w_ukv', 'w_out', 'ln1_g', 'ln1_b', 'w_ff1', 'w_ff2', 'ln2_g', 'ln2_b']
TWIN_DIFF_INPUT = 'x'
TWIN_INPUTS = ['x', 'c', 'ctx', 'c_ctx', 'ada_w', 'ada_b', 'w_in', 'swa_sink', 'dn_conv_w', 'dn_a_log', 'dn_dt_bias', 'dn_norm_g', 'ret_log1m_gamma', 'ret_norm_g', 'mla_q_norm', 'mla_w_uq', 'mla_kv_norm', 'mla_w_ukv', 'w_out', 'ln1_g', 'ln1_b', 'w_ff1', 'w_ff2', 'ln2_g', 'ln2_b', 'loss_target', 'm_c_ctx', 'm_ada_w', 'm_ada_b', 'm_w_in', 'm_swa_sink', 'm_dn_conv_w', 'm_dn_a_log', 'm_dn_dt_bias', 'm_dn_norm_g', 'm_ret_log1m_gamma', 'm_ret_norm_g', 'm_mla_q_norm', 'm_mla_w_uq', 'm_mla_kv_norm', 'm_mla_w_ukv', 'm_w_out', 'm_ln1_g', 'm_ln1_b', 'm_w_ff1', 'm_w_ff2', 'm_ln2_g', 'm_ln2_b', 'v_c_ctx', 'v_ada_w', 'v_ada_b', 'v_w_in', 'v_swa_sink', 'v_dn_conv_w', 'v_dn_a_log', 'v_dn_dt_bias', 'v_dn_norm_g', 'v_ret_log1m_gamma', 'v_ret_norm_g', 'v_mla_q_norm', 'v_mla_w_uq', 'v_mla_kv_norm', 'v_mla_w_ukv', 'v_w_out', 'v_ln1_g', 'v_ln1_b', 'v_w_ff1', 'v_w_ff2', 'v_ln2_g', 'v_ln2_b']
TWIN_OUTPUTS = ['loss', 'grad_x', 'grad_c_ctx', 'grad_ada_w', 'grad_ada_b', 'grad_w_in', 'grad_swa_sink', 'grad_dn_conv_w', 'grad_dn_a_log', 'grad_dn_dt_bias', 'grad_dn_norm_g', 'grad_ret_log1m_gamma', 'grad_ret_norm_g', 'grad_mla_q_norm', 'grad_mla_w_uq', 'grad_mla_kv_norm', 'grad_mla_w_ukv', 'grad_w_out', 'grad_ln1_g', 'grad_ln1_b', 'grad_w_ff1', 'grad_w_ff2', 'grad_ln2_g', 'grad_ln2_b', 'delta_c_ctx', 'delta_ada_w', 'delta_ada_b', 'delta_w_in', 'delta_swa_sink', 'delta_dn_conv_w', 'delta_dn_a_log', 'delta_dn_dt_bias', 'delta_dn_norm_g', 'delta_ret_log1m_gamma', 'delta_ret_norm_g', 'delta_mla_q_norm', 'delta_mla_w_uq', 'delta_mla_kv_norm', 'delta_mla_w_ukv', 'delta_w_out', 'delta_ln1_g', 'delta_ln1_b', 'delta_w_ff1', 'delta_w_ff2', 'delta_ln2_g', 'delta_ln2_b', 'new_m_c_ctx', 'new_m_ada_w', 'new_m_ada_b', 'new_m_w_in', 'new_m_swa_sink', 'new_m_dn_conv_w', 'new_m_dn_a_log', 'new_m_dn_dt_bias', 'new_m_dn_norm_g', 'new_m_ret_log1m_gamma', 'new_m_ret_norm_g', 'new_m_mla_q_norm', 'new_m_mla_w_uq', 'new_m_mla_kv_norm', 'new_m_mla_w_ukv', 'new_m_w_out', 'new_m_ln1_g', 'new_m_ln1_b', 'new_m_w_ff1', 'new_m_w_ff2', 'new_m_ln2_g', 'new_m_ln2_b', 'new_v_c_ctx', 'new_v_ada_w', 'new_v_ada_b', 'new_v_w_in', 'new_v_swa_sink', 'new_v_dn_conv_w', 'new_v_dn_a_log', 'new_v_dn_dt_bias', 'new_v_dn_norm_g', 'new_v_ret_log1m_gamma', 'new_v_ret_norm_g', 'new_v_mla_q_norm', 'new_v_mla_w_uq', 'new_v_mla_kv_norm', 'new_v_mla_w_ukv', 'new_v_w_out', 'new_v_ln1_g', 'new_v_ln1_b', 'new_v_w_ff1', 'new_v_w_ff2', 'new_v_ln2_g', 'new_v_ln2_b']
TWIN_LEAF_KINDS = {'loss': 'loss', 'grad_x': 'grad_x', 'grad_c_ctx': 'grad_w', 'grad_ada_w': 'grad_w', 'grad_ada_b': 'grad_w', 'grad_w_in': 'grad_w', 'grad_swa_sink': 'grad_w', 'grad_dn_conv_w': 'grad_w', 'grad_dn_a_log': 'grad_w', 'grad_dn_dt_bias': 'grad_w', 'grad_dn_norm_g': 'grad_w', 'grad_ret_log1m_gamma': 'grad_w', 'grad_ret_norm_g': 'grad_w', 'grad_mla_q_norm': 'grad_w', 'grad_mla_w_uq': 'grad_w', 'grad_mla_kv_norm': 'grad_w', 'grad_mla_w_ukv': 'grad_w', 'grad_w_out': 'grad_w', 'grad_ln1_g': 'grad_w', 'grad_ln1_b': 'grad_w', 'grad_w_ff1': 'grad_w', 'grad_w_ff2': 'grad_w', 'grad_ln2_g': 'grad_w', 'grad_ln2_b': 'grad_w', 'delta_c_ctx': 'delta_w', 'delta_ada_w': 'delta_w', 'delta_ada_b': 'delta_w', 'delta_w_in': 'delta_w', 'delta_swa_sink': 'delta_w', 'delta_dn_conv_w': 'delta_w', 'delta_dn_a_log': 'delta_w', 'delta_dn_dt_bias': 'delta_w', 'delta_dn_norm_g': 'delta_w', 'delta_ret_log1m_gamma': 'delta_w', 'delta_ret_norm_g': 'delta_w', 'delta_mla_q_norm': 'delta_w', 'delta_mla_w_uq': 'delta_w', 'delta_mla_kv_norm': 'delta_w', 'delta_mla_w_ukv': 'delta_w', 'delta_w_out': 'delta_w', 'delta_ln1_g': 'delta_w', 'delta_ln1_b': 'delta_w', 'delta_w_ff1': 'delta_w', 'delta_w_ff2': 'delta_w', 'delta_ln2_g': 'delta_w', 'delta_ln2_b': 'delta_w', 'new_m_c_ctx': 'new_m', 'new_m_ada_w': 'new_m', 'new_m_ada_b': 'new_m', 'new_m_w_in': 'new_m', 'new_m_swa_sink': 'new_m', 'new_m_dn_conv_w': 'new_m', 'new_m_dn_a_log': 'new_m', 'new_m_dn_dt_bias': 'new_m', 'new_m_dn_norm_g': 'new_m', 'new_m_ret_log1m_gamma': 'new_m', 'new_m_ret_norm_g': 'new_m', 'new_m_mla_q_norm': 'new_m', 'new_m_mla_w_uq': 'new_m', 'new_m_mla_kv_norm': 'new_m', 'new_m_mla_w_ukv': 'new_m', 'new_m_w_out': 'new_m', 'new_m_ln1_g': 'new_m', 'new_m_ln1_b': 'new_m', 'new_m_w_ff1': 'new_m', 'new_m_w_ff2': 'new_m', 'new_m_ln2_g': 'new_m', 'new_m_ln2_b': 'new_m', 'new_v_c_ctx': 'new_v', 'new_v_ada_w': 'new_v', 'new_v_ada_b': 'new_v', 'new_v_w_in': 'new_v', 'new_v_swa_sink': 'new_v', 'new_v_dn_conv_w': 'new_v', 'new_v_dn_a_log': 'new_v', 'new_v_dn_dt_bias': 'new_v', 'new_v_dn_norm_g': 'new_v', 'new_v_ret_log1m_gamma': 'new_v', 'new_v_ret_norm_g': 'new_v', 'new_v_mla_q_norm': 'new_v', 'new_v_mla_w_uq': 'new_v', 'new_v_mla_kv_norm': 'new_v', 'new_v_mla_w_ukv': 'new_v', 'new_v_w_out': 'new_v', 'new_v_ln1_g': 'new_v', 'new_v_ln1_b': 'new_v', 'new_v_w_ff1': 'new_v', 'new_v_w_ff2': 'new_v', 'new_v_ln2_g': 'new_v', 'new_v_ln2_b': 'new_v'}


def _forward(args):
    return _fwd_reference(*[args[k] for k in FWD_PARAMS])


def _output_shape():
    out = _jax.eval_shape(lambda: _forward(_fwd_setup_inputs(0)))
    return out.shape, out.dtype

N_MICROBATCH = 1
ADAM_LR = 0.001
ADAM_B1 = 0.9
ADAM_B2 = 0.999
ADAM_EPS = 1e-08
ADAM_WD = 0.01
ADAM_STEP = 10
PER_EXAMPLE_BATCH_AXIS = {'x': 0, 'c': 0, 'ctx': 0, 'loss_target': 0}
SHARED_INPUTS = []
_WEIGHT_DTYPES = {'c_ctx': _jnp.float32, 'ada_w': _jnp.float32, 'ada_b': _jnp.float32, 'w_in': _jnp.float32, 'swa_sink': _jnp.float32, 'dn_conv_w': _jnp.float32, 'dn_a_log': _jnp.float32, 'dn_dt_bias': _jnp.float32, 'dn_norm_g': _jnp.float32, 'ret_log1m_gamma': _jnp.float32, 'ret_norm_g': _jnp.float32, 'mla_q_norm': _jnp.float32, 'mla_w_uq': _jnp.float32, 'mla_kv_norm': _jnp.float32, 'mla_w_ukv': _jnp.float32, 'w_out': _jnp.float32, 'ln1_g': _jnp.float32, 'ln1_b': _jnp.float32, 'w_ff1': _jnp.float32, 'w_ff2': _jnp.float32, 'ln2_g': _jnp.float32, 'ln2_b': _jnp.float32}
MOMENT_SCALE = {'c_ctx': 3.991057e-02, 'ada_w': 8.585516e-02, 'ada_b': 1.605939e-01, 'w_in': 4.984744e-02, 'swa_sink': 2.221017e-04, 'dn_conv_w': 3.841668e-02, 'dn_a_log': 6.550229e-02, 'dn_dt_bias': 6.423888e-02, 'dn_norm_g': 1.112859e-01, 'ret_log1m_gamma': 1.634533e-01, 'ret_norm_g': 4.883650e-02, 'mla_q_norm': 7.503591e-03, 'mla_w_uq': 6.064914e-03, 'mla_kv_norm': 6.749701e-02, 'mla_w_ukv': 3.252719e-02, 'w_out': 9.327834e-02, 'ln1_g': 1.224037e+00, 'ln1_b': 6.825292e-01, 'w_ff1': 6.873596e-02, 'w_ff2': 2.845516e-01, 'ln2_g': 4.624933e+01, 'ln2_b': 7.828775e+00}


def _to_microbatches(a, axis):
    t = _jnp.moveaxis(a, axis, 0)
    t = t.reshape((N_MICROBATCH, t.shape[0] // N_MICROBATCH) + t.shape[1:])
    return _jnp.moveaxis(t, 1, axis + 1)


def setup_inputs(seed: int = 0) -> dict:
    inp = _fwd_setup_inputs(seed)
    key = _jax.random.fold_in(_jax.random.key(seed), 7919)
    shape, _ = _output_shape()
    out = dict(inp)
    out["loss_target"] = _jax.random.normal(_jax.random.fold_in(key, 0), shape, _jnp.float32)
    for i, name in enumerate(TWIN_WEIGHTS):
        w = inp[name].astype(_jnp.float32)
        if MOMENT_SCALE is None:
            s = _jnp.sqrt(_jnp.mean(_jnp.square(w)) + 1e-30)
        else:
            s = MOMENT_SCALE[name]
        km, kv = _jax.random.split(_jax.random.fold_in(key, i + 1))
        out[name] = w
        out["m_" + name] = s * _jax.random.normal(km, w.shape, _jnp.float32)
        out["v_" + name] = (s * s) * _jax.random.uniform(kv, w.shape, _jnp.float32, 0.5, 1.5)
    if N_MICROBATCH > 1:
        for name, axis in PER_EXAMPLE_BATCH_AXIS.items():
            out[name] = _to_microbatches(out[name], axis)
    return {'x': out['x'], 'c': out['c'], 'ctx': out['ctx'], 'c_ctx': out['c_ctx'], 'ada_w': out['ada_w'], 'ada_b': out['ada_b'], 'w_in': out['w_in'], 'swa_sink': out['swa_sink'], 'dn_conv_w': out['dn_conv_w'], 'dn_a_log': out['dn_a_log'], 'dn_dt_bias': out['dn_dt_bias'], 'dn_norm_g': out['dn_norm_g'], 'ret_log1m_gamma': out['ret_log1m_gamma'], 'ret_norm_g': out['ret_norm_g'], 'mla_q_norm': out['mla_q_norm'], 'mla_w_uq': out['mla_w_uq'], 'mla_kv_norm': out['mla_kv_norm'], 'mla_w_ukv': out['mla_w_ukv'], 'w_out': out['w_out'], 'ln1_g': out['ln1_g'], 'ln1_b': out['ln1_b'], 'w_ff1': out['w_ff1'], 'w_ff2': out['w_ff2'], 'ln2_g': out['ln2_g'], 'ln2_b': out['ln2_b'], 'loss_target': out['loss_target'], 'm_c_ctx': out['m_c_ctx'], 'm_ada_w': out['m_ada_w'], 'm_ada_b': out['m_ada_b'], 'm_w_in': out['m_w_in'], 'm_swa_sink': out['m_swa_sink'], 'm_dn_conv_w': out['m_dn_conv_w'], 'm_dn_a_log': out['m_dn_a_log'], 'm_dn_dt_bias': out['m_dn_dt_bias'], 'm_dn_norm_g': out['m_dn_norm_g'], 'm_ret_log1m_gamma': out['m_ret_log1m_gamma'], 'm_ret_norm_g': out['m_ret_norm_g'], 'm_mla_q_norm': out['m_mla_q_norm'], 'm_mla_w_uq': out['m_mla_w_uq'], 'm_mla_kv_norm': out['m_mla_kv_norm'], 'm_mla_w_ukv': out['m_mla_w_ukv'], 'm_w_out': out['m_w_out'], 'm_ln1_g': out['m_ln1_g'], 'm_ln1_b': out['m_ln1_b'], 'm_w_ff1': out['m_w_ff1'], 'm_w_ff2': out['m_w_ff2'], 'm_ln2_g': out['m_ln2_g'], 'm_ln2_b': out['m_ln2_b'], 'v_c_ctx': out['v_c_ctx'], 'v_ada_w': out['v_ada_w'], 'v_ada_b': out['v_ada_b'], 'v_w_in': out['v_w_in'], 'v_swa_sink': out['v_swa_sink'], 'v_dn_conv_w': out['v_dn_conv_w'], 'v_dn_a_log': out['v_dn_a_log'], 'v_dn_dt_bias': out['v_dn_dt_bias'], 'v_dn_norm_g': out['v_dn_norm_g'], 'v_ret_log1m_gamma': out['v_ret_log1m_gamma'], 'v_ret_norm_g': out['v_ret_norm_g'], 'v_mla_q_norm': out['v_mla_q_norm'], 'v_mla_w_uq': out['v_mla_w_uq'], 'v_mla_kv_norm': out['v_mla_kv_norm'], 'v_mla_w_ukv': out['v_mla_w_ukv'], 'v_w_out': out['v_w_out'], 'v_ln1_g': out['v_ln1_g'], 'v_ln1_b': out['v_ln1_b'], 'v_w_ff1': out['v_w_ff1'], 'v_w_ff2': out['v_w_ff2'], 'v_ln2_g': out['v_ln2_g'], 'v_ln2_b': out['v_ln2_b']}


def _loss(weights, diff, rest, loss_target):
    with _jax.named_scope("forward"):
        args = {**rest, TWIN_DIFF_INPUT: diff, **{k: w.astype(_WEIGHT_DTYPES[k]) for k, w in weights.items()}}
        y = _forward(args)
    with _jax.named_scope("loss_head"):
        err = _jnp.square(y.astype(_jnp.float32) - loss_target)
        return 0.5 * _jnp.sum(_jnp.mean(err, axis=-1)) if err.ndim else 0.5 * err


def _adamw(w, g, m, v):
    m = ADAM_B1 * m + (1.0 - ADAM_B1) * g
    v = ADAM_B2 * v + (1.0 - ADAM_B2) * _jnp.square(g)
    m_hat = m / (1.0 - ADAM_B1 ** ADAM_STEP)
    v_hat = v / (1.0 - ADAM_B2 ** ADAM_STEP)
    delta = -ADAM_LR * (m_hat / (_jnp.sqrt(v_hat) + ADAM_EPS) + ADAM_WD * w)
    return delta, m, v


def reference(x, c, ctx, c_ctx, ada_w, ada_b, w_in, swa_sink, dn_conv_w, dn_a_log, dn_dt_bias, dn_norm_g, ret_log1m_gamma, ret_norm_g, mla_q_norm, mla_w_uq, mla_kv_norm, mla_w_ukv, w_out, ln1_g, ln1_b, w_ff1, w_ff2, ln2_g, ln2_b, loss_target, m_c_ctx, m_ada_w, m_ada_b, m_w_in, m_swa_sink, m_dn_conv_w, m_dn_a_log, m_dn_dt_bias, m_dn_norm_g, m_ret_log1m_gamma, m_ret_norm_g, m_mla_q_norm, m_mla_w_uq, m_mla_kv_norm, m_mla_w_ukv, m_w_out, m_ln1_g, m_ln1_b, m_w_ff1, m_w_ff2, m_ln2_g, m_ln2_b, v_c_ctx, v_ada_w, v_ada_b, v_w_in, v_swa_sink, v_dn_conv_w, v_dn_a_log, v_dn_dt_bias, v_dn_norm_g, v_ret_log1m_gamma, v_ret_norm_g, v_mla_q_norm, v_mla_w_uq, v_mla_kv_norm, v_mla_w_ukv, v_w_out, v_ln1_g, v_ln1_b, v_w_ff1, v_w_ff2, v_ln2_g, v_ln2_b):
    given = dict(x=x, c=c, ctx=ctx, c_ctx=c_ctx, ada_w=ada_w, ada_b=ada_b, w_in=w_in, swa_sink=swa_sink, dn_conv_w=dn_conv_w, dn_a_log=dn_a_log, dn_dt_bias=dn_dt_bias, dn_norm_g=dn_norm_g, ret_log1m_gamma=ret_log1m_gamma, ret_norm_g=ret_norm_g, mla_q_norm=mla_q_norm, mla_w_uq=mla_w_uq, mla_kv_norm=mla_kv_norm, mla_w_ukv=mla_w_ukv, w_out=w_out, ln1_g=ln1_g, ln1_b=ln1_b, w_ff1=w_ff1, w_ff2=w_ff2, ln2_g=ln2_g, ln2_b=ln2_b, loss_target=loss_target, m_c_ctx=m_c_ctx, m_ada_w=m_ada_w, m_ada_b=m_ada_b, m_w_in=m_w_in, m_swa_sink=m_swa_sink, m_dn_conv_w=m_dn_conv_w, m_dn_a_log=m_dn_a_log, m_dn_dt_bias=m_dn_dt_bias, m_dn_norm_g=m_dn_norm_g, m_ret_log1m_gamma=m_ret_log1m_gamma, m_ret_norm_g=m_ret_norm_g, m_mla_q_norm=m_mla_q_norm, m_mla_w_uq=m_mla_w_uq, m_mla_kv_norm=m_mla_kv_norm, m_mla_w_ukv=m_mla_w_ukv, m_w_out=m_w_out, m_ln1_g=m_ln1_g, m_ln1_b=m_ln1_b, m_w_ff1=m_w_ff1, m_w_ff2=m_w_ff2, m_ln2_g=m_ln2_g, m_ln2_b=m_ln2_b, v_c_ctx=v_c_ctx, v_ada_w=v_ada_w, v_ada_b=v_ada_b, v_w_in=v_w_in, v_swa_sink=v_swa_sink, v_dn_conv_w=v_dn_conv_w, v_dn_a_log=v_dn_a_log, v_dn_dt_bias=v_dn_dt_bias, v_dn_norm_g=v_dn_norm_g, v_ret_log1m_gamma=v_ret_log1m_gamma, v_ret_norm_g=v_ret_norm_g, v_mla_q_norm=v_mla_q_norm, v_mla_w_uq=v_mla_w_uq, v_mla_kv_norm=v_mla_kv_norm, v_mla_w_ukv=v_mla_w_ukv, v_w_out=v_w_out, v_ln1_g=v_ln1_g, v_ln1_b=v_ln1_b, v_w_ff1=v_w_ff1, v_w_ff2=v_w_ff2, v_ln2_g=v_ln2_g, v_ln2_b=v_ln2_b)
    weights = {n: given[n] for n in TWIN_WEIGHTS}
    shared = {n: given[n] for n in SHARED_INPUTS}
    per_example = {n: given[n] for n in ['x', 'c', 'ctx']}
    grad_fn = _jax.value_and_grad(_loss, argnums=(0, 1))

    def one_microbatch(ex, loss_target):
        ex = dict(ex)
        diff = ex.pop(TWIN_DIFF_INPUT)
        return grad_fn(weights, diff, {**shared, **ex}, loss_target)

    if N_MICROBATCH == 1:
        loss, (grad_w, grad_x) = one_microbatch(per_example, given["loss_target"])
    else:
        def body(carry, xs):
            loss_sum, grad_sum = carry
            l_k, (gw_k, gx_k) = one_microbatch(xs[0], xs[1])
            with _jax.named_scope("update"):
                return (loss_sum + l_k, _jax.tree.map(_jnp.add, grad_sum, gw_k)), gx_k

        init = (_jnp.zeros((), _jnp.float32), _jax.tree.map(_jnp.zeros_like, weights))
        (loss, grad_w), grad_x = _jax.lax.scan(body, init, (per_example, given["loss_target"]))
    with _jax.named_scope("update"):
        delta_w, new_m, new_v = {}, {}, {}
        for n in TWIN_WEIGHTS:
            delta_w[n], new_m[n], new_v[n] = _adamw(weights[n], grad_w[n], given["m_" + n], given["v_" + n])
    return (loss, grad_x, *[grad_w[n] for n in TWIN_WEIGHTS], *[delta_w[n] for n in TWIN_WEIGHTS],
            *[new_m[n] for n in TWIN_WEIGHTS], *[new_v[n] for n in TWIN_WEIGHTS])
```

```python
import functools
import math

import jax
import jax.numpy as jnp
import numpy as np
from jax import lax
from jax.experimental import pallas as pl
from jax.experimental.pallas import tpu as pltpu

F32 = jnp.float32
BF16 = jnp.bfloat16
N_DEV = 8
MESH_AXES = ("x", "y", "c")

D_MODEL = 1024
DEPTH = 2
GRID_W = 64
SWA_HEADS, SWA_KV_HEADS, SWA_HEAD_DIM, SWA_WINDOW, SWA_BLOCK = 4, 2, 64, 128, 128
DN_HEADS, DN_HEAD_DIM, DN_CHUNK = 4, 64, 64
RET_HEADS, RET_QK_DIM, RET_V_DIM, RET_CHUNK = 4, 32, 64, 64
MLA_HEADS, MLA_Q_RANK, MLA_KV_RANK, MLA_NOPE_DIM, MLA_ROPE_DIM, MLA_V_DIM = 4, 256, 128, 64, 32, 64
D_FF = 4 * D_MODEL
ROPE_BASE = 10000.0
NORM_EPS = 1e-6
LN_EPS = 1e-5
DEEPNORM_ALPHA = (2 * DEPTH) ** 0.25
SWA_Q = SWA_HEADS * SWA_HEAD_DIM
SWA_KV = SWA_KV_HEADS * SWA_HEAD_DIM
DN_W = DN_HEADS * DN_HEAD_DIM
RET_QK = RET_HEADS * RET_QK_DIM
RET_V = RET_HEADS * RET_V_DIM
IN_SPLITS = (SWA_Q, SWA_KV, SWA_KV, 3 * DN_W, DN_W, 4 * DN_HEADS, RET_QK, RET_QK, RET_V, RET_V,
             MLA_Q_RANK, MLA_KV_RANK, MLA_ROPE_DIM)
IN_WIDTH = sum(IN_SPLITS)
IN_WIDTH_PAD = -(-IN_WIDTH // 128) * 128

ADAM_LR, ADAM_B1, ADAM_B2, ADAM_EPS, ADAM_WD, ADAM_STEP = 0.001, 0.9, 0.999, 1e-08, 0.01, 10

WEIGHTS = ['c_ctx', 'ada_w', 'ada_b', 'w_in', 'swa_sink', 'dn_conv_w', 'dn_a_log', 'dn_dt_bias', 'dn_norm_g',
           'ret_log1m_gamma', 'ret_norm_g', 'mla_q_norm', 'mla_w_uq', 'mla_kv_norm', 'mla_w_ukv', 'w_out', 'ln1_g',
           'ln1_b', 'w_ff1', 'w_ff2', 'ln2_g', 'ln2_b']
FWD_INPUTS = ['x', 'c', 'ctx'] + WEIGHTS
ARG_NAMES = FWD_INPUTS + ['loss_target'] + ['m_' + n for n in WEIGHTS] + ['v_' + n for n in WEIGHTS]

BIG = (('w_in', (D_MODEL, IN_WIDTH), 1), ('w_out', (D_MODEL, D_MODEL), 0), ('w_ff1', (D_MODEL, D_FF), 1),
       ('w_ff2', (D_FF, D_MODEL), 0), ('mla_w_uq', (MLA_Q_RANK, MLA_HEADS * (MLA_NOPE_DIM + MLA_ROPE_DIM)), 1),
       ('mla_w_ukv', (MLA_KV_RANK, MLA_HEADS * (MLA_NOPE_DIM + MLA_V_DIM)), 1))
SLAB_COLS = 1024
SMALL = ('swa_sink', 'dn_a_log', 'dn_dt_bias', 'dn_norm_g', 'ret_log1m_gamma', 'ret_norm_g', 'mla_q_norm',
         'mla_kv_norm', 'ln1_g', 'ln1_b', 'ln2_g', 'ln2_b')


def _pcall(body, **kw):
    return pl.pallas_call(body, **kw)


def _pick(n, cands):
    for cand in cands:
        if n % cand == 0:
            return cand
    return n


def _bdot(a, b, dims):
    return lax.dot_general(a.astype(BF16), b.astype(BF16), dims, preferred_element_type=F32)


def _lane0(t):
    return jnp.where(lax.broadcasted_iota(jnp.int32, t.shape, t.ndim - 1) == 0, t, 0.0)


NN = (((1,), (0,)), ((), ()))
NT = (((1,), (1,)), ((), ()))
TN = (((0,), (0,)), ((), ()))
BNN = (((2,), (1,)), ((0,), (0,)))
BNT = (((2,), (2,)), ((0,), (0,)))


def _mm_call(a, b, trans_a, name):
    if trans_a:
        kdim, m = a.shape
    else:
        m, kdim = a.shape
    n = b.shape[1]
    assert b.shape[0] == kdim
    tm = _pick(m, (512, 256, 128))
    tn = _pick(n, (512, 256, 128))
    tk = _pick(kdim, (1024, 512, 256, 128))

    def body(a_ref, b_ref, o_ref):
        k = pl.program_id(2)
        part = _bdot(a_ref[...], b_ref[...], TN if trans_a else NN)

        @pl.when(k == 0)
        def _():
            o_ref[...] = part

        @pl.when(k > 0)
        def _():
            o_ref[...] += part

    if trans_a:
        a_spec = pl.BlockSpec((tk, tm), lambda i, j, k: (k, i))
    else:
        a_spec = pl.BlockSpec((tm, tk), lambda i, j, k: (i, k))
    return _pcall(
        body, name=name, grid=(m // tm, n // tn, kdim // tk),
        in_specs=[a_spec, pl.BlockSpec((tk, tn), lambda i, j, k: (k, j))],
        out_specs=pl.BlockSpec((tm, tn), lambda i, j, k: (i, j)),
        out_shape=jax.ShapeDtypeStruct((m, n), F32),
        compiler_params=pltpu.CompilerParams(dimension_semantics=("parallel", "parallel", "arbitrary")),
    )(a, b)


@jax.custom_vjp
def matmul(a, b):
    return _mm_call(a, b, False, "mm_fwd")


def _matmul_fwd(a, b):
    return _mm_call(a, b, False, "mm_fwd"), (a, b)


def _matmul_bwd(res, g):
    a, b = res
    da = _mm_call(g, jnp.transpose(b), False, "mm_bwd_da")
    db = _mm_call(a, g, True, "mm_bwd_db")
    return da, db


matmul.defvjp(_matmul_fwd, _matmul_bwd)


def _attn_probs(q, k, sink, scale, has_sink):
    s = _bdot(q, k, NT) * scale
    m = jnp.max(s, axis=-1, keepdims=True)
    if has_sink:
        m = jnp.maximum(m, sink)
    p = jnp.exp(s - m)
    den = jnp.sum(p, axis=-1, keepdims=True)
    p_sink = None
    if has_sink:
        p_sink = jnp.exp(sink - m)
        den = den + p_sink
    inv = 1.0 / den
    if has_sink:
        p_sink = p_sink * inv
    return p * inv, p_sink


def _attn_full_fwd_call(q, k, v, sink, scale, has_sink):
    g, sq, dq = q.shape
    nk, dv = k.shape[1], v.shape[2]
    bq = _pick(sq, (256, 128))

    def body(q_ref, k_ref, v_ref, sink_ref, o_ref):
        p, _ = _attn_probs(q_ref[0], k_ref[0], sink_ref[0, :, 0:1], scale, has_sink)
        o_ref[0] = _bdot(p, v_ref[0], NN)

    return _pcall(
        body, name="attn_full_fwd", grid=(g, sq // bq),
        in_specs=[pl.BlockSpec((1, bq, dq), lambda b, i: (b, i, 0)), pl.BlockSpec((1, nk, dq), lambda b, i: (b, 0, 0)),
                  pl.BlockSpec((1, nk, dv), lambda b, i: (b, 0, 0)), pl.BlockSpec((1, 1, 128), lambda b, i: (b, 0, 0))],
        out_specs=pl.BlockSpec((1, bq, dv), lambda b, i: (b, i, 0)),
        out_shape=jax.ShapeDtypeStruct((g, sq, dv), F32),
        compiler_params=pltpu.CompilerParams(dimension_semantics=("parallel", "arbitrary")),
    )(q, k, v, sink)


def _attn_full_bwd_call(q, k, v, sink, o, do, scale, has_sink):
    g, sq, dq = q.shape
    nk, dv = k.shape[1], v.shape[2]
    bq = _pick(sq, (256, 128))

    def body(q_ref, k_ref, v_ref, sink_ref, o_ref, do_ref, dq_ref, dk_ref, dv_ref, dsink_ref):
        i = pl.program_id(1)
        qv, kv, vv, dov = q_ref[0], k_ref[0], v_ref[0], do_ref[0]
        p, p_sink = _attn_probs(qv, kv, sink_ref[0, :, 0:1], scale, has_sink)
        delta = jnp.sum(dov * o_ref[0], axis=-1, keepdims=True)
        dv_part = _bdot(p, dov, TN)
        dp = _bdot(dov, vv, NT)
        ds = p * (dp - delta) * scale
        dq_ref[0] = _bdot(ds, kv, NN)
        dk_part = _bdot(ds, qv, TN)
        if has_sink:
            dsk = jnp.broadcast_to(-jnp.sum(p_sink * delta, axis=0, keepdims=True), (1, 128))
        else:
            dsk = jnp.zeros((1, 128), F32)

        @pl.when(i == 0)
        def _():
            dk_ref[0] = dk_part
            dv_ref[0] = dv_part
            dsink_ref[0] = dsk

        @pl.when(i > 0)
        def _():
            dk_ref[0] += dk_part
            dv_ref[0] += dv_part
            dsink_ref[0] += dsk

    qspec = pl.BlockSpec((1, bq, dq), lambda b, i: (b, i, 0))
    kspec = pl.BlockSpec((1, nk, dq), lambda b, i: (b, 0, 0))
    vspec = pl.BlockSpec((1, nk, dv), lambda b, i: (b, 0, 0))
    ospec = pl.BlockSpec((1, bq, dv), lambda b, i: (b, i, 0))
    sspec = pl.BlockSpec((1, 1, 128), lambda b, i: (b, 0, 0))
    return _pcall(
        body, name="attn_full_bwd", grid=(g, sq // bq),
        in_specs=[qspec, kspec, vspec, sspec, ospec, ospec],
        out_specs=[qspec, kspec, vspec, sspec],
        out_shape=[jax.ShapeDtypeStruct(q.shape, F32), jax.ShapeDtypeStruct(k.shape, F32),
                   jax.ShapeDtypeStruct(v.shape, F32), jax.ShapeDtypeStruct(sink.shape, F32)],
        compiler_params=pltpu.CompilerParams(dimension_semantics=("parallel", "arbitrary")),
    )(q, k, v, sink, o, do)


@functools.partial(jax.custom_vjp, nondiff_argnums=(4, 5))
def attn_full(q, k, v, sink, scale, has_sink):
    return _attn_full_fwd_call(q, k, v, sink, scale, has_sink)


def _attn_full_fwd(q, k, v, sink, scale, has_sink):
    o = _attn_full_fwd_call(q, k, v, sink, scale, has_sink)
    return o, (q, k, v, sink, o)


def _attn_full_bwd(scale, has_sink, res, do):
    q, k, v, sink, o = res
    dq, dk, dv, dsink = _attn_full_bwd_call(q, k, v, sink, o, do, scale, has_sink)
    return dq, dk, dv, _lane0(dsink)


attn_full.defvjp(_attn_full_fwd, _attn_full_bwd)


def _swa_probs(q, kw, kc, sink, i, s_len, scale):
    w = SWA_BLOCK
    s_loc = _bdot(q, kw, NT) * scale
    qpos = i * w + lax.broadcasted_iota(jnp.int32, (w, 3 * w), 0)
    kpos = (i - 1) * w + lax.broadcasted_iota(jnp.int32, (w, 3 * w), 1)
    valid = (jnp.abs(kpos - qpos) <= SWA_WINDOW) & (kpos >= 0) & (kpos < s_len)
    s_loc = jnp.where(valid, s_loc, -jnp.inf)
    s_ctx = _bdot(q, kc, NT) * scale
    m = jnp.maximum(jnp.maximum(jnp.max(s_loc, axis=-1, keepdims=True), jnp.max(s_ctx, axis=-1, keepdims=True)), sink)
    p_loc = jnp.exp(s_loc - m)
    p_ctx = jnp.exp(s_ctx - m)
    p_sink = jnp.exp(sink - m)
    inv = 1.0 / (jnp.sum(p_loc, axis=-1, keepdims=True) + jnp.sum(p_ctx, axis=-1, keepdims=True) + p_sink)
    return p_loc * inv, p_ctx * inv, p_sink * inv


def _swa_fwd_call(q, kp, vp, kc, vc, sink, scale):
    g, s_len, d = q.shape
    l_ctx = kc.shape[1]
    w = SWA_BLOCK
    grp = SWA_HEADS // SWA_KV_HEADS

    def body(q_ref, kp_ref, vp_ref, kc_ref, vc_ref, sink_ref, o_ref):
        i = pl.program_id(1)
        start = pl.multiple_of(i * w, w)
        kw = kp_ref[0, pl.ds(start, 3 * w), :]
        vw = vp_ref[0, pl.ds(start, 3 * w), :]
        p_loc, p_ctx, _ = _swa_probs(q_ref[0], kw, kc_ref[0], sink_ref[0, :, 0:1], i, s_len, scale)
        o_ref[0] = _bdot(p_loc, vw, NN) + _bdot(p_ctx, vc_ref[0], NN)

    return _pcall(
        body, name="swa_fwd", grid=(g, s_len // w),
        in_specs=[pl.BlockSpec((1, w, d), lambda b, i: (b, i, 0)),
                  pl.BlockSpec((1, s_len + 2 * w, d), lambda b, i: (b // grp, 0, 0)),
                  pl.BlockSpec((1, s_len + 2 * w, d), lambda b, i: (b // grp, 0, 0)),
                  pl.BlockSpec((1, l_ctx, d), lambda b, i: (b // grp, 0, 0)),
                  pl.BlockSpec((1, l_ctx, d), lambda b, i: (b // grp, 0, 0)),
                  pl.BlockSpec((1, 1, 128), lambda b, i: (b, 0, 0))],
        out_specs=pl.BlockSpec((1, w, d), lambda b, i: (b, i, 0)),
        out_shape=jax.ShapeDtypeStruct(q.shape, F32),
        compiler_params=pltpu.CompilerParams(dimension_semantics=("parallel", "arbitrary")),
    )(q, kp, vp, kc, vc, sink)


def _swa_bwd_call(q, kp, vp, kc, vc, sink, o, do, scale):
    g, s_len, d = q.shape
    l_ctx = kc.shape[1]
    w = SWA_BLOCK
    grp = SWA_HEADS // SWA_KV_HEADS
    sp = s_len + 2 * w

    def body(q_ref, kp_ref, vp_ref, kc_ref, vc_ref, sink_ref, o_ref, do_ref,
             dq_ref, dkp_ref, dvp_ref, dkc_ref, dvc_ref, dsink_ref):
        i = pl.program_id(1)
        start = pl.multiple_of(i * w, w)
        qv, dov = q_ref[0], do_ref[0]
        kw = kp_ref[0, pl.ds(start, 3 * w), :]
        vw = vp_ref[0, pl.ds(start, 3 * w), :]
        kcv, vcv = kc_ref[0], vc_ref[0]
        p_loc, p_ctx, p_sink = _swa_probs(qv, kw, kcv, sink_ref[0, :, 0:1], i, s_len, scale)
        delta = jnp.sum(dov * o_ref[0], axis=-1, keepdims=True)
        ds_loc = p_loc * (_bdot(dov, vw, NT) - delta) * scale
        ds_ctx = p_ctx * (_bdot(dov, vcv, NT) - delta) * scale
        dq_ref[0] = _bdot(ds_loc, kw, NN) + _bdot(ds_ctx, kcv, NN)
        dsk = jnp.broadcast_to(-jnp.sum(p_sink * delta, axis=0, keepdims=True), (1, 128))

        @pl.when(i == 0)
        def _():
            dkp_ref[...] = jnp.zeros_like(dkp_ref)
            dvp_ref[...] = jnp.zeros_like(dvp_ref)
            dkc_ref[...] = jnp.zeros_like(dkc_ref)
            dvc_ref[...] = jnp.zeros_like(dvc_ref)
            dsink_ref[...] = jnp.zeros_like(dsink_ref)

        dkp_ref[0, pl.ds(start, 3 * w), :] += _bdot(ds_loc, qv, TN)
        dvp_ref[0, pl.ds(start, 3 * w), :] += _bdot(p_loc, dov, TN)
        dkc_ref[0] += _bdot(ds_ctx, qv, TN)
        dvc_ref[0] += _bdot(p_ctx, dov, TN)
        dsink_ref[0] += dsk

    qspec = pl.BlockSpec((1, w, d), lambda b, i: (b, i, 0))
    kin = pl.BlockSpec((1, sp, d), lambda b, i: (b // grp, 0, 0))
    cin = pl.BlockSpec((1, l_ctx, d), lambda b, i: (b // grp, 0, 0))
    kout = pl.BlockSpec((1, sp, d), lambda b, i: (b, 0, 0))
    cout = pl.BlockSpec((1, l_ctx, d), lambda b, i: (b, 0, 0))
    sspec = pl.BlockSpec((1, 1, 128), lambda b, i: (b, 0, 0))
    return _pcall(
        body, name="swa_bwd", grid=(g, s_len // w),
        in_specs=[qspec, kin, kin, cin, cin, sspec, qspec, qspec],
        out_specs=[qspec, kout, kout, cout, cout, sspec],
        out_shape=[jax.ShapeDtypeStruct(q.shape, F32), jax.ShapeDtypeStruct((g, sp, d), F32),
                   jax.ShapeDtypeStruct((g, sp, d), F32), jax.ShapeDtypeStruct((g, l_ctx, d), F32),
                   jax.ShapeDtypeStruct((g, l_ctx, d), F32), jax.ShapeDtypeStruct(sink.shape, F32)],
        compiler_params=pltpu.CompilerParams(dimension_semantics=("parallel", "arbitrary")),
    )(q, kp, vp, kc, vc, sink, o, do)


@functools.partial(jax.custom_vjp, nondiff_argnums=(6,))
def swa_attn(q, kp, vp, kc, vc, sink, scale):
    return _swa_fwd_call(q, kp, vp, kc, vc, sink, scale)


def _swa_attn_fwd(q, kp, vp, kc, vc, sink, scale):
    o = _swa_fwd_call(q, kp, vp, kc, vc, sink, scale)
    return o, (q, kp, vp, kc, vc, sink, o)


def _swa_attn_bwd(scale, res, do):
    q, kp, vp, kc, vc, sink, o = res
    grp = SWA_HEADS // SWA_KV_HEADS
    dq, dkp, dvp, dkc, dvc, dsink = _swa_bwd_call(q, kp, vp, kc, vc, sink, o, do, scale)
    pair = lambda t: t.reshape(t.shape[0] // grp, grp, *t.shape[1:]).sum(axis=1)
    return dq, pair(dkp), pair(dvp), pair(dkc), pair(dvc), _lane0(dsink)


swa_attn.defvjp(_swa_attn_fwd, _swa_attn_bwd)


def _f32dot(a, b, dims):
    return lax.dot_general(a, b, dims, precision=lax.Precision.HIGHEST, preferred_element_type=F32)


def _unit_lower_inverse(coef, descending):
    c = coef.shape[-1]
    row = lax.broadcasted_iota(jnp.int32, (1, c, c), 1)
    col = lax.broadcasted_iota(jnp.int32, (1, c, c), 2)
    t = jnp.broadcast_to((row == col).astype(F32), coef.shape)
    order = range(c - 2, -1, -1) if descending else range(1, c)
    for i in order:
        new_row = -jnp.sum(coef[:, :, i:i + 1] * t, axis=1, keepdims=True)
        t = t + jnp.where(row == i, new_row, 0.0)
    return t


def _dn_masks(c):
    row = lax.broadcasted_iota(jnp.int32, (1, c, c), 1)
    col = lax.broadcasted_iota(jnp.int32, (1, c, c), 2)
    return row, col


def _dn_fwd_call(q, k, k_t, v, gc, bb, gr):
    g, n, c, _ = q.shape

    def body(q_ref, k_ref, kt_ref, v_ref, gc_ref, bb_ref, gr_ref, o_ref, vn_ref, sall_ref, w_ref, u_ref, s_scr):
        i = pl.program_id(0)

        @pl.when(i == 0)
        def _():
            s_scr[...] = jnp.zeros_like(s_scr)

        qv, kv, ktv, vv, gcv, bv, grv = (q_ref[:, 0], k_ref[:, 0], kt_ref[:, 0], v_ref[:, 0], gc_ref[:, 0],
                                          bb_ref[:, 0], gr_ref[:, 0])
        row, col = _dn_masks(c)
        e = jnp.exp(gcv)
        kb = kv * bv
        decay = jnp.exp(jnp.where(row >= col, gcv - grv, -jnp.inf))
        decay_ts = jnp.exp(jnp.where(row < col, grv - gcv, -jnp.inf))
        t = _unit_lower_inverse(_bdot(kv, kb, BNT) * decay_ts, False)
        w = _f32dot(t, kb * e, BNN)
        u = _f32dot(t, vv * bv, BNN)
        glast = grv[:, :, c - 1:c]
        s = s_scr[...]
        sall_ref[:, 0] = s
        vnew = u - _bdot(w, s, BNN)
        o_ref[:, 0] = _bdot(qv * e, s, BNN) + _bdot(_bdot(qv, kv, BNT) * decay, vnew, BNN)
        vn_ref[:, 0] = vnew
        w_ref[:, 0] = w
        u_ref[:, 0] = u
        s_scr[...] = s * jnp.exp(glast) + _bdot(ktv * jnp.exp(glast - grv), vnew, BNN)

    blk = pl.BlockSpec((g, 1, c, c), lambda i: (0, i, 0, 0))
    rblk = pl.BlockSpec((g, 1, 1, c), lambda i: (0, i, 0, 0))
    big = jax.ShapeDtypeStruct((g, n, c, c), F32)
    return _pcall(
        body, name="dn_fwd", grid=(n,),
        in_specs=[blk, blk, blk, blk, blk, blk, rblk],
        out_specs=[blk] * 5, out_shape=[big] * 5,
        scratch_shapes=[pltpu.VMEM((g, c, c), F32)],
        compiler_params=pltpu.CompilerParams(dimension_semantics=("arbitrary",)),
    )(q, k, k_t, v, gc, bb, gr)


def _dn_bwd_call(q, k, q_t, k_t, v, gc, bb, gr, br, sall, vn, w, u, do):
    g, n, c, _ = q.shape

    def body(q_ref, k_ref, qt_ref, kt_ref, v_ref, gc_ref, bb_ref, gr_ref, br_ref, sall_ref, vn_ref, w_ref, u_ref, do_ref,
             dq_ref, dk_ref, dv_ref, dgc_ref, dbb_ref, dgr_ref, ds_scr):
        i = pl.program_id(0)

        @pl.when(i == 0)
        def _():
            ds_scr[...] = jnp.zeros_like(ds_scr)

        qv, kv, qtv, ktv, vv, gcv, bv, grv, brv = (q_ref[:, 0], k_ref[:, 0], qt_ref[:, 0], kt_ref[:, 0], v_ref[:, 0],
                                                    gc_ref[:, 0], bb_ref[:, 0], gr_ref[:, 0], br_ref[:, 0])
        s, vnew, w, u, dov = sall_ref[:, 0], vn_ref[:, 0], w_ref[:, 0], u_ref[:, 0], do_ref[:, 0]
        dsn = ds_scr[...]
        row, col = _dn_masks(c)
        e = jnp.exp(gcv)
        er = jnp.exp(grv)
        kb = kv * bv
        decay = jnp.exp(jnp.where(row >= col, gcv - grv, -jnp.inf))
        decay_s = jnp.where(row > col, decay, 0.0)
        decay_t = jnp.exp(jnp.where(row <= col, grv - gcv, -jnp.inf))
        decay_ts = jnp.where(row < col, decay_t, 0.0)
        kk = _bdot(kb, kv, BNT)
        tt = _unit_lower_inverse(kk * decay_s, True)
        glast = grv[:, :, c - 1:c]
        eg = jnp.exp(glast)
        x = jnp.exp(glast - gcv)
        kt = kv * x
        qk_raw = _bdot(qv, kv, BNT)
        w_t = _f32dot(ktv * (brv * er), tt, BNN)
        dvn = _bdot(_bdot(kv, qv, BNT) * decay_t, dov, BNN) + _bdot(kt, dsn, BNN)
        dqk = _bdot(dov, vnew, BNT)
        dqk_t = _bdot(vnew, dov, BNT)
        dqd = _bdot(dov, s, BNT)
        dkt = _bdot(vnew, dsn, BNT)
        deg = jnp.sum(jnp.sum(dsn * s, axis=2, keepdims=True), axis=1, keepdims=True)
        dw = -_bdot(dvn, s, BNT)
        ds_scr[...] = dsn * eg + _bdot(qtv * er, dov, BNN) - _bdot(w_t, dvn, BNN)
        dwp = _f32dot(tt, dw, BNN)
        dup = _f32dot(tt, dvn, BNN)
        d_a = -(_bdot(dwp, w, BNT) + _bdot(dup, u, BNT))
        d_at = -(_bdot(w, dwp, BNT) + _bdot(u, dup, BNT))
        dkb = _bdot(d_a * decay_s, kv, BNN) + dwp * e
        dkx = dkt * kv * x
        dq_ref[:, 0] = dqd * e + _bdot(dqk * decay, kv, BNN)
        dk_ref[:, 0] = _bdot(d_at * decay_ts, kb, BNN) + dkb * bv + dkt * x + _bdot(dqk_t * decay_t, qv, BNN)
        dv_ref[:, 0] = dup * bv
        dbb_ref[:, 0] = dkb * kv + dup * vv
        ddiff = dqk * qk_raw * decay + d_a * kk * decay_s
        dgc_ref[:, 0] = ddiff + (dwp * kb + dqd * qv) * e - dkx
        dglast = jnp.sum(jnp.sum(dkx, axis=2, keepdims=True), axis=1, keepdims=True) + deg * eg
        lane = lax.broadcasted_iota(jnp.int32, (1, 1, c), 2)
        dgr_ref[:, 0] = jnp.where(lane == c - 1, dglast, 0.0) - jnp.sum(ddiff, axis=1, keepdims=True)

    blk = pl.BlockSpec((g, 1, c, c), lambda i: (0, n - 1 - i, 0, 0))
    rblk = pl.BlockSpec((g, 1, 1, c), lambda i: (0, n - 1 - i, 0, 0))
    big = jax.ShapeDtypeStruct((g, n, c, c), F32)
    return _pcall(
        body, name="dn_bwd", grid=(n,),
        in_specs=[blk] * 7 + [rblk, rblk] + [blk] * 5,
        out_specs=[blk] * 5 + [rblk],
        out_shape=[big] * 5 + [jax.ShapeDtypeStruct((g, n, 1, c), F32)],
        scratch_shapes=[pltpu.VMEM((g, c, c), F32)],
        compiler_params=pltpu.CompilerParams(dimension_semantics=("arbitrary",)),
    )(q, k, q_t, k_t, v, gc, bb, gr, br, sall, vn, w, u, do)


_t = lambda a: jnp.swapaxes(a, -1, -2)


def _dn_forms(gcum, beta, d):
    lanes = lambda t: jnp.broadcast_to(t[..., None], t.shape + (d,))
    return lanes(gcum), lanes(beta), gcum[:, :, None, :], beta[:, :, None, :]


@jax.custom_vjp
def dn_chunked(q, k, v, gcum, beta):
    gc, bb, gr, _ = _dn_forms(gcum, beta, q.shape[-1])
    return _dn_fwd_call(q, k, _t(k), v, gc, bb, gr)[0]


def _dn_chunked_fwd(q, k, v, gcum, beta):
    gc, bb, gr, _ = _dn_forms(gcum, beta, q.shape[-1])
    o, vn, sall, w, u = _dn_fwd_call(q, k, _t(k), v, gc, bb, gr)
    return o, (q, k, v, gcum, beta, vn, sall, w, u)


def _dn_chunked_bwd(res, do):
    q, k, v, gcum, beta, vn, sall, w, u = res
    gc, bb, gr, br = _dn_forms(gcum, beta, q.shape[-1])
    dq, dk, dv, dgc, dbb, dgr = _dn_bwd_call(q, k, _t(q), _t(k), v, gc, bb, gr, br, sall, vn, w, u, do)
    return dq, dk, dv, jnp.sum(dgc, axis=-1) + dgr[:, :, 0, :], jnp.sum(dbb, axis=-1)


dn_chunked.defvjp(_dn_chunked_fwd, _dn_chunked_bwd)


def _ret_fwd_call(q, k, k_t, v, dmat, xi_b, zeta_r, gm):
    g, n, c, dk = q.shape
    dv = v.shape[-1]

    def body(q_ref, k_ref, kt_ref, v_ref, d_ref, xib_ref, zr_ref, gm_ref, o_ref, starts_ref, s_scr):
        i = pl.program_id(0)

        @pl.when(i == 0)
        def _():
            s_scr[...] = jnp.zeros_like(s_scr)

        qv, vv = q_ref[:, 0], v_ref[:, 0]
        s = s_scr[...]
        starts_ref[:, 0] = s
        o_ref[:, 0] = _bdot(_bdot(qv, k_ref[:, 0], BNT) * d_ref[...], vv, BNN) + _bdot(qv * xib_ref[...], s, BNN)
        s_scr[...] = s * gm_ref[...] + _bdot(kt_ref[:, 0] * zr_ref[...], vv, BNN)

    tok = lambda d: pl.BlockSpec((g, 1, c, d), lambda i: (0, i, 0, 0))
    const = lambda a, b: pl.BlockSpec((g, a, b), lambda i: (0, 0, 0))
    return _pcall(
        body, name="ret_fwd", grid=(n,),
        in_specs=[tok(dk), tok(dk), pl.BlockSpec((g, 1, dk, c), lambda i: (0, i, 0, 0)), tok(dv),
                  const(c, c), const(c, dk), const(1, c), const(dk, dv)],
        out_specs=[tok(dv), pl.BlockSpec((g, 1, dk, dv), lambda i: (0, i, 0, 0))],
        out_shape=[jax.ShapeDtypeStruct((g, n, c, dv), F32), jax.ShapeDtypeStruct((g, n, dk, dv), F32)],
        scratch_shapes=[pltpu.VMEM((g, dk, dv), F32)],
        compiler_params=pltpu.CompilerParams(dimension_semantics=("arbitrary",)),
    )(q, k, k_t, v, dmat, xi_b, zeta_r, gm)


def _ret_bwd_call(q, k, q_t, k_t, v, dmat, dmat_t, xi_b, xi_r, zeta_b, gm, starts, do):
    g, n, c, dk = q.shape
    dv = v.shape[-1]

    def body(q_ref, k_ref, qt_ref, kt_ref, v_ref, d_ref, dt_ref, xib_ref, xr_ref, zb_ref, gm_ref, starts_ref, do_ref,
             dq_ref, dk_ref, dv_ref, dd_ref, dxib_ref, dzb_ref, dgm_ref, ds_scr):
        i = pl.program_id(0)

        @pl.when(i == 0)
        def _():
            ds_scr[...] = jnp.zeros_like(ds_scr)
            dd_ref[...] = jnp.zeros_like(dd_ref)
            dxib_ref[...] = jnp.zeros_like(dxib_ref)
            dzb_ref[...] = jnp.zeros_like(dzb_ref)
            dgm_ref[...] = jnp.zeros_like(dgm_ref)

        qv, kv, vv, dov = q_ref[:, 0], k_ref[:, 0], v_ref[:, 0], do_ref[:, 0]
        s, dsn = starts_ref[:, 0], ds_scr[...]
        dm, dmt, zb = d_ref[...], dt_ref[...], zb_ref[...]
        qk_raw = _bdot(qv, kv, BNT)
        dqkd = _bdot(dov, vv, BNT)
        do_s = _bdot(dov, s, BNT)
        dkz = _bdot(vv, dsn, BNT)
        dq_ref[:, 0] = _bdot(dqkd * dm, kv, BNN) + do_s * xib_ref[...]
        dk_ref[:, 0] = _bdot(_bdot(vv, dov, BNT) * dmt, qv, BNN) + dkz * zb
        dv_ref[:, 0] = _bdot(_bdot(kv, qv, BNT) * dmt, dov, BNN) + _bdot(kv * zb, dsn, BNN)
        dd_ref[...] += dqkd * qk_raw
        dxib_ref[...] += do_s * qv
        dzb_ref[...] += dkz * kv
        dgm_ref[...] += dsn * s
        ds_scr[...] = dsn * gm_ref[...] + _bdot(qt_ref[:, 0] * xr_ref[...], dov, BNN)

    tok = lambda d: pl.BlockSpec((g, 1, c, d), lambda i: (0, n - 1 - i, 0, 0))
    tok_t = pl.BlockSpec((g, 1, dk, c), lambda i: (0, n - 1 - i, 0, 0))
    const = lambda a, b: pl.BlockSpec((g, a, b), lambda i: (0, 0, 0))
    sds = lambda *s: jax.ShapeDtypeStruct(s, F32)
    return _pcall(
        body, name="ret_bwd", grid=(n,),
        in_specs=[tok(dk), tok(dk), tok_t, tok_t, tok(dv), const(c, c), const(c, c), const(c, dk), const(1, c),
                  const(c, dk), const(dk, dv), pl.BlockSpec((g, 1, dk, dv), lambda i: (0, n - 1 - i, 0, 0)), tok(dv)],
        out_specs=[tok(dk), tok(dk), tok(dv), const(c, c), const(c, dk), const(c, dk), const(dk, dv)],
        out_shape=[sds(g, n, c, dk), sds(g, n, c, dk), sds(g, n, c, dv), sds(g, c, c), sds(g, c, dk), sds(g, c, dk),
                   sds(g, dk, dv)],
        scratch_shapes=[pltpu.VMEM((g, dk, dv), F32)],
        compiler_params=pltpu.CompilerParams(dimension_semantics=("arbitrary",)),
    )(q, k, q_t, k_t, v, dmat, dmat_t, xi_b, xi_r, zeta_b, gm, starts, do)


def _ret_forms(xi, zeta, gm, dk, dv):
    lanes = lambda t: jnp.broadcast_to(t[..., None], t.shape + (dk,))
    return lanes(xi), xi[:, None, :], lanes(zeta), zeta[:, None, :], jnp.broadcast_to(gm[:, None, None], gm.shape + (dk, dv))


@jax.custom_vjp
def ret_chunked(q, k, v, dmat, xi, zeta, gm):
    xi_b, _, _, zeta_r, gm_f = _ret_forms(xi, zeta, gm, q.shape[-1], v.shape[-1])
    return _ret_fwd_call(q, k, _t(k), v, dmat, xi_b, zeta_r, gm_f)[0]


def _ret_chunked_fwd(q, k, v, dmat, xi, zeta, gm):
    xi_b, _, _, zeta_r, gm_f = _ret_forms(xi, zeta, gm, q.shape[-1], v.shape[-1])
    o, starts = _ret_fwd_call(q, k, _t(k), v, dmat, xi_b, zeta_r, gm_f)
    return o, (q, k, v, dmat, xi, zeta, gm, starts)


def _ret_chunked_bwd(res, do):
    q, k, v, dmat, xi, zeta, gm, starts = res
    xi_b, xi_r, zeta_b, _, gm_f = _ret_forms(xi, zeta, gm, q.shape[-1], v.shape[-1])
    dq, dk, dv, dd, dxib, dzb, dgm = _ret_bwd_call(q, k, _t(q), _t(k), v, dmat, _t(dmat), xi_b, xi_r, zeta_b, gm_f, starts, do)
    return dq, dk, dv, dd, jnp.sum(dxib, axis=-1), jnp.sum(dzb, axis=-1), jnp.sum(dgm, axis=(1, 2))


ret_chunked.defvjp(_ret_chunked_fwd, _ret_chunked_bwd)


def _peer(k):
    mx, my, mc = lax.axis_index("x"), lax.axis_index("y"), lax.axis_index("c")
    px = 1 - mx if k & 4 else mx
    py = 1 - my if k & 2 else my
    pc = 1 - mc if k & 1 else mc
    return (px, py, pc), 4 * px + 2 * py + pc


def _exchange_call(x, all_to_all, name):
    blk = x.shape[1:] if all_to_all else x.shape

    def body(x_ref, out_ref, send_sems, recv_sems, local_sem):
        me = 4 * lax.axis_index("x") + 2 * lax.axis_index("y") + lax.axis_index("c")
        mine = pltpu.make_async_copy(x_ref.at[me] if all_to_all else x_ref, out_ref.at[me], local_sem)
        mine.start()
        sends = []
        for k in range(1, N_DEV):
            dev, idx = _peer(k)
            cp = pltpu.make_async_remote_copy(
                src_ref=x_ref.at[idx] if all_to_all else x_ref, dst_ref=out_ref.at[me],
                send_sem=send_sems.at[k - 1], recv_sem=recv_sems.at[k - 1],
                device_id=dev, device_id_type=pl.DeviceIdType.MESH)
            cp.start()
            sends.append(cp)
        for k in range(1, N_DEV):
            dev, idx = _peer(k)
            pltpu.make_async_remote_copy(
                src_ref=x_ref.at[idx] if all_to_all else x_ref, dst_ref=out_ref.at[idx],
                send_sem=send_sems.at[k - 1], recv_sem=recv_sems.at[k - 1],
                device_id=dev, device_id_type=pl.DeviceIdType.MESH).wait_recv()
        for cp in sends:
            cp.wait_send()
        mine.wait()

    return _pcall(
        body, name=name,
        in_specs=[pl.BlockSpec(memory_space=pl.ANY)], out_specs=pl.BlockSpec(memory_space=pl.ANY),
        out_shape=jax.ShapeDtypeStruct((N_DEV,) + tuple(blk), x.dtype),
        scratch_shapes=[pltpu.SemaphoreType.DMA((N_DEV - 1,)), pltpu.SemaphoreType.DMA((N_DEV - 1,)),
                        pltpu.SemaphoreType.DMA],
    )(x)


def _sum8_call(x, name):
    _, r, c = x.shape
    tr = _pick(r, (256, 160, 128, 72, 64, 32, 16, 8))

    def body(x_ref, o_ref):
        acc = x_ref[0].astype(F32)
        for d in range(1, N_DEV):
            acc = acc + x_ref[d].astype(F32)
        o_ref[...] = acc

    return _pcall(
        body, name=name, grid=(r // tr,),
        in_specs=[pl.BlockSpec((N_DEV, tr, c), lambda i: (0, i, 0))],
        out_specs=pl.BlockSpec((tr, c), lambda i: (i, 0)),
        out_shape=jax.ShapeDtypeStruct((r, c), F32),
        compiler_params=pltpu.CompilerParams(dimension_semantics=("parallel",)),
    )(x)


def _adamw_call(w, g, m, v, name):
    r, c = w.shape
    tr = _pick(r, (256, 128, 64, 32, 16, 8))
    bc1 = 1.0 - ADAM_B1 ** ADAM_STEP
    bc2 = 1.0 - ADAM_B2 ** ADAM_STEP

    def body(w_ref, g_ref, m_ref, v_ref, d_ref, nm_ref, nv_ref):
        gv = g_ref[...]
        nm = ADAM_B1 * m_ref[...] + (1.0 - ADAM_B1) * gv
        nv = ADAM_B2 * v_ref[...] + (1.0 - ADAM_B2) * jnp.square(gv)
        d_ref[...] = -ADAM_LR * ((nm / bc1) / (jnp.sqrt(nv / bc2) + ADAM_EPS) + ADAM_WD * w_ref[...])
        nm_ref[...] = nm
        nv_ref[...] = nv

    spec = pl.BlockSpec((tr, c), lambda i: (i, 0))
    sds = jax.ShapeDtypeStruct((r, c), F32)
    return _pcall(
        body, name=name, grid=(r // tr,), in_specs=[spec] * 4, out_specs=[spec] * 3, out_shape=[sds] * 3,
        compiler_params=pltpu.CompilerParams(dimension_semantics=("parallel",)),
    )(w, g, m, v)


def layer_norm(x, g, b):
    mu = jnp.mean(x, axis=-1, keepdims=True)
    var = jnp.mean(jnp.square(x - mu), axis=-1, keepdims=True)
    return (x - mu) * lax.rsqrt(var + LN_EPS) * g + b


def rms_norm(x, g):
    return x * lax.rsqrt(jnp.mean(x * x, axis=-1, keepdims=True) + NORM_EPS) * g


def head_layer_norm(o, g):
    b_, t_, h_, d_ = o.shape
    mu = jnp.mean(o, axis=-1, keepdims=True)
    var = jnp.mean(jnp.square(o - mu), axis=-1, keepdims=True)
    return ((o - mu) * lax.rsqrt(var + NORM_EPS)).reshape(b_, t_, h_ * d_) * g


def l2norm(t):
    return t * lax.rsqrt(jnp.sum(t * t, axis=-1, keepdims=True) + NORM_EPS)


def rope_freqs(dim):
    return ROPE_BASE ** (-jnp.arange(0, dim, 2, dtype=F32) / dim)


def axial_rope(rows, rot_dim):
    row = jnp.broadcast_to(jnp.arange(rows, dtype=F32)[:, None], (rows, GRID_W)).reshape(-1)
    col = jnp.broadcast_to(jnp.arange(GRID_W, dtype=F32)[None, :], (rows, GRID_W)).reshape(-1)
    inv = rope_freqs(rot_dim // 2)
    ang = jnp.concatenate([row[:, None] * inv, col[:, None] * inv], axis=-1)
    return jnp.cos(ang), jnp.sin(ang)


def sequence_rope(n_tok, rot_dim):
    ang = jnp.arange(n_tok, dtype=F32)[:, None] * rope_freqs(rot_dim)
    return jnp.cos(ang), jnp.sin(ang)


def apply_rope(x, cos, sin):
    x1, x2 = jnp.split(x, 2, axis=-1)
    c = cos[:, None, :]
    s = sin[:, None, :]
    return jnp.concatenate([x1 * c - x2 * s, x1 * s + x2 * c], axis=-1)


def _flip_t(t):
    return jnp.flip(t, axis=2)


def _split_columns(z):
    idx = np.cumsum(np.array(IN_SPLITS))[:-1].tolist()
    return jnp.split(z, idx, axis=-1)


def short_conv(x, w):
    k_width, ch = w.shape
    pad = k_width // 2
    return lax.conv_general_dilated(x, w[:, None, :], window_strides=(1,), padding=[(pad, pad)],
                                    dimension_numbers=('NWC', 'WIO', 'NWC'), feature_group_count=ch)


def _to_heads(t, h, d):
    b_, t_, _ = t.shape
    return t.reshape(b_, t_, h, d).transpose(0, 2, 1, 3).reshape(b_ * h, t_, d)


def _from_heads(t, b_):
    g, t_, d = t.shape
    return t.reshape(b_, g // b_, t_, d).transpose(0, 2, 1, 3).reshape(b_, t_, (g // b_) * d)


def _lane_scalar(vals):
    return jnp.broadcast_to(vals[:, None, None], (vals.shape[0], 1, 128))


def swa_group(q, k, v, qc, kc, vc, sink, cos, sin, with_ctx_out):
    b_, s_, _ = q.shape
    l_ = kc.shape[1]
    grp = SWA_HEADS // SWA_KV_HEADS
    d = SWA_HEAD_DIM
    w_ = SWA_BLOCK
    scale = d ** -0.5
    qh = apply_rope(q.reshape(b_, s_, SWA_HEADS, d), cos, sin).transpose(0, 2, 1, 3).reshape(b_ * SWA_HEADS, s_, d)
    kh = apply_rope(k.reshape(b_, s_, SWA_KV_HEADS, d), cos, sin).transpose(0, 2, 1, 3).reshape(b_ * SWA_KV_HEADS, s_, d)
    vh = _to_heads(v, SWA_KV_HEADS, d)
    kch = _to_heads(kc, SWA_KV_HEADS, d)
    vch = _to_heads(vc, SWA_KV_HEADS, d)
    padk = lambda t: jnp.pad(t, ((0, 0), (w_, w_), (0, 0)))
    sink_g = _lane_scalar(jnp.tile(sink, b_))
    y = _from_heads(swa_attn(qh, padk(kh), padk(vh), kch, vch, sink_g, scale), b_)
    yc = None
    if with_ctx_out:
        qch = _to_heads(qc, SWA_HEADS, d)
        rep = lambda t: jnp.repeat(t.reshape(b_, SWA_KV_HEADS, l_, d), grp, axis=1).reshape(b_ * SWA_HEADS, l_, d)
        yc = _from_heads(attn_full(qch, rep(kch), rep(vch), sink_g, scale, True), b_)
    return y, yc


def gated_delta_chunked(q, k, v, log_g, beta):
    g_, t_, dk = k.shape
    dv = v.shape[-1]
    c_ = DN_CHUNK
    n = t_ // c_
    assert dk == c_ and dv == c_
    g_cum = jnp.cumsum(log_g.reshape(g_, n, c_), axis=-1)
    o = dn_chunked(q.reshape(g_, n, c_, dk), k.reshape(g_, n, c_, dk), v.reshape(g_, n, c_, dv), g_cum,
                   beta.reshape(g_, n, c_))
    return o.reshape(g_, t_, dv)


def deltanet_group(qkv, z, ab, qkv_c, z_c, ab_c, conv_w, a_log, dt_bias, norm_g, with_ctx_out):
    def prep(qkv_, ab_):
        b_, t_, _ = qkv_.shape
        y = jax.nn.silu(short_conv(qkv_, conv_w))
        q, k, v = [t.reshape(b_, t_, DN_HEADS, DN_HEAD_DIM).transpose(0, 2, 1, 3) for t in jnp.split(y, 3, axis=-1)]
        q = l2norm(q) * DN_HEAD_DIM ** -0.5
        k = l2norm(k)
        ab_ = ab_.reshape(b_, t_, 2, 2, DN_HEADS)
        log_g = -jnp.exp(a_log) * jax.nn.softplus(ab_[:, :, :, 0] + dt_bias)
        beta = jax.nn.sigmoid(ab_[:, :, :, 1])
        return q, k, v, log_g.transpose(2, 0, 3, 1), beta.transpose(2, 0, 3, 1)

    def out(o, z_):
        b_, t_, _ = z_.shape
        o = rms_norm(o.transpose(0, 2, 1, 3), norm_g) * jax.nn.silu(z_).reshape(b_, t_, DN_HEADS, DN_HEAD_DIM)
        return o.reshape(b_, t_, DN_W)

    qc, kc, vc, lgc, bc = prep(qkv_c, ab_c)
    q, k, v, lg, bt = prep(qkv, ab)
    b_, l_, s_ = qkv.shape[0], qkv_c.shape[1], qkv.shape[1]
    seq = lambda tc, tl: jnp.stack([jnp.concatenate([tc, tl], axis=2),
                                    jnp.concatenate([_flip_t(tc), _flip_t(tl)], axis=2)])
    seq_g = lambda tc, tl: jnp.stack([jnp.concatenate([tc[0], tl[0]], axis=2),
                                      jnp.concatenate([_flip_t(tc[1]), _flip_t(tl[1])], axis=2)])
    flat = lambda t: t.reshape((2 * b_ * DN_HEADS,) + t.shape[3:])
    o = gated_delta_chunked(flat(seq(qc, q)), flat(seq(kc, k)), flat(seq(vc, v)), flat(seq_g(lgc, lg)), flat(seq_g(bc, bt)))
    o = o.reshape(2, b_, DN_HEADS, l_ + s_, DN_HEAD_DIM)
    y = out(o[0][:, :, l_:] + _flip_t(o[1][:, :, l_:]), z)
    yc = out(o[0][:, :, :l_] + _flip_t(o[1][:, :, :l_]), z_c) if with_ctx_out else None
    return y, yc


def retention_core(q, k, v, log_gamma):
    _, b_, h_, t_, dk = q.shape
    dv = v.shape[-1]
    c_ = RET_CHUNK
    n = t_ // c_
    g_ = 2 * b_ * h_
    pos = jnp.arange(c_, dtype=F32)
    per_g = lambda t: jnp.broadcast_to(t[:, None], (2, b_) + t.shape[1:]).reshape((g_,) + t.shape[2:])
    zeta = jnp.exp((c_ - 1 - pos) * log_gamma[..., None])
    xi = jnp.exp((pos + 1.0) * log_gamma[..., None])
    rel = pos[:, None] - pos[None, :]
    dmat = jnp.where(rel >= 0, jnp.exp(jnp.maximum(rel, 0.0) * log_gamma[..., None, None]), 0.0)
    gm = jnp.exp(c_ * log_gamma)
    o = ret_chunked(q.reshape(g_, n, c_, dk), k.reshape(g_, n, c_, dk), v.reshape(g_, n, c_, dv),
                    per_g(dmat), per_g(xi), per_g(zeta), per_g(gm))
    return o.reshape(2, b_, h_, t_, dv)


def retention_group(q, k, v, g, qc, kc, vc, gc, log1m_gamma, norm_g, cos, sin, with_ctx_out):
    log_gamma = jnp.log1p(-jnp.exp(log1m_gamma))
    heads = lambda t, dh: t.reshape(t.shape[0], t.shape[1], RET_HEADS, dh)
    bhtd = lambda t: t.transpose(0, 2, 1, 3)
    sc = RET_QK_DIM ** -0.5
    l_ = kc.shape[1]
    q = bhtd(apply_rope(heads(q, RET_QK_DIM), cos, sin)) * sc
    k = bhtd(apply_rope(heads(k, RET_QK_DIM), cos, sin))
    v = bhtd(heads(v, RET_V_DIM))
    kc = bhtd(heads(kc, RET_QK_DIM))
    vc = bhtd(heads(vc, RET_V_DIM))
    qcs = bhtd(heads(qc, RET_QK_DIM)) * sc

    def out(o, g_):
        return head_layer_norm(o.transpose(0, 2, 1, 3), norm_g) * jax.nn.silu(g_)

    seq = lambda tc, tl: jnp.stack([jnp.concatenate([tc, tl], axis=2),
                                    jnp.concatenate([_flip_t(tc), _flip_t(tl)], axis=2)])
    o = retention_core(seq(qcs, q), seq(kc, k), seq(vc, v), log_gamma)
    o_f, o_b = o[0], o[1]
    y = out(o_f[:, :, l_:] + _flip_t(o_b[:, :, l_:]), g)
    yc = out(o_f[:, :, :l_] + _flip_t(o_b[:, :, :l_]), gc) if with_ctx_out else None
    return y, yc


def mla_group(cq, ckv, kr, cq_c, ckv_c, kr_c, q_norm, w_uq, kv_norm, w_ukv, cos, sin, with_ctx_out):
    b_, s_, _ = cq.shape
    l_ = cq_c.shape[1]
    dqk = MLA_NOPE_DIM + MLA_ROPE_DIM
    rows = lambda tl, tc: jnp.concatenate([tl.reshape(b_ * s_, -1), tc.reshape(b_ * l_, -1)], axis=0)
    qa = matmul(rms_norm(rows(cq, cq_c), q_norm), w_uq)
    kva = matmul(rms_norm(rows(ckv, ckv_c), kv_norm), w_ukv)
    q = qa[:b_ * s_].reshape(b_, s_, MLA_HEADS, dqk)
    qc = qa[b_ * s_:].reshape(b_, l_, MLA_HEADS, dqk)
    q = jnp.concatenate([q[..., :MLA_NOPE_DIM], apply_rope(q[..., MLA_NOPE_DIM:], cos, sin)], axis=-1)
    kv = kva[:b_ * s_].reshape(b_, s_, MLA_HEADS, MLA_NOPE_DIM + MLA_V_DIM)
    kvc = kva[b_ * s_:].reshape(b_, l_, MLA_HEADS, MLA_NOPE_DIM + MLA_V_DIM)
    kr = apply_rope(kr[:, :, None, :], cos, sin)
    k = jnp.concatenate([kv[..., :MLA_NOPE_DIM], jnp.broadcast_to(kr, (b_, s_, MLA_HEADS, MLA_ROPE_DIM))], axis=-1)
    kc = jnp.concatenate([kvc[..., :MLA_NOPE_DIM],
                          jnp.broadcast_to(kr_c[:, :, None, :], (b_, l_, MLA_HEADS, MLA_ROPE_DIM))], axis=-1)
    v, vc = kv[..., MLA_NOPE_DIM:], kvc[..., MLA_NOPE_DIM:]
    hd = lambda t: t.transpose(0, 2, 1, 3).reshape(b_ * MLA_HEADS, t.shape[1], t.shape[3])
    scale = dqk ** -0.5
    no_sink = jnp.zeros((b_ * MLA_HEADS, 1, 128), F32)
    kch, vch = hd(kc), hd(vc)
    y = attn_full(hd(q), jnp.concatenate([hd(k), kch], axis=1), jnp.concatenate([hd(v), vch], axis=1), no_sink, scale, False)
    y = _from_heads(y, b_)
    yc = _from_heads(attn_full(hd(qc), kch, vch, no_sink, scale, False), b_) if with_ctx_out else None
    return y, yc


def token_mixers(zl, zc, p, layer, rope, with_ctx_out):
    (a_q, a_k, a_v, b_qkv, b_z, b_ab, c_q, c_k, c_v, c_g, d_cq, d_ckv, d_kr) = _split_columns(zl)
    (a_qc, a_kc, a_vc, b_qkvc, b_zc, b_abc, c_qc, c_kc, c_vc, c_gc, d_cqc, d_ckvc, d_krc) = _split_columns(zc)
    swa_cos, swa_sin, ret_cos, ret_sin, mla_cos, mla_sin = rope
    ya, yac = swa_group(a_q, a_k, a_v, a_qc, a_kc, a_vc, p['swa_sink'][layer], swa_cos, swa_sin, with_ctx_out)
    yb, ybc = deltanet_group(b_qkv, b_z, b_ab, b_qkvc, b_zc, b_abc, p['dn_conv_w'][layer], p['dn_a_log'][layer],
                             p['dn_dt_bias'][layer], p['dn_norm_g'][layer], with_ctx_out)
    yr, yrc = retention_group(c_q, c_k, c_v, c_g, c_qc, c_kc, c_vc, c_gc, p['ret_log1m_gamma'][layer],
                              p['ret_norm_g'][layer], ret_cos, ret_sin, with_ctx_out)
    yd, ydc = mla_group(d_cq, d_ckv, d_kr, d_cqc, d_ckvc, d_krc, p['mla_q_norm'][layer], p['mla_w_uq'][layer],
                        p['mla_kv_norm'][layer], p['mla_w_ukv'][layer], mla_cos, mla_sin, with_ctx_out)
    y = jnp.concatenate([ya, yb, yr, yd], axis=-1)
    yc = jnp.concatenate([yac, ybc, yrc, ydc], axis=-1) if with_ctx_out else None
    return y, yc


def local_loss(p, x, ctx, loss_target):
    b_, n_tok, d_ = x.shape
    l_ = ctx.shape[1]
    rows = n_tok // GRID_W
    rope = (*axial_rope(rows, SWA_HEAD_DIM), *sequence_rope(n_tok, RET_QK_DIM), *axial_rope(rows, MLA_ROPE_DIM))
    xc = ctx
    for layer in range(DEPTH):
        with_ctx_out = layer < DEPTH - 1
        sh1, sc1, g1, sh2, sc2, g2 = jnp.split(p['mod'][layer][:, None, :], 6, axis=-1)
        csh1, csc1, cg1, csh2, csc2, cg2 = jnp.split(p['cmod'][layer][None, None, :], 6, axis=-1)
        h = (x * (1 + sc1) + sh1).reshape(b_ * n_tok, d_)
        hc = (xc * (1 + csc1) + csh1).reshape(b_ * l_, d_)
        z = matmul(jnp.concatenate([h, hc], axis=0), p['w_in'][layer])
        zl = z[:b_ * n_tok, :IN_WIDTH].reshape(b_, n_tok, IN_WIDTH)
        zc = z[b_ * n_tok:, :IN_WIDTH].reshape(b_, l_, IN_WIDTH)
        y, yc = token_mixers(zl, zc, p, layer, rope, with_ctx_out)
        if with_ctx_out:
            yo = matmul(jnp.concatenate([y.reshape(b_ * n_tok, d_), yc.reshape(b_ * l_, d_)], axis=0), p['w_out'][layer])
            x = layer_norm(DEEPNORM_ALPHA * x + g1 * yo[:b_ * n_tok].reshape(b_, n_tok, d_), p['ln1_g'][layer], p['ln1_b'][layer])
            xc = layer_norm(DEEPNORM_ALPHA * xc + cg1 * yo[b_ * n_tok:].reshape(b_, l_, d_), p['ln1_g'][layer], p['ln1_b'][layer])
            h2 = jnp.concatenate([(x * (1 + sc2) + sh2).reshape(b_ * n_tok, d_),
                                  (xc * (1 + csc2) + csh2).reshape(b_ * l_, d_)], axis=0)
            f = matmul(jnp.square(jax.nn.relu(matmul(h2, p['w_ff1'][layer]))), p['w_ff2'][layer])
            x = layer_norm(DEEPNORM_ALPHA * x + g2 * f[:b_ * n_tok].reshape(b_, n_tok, d_), p['ln2_g'][layer], p['ln2_b'][layer])
            xc = layer_norm(DEEPNORM_ALPHA * xc + cg2 * f[b_ * n_tok:].reshape(b_, l_, d_), p['ln2_g'][layer], p['ln2_b'][layer])
        else:
            yo = matmul(y.reshape(b_ * n_tok, d_), p['w_out'][layer])
            x = layer_norm(DEEPNORM_ALPHA * x + g1 * yo.reshape(b_, n_tok, d_), p['ln1_g'][layer], p['ln1_b'][layer])
            h2 = (x * (1 + sc2) + sh2).reshape(b_ * n_tok, d_)
            f = matmul(jnp.square(jax.nn.relu(matmul(h2, p['w_ff1'][layer]))), p['w_ff2'][layer])
            x = layer_norm(DEEPNORM_ALPHA * x + g2 * f.reshape(b_, n_tok, d_), p['ln2_g'][layer], p['ln2_b'][layer])
    err = jnp.square(x - loss_target)
    return 0.5 * jnp.sum(jnp.mean(err, axis=-1))


def _shard_shape(shape, axis):
    s = list(shape)
    s[axis] //= N_DEV
    return tuple(s)


_SLAB_SIZES = [int(np.prod(_shard_shape(shape, axis))) for _, shape, axis in BIG]
_SLAB_ROWS = -(-sum(_SLAB_SIZES) // (SLAB_COLS * 16)) * 16


def _pack_rows(flat):
    padn = _SLAB_ROWS * SLAB_COLS - flat.shape[-1]
    flat = jnp.pad(flat, [(0, 0)] * (flat.ndim - 1) + [(0, padn)])
    return flat.reshape(flat.shape[:-1] + (_SLAB_ROWS, SLAB_COLS))


def _pack_local(a):
    per_layer = [_pack_rows(jnp.concatenate([a[name][l].reshape(-1) for name, _, _ in BIG])) for l in range(DEPTH)]
    return jnp.concatenate(per_layer, axis=0).astype(BF16)


def _unpack_full(slab):
    slab = slab.reshape(N_DEV, DEPTH, _SLAB_ROWS * SLAB_COLS)
    out, off = {}, 0
    for (name, shape, axis), size in zip(BIG, _SLAB_SIZES):
        r, c = _shard_shape(shape, axis)
        pieces = slab[:, :, off:off + size].reshape(N_DEV, DEPTH, r, c)
        if axis == 0:
            full = pieces.transpose(1, 0, 2, 3).reshape(DEPTH, N_DEV * r, c)
        else:
            full = pieces.transpose(1, 2, 0, 3).reshape(DEPTH, r, N_DEV * c)
        out[name] = full.astype(F32)
        off += size
    return out


def _pack_grads(grads):
    parts = []
    for name, shape, axis in BIG:
        g = grads[name]
        r, c = _shard_shape(shape, axis)
        if axis == 0:
            pieces = g.reshape(DEPTH, N_DEV, r, c).transpose(1, 0, 2, 3)
        else:
            pieces = g.reshape(DEPTH, r, N_DEV, c).transpose(2, 0, 1, 3)
        parts.append(pieces.reshape(N_DEV, DEPTH, r * c))
    slab = _pack_rows(jnp.concatenate(parts, axis=-1))
    return slab.reshape(N_DEV, DEPTH * _SLAB_ROWS, SLAB_COLS).astype(BF16)


def _unpack_shards(slab):
    slab = slab.reshape(DEPTH, _SLAB_ROWS * SLAB_COLS)
    out, off = {}, 0
    for (name, shape, axis), size in zip(BIG, _SLAB_SIZES):
        out[name] = slab[:, off:off + size].reshape((DEPTH,) + _shard_shape(shape, axis))
        off += size
    return out


def _pad_vec(vec, rows_multiple=8):
    n = vec.shape[0]
    rows = -(-n // (128 * rows_multiple)) * rows_multiple
    return jnp.pad(vec, (0, rows * 128 - n)).reshape(rows, 128)


def _adamw(w, g, m, v, name):
    shape = w.shape
    if w.ndim >= 2 and shape[-1] >= 128:
        as2 = lambda t: t.reshape(-1, shape[-1])
        d, nm, nv = _adamw_call(as2(w), as2(g), as2(m), as2(v), name)
        return d.reshape(shape), nm.reshape(shape), nv.reshape(shape)
    n = int(np.prod(shape))
    as2 = lambda t: _pad_vec(t.reshape(-1))
    d, nm, nv = _adamw_call(as2(w), as2(g), as2(m), as2(v), name)
    un = lambda t: t.reshape(-1)[:n].reshape(shape)
    return un(d), un(nm), un(nv)


def kernel(x, c, ctx, c_ctx, ada_w, ada_b, w_in, swa_sink, dn_conv_w, dn_a_log, dn_dt_bias, dn_norm_g, ret_log1m_gamma, ret_norm_g, mla_q_norm, mla_w_uq, mla_kv_norm, mla_w_ukv, w_out, ln1_g, ln1_b, w_ff1, w_ff2, ln2_g, ln2_b, loss_target, m_c_ctx, m_ada_w, m_ada_b, m_w_in, m_swa_sink, m_dn_conv_w, m_dn_a_log, m_dn_dt_bias, m_dn_norm_g, m_ret_log1m_gamma, m_ret_norm_g, m_mla_q_norm, m_mla_w_uq, m_mla_kv_norm, m_mla_w_ukv, m_w_out, m_ln1_g, m_ln1_b, m_w_ff1, m_w_ff2, m_ln2_g, m_ln2_b, v_c_ctx, v_ada_w, v_ada_b, v_w_in, v_swa_sink, v_dn_conv_w, v_dn_a_log, v_dn_dt_bias, v_dn_norm_g, v_ret_log1m_gamma, v_ret_norm_g, v_mla_q_norm, v_mla_w_uq, v_mla_kv_norm, v_mla_w_ukv, v_w_out, v_ln1_g, v_ln1_b, v_w_ff1, v_w_ff2, v_ln2_g, v_ln2_b):
    a = dict(zip(ARG_NAMES, (x, c, ctx, c_ctx, ada_w, ada_b, w_in, swa_sink, dn_conv_w, dn_a_log, dn_dt_bias, dn_norm_g, ret_log1m_gamma, ret_norm_g, mla_q_norm, mla_w_uq, mla_kv_norm, mla_w_ukv, w_out, ln1_g, ln1_b, w_ff1, w_ff2, ln2_g, ln2_b, loss_target, m_c_ctx, m_ada_w, m_ada_b, m_w_in, m_swa_sink, m_dn_conv_w, m_dn_a_log, m_dn_dt_bias, m_dn_norm_g, m_ret_log1m_gamma, m_ret_norm_g, m_mla_q_norm, m_mla_w_uq, m_mla_kv_norm, m_mla_w_ukv, m_w_out, m_ln1_g, m_ln1_b, m_w_ff1, m_w_ff2, m_ln2_g, m_ln2_b, v_c_ctx, v_ada_w, v_ada_b, v_w_in, v_swa_sink, v_dn_conv_w, v_dn_a_log, v_dn_dt_bias, v_dn_norm_g, v_ret_log1m_gamma, v_ret_norm_g, v_mla_q_norm, v_mla_w_uq, v_mla_kv_norm, v_mla_w_ukv, v_w_out, v_ln1_g, v_ln1_b, v_w_ff1, v_w_ff2, v_ln2_g, v_ln2_b)))
    me = 4 * lax.axis_index("x") + 2 * lax.axis_index("y") + lax.axis_index("c")
    b_loc = x.shape[0]
    n_ex = N_DEV * b_loc
    conv_k, conv_c = dn_conv_w.shape[1], dn_conv_w.shape[2]
    ada_cols = ada_w.shape[2]

    small_in = jnp.concatenate([c.reshape(-1), dn_conv_w.reshape(-1)])
    small_all = _exchange_call(_pad_vec(small_in), False, "gather_small").reshape(N_DEV, -1)
    c_all = small_all[:, :b_loc * D_MODEL].reshape(n_ex, D_MODEL)
    conv_all = small_all[:, b_loc * D_MODEL:b_loc * D_MODEL + DEPTH * conv_k * conv_c].reshape(N_DEV, DEPTH, conv_k, conv_c)
    conv_full = conv_all.transpose(1, 2, 0, 3).reshape(DEPTH, conv_k, N_DEV * conv_c)
    big = _unpack_full(_exchange_call(_pack_local(a), False, "gather_weights"))
    big['w_in'] = jnp.pad(big['w_in'], ((0, 0), (0, 0), (0, IN_WIDTH_PAD - IN_WIDTH)))

    n_rows = -(-(n_ex + 1) // 16) * 16
    silu_cc = jax.nn.silu(c_ctx)
    a_rows = jnp.concatenate([jax.nn.silu(c_all), silu_cc[None], jnp.zeros((n_rows - n_ex - 1, D_MODEL), F32)], axis=0)
    m_loc = jnp.concatenate([_mm_call(a_rows, ada_w[l], False, "ada_fwd") for l in range(DEPTH)], axis=0)
    m_all = _exchange_call(m_loc, False, "gather_mod").reshape(N_DEV, DEPTH, n_rows, ada_cols)
    mod_full = m_all.transpose(1, 2, 0, 3).reshape(DEPTH, n_rows, N_DEV * ada_cols) + ada_b[:, None, :]
    mod = lax.dynamic_slice_in_dim(mod_full, me * b_loc, b_loc, axis=1)
    cmod = mod_full[:, n_ex]

    p = dict(big)
    p.update(mod=mod, cmod=cmod, dn_conv_w=conv_full)
    for name in SMALL:
        p[name] = a[name]
    loss_loc, (gp, gx) = jax.value_and_grad(local_loss, argnums=(0, 1))(p, x, ctx, loss_target)
    loss = lax.psum(loss_loc, MESH_AXES)

    gp['w_in'] = gp['w_in'][:, :, :IN_WIDTH]
    g_big = _unpack_shards(_sum8_call(_exchange_call(_pack_grads(gp), True, "scatter_grads"), "sum_grads"))

    d_loc = jnp.concatenate([gp['mod'], gp['cmod'][:, None, :]], axis=1).reshape(DEPTH * (b_loc + 1), -1)
    d_loc = jnp.pad(d_loc, ((0, 8 - DEPTH * (b_loc + 1)), (0, 0)))
    d_all = _exchange_call(d_loc, False, "gather_dmod")[:, :DEPTH * (b_loc + 1)].reshape(N_DEV, DEPTH, b_loc + 1, -1)
    d_rows = d_all[:, :, :b_loc].transpose(1, 0, 2, 3).reshape(DEPTH, n_ex, -1)
    d_crow = d_all[0, :, b_loc]
    for d in range(1, N_DEV):
        d_crow = d_crow + d_all[d, :, b_loc]
    dm_full = jnp.concatenate([d_rows, d_crow[:, None, :], jnp.zeros((DEPTH, n_rows - n_ex - 1, d_rows.shape[-1]), F32)], axis=1)
    g_ada_b = jnp.sum(dm_full, axis=1)
    dm_mine = lax.dynamic_slice_in_dim(dm_full, me * ada_cols, ada_cols, axis=2)
    g_ada_w = jnp.stack([_mm_call(a_rows, dm_mine[l], True, "ada_bwd_w") for l in range(DEPTH)])
    crow8 = jnp.concatenate([dm_mine[:, n_ex:n_ex + 1], jnp.zeros((DEPTH, 15, ada_cols), F32)], axis=1)
    dsilu_part = sum(_mm_call(crow8[l], jnp.transpose(ada_w[l]), False, "ada_bwd_c")[0] for l in range(DEPTH))

    small_g = jnp.concatenate([gp[name].reshape(-1) for name in SMALL] + [gp['dn_conv_w'].reshape(-1), dsilu_part])
    small_sum = _sum8_call(_exchange_call(_pad_vec(small_g), False, "gather_small_grads"), "sum_small_grads").reshape(-1)
    g_all, off = {}, 0
    for name in SMALL:
        n = int(np.prod(a[name].shape))
        g_all[name] = small_sum[off:off + n].reshape(a[name].shape)
        off += n
    n = DEPTH * conv_k * N_DEV * conv_c
    g_conv_full = small_sum[off:off + n].reshape(DEPTH, conv_k, N_DEV * conv_c)
    g_all['dn_conv_w'] = lax.dynamic_slice_in_dim(g_conv_full, me * conv_c, conv_c, axis=2)
    off += n
    dsilu = small_sum[off:off + D_MODEL]
    sig = jax.nn.sigmoid(c_ctx)
    g_all['c_ctx'] = dsilu * (sig * (1 + c_ctx * (1 - sig)))
    g_all['ada_w'] = g_ada_w
    g_all['ada_b'] = g_ada_b
    g_all.update(g_big)

    delta, new_m, new_v = {}, {}, {}
    for name in WEIGHTS:
        delta[name], new_m[name], new_v[name] = _adamw(a[name], g_all[name], a['m_' + name], a['v_' + name], "adamw_" + name)
    return (loss, gx, *[g_all[n] for n in WEIGHTS], *[delta[n] for n in WEIGHTS],
            *[new_m[n] for n in WEIGHTS], *[new_v[n] for n in WEIGHTS])
```

```python
import functools
import math

import jax
import jax.numpy as jnp
import numpy as np
from jax import lax
from jax.experimental import pallas as pl
from jax.experimental.pallas import tpu as pltpu

F32 = jnp.float32
BF16 = jnp.bfloat16
N_DEV = 8
MESH_AXES = ("x", "y", "c")

D_MODEL = 1024
DEPTH = 2
GRID_W = 64
SWA_HEADS, SWA_KV_HEADS, SWA_HEAD_DIM, SWA_WINDOW, SWA_BLOCK = 4, 2, 64, 128, 128
DN_HEADS, DN_HEAD_DIM, DN_CHUNK = 4, 64, 64
RET_HEADS, RET_QK_DIM, RET_V_DIM, RET_CHUNK = 4, 32, 64, 64
MLA_HEADS, MLA_Q_RANK, MLA_KV_RANK, MLA_NOPE_DIM, MLA_ROPE_DIM, MLA_V_DIM = 4, 256, 128, 64, 32, 64
D_FF = 4 * D_MODEL
ROPE_BASE = 10000.0
NORM_EPS = 1e-6
LN_EPS = 1e-5
DEEPNORM_ALPHA = (2 * DEPTH) ** 0.25
SWA_Q = SWA_HEADS * SWA_HEAD_DIM
SWA_KV = SWA_KV_HEADS * SWA_HEAD_DIM
DN_W = DN_HEADS * DN_HEAD_DIM
RET_QK = RET_HEADS * RET_QK_DIM
RET_V = RET_HEADS * RET_V_DIM
IN_SPLITS = (SWA_Q, SWA_KV, SWA_KV, 3 * DN_W, DN_W, 4 * DN_HEADS, RET_QK, RET_QK, RET_V, RET_V,
             MLA_Q_RANK, MLA_KV_RANK, MLA_ROPE_DIM)
IN_WIDTH = sum(IN_SPLITS)
IN_WIDTH_PAD = -(-IN_WIDTH // 128) * 128

ADAM_LR, ADAM_B1, ADAM_B2, ADAM_EPS, ADAM_WD, ADAM_STEP = 0.001, 0.9, 0.999, 1e-08, 0.01, 10

WEIGHTS = ['c_ctx', 'ada_w', 'ada_b', 'w_in', 'swa_sink', 'dn_conv_w', 'dn_a_log', 'dn_dt_bias', 'dn_norm_g',
           'ret_log1m_gamma', 'ret_norm_g', 'mla_q_norm', 'mla_w_uq', 'mla_kv_norm', 'mla_w_ukv', 'w_out', 'ln1_g',
           'ln1_b', 'w_ff1', 'w_ff2', 'ln2_g', 'ln2_b']
FWD_INPUTS = ['x', 'c', 'ctx'] + WEIGHTS
ARG_NAMES = FWD_INPUTS + ['loss_target'] + ['m_' + n for n in WEIGHTS] + ['v_' + n for n in WEIGHTS]

BIG = (('w_in', (D_MODEL, IN_WIDTH), 1), ('w_out', (D_MODEL, D_MODEL), 0), ('w_ff1', (D_MODEL, D_FF), 1),
       ('w_ff2', (D_FF, D_MODEL), 0), ('mla_w_uq', (MLA_Q_RANK, MLA_HEADS * (MLA_NOPE_DIM + MLA_ROPE_DIM)), 1),
       ('mla_w_ukv', (MLA_KV_RANK, MLA_HEADS * (MLA_NOPE_DIM + MLA_V_DIM)), 1))
SLAB_COLS = 1024
SMALL = ('swa_sink', 'dn_a_log', 'dn_dt_bias', 'dn_norm_g', 'ret_log1m_gamma', 'ret_norm_g', 'mla_q_norm',
         'mla_kv_norm', 'ln1_g', 'ln1_b', 'ln2_g', 'ln2_b')


def _pcall(body, **kw):
    return pl.pallas_call(body, **kw)


def _pick(n, cands):
    for cand in cands:
        if n % cand == 0:
            return cand
    return n


def _bdot(a, b, dims):
    return lax.dot_general(a.astype(BF16), b.astype(BF16), dims, preferred_element_type=F32)


def _lane0(t):
    return jnp.where(lax.broadcasted_iota(jnp.int32, t.shape, t.ndim - 1) == 0, t, 0.0)


NN = (((1,), (0,)), ((), ()))
NT = (((1,), (1,)), ((), ()))
TN = (((0,), (0,)), ((), ()))
BNN = (((2,), (1,)), ((0,), (0,)))
BNT = (((2,), (2,)), ((0,), (0,)))


MM_ROW_TILE_MAX = 1088
MM_COL_TILE_MAX = 1408
MM_TOKEN_TILE_MAX = 544
VMEM_LIMIT_MAX = 60 * 1024 * 1024


def _tile(n, cap, align):
    best = None
    for t in range(align, min(n, cap) + 1, align):
        if n % t == 0:
            best = t
    return best or n


def _relu2(t):
    return jnp.square(jnp.maximum(t, 0.0))


def _mm_call(a, b, trans_a, name, act_a=False, epi=None):
    if trans_a:
        kdim, m = a.shape
        tk = _tile(kdim, MM_TOKEN_TILE_MAX, 8)
        tm = _tile(m, 1024, 128)
    else:
        m, kdim = a.shape
        tk = _tile(kdim, MM_COL_TILE_MAX, 128)
        tm = _tile(m, MM_ROW_TILE_MAX, 8)
    n = b.shape[1]
    assert b.shape[0] == kdim
    tn = _tile(n, MM_COL_TILE_MAX, 128)
    nk = kdim // tk

    def body(*refs):
        a_ref, b_ref = refs[0], refs[1]
        e_ref = refs[2] if epi is not None else None
        o_ref = refs[-1]
        k = pl.program_id(2)
        av = a_ref[...]
        if act_a:
            av = _relu2(av)
        part = _bdot(av, b_ref[...], TN if trans_a else NN)

        def finish(t):
            return t * (2.0 * jnp.maximum(e_ref[...], 0.0)) if epi is not None else t

        if nk == 1:
            o_ref[...] = finish(part)
        else:
            @pl.when(k == 0)
            def _():
                o_ref[...] = part

            @pl.when((k > 0) & (k < nk - 1))
            def _():
                o_ref[...] += part

            @pl.when(k == nk - 1)
            def _():
                o_ref[...] = finish(o_ref[...] + part)

    if trans_a:
        a_spec = pl.BlockSpec((tk, tm), lambda i, j, k: (k, i))
    else:
        a_spec = pl.BlockSpec((tm, tk), lambda i, j, k: (i, k))
    o_spec = pl.BlockSpec((tm, tn), lambda i, j, k: (i, j))
    in_specs = [a_spec, pl.BlockSpec((tk, tn), lambda i, j, k: (k, j))] + ([o_spec] if epi is not None else [])
    tiles = tm * tk * a.dtype.itemsize + tk * tn * b.dtype.itemsize + tm * tn * 4 * (2 if epi is not None else 1)
    temps = tm * tk * (2 + (4 if act_a else 0)) + tk * tn * 2 + 2 * tm * tn * 4
    return _pcall(
        body, name=name, grid=(m // tm, n // tn, nk), in_specs=in_specs, out_specs=o_spec,
        out_shape=jax.ShapeDtypeStruct((m, n), F32),
        compiler_params=pltpu.CompilerParams(dimension_semantics=("parallel", "parallel", "arbitrary"),
                                             vmem_limit_bytes=min(2 * tiles + temps + (4 << 20), VMEM_LIMIT_MAX)),
    )(*((a, b) + ((epi,) if epi is not None else ())))


@jax.custom_vjp
def matmul(a, b):
    return _mm_call(a, b.astype(BF16), False, "mm_fwd")


def _matmul_fwd(a, b):
    bb = b.astype(BF16)
    return _mm_call(a, bb, False, "mm_fwd"), (a, bb)


def _matmul_bwd(res, g):
    a, bb = res
    da = _mm_call(g, jnp.transpose(bb), False, "mm_bwd_da")
    db = _mm_call(a, g, True, "mm_bwd_db")
    return da, db


matmul.defvjp(_matmul_fwd, _matmul_bwd)


@jax.custom_vjp
def matmul_relu2(a, b):
    return _mm_call(a, b.astype(BF16), False, "mm_act_fwd", act_a=True)


def _matmul_relu2_fwd(a, b):
    bb = b.astype(BF16)
    return _mm_call(a, bb, False, "mm_act_fwd", act_a=True), (a, bb)


def _matmul_relu2_bwd(res, g):
    a, bb = res
    da = _mm_call(g, jnp.transpose(bb), False, "mm_act_bwd_da", epi=a)
    db = _mm_call(a, g, True, "mm_act_bwd_db", act_a=True)
    return da, db


matmul_relu2.defvjp(_matmul_relu2_fwd, _matmul_relu2_bwd)


LN_ROW_TILE = 256


def _ln_group_map(group_rows, n_groups):
    per = group_rows // LN_ROW_TILE
    return lambda i: (jnp.minimum(i // per, n_groups - 1), 0, 0)


def _ln_stats(x, y, gate):
    pre = DEEPNORM_ALPHA * x + gate * y
    mu = jnp.mean(pre, axis=-1, keepdims=True)
    cen = pre - mu
    rstd = lax.rsqrt(jnp.mean(jnp.square(cen), axis=-1, keepdims=True) + LN_EPS)
    return cen * rstd, rstd


def _ln_mod_fwd_call(x, y, gate, gamma, beta, sc, sh, group_rows):
    r, d = x.shape
    ng = gate.shape[0]
    gmap = _ln_group_map(group_rows, ng)

    def body(x_ref, y_ref, gate_ref, gamma_ref, beta_ref, sc_ref, sh_ref, xn_ref, h_ref):
        xh, _ = _ln_stats(x_ref[...], y_ref[...], gate_ref[0])
        xn = xh * gamma_ref[...] + beta_ref[...]
        xn_ref[...] = xn
        h_ref[...] = xn * (1.0 + sc_ref[0]) + sh_ref[0]

    row = pl.BlockSpec((LN_ROW_TILE, d), lambda i: (i, 0))
    grp = pl.BlockSpec((1, 1, d), gmap)
    vec = pl.BlockSpec((1, d), lambda i: (0, 0))
    return _pcall(
        body, name="ln_mod_fwd", grid=(r // LN_ROW_TILE,),
        in_specs=[row, row, grp, vec, vec, grp, grp], out_specs=[row, row],
        out_shape=[jax.ShapeDtypeStruct((r, d), F32)] * 2,
        compiler_params=pltpu.CompilerParams(dimension_semantics=("parallel",)),
    )(x, y, gate, gamma, beta, sc, sh)


def _ln_mod_bwd_call(x, y, gate, gamma, beta, sc, dxn, dh, group_rows):
    r, d = x.shape
    ng = gate.shape[0]
    gmap = _ln_group_map(group_rows, ng)
    per = group_rows // LN_ROW_TILE

    def body(x_ref, y_ref, gate_ref, gamma_ref, beta_ref, sc_ref, dxn_ref, dh_ref,
             dx_ref, dy_ref, dgate_ref, dgamma_ref, dbeta_ref, dsc_ref, dsh_ref):
        i = pl.program_id(0)
        yv, gate_v, gamma_v = y_ref[...], gate_ref[0], gamma_ref[...]
        xh, rstd = _ln_stats(x_ref[...], yv, gate_v)
        dhv = dh_ref[...]
        dtot = dxn_ref[...] + dhv * (1.0 + sc_ref[0])
        dxh = dtot * gamma_v
        dpre = rstd * (dxh - jnp.mean(dxh, axis=-1, keepdims=True) - xh * jnp.mean(dxh * xh, axis=-1, keepdims=True))
        dx_ref[...] = DEEPNORM_ALPHA * dpre
        dy_ref[...] = gate_v * dpre
        col = lambda t: jnp.sum(t, axis=0, keepdims=True)

        @pl.when(i == 0)
        def _():
            dgamma_ref[...] = jnp.zeros_like(dgamma_ref)
            dbeta_ref[...] = jnp.zeros_like(dbeta_ref)

        first_of_group = (i % per == 0) | (i == (ng - 1) * per)

        @pl.when(first_of_group & (i <= (ng - 1) * per))
        def _():
            dgate_ref[...] = jnp.zeros_like(dgate_ref)
            dsc_ref[...] = jnp.zeros_like(dsc_ref)
            dsh_ref[...] = jnp.zeros_like(dsh_ref)

        dgamma_ref[...] += col(dtot * xh)
        dbeta_ref[...] += col(dtot)
        dgate_ref[0] += col(dpre * yv)
        dsc_ref[0] += col(dhv * (xh * gamma_v + beta_ref[...]))
        dsh_ref[0] += col(dhv)

    row = pl.BlockSpec((LN_ROW_TILE, d), lambda i: (i, 0))
    grp = pl.BlockSpec((1, 1, d), gmap)
    vec = pl.BlockSpec((1, d), lambda i: (0, 0))
    big = jax.ShapeDtypeStruct((r, d), F32)
    gs = jax.ShapeDtypeStruct((ng, 1, d), F32)
    vs = jax.ShapeDtypeStruct((1, d), F32)
    return _pcall(
        body, name="ln_mod_bwd", grid=(r // LN_ROW_TILE,),
        in_specs=[row, row, grp, vec, vec, grp, row, row],
        out_specs=[row, row, grp, vec, vec, grp, grp],
        out_shape=[big, big, gs, vs, vs, gs, gs],
        compiler_params=pltpu.CompilerParams(dimension_semantics=("arbitrary",)),
    )(x, y, gate, gamma, beta, sc, dxn, dh)


@functools.partial(jax.custom_vjp, nondiff_argnums=(7,))
def ln_mod(x, y, gate, gamma, beta, sc, sh, group_rows):
    return tuple(_ln_mod_fwd_call(x, y, gate, gamma, beta, sc, sh, group_rows))


def _ln_mod_fwd(x, y, gate, gamma, beta, sc, sh, group_rows):
    xn, h = _ln_mod_fwd_call(x, y, gate, gamma, beta, sc, sh, group_rows)
    return (xn, h), (x, y, gate, gamma, beta, sc)


def _ln_mod_bwd(group_rows, res, cts):
    x, y, gate, gamma, beta, sc = res
    dxn, dh = cts
    return tuple(_ln_mod_bwd_call(x, y, gate, gamma, beta, sc, dxn, dh, group_rows))


ln_mod.defvjp(_ln_mod_fwd, _ln_mod_bwd)


def _attn_probs(q, k, sink, scale, has_sink):
    s = _bdot(q, k, NT) * scale
    m = jnp.max(s, axis=-1, keepdims=True)
    if has_sink:
        m = jnp.maximum(m, sink)
    p = jnp.exp(s - m)
    den = jnp.sum(p, axis=-1, keepdims=True)
    p_sink = None
    if has_sink:
        p_sink = jnp.exp(sink - m)
        den = den + p_sink
    inv = 1.0 / den
    if has_sink:
        p_sink = p_sink * inv
    return p * inv, p_sink


def _attn_full_fwd_call(q, k, v, sink, scale, has_sink):
    g, sq, dq = q.shape
    nk, dv = k.shape[1], v.shape[2]
    bq = _pick(sq, (256, 128))

    def body(q_ref, k_ref, v_ref, sink_ref, o_ref):
        p, _ = _attn_probs(q_ref[0], k_ref[0], sink_ref[0, :, 0:1], scale, has_sink)
        o_ref[0] = _bdot(p, v_ref[0], NN)

    return _pcall(
        body, name="attn_full_fwd", grid=(g, sq // bq),
        in_specs=[pl.BlockSpec((1, bq, dq), lambda b, i: (b, i, 0)), pl.BlockSpec((1, nk, dq), lambda b, i: (b, 0, 0)),
                  pl.BlockSpec((1, nk, dv), lambda b, i: (b, 0, 0)), pl.BlockSpec((1, 1, 128), lambda b, i: (b, 0, 0))],
        out_specs=pl.BlockSpec((1, bq, dv), lambda b, i: (b, i, 0)),
        out_shape=jax.ShapeDtypeStruct((g, sq, dv), F32),
        compiler_params=pltpu.CompilerParams(dimension_semantics=("parallel", "arbitrary")),
    )(q, k, v, sink)


def _attn_full_bwd_call(q, k, v, sink, o, do, scale, has_sink):
    g, sq, dq = q.shape
    nk, dv = k.shape[1], v.shape[2]
    bq = _pick(sq, (256, 128))

    def body(q_ref, k_ref, v_ref, sink_ref, o_ref, do_ref, dq_ref, dk_ref, dv_ref, dsink_ref):
        i = pl.program_id(1)
        qv, kv, vv, dov = q_ref[0], k_ref[0], v_ref[0], do_ref[0]
        p, p_sink = _attn_probs(qv, kv, sink_ref[0, :, 0:1], scale, has_sink)
        delta = jnp.sum(dov * o_ref[0], axis=-1, keepdims=True)
        dv_part = _bdot(p, dov, TN)
        dp = _bdot(dov, vv, NT)
        ds = p * (dp - delta) * scale
        dq_ref[0] = _bdot(ds, kv, NN)
        dk_part = _bdot(ds, qv, TN)
        if has_sink:
            dsk = jnp.broadcast_to(-jnp.sum(p_sink * delta, axis=0, keepdims=True), (1, 128))
        else:
            dsk = jnp.zeros((1, 128), F32)

        @pl.when(i == 0)
        def _():
            dk_ref[0] = dk_part
            dv_ref[0] = dv_part
            dsink_ref[0] = dsk

        @pl.when(i > 0)
        def _():
            dk_ref[0] += dk_part
            dv_ref[0] += dv_part
            dsink_ref[0] += dsk

    qspec = pl.BlockSpec((1, bq, dq), lambda b, i: (b, i, 0))
    kspec = pl.BlockSpec((1, nk, dq), lambda b, i: (b, 0, 0))
    vspec = pl.BlockSpec((1, nk, dv), lambda b, i: (b, 0, 0))
    ospec = pl.BlockSpec((1, bq, dv), lambda b, i: (b, i, 0))
    sspec = pl.BlockSpec((1, 1, 128), lambda b, i: (b, 0, 0))
    return _pcall(
        body, name="attn_full_bwd", grid=(g, sq // bq),
        in_specs=[qspec, kspec, vspec, sspec, ospec, ospec],
        out_specs=[qspec, kspec, vspec, sspec],
        out_shape=[jax.ShapeDtypeStruct(q.shape, F32), jax.ShapeDtypeStruct(k.shape, F32),
                   jax.ShapeDtypeStruct(v.shape, F32), jax.ShapeDtypeStruct(sink.shape, F32)],
        compiler_params=pltpu.CompilerParams(dimension_semantics=("parallel", "arbitrary")),
    )(q, k, v, sink, o, do)


@functools.partial(jax.custom_vjp, nondiff_argnums=(4, 5))
def attn_full(q, k, v, sink, scale, has_sink):
    return _attn_full_fwd_call(q, k, v, sink, scale, has_sink)


def _attn_full_fwd(q, k, v, sink, scale, has_sink):
    o = _attn_full_fwd_call(q, k, v, sink, scale, has_sink)
    return o, (q, k, v, sink, o)


def _attn_full_bwd(scale, has_sink, res, do):
    q, k, v, sink, o = res
    dq, dk, dv, dsink = _attn_full_bwd_call(q, k, v, sink, o, do, scale, has_sink)
    return dq, dk, dv, _lane0(dsink)


attn_full.defvjp(_attn_full_fwd, _attn_full_bwd)


def _swa_probs(q, kw, kc, sink, i, s_len, scale):
    w = SWA_BLOCK
    s_loc = _bdot(q, kw, NT) * scale
    qpos = i * w + lax.broadcasted_iota(jnp.int32, (w, 3 * w), 0)
    kpos = (i - 1) * w + lax.broadcasted_iota(jnp.int32, (w, 3 * w), 1)
    valid = (jnp.abs(kpos - qpos) <= SWA_WINDOW) & (kpos >= 0) & (kpos < s_len)
    s_loc = jnp.where(valid, s_loc, -jnp.inf)
    s_ctx = _bdot(q, kc, NT) * scale
    m = jnp.maximum(jnp.maximum(jnp.max(s_loc, axis=-1, keepdims=True), jnp.max(s_ctx, axis=-1, keepdims=True)), sink)
    p_loc = jnp.exp(s_loc - m)
    p_ctx = jnp.exp(s_ctx - m)
    p_sink = jnp.exp(sink - m)
    inv = 1.0 / (jnp.sum(p_loc, axis=-1, keepdims=True) + jnp.sum(p_ctx, axis=-1, keepdims=True) + p_sink)
    return p_loc * inv, p_ctx * inv, p_sink * inv


def _swa_fwd_call(q, kp, vp, kc, vc, sink, scale):
    g, s_len, d = q.shape
    l_ctx = kc.shape[1]
    w = SWA_BLOCK
    grp = SWA_HEADS // SWA_KV_HEADS

    def body(q_ref, kp_ref, vp_ref, kc_ref, vc_ref, sink_ref, o_ref):
        i = pl.program_id(1)
        start = pl.multiple_of(i * w, w)
        kw = kp_ref[0, pl.ds(start, 3 * w), :]
        vw = vp_ref[0, pl.ds(start, 3 * w), :]
        p_loc, p_ctx, _ = _swa_probs(q_ref[0], kw, kc_ref[0], sink_ref[0, :, 0:1], i, s_len, scale)
        o_ref[0] = _bdot(p_loc, vw, NN) + _bdot(p_ctx, vc_ref[0], NN)

    return _pcall(
        body, name="swa_fwd", grid=(g, s_len // w),
        in_specs=[pl.BlockSpec((1, w, d), lambda b, i: (b, i, 0)),
                  pl.BlockSpec((1, s_len + 2 * w, d), lambda b, i: (b // grp, 0, 0)),
                  pl.BlockSpec((1, s_len + 2 * w, d), lambda b, i: (b // grp, 0, 0)),
                  pl.BlockSpec((1, l_ctx, d), lambda b, i: (b // grp, 0, 0)),
                  pl.BlockSpec((1, l_ctx, d), lambda b, i: (b // grp, 0, 0)),
                  pl.BlockSpec((1, 1, 128), lambda b, i: (b, 0, 0))],
        out_specs=pl.BlockSpec((1, w, d), lambda b, i: (b, i, 0)),
        out_shape=jax.ShapeDtypeStruct(q.shape, F32),
        compiler_params=pltpu.CompilerParams(dimension_semantics=("parallel", "arbitrary")),
    )(q, kp, vp, kc, vc, sink)


def _swa_bwd_call(q, kp, vp, kc, vc, sink, o, do, scale):
    g, s_len, d = q.shape
    l_ctx = kc.shape[1]
    w = SWA_BLOCK
    grp = SWA_HEADS // SWA_KV_HEADS
    sp = s_len + 2 * w

    def body(q_ref, kp_ref, vp_ref, kc_ref, vc_ref, sink_ref, o_ref, do_ref,
             dq_ref, dkp_ref, dvp_ref, dkc_ref, dvc_ref, dsink_ref):
        i = pl.program_id(1)
        start = pl.multiple_of(i * w, w)
        qv, dov = q_ref[0], do_ref[0]
        kw = kp_ref[0, pl.ds(start, 3 * w), :]
        vw = vp_ref[0, pl.ds(start, 3 * w), :]
        kcv, vcv = kc_ref[0], vc_ref[0]
        p_loc, p_ctx, p_sink = _swa_probs(qv, kw, kcv, sink_ref[0, :, 0:1], i, s_len, scale)
        delta = jnp.sum(dov * o_ref[0], axis=-1, keepdims=True)
        ds_loc = p_loc * (_bdot(dov, vw, NT) - delta) * scale
        ds_ctx = p_ctx * (_bdot(dov, vcv, NT) - delta) * scale
        dq_ref[0] = _bdot(ds_loc, kw, NN) + _bdot(ds_ctx, kcv, NN)
        dsk = jnp.broadcast_to(-jnp.sum(p_sink * delta, axis=0, keepdims=True), (1, 128))

        @pl.when(i == 0)
        def _():
            dkp_ref[...] = jnp.zeros_like(dkp_ref)
            dvp_ref[...] = jnp.zeros_like(dvp_ref)
            dkc_ref[...] = jnp.zeros_like(dkc_ref)
            dvc_ref[...] = jnp.zeros_like(dvc_ref)
            dsink_ref[...] = jnp.zeros_like(dsink_ref)

        dkp_ref[0, pl.ds(start, 3 * w), :] += _bdot(ds_loc, qv, TN)
        dvp_ref[0, pl.ds(start, 3 * w), :] += _bdot(p_loc, dov, TN)
        dkc_ref[0] += _bdot(ds_ctx, qv, TN)
        dvc_ref[0] += _bdot(p_ctx, dov, TN)
        dsink_ref[0] += dsk

    qspec = pl.BlockSpec((1, w, d), lambda b, i: (b, i, 0))
    kin = pl.BlockSpec((1, sp, d), lambda b, i: (b // grp, 0, 0))
    cin = pl.BlockSpec((1, l_ctx, d), lambda b, i: (b // grp, 0, 0))
    kout = pl.BlockSpec((1, sp, d), lambda b, i: (b, 0, 0))
    cout = pl.BlockSpec((1, l_ctx, d), lambda b, i: (b, 0, 0))
    sspec = pl.BlockSpec((1, 1, 128), lambda b, i: (b, 0, 0))
    return _pcall(
        body, name="swa_bwd", grid=(g, s_len // w),
        in_specs=[qspec, kin, kin, cin, cin, sspec, qspec, qspec],
        out_specs=[qspec, kout, kout, cout, cout, sspec],
        out_shape=[jax.ShapeDtypeStruct(q.shape, F32), jax.ShapeDtypeStruct((g, sp, d), F32),
                   jax.ShapeDtypeStruct((g, sp, d), F32), jax.ShapeDtypeStruct((g, l_ctx, d), F32),
                   jax.ShapeDtypeStruct((g, l_ctx, d), F32), jax.ShapeDtypeStruct(sink.shape, F32)],
        compiler_params=pltpu.CompilerParams(dimension_semantics=("parallel", "arbitrary")),
    )(q, kp, vp, kc, vc, sink, o, do)


@functools.partial(jax.custom_vjp, nondiff_argnums=(6,))
def swa_attn(q, kp, vp, kc, vc, sink, scale):
    return _swa_fwd_call(q, kp, vp, kc, vc, sink, scale)


def _swa_attn_fwd(q, kp, vp, kc, vc, sink, scale):
    o = _swa_fwd_call(q, kp, vp, kc, vc, sink, scale)
    return o, (q, kp, vp, kc, vc, sink, o)


def _swa_attn_bwd(scale, res, do):
    q, kp, vp, kc, vc, sink, o = res
    grp = SWA_HEADS // SWA_KV_HEADS
    dq, dkp, dvp, dkc, dvc, dsink = _swa_bwd_call(q, kp, vp, kc, vc, sink, o, do, scale)
    pair = lambda t: t.reshape(t.shape[0] // grp, grp, *t.shape[1:]).sum(axis=1)
    return dq, pair(dkp), pair(dvp), pair(dkc), pair(dvc), _lane0(dsink)


swa_attn.defvjp(_swa_attn_fwd, _swa_attn_bwd)


def _f32dot(a, b, dims):
    return lax.dot_general(a, b, dims, precision=lax.Precision.HIGHEST, preferred_element_type=F32)


def _unit_lower_inverse(coef, descending):
    c = coef.shape[-1]
    row = lax.broadcasted_iota(jnp.int32, (1, c, c), 1)
    col = lax.broadcasted_iota(jnp.int32, (1, c, c), 2)
    t = jnp.broadcast_to((row == col).astype(F32), coef.shape)
    order = range(c - 2, -1, -1) if descending else range(1, c)
    for i in order:
        new_row = -jnp.sum(coef[:, :, i:i + 1] * t, axis=1, keepdims=True)
        t = t + jnp.where(row == i, new_row, 0.0)
    return t


def _dn_masks(c):
    row = lax.broadcasted_iota(jnp.int32, (1, c, c), 1)
    col = lax.broadcasted_iota(jnp.int32, (1, c, c), 2)
    return row, col


def _dn_fwd_call(q, k, k_t, v, gc, bb, gr):
    g, n, c, _ = q.shape

    def body(q_ref, k_ref, kt_ref, v_ref, gc_ref, bb_ref, gr_ref, o_ref, vn_ref, sall_ref, w_ref, u_ref, s_scr):
        i = pl.program_id(0)

        @pl.when(i == 0)
        def _():
            s_scr[...] = jnp.zeros_like(s_scr)

        qv, kv, ktv, vv, gcv, bv, grv = (q_ref[:, 0], k_ref[:, 0], kt_ref[:, 0], v_ref[:, 0], gc_ref[:, 0],
                                          bb_ref[:, 0], gr_ref[:, 0])
        row, col = _dn_masks(c)
        e = jnp.exp(gcv)
        kb = kv * bv
        decay = jnp.exp(jnp.where(row >= col, gcv - grv, -jnp.inf))
        decay_ts = jnp.exp(jnp.where(row < col, grv - gcv, -jnp.inf))
        t = _unit_lower_inverse(_bdot(kv, kb, BNT) * decay_ts, False)
        w = _f32dot(t, kb * e, BNN)
        u = _f32dot(t, vv * bv, BNN)
        glast = grv[:, :, c - 1:c]
        s = s_scr[...]
        sall_ref[:, 0] = s
        vnew = u - _bdot(w, s, BNN)
        o_ref[:, 0] = _bdot(qv * e, s, BNN) + _bdot(_bdot(qv, kv, BNT) * decay, vnew, BNN)
        vn_ref[:, 0] = vnew
        w_ref[:, 0] = w
        u_ref[:, 0] = u
        s_scr[...] = s * jnp.exp(glast) + _bdot(ktv * jnp.exp(glast - grv), vnew, BNN)

    blk = pl.BlockSpec((g, 1, c, c), lambda i: (0, i, 0, 0))
    rblk = pl.BlockSpec((g, 1, 1, c), lambda i: (0, i, 0, 0))
    big = jax.ShapeDtypeStruct((g, n, c, c), F32)
    return _pcall(
        body, name="dn_fwd", grid=(n,),
        in_specs=[blk, blk, blk, blk, blk, blk, rblk],
        out_specs=[blk] * 5, out_shape=[big] * 5,
        scratch_shapes=[pltpu.VMEM((g, c, c), F32)],
        compiler_params=pltpu.CompilerParams(dimension_semantics=("arbitrary",)),
    )(q, k, k_t, v, gc, bb, gr)


def _dn_bwd_call(q, k, q_t, k_t, v, gc, bb, gr, br, sall, vn, w, u, do):
    g, n, c, _ = q.shape

    def body(q_ref, k_ref, qt_ref, kt_ref, v_ref, gc_ref, bb_ref, gr_ref, br_ref, sall_ref, vn_ref, w_ref, u_ref, do_ref,
             dq_ref, dk_ref, dv_ref, dgc_ref, dbb_ref, dgr_ref, ds_scr):
        i = pl.program_id(0)

        @pl.when(i == 0)
        def _():
            ds_scr[...] = jnp.zeros_like(ds_scr)

        qv, kv, qtv, ktv, vv, gcv, bv, grv, brv = (q_ref[:, 0], k_ref[:, 0], qt_ref[:, 0], kt_ref[:, 0], v_ref[:, 0],
                                                    gc_ref[:, 0], bb_ref[:, 0], gr_ref[:, 0], br_ref[:, 0])
        s, vnew, w, u, dov = sall_ref[:, 0], vn_ref[:, 0], w_ref[:, 0], u_ref[:, 0], do_ref[:, 0]
        dsn = ds_scr[...]
        row, col = _dn_masks(c)
        e = jnp.exp(gcv)
        er = jnp.exp(grv)
        kb = kv * bv
        decay = jnp.exp(jnp.where(row >= col, gcv - grv, -jnp.inf))
        decay_s = jnp.where(row > col, decay, 0.0)
        decay_t = jnp.exp(jnp.where(row <= col, grv - gcv, -jnp.inf))
        decay_ts = jnp.where(row < col, decay_t, 0.0)
        kk = _bdot(kb, kv, BNT)
        tt = _unit_lower_inverse(kk * decay_s, True)
        glast = grv[:, :, c - 1:c]
        eg = jnp.exp(glast)
        x = jnp.exp(glast - gcv)
        kt = kv * x
        qk_raw = _bdot(qv, kv, BNT)
        w_t = _f32dot(ktv * (brv * er), tt, BNN)
        dvn = _bdot(_bdot(kv, qv, BNT) * decay_t, dov, BNN) + _bdot(kt, dsn, BNN)
        dqk = _bdot(dov, vnew, BNT)
        dqk_t = _bdot(vnew, dov, BNT)
        dqd = _bdot(dov, s, BNT)
        dkt = _bdot(vnew, dsn, BNT)
        deg = jnp.sum(jnp.sum(dsn * s, axis=2, keepdims=True), axis=1, keepdims=True)
        dw = -_bdot(dvn, s, BNT)
        ds_scr[...] = dsn * eg + _bdot(qtv * er, dov, BNN) - _bdot(w_t, dvn, BNN)
        dwp = _f32dot(tt, dw, BNN)
        dup = _f32dot(tt, dvn, BNN)
        d_a = -(_bdot(dwp, w, BNT) + _bdot(dup, u, BNT))
        d_at = -(_bdot(w, dwp, BNT) + _bdot(u, dup, BNT))
        dkb = _bdot(d_a * decay_s, kv, BNN) + dwp * e
        dkx = dkt * kv * x
        dq_ref[:, 0] = dqd * e + _bdot(dqk * decay, kv, BNN)
        dk_ref[:, 0] = _bdot(d_at * decay_ts, kb, BNN) + dkb * bv + dkt * x + _bdot(dqk_t * decay_t, qv, BNN)
        dv_ref[:, 0] = dup * bv
        dbb_ref[:, 0] = dkb * kv + dup * vv
        ddiff = dqk * qk_raw * decay + d_a * kk * decay_s
        dgc_ref[:, 0] = ddiff + (dwp * kb + dqd * qv) * e - dkx
        dglast = jnp.sum(jnp.sum(dkx, axis=2, keepdims=True), axis=1, keepdims=True) + deg * eg
        lane = lax.broadcasted_iota(jnp.int32, (1, 1, c), 2)
        dgr_ref[:, 0] = jnp.where(lane == c - 1, dglast, 0.0) - jnp.sum(ddiff, axis=1, keepdims=True)

    blk = pl.BlockSpec((g, 1, c, c), lambda i: (0, n - 1 - i, 0, 0))
    rblk = pl.BlockSpec((g, 1, 1, c), lambda i: (0, n - 1 - i, 0, 0))
    big = jax.ShapeDtypeStruct((g, n, c, c), F32)
    return _pcall(
        body, name="dn_bwd", grid=(n,),
        in_specs=[blk] * 7 + [rblk, rblk] + [blk] * 5,
        out_specs=[blk] * 5 + [rblk],
        out_shape=[big] * 5 + [jax.ShapeDtypeStruct((g, n, 1, c), F32)],
        scratch_shapes=[pltpu.VMEM((g, c, c), F32)],
        compiler_params=pltpu.CompilerParams(dimension_semantics=("arbitrary",)),
    )(q, k, q_t, k_t, v, gc, bb, gr, br, sall, vn, w, u, do)


_t = lambda a: jnp.swapaxes(a, -1, -2)


def _dn_forms(gcum, beta, d):
    lanes = lambda t: jnp.broadcast_to(t[..., None], t.shape + (d,))
    return lanes(gcum), lanes(beta), gcum[:, :, None, :], beta[:, :, None, :]


@jax.custom_vjp
def dn_chunked(q, k, v, gcum, beta):
    gc, bb, gr, _ = _dn_forms(gcum, beta, q.shape[-1])
    return _dn_fwd_call(q, k, _t(k), v, gc, bb, gr)[0]


def _dn_chunked_fwd(q, k, v, gcum, beta):
    gc, bb, gr, _ = _dn_forms(gcum, beta, q.shape[-1])
    o, vn, sall, w, u = _dn_fwd_call(q, k, _t(k), v, gc, bb, gr)
    return o, (q, k, v, gcum, beta, vn, sall, w, u)


def _dn_chunked_bwd(res, do):
    q, k, v, gcum, beta, vn, sall, w, u = res
    gc, bb, gr, br = _dn_forms(gcum, beta, q.shape[-1])
    dq, dk, dv, dgc, dbb, dgr = _dn_bwd_call(q, k, _t(q), _t(k), v, gc, bb, gr, br, sall, vn, w, u, do)
    return dq, dk, dv, jnp.sum(dgc, axis=-1) + dgr[:, :, 0, :], jnp.sum(dbb, axis=-1)


dn_chunked.defvjp(_dn_chunked_fwd, _dn_chunked_bwd)


def _ret_fwd_call(q, k, k_t, v, dmat, xi_b, zeta_r, gm):
    g, n, c, dk = q.shape
    dv = v.shape[-1]

    def body(q_ref, k_ref, kt_ref, v_ref, d_ref, xib_ref, zr_ref, gm_ref, o_ref, starts_ref, s_scr):
        i = pl.program_id(0)

        @pl.when(i == 0)
        def _():
            s_scr[...] = jnp.zeros_like(s_scr)

        qv, vv = q_ref[:, 0], v_ref[:, 0]
        s = s_scr[...]
        starts_ref[:, 0] = s
        o_ref[:, 0] = _bdot(_bdot(qv, k_ref[:, 0], BNT) * d_ref[...], vv, BNN) + _bdot(qv * xib_ref[...], s, BNN)
        s_scr[...] = s * gm_ref[...] + _bdot(kt_ref[:, 0] * zr_ref[...], vv, BNN)

    tok = lambda d: pl.BlockSpec((g, 1, c, d), lambda i: (0, i, 0, 0))
    const = lambda a, b: pl.BlockSpec((g, a, b), lambda i: (0, 0, 0))
    return _pcall(
        body, name="ret_fwd", grid=(n,),
        in_specs=[tok(dk), tok(dk), pl.BlockSpec((g, 1, dk, c), lambda i: (0, i, 0, 0)), tok(dv),
                  const(c, c), const(c, dk), const(1, c), const(dk, dv)],
        out_specs=[tok(dv), pl.BlockSpec((g, 1, dk, dv), lambda i: (0, i, 0, 0))],
        out_shape=[jax.ShapeDtypeStruct((g, n, c, dv), F32), jax.ShapeDtypeStruct((g, n, dk, dv), F32)],
        scratch_shapes=[pltpu.VMEM((g, dk, dv), F32)],
        compiler_params=pltpu.CompilerParams(dimension_semantics=("arbitrary",)),
    )(q, k, k_t, v, dmat, xi_b, zeta_r, gm)


def _ret_bwd_call(q, k, q_t, k_t, v, dmat, dmat_t, xi_b, xi_r, zeta_b, gm, starts, do):
    g, n, c, dk = q.shape
    dv = v.shape[-1]

    def body(q_ref, k_ref, qt_ref, kt_ref, v_ref, d_ref, dt_ref, xib_ref, xr_ref, zb_ref, gm_ref, starts_ref, do_ref,
             dq_ref, dk_ref, dv_ref, dd_ref, dxib_ref, dzb_ref, dgm_ref, ds_scr):
        i = pl.program_id(0)

        @pl.when(i == 0)
        def _():
            ds_scr[...] = jnp.zeros_like(ds_scr)
            dd_ref[...] = jnp.zeros_like(dd_ref)
            dxib_ref[...] = jnp.zeros_like(dxib_ref)
            dzb_ref[...] = jnp.zeros_like(dzb_ref)
            dgm_ref[...] = jnp.zeros_like(dgm_ref)

        qv, kv, vv, dov = q_ref[:, 0], k_ref[:, 0], v_ref[:, 0], do_ref[:, 0]
        s, dsn = starts_ref[:, 0], ds_scr[...]
        dm, dmt, zb = d_ref[...], dt_ref[...], zb_ref[...]
        qk_raw = _bdot(qv, kv, BNT)
        dqkd = _bdot(dov, vv, BNT)
        do_s = _bdot(dov, s, BNT)
        dkz = _bdot(vv, dsn, BNT)
        dq_ref[:, 0] = _bdot(dqkd * dm, kv, BNN) + do_s * xib_ref[...]
        dk_ref[:, 0] = _bdot(_bdot(vv, dov, BNT) * dmt, qv, BNN) + dkz * zb
        dv_ref[:, 0] = _bdot(_bdot(kv, qv, BNT) * dmt, dov, BNN) + _bdot(kv * zb, dsn, BNN)
        dd_ref[...] += dqkd * qk_raw
        dxib_ref[...] += do_s * qv
        dzb_ref[...] += dkz * kv
        dgm_ref[...] += dsn * s
        ds_scr[...] = dsn * gm_ref[...] + _bdot(qt_ref[:, 0] * xr_ref[...], dov, BNN)

    tok = lambda d: pl.BlockSpec((g, 1, c, d), lambda i: (0, n - 1 - i, 0, 0))
    tok_t = pl.BlockSpec((g, 1, dk, c), lambda i: (0, n - 1 - i, 0, 0))
    const = lambda a, b: pl.BlockSpec((g, a, b), lambda i: (0, 0, 0))
    sds = lambda *s: jax.ShapeDtypeStruct(s, F32)
    return _pcall(
        body, name="ret_bwd", grid=(n,),
        in_specs=[tok(dk), tok(dk), tok_t, tok_t, tok(dv), const(c, c), const(c, c), const(c, dk), const(1, c),
                  const(c, dk), const(dk, dv), pl.BlockSpec((g, 1, dk, dv), lambda i: (0, n - 1 - i, 0, 0)), tok(dv)],
        out_specs=[tok(dk), tok(dk), tok(dv), const(c, c), const(c, dk), const(c, dk), const(dk, dv)],
        out_shape=[sds(g, n, c, dk), sds(g, n, c, dk), sds(g, n, c, dv), sds(g, c, c), sds(g, c, dk), sds(g, c, dk),
                   sds(g, dk, dv)],
        scratch_shapes=[pltpu.VMEM((g, dk, dv), F32)],
        compiler_params=pltpu.CompilerParams(dimension_semantics=("arbitrary",)),
    )(q, k, q_t, k_t, v, dmat, dmat_t, xi_b, xi_r, zeta_b, gm, starts, do)


def _ret_forms(xi, zeta, gm, dk, dv):
    lanes = lambda t: jnp.broadcast_to(t[..., None], t.shape + (dk,))
    return lanes(xi), xi[:, None, :], lanes(zeta), zeta[:, None, :], jnp.broadcast_to(gm[:, None, None], gm.shape + (dk, dv))


@jax.custom_vjp
def ret_chunked(q, k, v, dmat, xi, zeta, gm):
    xi_b, _, _, zeta_r, gm_f = _ret_forms(xi, zeta, gm, q.shape[-1], v.shape[-1])
    return _ret_fwd_call(q, k, _t(k), v, dmat, xi_b, zeta_r, gm_f)[0]


def _ret_chunked_fwd(q, k, v, dmat, xi, zeta, gm):
    xi_b, _, _, zeta_r, gm_f = _ret_forms(xi, zeta, gm, q.shape[-1], v.shape[-1])
    o, starts = _ret_fwd_call(q, k, _t(k), v, dmat, xi_b, zeta_r, gm_f)
    return o, (q, k, v, dmat, xi, zeta, gm, starts)


def _ret_chunked_bwd(res, do):
    q, k, v, dmat, xi, zeta, gm, starts = res
    xi_b, xi_r, zeta_b, _, gm_f = _ret_forms(xi, zeta, gm, q.shape[-1], v.shape[-1])
    dq, dk, dv, dd, dxib, dzb, dgm = _ret_bwd_call(q, k, _t(q), _t(k), v, dmat, _t(dmat), xi_b, xi_r, zeta_b, gm_f, starts, do)
    return dq, dk, dv, dd, jnp.sum(dxib, axis=-1), jnp.sum(dzb, axis=-1), jnp.sum(dgm, axis=(1, 2))


ret_chunked.defvjp(_ret_chunked_fwd, _ret_chunked_bwd)


def _peer(k):
    mx, my, mc = lax.axis_index("x"), lax.axis_index("y"), lax.axis_index("c")
    px = 1 - mx if k & 4 else mx
    py = 1 - my if k & 2 else my
    pc = 1 - mc if k & 1 else mc
    return (px, py, pc), 4 * px + 2 * py + pc


def _exchange_call(x, all_to_all, name):
    blk = x.shape[1:] if all_to_all else x.shape

    def body(x_ref, out_ref, send_sems, recv_sems, local_sem):
        me = 4 * lax.axis_index("x") + 2 * lax.axis_index("y") + lax.axis_index("c")
        mine = pltpu.make_async_copy(x_ref.at[me] if all_to_all else x_ref, out_ref.at[me], local_sem)
        mine.start()
        sends = []
        for k in range(1, N_DEV):
            dev, idx = _peer(k)
            cp = pltpu.make_async_remote_copy(
                src_ref=x_ref.at[idx] if all_to_all else x_ref, dst_ref=out_ref.at[me],
                send_sem=send_sems.at[k - 1], recv_sem=recv_sems.at[k - 1],
                device_id=dev, device_id_type=pl.DeviceIdType.MESH)
            cp.start()
            sends.append(cp)
        for k in range(1, N_DEV):
            dev, idx = _peer(k)
            pltpu.make_async_remote_copy(
                src_ref=x_ref.at[idx] if all_to_all else x_ref, dst_ref=out_ref.at[idx],
                send_sem=send_sems.at[k - 1], recv_sem=recv_sems.at[k - 1],
                device_id=dev, device_id_type=pl.DeviceIdType.MESH).wait_recv()
        for cp in sends:
            cp.wait_send()
        mine.wait()

    return _pcall(
        body, name=name,
        in_specs=[pl.BlockSpec(memory_space=pl.ANY)], out_specs=pl.BlockSpec(memory_space=pl.ANY),
        out_shape=jax.ShapeDtypeStruct((N_DEV,) + tuple(blk), x.dtype),
        scratch_shapes=[pltpu.SemaphoreType.DMA((N_DEV - 1,)), pltpu.SemaphoreType.DMA((N_DEV - 1,)),
                        pltpu.SemaphoreType.DMA],
    )(x)


def _sum8_call(x, name):
    _, r, c = x.shape
    tr = _pick(r, (256, 160, 128, 72, 64, 32, 16, 8))

    def body(x_ref, o_ref):
        acc = x_ref[0].astype(F32)
        for d in range(1, N_DEV):
            acc = acc + x_ref[d].astype(F32)
        o_ref[...] = acc

    return _pcall(
        body, name=name, grid=(r // tr,),
        in_specs=[pl.BlockSpec((N_DEV, tr, c), lambda i: (0, i, 0))],
        out_specs=pl.BlockSpec((tr, c), lambda i: (i, 0)),
        out_shape=jax.ShapeDtypeStruct((r, c), F32),
        compiler_params=pltpu.CompilerParams(dimension_semantics=("parallel",)),
    )(x)


def _adamw_call(w, g, m, v, name):
    r, c = w.shape
    tr = _pick(r, (256, 128, 64, 32, 16, 8))
    bc1 = 1.0 - ADAM_B1 ** ADAM_STEP
    bc2 = 1.0 - ADAM_B2 ** ADAM_STEP

    def body(w_ref, g_ref, m_ref, v_ref, d_ref, nm_ref, nv_ref):
        gv = g_ref[...]
        nm = ADAM_B1 * m_ref[...] + (1.0 - ADAM_B1) * gv
        nv = ADAM_B2 * v_ref[...] + (1.0 - ADAM_B2) * jnp.square(gv)
        d_ref[...] = -ADAM_LR * ((nm / bc1) / (jnp.sqrt(nv / bc2) + ADAM_EPS) + ADAM_WD * w_ref[...])
        nm_ref[...] = nm
        nv_ref[...] = nv

    spec = pl.BlockSpec((tr, c), lambda i: (i, 0))
    sds = jax.ShapeDtypeStruct((r, c), F32)
    return _pcall(
        body, name=name, grid=(r // tr,), in_specs=[spec] * 4, out_specs=[spec] * 3, out_shape=[sds] * 3,
        compiler_params=pltpu.CompilerParams(dimension_semantics=("parallel",)),
    )(w, g, m, v)


def layer_norm(x, g, b):
    mu = jnp.mean(x, axis=-1, keepdims=True)
    var = jnp.mean(jnp.square(x - mu), axis=-1, keepdims=True)
    return (x - mu) * lax.rsqrt(var + LN_EPS) * g + b


def rms_norm(x, g):
    return x * lax.rsqrt(jnp.mean(x * x, axis=-1, keepdims=True) + NORM_EPS) * g


def head_layer_norm(o, g):
    b_, t_, h_, d_ = o.shape
    mu = jnp.mean(o, axis=-1, keepdims=True)
    var = jnp.mean(jnp.square(o - mu), axis=-1, keepdims=True)
    return ((o - mu) * lax.rsqrt(var + NORM_EPS)).reshape(b_, t_, h_ * d_) * g


def l2norm(t):
    return t * lax.rsqrt(jnp.sum(t * t, axis=-1, keepdims=True) + NORM_EPS)


def rope_freqs(dim):
    return ROPE_BASE ** (-jnp.arange(0, dim, 2, dtype=F32) / dim)


def axial_rope(rows, rot_dim):
    row = jnp.broadcast_to(jnp.arange(rows, dtype=F32)[:, None], (rows, GRID_W)).reshape(-1)
    col = jnp.broadcast_to(jnp.arange(GRID_W, dtype=F32)[None, :], (rows, GRID_W)).reshape(-1)
    inv = rope_freqs(rot_dim // 2)
    ang = jnp.concatenate([row[:, None] * inv, col[:, None] * inv], axis=-1)
    return jnp.cos(ang), jnp.sin(ang)


def sequence_rope(n_tok, rot_dim):
    ang = jnp.arange(n_tok, dtype=F32)[:, None] * rope_freqs(rot_dim)
    return jnp.cos(ang), jnp.sin(ang)


def apply_rope(x, cos, sin):
    x1, x2 = jnp.split(x, 2, axis=-1)
    c = cos[:, None, :]
    s = sin[:, None, :]
    return jnp.concatenate([x1 * c - x2 * s, x1 * s + x2 * c], axis=-1)


def _flip_t(t):
    return jnp.flip(t, axis=2)


def _split_columns(z):
    idx = np.cumsum(np.array(IN_SPLITS))[:-1].tolist()
    return jnp.split(z, idx, axis=-1)


def short_conv(x, w):
    k_width, ch = w.shape
    pad = k_width // 2
    return lax.conv_general_dilated(x, w[:, None, :], window_strides=(1,), padding=[(pad, pad)],
                                    dimension_numbers=('NWC', 'WIO', 'NWC'), feature_group_count=ch)


def _to_heads(t, h, d):
    b_, t_, _ = t.shape
    return t.reshape(b_, t_, h, d).transpose(0, 2, 1, 3).reshape(b_ * h, t_, d)


def _from_heads(t, b_):
    g, t_, d = t.shape
    return t.reshape(b_, g // b_, t_, d).transpose(0, 2, 1, 3).reshape(b_, t_, (g // b_) * d)


def _lane_scalar(vals):
    return jnp.broadcast_to(vals[:, None, None], (vals.shape[0], 1, 128))


def swa_group(q, k, v, qc, kc, vc, sink, cos, sin, with_ctx_out):
    b_, s_, _ = q.shape
    l_ = kc.shape[1]
    grp = SWA_HEADS // SWA_KV_HEADS
    d = SWA_HEAD_DIM
    w_ = SWA_BLOCK
    scale = d ** -0.5
    qh = apply_rope(q.reshape(b_, s_, SWA_HEADS, d), cos, sin).transpose(0, 2, 1, 3).reshape(b_ * SWA_HEADS, s_, d)
    kh = apply_rope(k.reshape(b_, s_, SWA_KV_HEADS, d), cos, sin).transpose(0, 2, 1, 3).reshape(b_ * SWA_KV_HEADS, s_, d)
    vh = _to_heads(v, SWA_KV_HEADS, d)
    kch = _to_heads(kc, SWA_KV_HEADS, d)
    vch = _to_heads(vc, SWA_KV_HEADS, d)
    padk = lambda t: jnp.pad(t, ((0, 0), (w_, w_), (0, 0)))
    sink_g = _lane_scalar(jnp.tile(sink, b_))
    y = _from_heads(swa_attn(qh, padk(kh), padk(vh), kch, vch, sink_g, scale), b_)
    yc = None
    if with_ctx_out:
        qch = _to_heads(qc, SWA_HEADS, d)
        rep = lambda t: jnp.repeat(t.reshape(b_, SWA_KV_HEADS, l_, d), grp, axis=1).reshape(b_ * SWA_HEADS, l_, d)
        yc = _from_heads(attn_full(qch, rep(kch), rep(vch), sink_g, scale, True), b_)
    return y, yc


def gated_delta_chunked(q, k, v, log_g, beta):
    g_, t_, dk = k.shape
    dv = v.shape[-1]
    c_ = DN_CHUNK
    n = t_ // c_
    assert dk == c_ and dv == c_
    g_cum = jnp.cumsum(log_g.reshape(g_, n, c_), axis=-1)
    o = dn_chunked(q.reshape(g_, n, c_, dk), k.reshape(g_, n, c_, dk), v.reshape(g_, n, c_, dv), g_cum,
                   beta.reshape(g_, n, c_))
    return o.reshape(g_, t_, dv)


def deltanet_group(qkv, z, ab, qkv_c, z_c, ab_c, conv_w, a_log, dt_bias, norm_g, with_ctx_out):
    def prep(qkv_, ab_):
        b_, t_, _ = qkv_.shape
        y = jax.nn.silu(short_conv(qkv_, conv_w))
        q, k, v = [t.reshape(b_, t_, DN_HEADS, DN_HEAD_DIM).transpose(0, 2, 1, 3) for t in jnp.split(y, 3, axis=-1)]
        q = l2norm(q) * DN_HEAD_DIM ** -0.5
        k = l2norm(k)
        ab_ = ab_.reshape(b_, t_, 2, 2, DN_HEADS)
        log_g = -jnp.exp(a_log) * jax.nn.softplus(ab_[:, :, :, 0] + dt_bias)
        beta = jax.nn.sigmoid(ab_[:, :, :, 1])
        return q, k, v, log_g.transpose(2, 0, 3, 1), beta.transpose(2, 0, 3, 1)

    def out(o, z_):
        b_, t_, _ = z_.shape
        o = rms_norm(o.transpose(0, 2, 1, 3), norm_g) * jax.nn.silu(z_).reshape(b_, t_, DN_HEADS, DN_HEAD_DIM)
        return o.reshape(b_, t_, DN_W)

    qc, kc, vc, lgc, bc = prep(qkv_c, ab_c)
    q, k, v, lg, bt = prep(qkv, ab)
    b_, l_, s_ = qkv.shape[0], qkv_c.shape[1], qkv.shape[1]
    seq = lambda tc, tl: jnp.stack([jnp.concatenate([tc, tl], axis=2),
                                    jnp.concatenate([_flip_t(tc), _flip_t(tl)], axis=2)])
    seq_g = lambda tc, tl: jnp.stack([jnp.concatenate([tc[0], tl[0]], axis=2),
                                      jnp.concatenate([_flip_t(tc[1]), _flip_t(tl[1])], axis=2)])
    flat = lambda t: t.reshape((2 * b_ * DN_HEADS,) + t.shape[3:])
    o = gated_delta_chunked(flat(seq(qc, q)), flat(seq(kc, k)), flat(seq(vc, v)), flat(seq_g(lgc, lg)), flat(seq_g(bc, bt)))
    o = o.reshape(2, b_, DN_HEADS, l_ + s_, DN_HEAD_DIM)
    y = out(o[0][:, :, l_:] + _flip_t(o[1][:, :, l_:]), z)
    yc = out(o[0][:, :, :l_] + _flip_t(o[1][:, :, :l_]), z_c) if with_ctx_out else None
    return y, yc


def retention_core(q, k, v, log_gamma):
    _, b_, h_, t_, dk = q.shape
    dv = v.shape[-1]
    c_ = RET_CHUNK
    n = t_ // c_
    g_ = 2 * b_ * h_
    pos = jnp.arange(c_, dtype=F32)
    per_g = lambda t: jnp.broadcast_to(t[:, None], (2, b_) + t.shape[1:]).reshape((g_,) + t.shape[2:])
    zeta = jnp.exp((c_ - 1 - pos) * log_gamma[..., None])
    xi = jnp.exp((pos + 1.0) * log_gamma[..., None])
    rel = pos[:, None] - pos[None, :]
    dmat = jnp.where(rel >= 0, jnp.exp(jnp.maximum(rel, 0.0) * log_gamma[..., None, None]), 0.0)
    gm = jnp.exp(c_ * log_gamma)
    o = ret_chunked(q.reshape(g_, n, c_, dk), k.reshape(g_, n, c_, dk), v.reshape(g_, n, c_, dv),
                    per_g(dmat), per_g(xi), per_g(zeta), per_g(gm))
    return o.reshape(2, b_, h_, t_, dv)


def retention_group(q, k, v, g, qc, kc, vc, gc, log1m_gamma, norm_g, cos, sin, with_ctx_out):
    log_gamma = jnp.log1p(-jnp.exp(log1m_gamma))
    heads = lambda t, dh: t.reshape(t.shape[0], t.shape[1], RET_HEADS, dh)
    bhtd = lambda t: t.transpose(0, 2, 1, 3)
    sc = RET_QK_DIM ** -0.5
    l_ = kc.shape[1]
    q = bhtd(apply_rope(heads(q, RET_QK_DIM), cos, sin)) * sc
    k = bhtd(apply_rope(heads(k, RET_QK_DIM), cos, sin))
    v = bhtd(heads(v, RET_V_DIM))
    kc = bhtd(heads(kc, RET_QK_DIM))
    vc = bhtd(heads(vc, RET_V_DIM))
    qcs = bhtd(heads(qc, RET_QK_DIM)) * sc

    def out(o, g_):
        return head_layer_norm(o.transpose(0, 2, 1, 3), norm_g) * jax.nn.silu(g_)

    seq = lambda tc, tl: jnp.stack([jnp.concatenate([tc, tl], axis=2),
                                    jnp.concatenate([_flip_t(tc), _flip_t(tl)], axis=2)])
    o = retention_core(seq(qcs, q), seq(kc, k), seq(vc, v), log_gamma)
    o_f, o_b = o[0], o[1]
    y = out(o_f[:, :, l_:] + _flip_t(o_b[:, :, l_:]), g)
    yc = out(o_f[:, :, :l_] + _flip_t(o_b[:, :, :l_]), gc) if with_ctx_out else None
    return y, yc


def mla_group(cq, ckv, kr, cq_c, ckv_c, kr_c, q_norm, w_uq, kv_norm, w_ukv, cos, sin, with_ctx_out):
    b_, s_, _ = cq.shape
    l_ = cq_c.shape[1]
    dqk = MLA_NOPE_DIM + MLA_ROPE_DIM
    rows = lambda tl, tc: jnp.concatenate([tl.reshape(b_ * s_, -1), tc.reshape(b_ * l_, -1)], axis=0)
    qa = matmul(rms_norm(rows(cq, cq_c), q_norm), w_uq)
    kva = matmul(rms_norm(rows(ckv, ckv_c), kv_norm), w_ukv)
    q = qa[:b_ * s_].reshape(b_, s_, MLA_HEADS, dqk)
    qc = qa[b_ * s_:].reshape(b_, l_, MLA_HEADS, dqk)
    q = jnp.concatenate([q[..., :MLA_NOPE_DIM], apply_rope(q[..., MLA_NOPE_DIM:], cos, sin)], axis=-1)
    kv = kva[:b_ * s_].reshape(b_, s_, MLA_HEADS, MLA_NOPE_DIM + MLA_V_DIM)
    kvc = kva[b_ * s_:].reshape(b_, l_, MLA_HEADS, MLA_NOPE_DIM + MLA_V_DIM)
    kr = apply_rope(kr[:, :, None, :], cos, sin)
    k = jnp.concatenate([kv[..., :MLA_NOPE_DIM], jnp.broadcast_to(kr, (b_, s_, MLA_HEADS, MLA_ROPE_DIM))], axis=-1)
    kc = jnp.concatenate([kvc[..., :MLA_NOPE_DIM],
                          jnp.broadcast_to(kr_c[:, :, None, :], (b_, l_, MLA_HEADS, MLA_ROPE_DIM))], axis=-1)
    v, vc = kv[..., MLA_NOPE_DIM:], kvc[..., MLA_NOPE_DIM:]
    hd = lambda t: t.transpose(0, 2, 1, 3).reshape(b_ * MLA_HEADS, t.shape[1], t.shape[3])
    scale = dqk ** -0.5
    no_sink = jnp.zeros((b_ * MLA_HEADS, 1, 128), F32)
    kch, vch = hd(kc), hd(vc)
    y = attn_full(hd(q), jnp.concatenate([hd(k), kch], axis=1), jnp.concatenate([hd(v), vch], axis=1), no_sink, scale, False)
    y = _from_heads(y, b_)
    yc = _from_heads(attn_full(hd(qc), kch, vch, no_sink, scale, False), b_) if with_ctx_out else None
    return y, yc


def token_mixers(zl, zc, p, layer, rope, with_ctx_out):
    (a_q, a_k, a_v, b_qkv, b_z, b_ab, c_q, c_k, c_v, c_g, d_cq, d_ckv, d_kr) = _split_columns(zl)
    (a_qc, a_kc, a_vc, b_qkvc, b_zc, b_abc, c_qc, c_kc, c_vc, c_gc, d_cqc, d_ckvc, d_krc) = _split_columns(zc)
    swa_cos, swa_sin, ret_cos, ret_sin, mla_cos, mla_sin = rope
    ya, yac = swa_group(a_q, a_k, a_v, a_qc, a_kc, a_vc, p['swa_sink'][layer], swa_cos, swa_sin, with_ctx_out)
    yb, ybc = deltanet_group(b_qkv, b_z, b_ab, b_qkvc, b_zc, b_abc, p['dn_conv_w'][layer], p['dn_a_log'][layer],
                             p['dn_dt_bias'][layer], p['dn_norm_g'][layer], with_ctx_out)
    yr, yrc = retention_group(c_q, c_k, c_v, c_g, c_qc, c_kc, c_vc, c_gc, p['ret_log1m_gamma'][layer],
                              p['ret_norm_g'][layer], ret_cos, ret_sin, with_ctx_out)
    yd, ydc = mla_group(d_cq, d_ckv, d_kr, d_cqc, d_ckvc, d_krc, p['mla_q_norm'][layer], p['mla_w_uq'][layer],
                        p['mla_kv_norm'][layer], p['mla_w_ukv'][layer], mla_cos, mla_sin, with_ctx_out)
    y = jnp.concatenate([ya, yb, yr, yd], axis=-1)
    yc = jnp.concatenate([yac, ybc, yrc, ydc], axis=-1) if with_ctx_out else None
    return y, yc


def local_loss(p, x, ctx, loss_target):
    b_, n_tok, d_ = x.shape
    l_ = ctx.shape[1]
    rows = n_tok // GRID_W
    rope = (*axial_rope(rows, SWA_HEAD_DIM), *sequence_rope(n_tok, RET_QK_DIM), *axial_rope(rows, MLA_ROPE_DIM))
    rl, rc = b_ * n_tok, b_ * l_
    mods = [jnp.concatenate([p['mod'][layer], p['cmod'][layer][None]], axis=0) for layer in range(DEPTH)]
    part = lambda layer, j: mods[layer][:, j * d_:(j + 1) * d_][:, None, :]
    vec = lambda name, layer: p[name][layer][None, :]
    xr = jnp.concatenate([x.reshape(rl, d_), ctx.reshape(rc, d_)], axis=0)
    sh1, sc1 = part(0, 0), part(0, 1)
    h = jnp.concatenate([(x * (1 + sc1[:b_]) + sh1[:b_]).reshape(rl, d_), (ctx * (1 + sc1[b_]) + sh1[b_]).reshape(rc, d_)],
                        axis=0)
    for layer in range(DEPTH):
        with_ctx_out = layer < DEPTH - 1
        g1, sh2, sc2, g2 = part(layer, 2), part(layer, 3), part(layer, 4), part(layer, 5)
        z = matmul(h, p['w_in'][layer])
        zl = z[:rl, :IN_WIDTH].reshape(b_, n_tok, IN_WIDTH)
        zc = z[rl:, :IN_WIDTH].reshape(b_, l_, IN_WIDTH)
        y, yc = token_mixers(zl, zc, p, layer, rope, with_ctx_out)
        if with_ctx_out:
            yo = matmul(jnp.concatenate([y.reshape(rl, d_), yc.reshape(rc, d_)], axis=0), p['w_out'][layer])
            xr, h2 = ln_mod(xr, yo, g1, vec('ln1_g', layer), vec('ln1_b', layer), sc2, sh2, n_tok)
            f = matmul_relu2(matmul(h2, p['w_ff1'][layer]), p['w_ff2'][layer])
            xr, h = ln_mod(xr, f, g2, vec('ln2_g', layer), vec('ln2_b', layer), part(layer + 1, 1), part(layer + 1, 0), n_tok)
        else:
            lat = lambda t: t[:b_]
            yo = matmul(y.reshape(rl, d_), p['w_out'][layer])
            xl, h2 = ln_mod(xr[:rl], yo, lat(g1), vec('ln1_g', layer), vec('ln1_b', layer), lat(sc2), lat(sh2), n_tok)
            f = matmul_relu2(matmul(h2, p['w_ff1'][layer]), p['w_ff2'][layer])
            none = jnp.zeros((b_, 1, d_), F32)
            xl, _ = ln_mod(xl, f, lat(g2), vec('ln2_g', layer), vec('ln2_b', layer), none, none, n_tok)
    err = jnp.square(xl - loss_target.reshape(rl, d_))
    return 0.5 * jnp.sum(jnp.mean(err, axis=-1))


def _shard_shape(shape, axis):
    s = list(shape)
    s[axis] //= N_DEV
    return tuple(s)


_SLAB_SIZES = [int(np.prod(_shard_shape(shape, axis))) for _, shape, axis in BIG]
_SLAB_ROWS = -(-sum(_SLAB_SIZES) // (SLAB_COLS * 16)) * 16


def _pack_rows(flat):
    padn = _SLAB_ROWS * SLAB_COLS - flat.shape[-1]
    flat = jnp.pad(flat, [(0, 0)] * (flat.ndim - 1) + [(0, padn)])
    return flat.reshape(flat.shape[:-1] + (_SLAB_ROWS, SLAB_COLS))


def _pack_local(a):
    per_layer = [_pack_rows(jnp.concatenate([a[name][l].reshape(-1) for name, _, _ in BIG])) for l in range(DEPTH)]
    return jnp.concatenate(per_layer, axis=0).astype(BF16)


def _unpack_full(slab):
    slab = slab.reshape(N_DEV, DEPTH, _SLAB_ROWS * SLAB_COLS)
    out, off = {}, 0
    for (name, shape, axis), size in zip(BIG, _SLAB_SIZES):
        r, c = _shard_shape(shape, axis)
        pieces = slab[:, :, off:off + size].reshape(N_DEV, DEPTH, r, c)
        if axis == 0:
            full = pieces.transpose(1, 0, 2, 3).reshape(DEPTH, N_DEV * r, c)
        else:
            full = pieces.transpose(1, 2, 0, 3).reshape(DEPTH, r, N_DEV * c)
        out[name] = full.astype(F32)
        off += size
    return out


def _pack_grads(grads):
    parts = []
    for name, shape, axis in BIG:
        g = grads[name]
        r, c = _shard_shape(shape, axis)
        if axis == 0:
            pieces = g.reshape(DEPTH, N_DEV, r, c).transpose(1, 0, 2, 3)
        else:
            pieces = g.reshape(DEPTH, r, N_DEV, c).transpose(2, 0, 1, 3)
        parts.append(pieces.reshape(N_DEV, DEPTH, r * c))
    slab = _pack_rows(jnp.concatenate(parts, axis=-1))
    return slab.reshape(N_DEV, DEPTH * _SLAB_ROWS, SLAB_COLS).astype(BF16)


def _unpack_shards(slab):
    slab = slab.reshape(DEPTH, _SLAB_ROWS * SLAB_COLS)
    out, off = {}, 0
    for (name, shape, axis), size in zip(BIG, _SLAB_SIZES):
        out[name] = slab[:, off:off + size].reshape((DEPTH,) + _shard_shape(shape, axis))
        off += size
    return out


def _pad_vec(vec, rows_multiple=8):
    n = vec.shape[0]
    rows = -(-n // (128 * rows_multiple)) * rows_multiple
    return jnp.pad(vec, (0, rows * 128 - n)).reshape(rows, 128)


def _adamw(w, g, m, v, name):
    shape = w.shape
    if w.ndim >= 2 and shape[-1] >= 128:
        as2 = lambda t: t.reshape(-1, shape[-1])
        d, nm, nv = _adamw_call(as2(w), as2(g), as2(m), as2(v), name)
        return d.reshape(shape), nm.reshape(shape), nv.reshape(shape)
    n = int(np.prod(shape))
    as2 = lambda t: _pad_vec(t.reshape(-1))
    d, nm, nv = _adamw_call(as2(w), as2(g), as2(m), as2(v), name)
    un = lambda t: t.reshape(-1)[:n].reshape(shape)
    return un(d), un(nm), un(nv)


def kernel(x, c, ctx, c_ctx, ada_w, ada_b, w_in, swa_sink, dn_conv_w, dn_a_log, dn_dt_bias, dn_norm_g, ret_log1m_gamma, ret_norm_g, mla_q_norm, mla_w_uq, mla_kv_norm, mla_w_ukv, w_out, ln1_g, ln1_b, w_ff1, w_ff2, ln2_g, ln2_b, loss_target, m_c_ctx, m_ada_w, m_ada_b, m_w_in, m_swa_sink, m_dn_conv_w, m_dn_a_log, m_dn_dt_bias, m_dn_norm_g, m_ret_log1m_gamma, m_ret_norm_g, m_mla_q_norm, m_mla_w_uq, m_mla_kv_norm, m_mla_w_ukv, m_w_out, m_ln1_g, m_ln1_b, m_w_ff1, m_w_ff2, m_ln2_g, m_ln2_b, v_c_ctx, v_ada_w, v_ada_b, v_w_in, v_swa_sink, v_dn_conv_w, v_dn_a_log, v_dn_dt_bias, v_dn_norm_g, v_ret_log1m_gamma, v_ret_norm_g, v_mla_q_norm, v_mla_w_uq, v_mla_kv_norm, v_mla_w_ukv, v_w_out, v_ln1_g, v_ln1_b, v_w_ff1, v_w_ff2, v_ln2_g, v_ln2_b):
    a = dict(zip(ARG_NAMES, (x, c, ctx, c_ctx, ada_w, ada_b, w_in, swa_sink, dn_conv_w, dn_a_log, dn_dt_bias, dn_norm_g, ret_log1m_gamma, ret_norm_g, mla_q_norm, mla_w_uq, mla_kv_norm, mla_w_ukv, w_out, ln1_g, ln1_b, w_ff1, w_ff2, ln2_g, ln2_b, loss_target, m_c_ctx, m_ada_w, m_ada_b, m_w_in, m_swa_sink, m_dn_conv_w, m_dn_a_log, m_dn_dt_bias, m_dn_norm_g, m_ret_log1m_gamma, m_ret_norm_g, m_mla_q_norm, m_mla_w_uq, m_mla_kv_norm, m_mla_w_ukv, m_w_out, m_ln1_g, m_ln1_b, m_w_ff1, m_w_ff2, m_ln2_g, m_ln2_b, v_c_ctx, v_ada_w, v_ada_b, v_w_in, v_swa_sink, v_dn_conv_w, v_dn_a_log, v_dn_dt_bias, v_dn_norm_g, v_ret_log1m_gamma, v_ret_norm_g, v_mla_q_norm, v_mla_w_uq, v_mla_kv_norm, v_mla_w_ukv, v_w_out, v_ln1_g, v_ln1_b, v_w_ff1, v_w_ff2, v_ln2_g, v_ln2_b)))
    me = 4 * lax.axis_index("x") + 2 * lax.axis_index("y") + lax.axis_index("c")
    b_loc = x.shape[0]
    n_ex = N_DEV * b_loc
    conv_k, conv_c = dn_conv_w.shape[1], dn_conv_w.shape[2]
    ada_cols = ada_w.shape[2]

    small_in = jnp.concatenate([c.reshape(-1), dn_conv_w.reshape(-1)])
    small_all = _exchange_call(_pad_vec(small_in), False, "gather_small").reshape(N_DEV, -1)
    c_all = small_all[:, :b_loc * D_MODEL].reshape(n_ex, D_MODEL)
    conv_all = small_all[:, b_loc * D_MODEL:b_loc * D_MODEL + DEPTH * conv_k * conv_c].reshape(N_DEV, DEPTH, conv_k, conv_c)
    conv_full = conv_all.transpose(1, 2, 0, 3).reshape(DEPTH, conv_k, N_DEV * conv_c)
    big = _unpack_full(_exchange_call(_pack_local(a), False, "gather_weights"))
    big['w_in'] = jnp.pad(big['w_in'], ((0, 0), (0, 0), (0, IN_WIDTH_PAD - IN_WIDTH)))

    n_rows = -(-(n_ex + 1) // 16) * 16
    silu_cc = jax.nn.silu(c_ctx)
    a_rows = jnp.concatenate([jax.nn.silu(c_all), silu_cc[None], jnp.zeros((n_rows - n_ex - 1, D_MODEL), F32)], axis=0)
    m_loc = jnp.concatenate([_mm_call(a_rows, ada_w[l], False, "ada_fwd") for l in range(DEPTH)], axis=0)
    m_all = _exchange_call(m_loc, False, "gather_mod").reshape(N_DEV, DEPTH, n_rows, ada_cols)
    mod_full = m_all.transpose(1, 2, 0, 3).reshape(DEPTH, n_rows, N_DEV * ada_cols) + ada_b[:, None, :]
    mod = lax.dynamic_slice_in_dim(mod_full, me * b_loc, b_loc, axis=1)
    cmod = mod_full[:, n_ex]

    p = dict(big)
    p.update(mod=mod, cmod=cmod, dn_conv_w=conv_full)
    for name in SMALL:
        p[name] = a[name]
    loss_loc, (gp, gx) = jax.value_and_grad(local_loss, argnums=(0, 1))(p, x, ctx, loss_target)
    loss = lax.psum(loss_loc, MESH_AXES)

    gp['w_in'] = gp['w_in'][:, :, :IN_WIDTH]
    g_big = _unpack_shards(_sum8_call(_exchange_call(_pack_grads(gp), True, "scatter_grads"), "sum_grads"))

    d_loc = jnp.concatenate([gp['mod'], gp['cmod'][:, None, :]], axis=1).reshape(DEPTH * (b_loc + 1), -1)
    d_loc = jnp.pad(d_loc, ((0, 8 - DEPTH * (b_loc + 1)), (0, 0)))
    d_all = _exchange_call(d_loc, False, "gather_dmod")[:, :DEPTH * (b_loc + 1)].reshape(N_DEV, DEPTH, b_loc + 1, -1)
    d_rows = d_all[:, :, :b_loc].transpose(1, 0, 2, 3).reshape(DEPTH, n_ex, -1)
    d_crow = d_all[0, :, b_loc]
    for d in range(1, N_DEV):
        d_crow = d_crow + d_all[d, :, b_loc]
    dm_full = jnp.concatenate([d_rows, d_crow[:, None, :], jnp.zeros((DEPTH, n_rows - n_ex - 1, d_rows.shape[-1]), F32)], axis=1)
    g_ada_b = jnp.sum(dm_full, axis=1)
    dm_mine = lax.dynamic_slice_in_dim(dm_full, me * ada_cols, ada_cols, axis=2)
    g_ada_w = jnp.stack([_mm_call(a_rows, dm_mine[l], True, "ada_bwd_w") for l in range(DEPTH)])
    crow8 = jnp.concatenate([dm_mine[:, n_ex:n_ex + 1], jnp.zeros((DEPTH, 15, ada_cols), F32)], axis=1)
    dsilu_part = sum(_mm_call(crow8[l], jnp.transpose(ada_w[l]), False, "ada_bwd_c")[0] for l in range(DEPTH))

    small_g = jnp.concatenate([gp[name].reshape(-1) for name in SMALL] + [gp['dn_conv_w'].reshape(-1), dsilu_part])
    small_sum = _sum8_call(_exchange_call(_pad_vec(small_g), False, "gather_small_grads"), "sum_small_grads").reshape(-1)
    g_all, off = {}, 0
    for name in SMALL:
        n = int(np.prod(a[name].shape))
        g_all[name] = small_sum[off:off + n].reshape(a[name].shape)
        off += n
    n = DEPTH * conv_k * N_DEV * conv_c
    g_conv_full = small_sum[off:off + n].reshape(DEPTH, conv_k, N_DEV * conv_c)
    g_all['dn_conv_w'] = lax.dynamic_slice_in_dim(g_conv_full, me * conv_c, conv_c, axis=2)
    off += n
    dsilu = small_sum[off:off + D_MODEL]
    sig = jax.nn.sigmoid(c_ctx)
    g_all['c_ctx'] = dsilu * (sig * (1 + c_ctx * (1 - sig)))
    g_all['ada_w'] = g_ada_w
    g_all['ada_b'] = g_ada_b
    g_all.update(g_big)

    delta, new_m, new_v = {}, {}, {}
    for name in WEIGHTS:
        delta[name], new_m[name], new_v[name] = _adamw(a[name], g_all[name], a['m_' + name], a['v_' + name], "adamw_" + name)
    return (loss, gx, *[g_all[n] for n in WEIGHTS], *[delta[n] for n in WEIGHTS],
            *[new_m[n] for n in WEIGHTS], *[new_v[n] for n in WEIGHTS])
```

```python
import functools
import math

import jax
import jax.numpy as jnp
import numpy as np
from jax import lax
from jax.experimental import pallas as pl
from jax.experimental.pallas import tpu as pltpu

F32 = jnp.float32
BF16 = jnp.bfloat16
N_DEV = 8
MESH_AXES = ("x", "y", "c")

D_MODEL = 1024
DEPTH = 2
GRID_W = 64
SWA_HEADS, SWA_KV_HEADS, SWA_HEAD_DIM, SWA_WINDOW, SWA_BLOCK = 4, 2, 64, 128, 128
DN_HEADS, DN_HEAD_DIM, DN_CHUNK = 4, 64, 64
RET_HEADS, RET_QK_DIM, RET_V_DIM, RET_CHUNK = 4, 32, 64, 64
MLA_HEADS, MLA_Q_RANK, MLA_KV_RANK, MLA_NOPE_DIM, MLA_ROPE_DIM, MLA_V_DIM = 4, 256, 128, 64, 32, 64
D_FF = 4 * D_MODEL
ROPE_BASE = 10000.0
NORM_EPS = 1e-6
LN_EPS = 1e-5
DEEPNORM_ALPHA = (2 * DEPTH) ** 0.25
SWA_Q = SWA_HEADS * SWA_HEAD_DIM
SWA_KV = SWA_KV_HEADS * SWA_HEAD_DIM
DN_W = DN_HEADS * DN_HEAD_DIM
RET_QK = RET_HEADS * RET_QK_DIM
RET_V = RET_HEADS * RET_V_DIM
IN_SPLITS = (SWA_Q, SWA_KV, SWA_KV, 3 * DN_W, DN_W, 4 * DN_HEADS, RET_QK, RET_QK, RET_V, RET_V,
             MLA_Q_RANK, MLA_KV_RANK, MLA_ROPE_DIM)
IN_WIDTH = sum(IN_SPLITS)
IN_WIDTH_PAD = -(-IN_WIDTH // 128) * 128

ADAM_LR, ADAM_B1, ADAM_B2, ADAM_EPS, ADAM_WD, ADAM_STEP = 0.001, 0.9, 0.999, 1e-08, 0.01, 10

WEIGHTS = ['c_ctx', 'ada_w', 'ada_b', 'w_in', 'swa_sink', 'dn_conv_w', 'dn_a_log', 'dn_dt_bias', 'dn_norm_g',
           'ret_log1m_gamma', 'ret_norm_g', 'mla_q_norm', 'mla_w_uq', 'mla_kv_norm', 'mla_w_ukv', 'w_out', 'ln1_g',
           'ln1_b', 'w_ff1', 'w_ff2', 'ln2_g', 'ln2_b']
FWD_INPUTS = ['x', 'c', 'ctx'] + WEIGHTS
ARG_NAMES = FWD_INPUTS + ['loss_target'] + ['m_' + n for n in WEIGHTS] + ['v_' + n for n in WEIGHTS]

BIG = (('w_in', (D_MODEL, IN_WIDTH), 1), ('w_out', (D_MODEL, D_MODEL), 0), ('w_ff1', (D_MODEL, D_FF), 1),
       ('w_ff2', (D_FF, D_MODEL), 0), ('mla_w_uq', (MLA_Q_RANK, MLA_HEADS * (MLA_NOPE_DIM + MLA_ROPE_DIM)), 1),
       ('mla_w_ukv', (MLA_KV_RANK, MLA_HEADS * (MLA_NOPE_DIM + MLA_V_DIM)), 1))
SMALL = ('swa_sink', 'dn_a_log', 'dn_dt_bias', 'dn_norm_g', 'ret_log1m_gamma', 'ret_norm_g', 'mla_q_norm',
         'mla_kv_norm', 'ln1_g', 'ln1_b', 'ln2_g', 'ln2_b')


def _pcall(body, **kw):
    return pl.pallas_call(body, **kw)


def _pick(n, cands):
    for cand in cands:
        if n % cand == 0:
            return cand
    return n


def _bdot(a, b, dims):
    return lax.dot_general(a.astype(BF16), b.astype(BF16), dims, preferred_element_type=F32)


def _lane0(t):
    return jnp.where(lax.broadcasted_iota(jnp.int32, t.shape, t.ndim - 1) == 0, t, 0.0)


NN = (((1,), (0,)), ((), ()))
NT = (((1,), (1,)), ((), ()))
TN = (((0,), (0,)), ((), ()))
BNN = (((2,), (1,)), ((0,), (0,)))
BNT = (((2,), (2,)), ((0,), (0,)))


MM_ROW_TILE_MAX = 1088
MM_COL_TILE_MAX = 1408
MM_TOKEN_TILE_MAX = 544
VMEM_LIMIT_MAX = 60 * 1024 * 1024


def _tile(n, cap, align):
    best = None
    for t in range(align, min(n, cap) + 1, align):
        if n % t == 0:
            best = t
    return best or n


def _relu2(t):
    return jnp.square(jnp.maximum(t, 0.0))


def _mm_call(a, b, trans_a, name, act_a=False, epi=None):
    if trans_a:
        kdim, m = a.shape
        tk = _tile(kdim, MM_TOKEN_TILE_MAX, 8)
        tm = _tile(m, 1024, 128)
    else:
        m, kdim = a.shape
        tk = _tile(kdim, MM_COL_TILE_MAX, 128)
        tm = _tile(m, MM_ROW_TILE_MAX, 8)
    n = b.shape[1]
    assert b.shape[0] == kdim
    tn = _tile(n, MM_COL_TILE_MAX, 128)
    nk = kdim // tk

    def body(*refs):
        a_ref, b_ref = refs[0], refs[1]
        e_ref = refs[2] if epi is not None else None
        o_ref = refs[-1]
        k = pl.program_id(2)
        av = a_ref[...]
        if act_a:
            av = _relu2(av)
        part = _bdot(av, b_ref[...], TN if trans_a else NN)

        def finish(t):
            return t * (2.0 * jnp.maximum(e_ref[...], 0.0)) if epi is not None else t

        if nk == 1:
            o_ref[...] = finish(part)
        else:
            @pl.when(k == 0)
            def _():
                o_ref[...] = part

            @pl.when((k > 0) & (k < nk - 1))
            def _():
                o_ref[...] += part

            @pl.when(k == nk - 1)
            def _():
                o_ref[...] = finish(o_ref[...] + part)

    if trans_a:
        a_spec = pl.BlockSpec((tk, tm), lambda i, j, k: (k, i))
    else:
        a_spec = pl.BlockSpec((tm, tk), lambda i, j, k: (i, k))
    o_spec = pl.BlockSpec((tm, tn), lambda i, j, k: (i, j))
    in_specs = [a_spec, pl.BlockSpec((tk, tn), lambda i, j, k: (k, j))] + ([o_spec] if epi is not None else [])
    tiles = tm * tk * a.dtype.itemsize + tk * tn * b.dtype.itemsize + tm * tn * 4 * (2 if epi is not None else 1)
    temps = tm * tk * (2 + (4 if act_a else 0)) + tk * tn * 2 + 2 * tm * tn * 4
    return _pcall(
        body, name=name, grid=(m // tm, n // tn, nk), in_specs=in_specs, out_specs=o_spec,
        out_shape=jax.ShapeDtypeStruct((m, n), F32),
        compiler_params=pltpu.CompilerParams(dimension_semantics=("parallel", "parallel", "arbitrary"),
                                             vmem_limit_bytes=min(2 * tiles + temps + (4 << 20), VMEM_LIMIT_MAX)),
    )(*((a, b) + ((epi,) if epi is not None else ())))


@jax.custom_vjp
def matmul(a, b):
    return _mm_call(a, b.astype(BF16), False, "mm_fwd")


def _matmul_fwd(a, b):
    bb = b.astype(BF16)
    return _mm_call(a, bb, False, "mm_fwd"), (a, bb)


def _matmul_bwd(res, g):
    a, bb = res
    da = _mm_call(g, jnp.transpose(bb), False, "mm_bwd_da")
    db = _mm_call(a, g, True, "mm_bwd_db")
    return da, db


matmul.defvjp(_matmul_fwd, _matmul_bwd)


@jax.custom_vjp
def matmul_relu2(a, b):
    return _mm_call(a, b.astype(BF16), False, "mm_act_fwd", act_a=True)


def _matmul_relu2_fwd(a, b):
    bb = b.astype(BF16)
    return _mm_call(a, bb, False, "mm_act_fwd", act_a=True), (a, bb)


def _matmul_relu2_bwd(res, g):
    a, bb = res
    da = _mm_call(g, jnp.transpose(bb), False, "mm_act_bwd_da", epi=a)
    db = _mm_call(a, g, True, "mm_act_bwd_db", act_a=True)
    return da, db


matmul_relu2.defvjp(_matmul_relu2_fwd, _matmul_relu2_bwd)


LN_ROW_TILE = 256


def _ln_group_map(group_rows, n_groups):
    per = group_rows // LN_ROW_TILE
    return lambda i: (jnp.minimum(i // per, n_groups - 1), 0, 0)


def _ln_stats(x, y, gate):
    pre = DEEPNORM_ALPHA * x + gate * y
    mu = jnp.mean(pre, axis=-1, keepdims=True)
    cen = pre - mu
    rstd = lax.rsqrt(jnp.mean(jnp.square(cen), axis=-1, keepdims=True) + LN_EPS)
    return cen * rstd, rstd


def _ln_mod_fwd_call(x, y, gate, gamma, beta, sc, sh, group_rows):
    r, d = x.shape
    ng = gate.shape[0]
    gmap = _ln_group_map(group_rows, ng)

    def body(x_ref, y_ref, gate_ref, gamma_ref, beta_ref, sc_ref, sh_ref, xn_ref, h_ref):
        xh, _ = _ln_stats(x_ref[...], y_ref[...], gate_ref[0])
        xn = xh * gamma_ref[...] + beta_ref[...]
        xn_ref[...] = xn
        h_ref[...] = xn * (1.0 + sc_ref[0]) + sh_ref[0]

    row = pl.BlockSpec((LN_ROW_TILE, d), lambda i: (i, 0))
    grp = pl.BlockSpec((1, 1, d), gmap)
    vec = pl.BlockSpec((1, d), lambda i: (0, 0))
    return _pcall(
        body, name="ln_mod_fwd", grid=(r // LN_ROW_TILE,),
        in_specs=[row, row, grp, vec, vec, grp, grp], out_specs=[row, row],
        out_shape=[jax.ShapeDtypeStruct((r, d), F32)] * 2,
        compiler_params=pltpu.CompilerParams(dimension_semantics=("parallel",)),
    )(x, y, gate, gamma, beta, sc, sh)


def _ln_mod_bwd_call(x, y, gate, gamma, beta, sc, dxn, dh, group_rows):
    r, d = x.shape
    ng = gate.shape[0]
    gmap = _ln_group_map(group_rows, ng)
    per = group_rows // LN_ROW_TILE

    def body(x_ref, y_ref, gate_ref, gamma_ref, beta_ref, sc_ref, dxn_ref, dh_ref,
             dx_ref, dy_ref, dgate_ref, dgamma_ref, dbeta_ref, dsc_ref, dsh_ref):
        i = pl.program_id(0)
        yv, gate_v, gamma_v = y_ref[...], gate_ref[0], gamma_ref[...]
        xh, rstd = _ln_stats(x_ref[...], yv, gate_v)
        dhv = dh_ref[...]
        dtot = dxn_ref[...] + dhv * (1.0 + sc_ref[0])
        dxh = dtot * gamma_v
        dpre = rstd * (dxh - jnp.mean(dxh, axis=-1, keepdims=True) - xh * jnp.mean(dxh * xh, axis=-1, keepdims=True))
        dx_ref[...] = DEEPNORM_ALPHA * dpre
        dy_ref[...] = gate_v * dpre
        col = lambda t: jnp.sum(t, axis=0, keepdims=True)

        @pl.when(i == 0)
        def _():
            dgamma_ref[...] = jnp.zeros_like(dgamma_ref)
            dbeta_ref[...] = jnp.zeros_like(dbeta_ref)

        first_of_group = (i % per == 0) | (i == (ng - 1) * per)

        @pl.when(first_of_group & (i <= (ng - 1) * per))
        def _():
            dgate_ref[...] = jnp.zeros_like(dgate_ref)
            dsc_ref[...] = jnp.zeros_like(dsc_ref)
            dsh_ref[...] = jnp.zeros_like(dsh_ref)

        dgamma_ref[...] += col(dtot * xh)
        dbeta_ref[...] += col(dtot)
        dgate_ref[0] += col(dpre * yv)
        dsc_ref[0] += col(dhv * (xh * gamma_v + beta_ref[...]))
        dsh_ref[0] += col(dhv)

    row = pl.BlockSpec((LN_ROW_TILE, d), lambda i: (i, 0))
    grp = pl.BlockSpec((1, 1, d), gmap)
    vec = pl.BlockSpec((1, d), lambda i: (0, 0))
    big = jax.ShapeDtypeStruct((r, d), F32)
    gs = jax.ShapeDtypeStruct((ng, 1, d), F32)
    vs = jax.ShapeDtypeStruct((1, d), F32)
    return _pcall(
        body, name="ln_mod_bwd", grid=(r // LN_ROW_TILE,),
        in_specs=[row, row, grp, vec, vec, grp, row, row],
        out_specs=[row, row, grp, vec, vec, grp, grp],
        out_shape=[big, big, gs, vs, vs, gs, gs],
        compiler_params=pltpu.CompilerParams(dimension_semantics=("arbitrary",)),
    )(x, y, gate, gamma, beta, sc, dxn, dh)


@functools.partial(jax.custom_vjp, nondiff_argnums=(7,))
def ln_mod(x, y, gate, gamma, beta, sc, sh, group_rows):
    return tuple(_ln_mod_fwd_call(x, y, gate, gamma, beta, sc, sh, group_rows))


def _ln_mod_fwd(x, y, gate, gamma, beta, sc, sh, group_rows):
    xn, h = _ln_mod_fwd_call(x, y, gate, gamma, beta, sc, sh, group_rows)
    return (xn, h), (x, y, gate, gamma, beta, sc)


def _ln_mod_bwd(group_rows, res, cts):
    x, y, gate, gamma, beta, sc = res
    dxn, dh = cts
    return tuple(_ln_mod_bwd_call(x, y, gate, gamma, beta, sc, dxn, dh, group_rows))


ln_mod.defvjp(_ln_mod_fwd, _ln_mod_bwd)


def _attn_probs(q, k, sink, scale, has_sink):
    s = _bdot(q, k, NT) * scale
    m = jnp.max(s, axis=-1, keepdims=True)
    if has_sink:
        m = jnp.maximum(m, sink)
    p = jnp.exp(s - m)
    den = jnp.sum(p, axis=-1, keepdims=True)
    p_sink = None
    if has_sink:
        p_sink = jnp.exp(sink - m)
        den = den + p_sink
    inv = 1.0 / den
    if has_sink:
        p_sink = p_sink * inv
    return p * inv, p_sink


def _attn_full_fwd_call(q, k, v, sink, scale, has_sink):
    g, sq, dq = q.shape
    nk, dv = k.shape[1], v.shape[2]
    bq = _pick(sq, (256, 128))

    def body(q_ref, k_ref, v_ref, sink_ref, o_ref):
        p, _ = _attn_probs(q_ref[0], k_ref[0], sink_ref[0, :, 0:1], scale, has_sink)
        o_ref[0] = _bdot(p, v_ref[0], NN)

    return _pcall(
        body, name="attn_full_fwd", grid=(g, sq // bq),
        in_specs=[pl.BlockSpec((1, bq, dq), lambda b, i: (b, i, 0)), pl.BlockSpec((1, nk, dq), lambda b, i: (b, 0, 0)),
                  pl.BlockSpec((1, nk, dv), lambda b, i: (b, 0, 0)), pl.BlockSpec((1, 1, 128), lambda b, i: (b, 0, 0))],
        out_specs=pl.BlockSpec((1, bq, dv), lambda b, i: (b, i, 0)),
        out_shape=jax.ShapeDtypeStruct((g, sq, dv), F32),
        compiler_params=pltpu.CompilerParams(dimension_semantics=("parallel", "arbitrary")),
    )(q, k, v, sink)


def _attn_full_bwd_call(q, k, v, sink, o, do, scale, has_sink):
    g, sq, dq = q.shape
    nk, dv = k.shape[1], v.shape[2]
    bq = _pick(sq, (256, 128))

    def body(q_ref, k_ref, v_ref, sink_ref, o_ref, do_ref, dq_ref, dk_ref, dv_ref, dsink_ref):
        i = pl.program_id(1)
        qv, kv, vv, dov = q_ref[0], k_ref[0], v_ref[0], do_ref[0]
        p, p_sink = _attn_probs(qv, kv, sink_ref[0, :, 0:1], scale, has_sink)
        delta = jnp.sum(dov * o_ref[0], axis=-1, keepdims=True)
        dv_part = _bdot(p, dov, TN)
        dp = _bdot(dov, vv, NT)
        ds = p * (dp - delta) * scale
        dq_ref[0] = _bdot(ds, kv, NN)
        dk_part = _bdot(ds, qv, TN)
        if has_sink:
            dsk = jnp.broadcast_to(-jnp.sum(p_sink * delta, axis=0, keepdims=True), (1, 128))
        else:
            dsk = jnp.zeros((1, 128), F32)

        @pl.when(i == 0)
        def _():
            dk_ref[0] = dk_part
            dv_ref[0] = dv_part
            dsink_ref[0] = dsk

        @pl.when(i > 0)
        def _():
            dk_ref[0] += dk_part
            dv_ref[0] += dv_part
            dsink_ref[0] += dsk

    qspec = pl.BlockSpec((1, bq, dq), lambda b, i: (b, i, 0))
    kspec = pl.BlockSpec((1, nk, dq), lambda b, i: (b, 0, 0))
    vspec = pl.BlockSpec((1, nk, dv), lambda b, i: (b, 0, 0))
    ospec = pl.BlockSpec((1, bq, dv), lambda b, i: (b, i, 0))
    sspec = pl.BlockSpec((1, 1, 128), lambda b, i: (b, 0, 0))
    return _pcall(
        body, name="attn_full_bwd", grid=(g, sq // bq),
        in_specs=[qspec, kspec, vspec, sspec, ospec, ospec],
        out_specs=[qspec, kspec, vspec, sspec],
        out_shape=[jax.ShapeDtypeStruct(q.shape, F32), jax.ShapeDtypeStruct(k.shape, F32),
                   jax.ShapeDtypeStruct(v.shape, F32), jax.ShapeDtypeStruct(sink.shape, F32)],
        compiler_params=pltpu.CompilerParams(dimension_semantics=("parallel", "arbitrary")),
    )(q, k, v, sink, o, do)


@functools.partial(jax.custom_vjp, nondiff_argnums=(4, 5))
def attn_full(q, k, v, sink, scale, has_sink):
    return _attn_full_fwd_call(q, k, v, sink, scale, has_sink)


def _attn_full_fwd(q, k, v, sink, scale, has_sink):
    o = _attn_full_fwd_call(q, k, v, sink, scale, has_sink)
    return o, (q, k, v, sink, o)


def _attn_full_bwd(scale, has_sink, res, do):
    q, k, v, sink, o = res
    dq, dk, dv, dsink = _attn_full_bwd_call(q, k, v, sink, o, do, scale, has_sink)
    return dq, dk, dv, _lane0(dsink)


attn_full.defvjp(_attn_full_fwd, _attn_full_bwd)


def _swa_probs(q, kw, kc, sink, i, s_len, scale):
    w = SWA_BLOCK
    s_loc = _bdot(q, kw, NT) * scale
    qpos = i * w + lax.broadcasted_iota(jnp.int32, (w, 3 * w), 0)
    kpos = (i - 1) * w + lax.broadcasted_iota(jnp.int32, (w, 3 * w), 1)
    valid = (jnp.abs(kpos - qpos) <= SWA_WINDOW) & (kpos >= 0) & (kpos < s_len)
    s_loc = jnp.where(valid, s_loc, -jnp.inf)
    s_ctx = _bdot(q, kc, NT) * scale
    m = jnp.maximum(jnp.maximum(jnp.max(s_loc, axis=-1, keepdims=True), jnp.max(s_ctx, axis=-1, keepdims=True)), sink)
    p_loc = jnp.exp(s_loc - m)
    p_ctx = jnp.exp(s_ctx - m)
    p_sink = jnp.exp(sink - m)
    inv = 1.0 / (jnp.sum(p_loc, axis=-1, keepdims=True) + jnp.sum(p_ctx, axis=-1, keepdims=True) + p_sink)
    return p_loc * inv, p_ctx * inv, p_sink * inv


def _swa_fwd_call(q, kp, vp, kc, vc, sink, scale):
    g, s_len, d = q.shape
    l_ctx = kc.shape[1]
    w = SWA_BLOCK
    grp = SWA_HEADS // SWA_KV_HEADS

    def body(q_ref, kp_ref, vp_ref, kc_ref, vc_ref, sink_ref, o_ref):
        i = pl.program_id(1)
        start = pl.multiple_of(i * w, w)
        kw = kp_ref[0, pl.ds(start, 3 * w), :]
        vw = vp_ref[0, pl.ds(start, 3 * w), :]
        p_loc, p_ctx, _ = _swa_probs(q_ref[0], kw, kc_ref[0], sink_ref[0, :, 0:1], i, s_len, scale)
        o_ref[0] = _bdot(p_loc, vw, NN) + _bdot(p_ctx, vc_ref[0], NN)

    return _pcall(
        body, name="swa_fwd", grid=(g, s_len // w),
        in_specs=[pl.BlockSpec((1, w, d), lambda b, i: (b, i, 0)),
                  pl.BlockSpec((1, s_len + 2 * w, d), lambda b, i: (b // grp, 0, 0)),
                  pl.BlockSpec((1, s_len + 2 * w, d), lambda b, i: (b // grp, 0, 0)),
                  pl.BlockSpec((1, l_ctx, d), lambda b, i: (b // grp, 0, 0)),
                  pl.BlockSpec((1, l_ctx, d), lambda b, i: (b // grp, 0, 0)),
                  pl.BlockSpec((1, 1, 128), lambda b, i: (b, 0, 0))],
        out_specs=pl.BlockSpec((1, w, d), lambda b, i: (b, i, 0)),
        out_shape=jax.ShapeDtypeStruct(q.shape, F32),
        compiler_params=pltpu.CompilerParams(dimension_semantics=("parallel", "arbitrary")),
    )(q, kp, vp, kc, vc, sink)


def _swa_bwd_call(q, kp, vp, kc, vc, sink, o, do, scale):
    g, s_len, d = q.shape
    l_ctx = kc.shape[1]
    w = SWA_BLOCK
    grp = SWA_HEADS // SWA_KV_HEADS
    sp = s_len + 2 * w

    def body(q_ref, kp_ref, vp_ref, kc_ref, vc_ref, sink_ref, o_ref, do_ref,
             dq_ref, dkp_ref, dvp_ref, dkc_ref, dvc_ref, dsink_ref):
        i = pl.program_id(1)
        start = pl.multiple_of(i * w, w)
        qv, dov = q_ref[0], do_ref[0]
        kw = kp_ref[0, pl.ds(start, 3 * w), :]
        vw = vp_ref[0, pl.ds(start, 3 * w), :]
        kcv, vcv = kc_ref[0], vc_ref[0]
        p_loc, p_ctx, p_sink = _swa_probs(qv, kw, kcv, sink_ref[0, :, 0:1], i, s_len, scale)
        delta = jnp.sum(dov * o_ref[0], axis=-1, keepdims=True)
        ds_loc = p_loc * (_bdot(dov, vw, NT) - delta) * scale
        ds_ctx = p_ctx * (_bdot(dov, vcv, NT) - delta) * scale
        dq_ref[0] = _bdot(ds_loc, kw, NN) + _bdot(ds_ctx, kcv, NN)
        dsk = jnp.broadcast_to(-jnp.sum(p_sink * delta, axis=0, keepdims=True), (1, 128))

        @pl.when(i == 0)
        def _():
            dkp_ref[...] = jnp.zeros_like(dkp_ref)
            dvp_ref[...] = jnp.zeros_like(dvp_ref)
            dkc_ref[...] = jnp.zeros_like(dkc_ref)
            dvc_ref[...] = jnp.zeros_like(dvc_ref)
            dsink_ref[...] = jnp.zeros_like(dsink_ref)

        dkp_ref[0, pl.ds(start, 3 * w), :] += _bdot(ds_loc, qv, TN)
        dvp_ref[0, pl.ds(start, 3 * w), :] += _bdot(p_loc, dov, TN)
        dkc_ref[0] += _bdot(ds_ctx, qv, TN)
        dvc_ref[0] += _bdot(p_ctx, dov, TN)
        dsink_ref[0] += dsk

    qspec = pl.BlockSpec((1, w, d), lambda b, i: (b, i, 0))
    kin = pl.BlockSpec((1, sp, d), lambda b, i: (b // grp, 0, 0))
    cin = pl.BlockSpec((1, l_ctx, d), lambda b, i: (b // grp, 0, 0))
    kout = pl.BlockSpec((1, sp, d), lambda b, i: (b, 0, 0))
    cout = pl.BlockSpec((1, l_ctx, d), lambda b, i: (b, 0, 0))
    sspec = pl.BlockSpec((1, 1, 128), lambda b, i: (b, 0, 0))
    return _pcall(
        body, name="swa_bwd", grid=(g, s_len // w),
        in_specs=[qspec, kin, kin, cin, cin, sspec, qspec, qspec],
        out_specs=[qspec, kout, kout, cout, cout, sspec],
        out_shape=[jax.ShapeDtypeStruct(q.shape, F32), jax.ShapeDtypeStruct((g, sp, d), F32),
                   jax.ShapeDtypeStruct((g, sp, d), F32), jax.ShapeDtypeStruct((g, l_ctx, d), F32),
                   jax.ShapeDtypeStruct((g, l_ctx, d), F32), jax.ShapeDtypeStruct(sink.shape, F32)],
        compiler_params=pltpu.CompilerParams(dimension_semantics=("parallel", "arbitrary")),
    )(q, kp, vp, kc, vc, sink, o, do)


@functools.partial(jax.custom_vjp, nondiff_argnums=(6,))
def swa_attn(q, kp, vp, kc, vc, sink, scale):
    return _swa_fwd_call(q, kp, vp, kc, vc, sink, scale)


def _swa_attn_fwd(q, kp, vp, kc, vc, sink, scale):
    o = _swa_fwd_call(q, kp, vp, kc, vc, sink, scale)
    return o, (q, kp, vp, kc, vc, sink, o)


def _swa_attn_bwd(scale, res, do):
    q, kp, vp, kc, vc, sink, o = res
    grp = SWA_HEADS // SWA_KV_HEADS
    dq, dkp, dvp, dkc, dvc, dsink = _swa_bwd_call(q, kp, vp, kc, vc, sink, o, do, scale)
    pair = lambda t: t.reshape(t.shape[0] // grp, grp, *t.shape[1:]).sum(axis=1)
    return dq, pair(dkp), pair(dvp), pair(dkc), pair(dvc), _lane0(dsink)


swa_attn.defvjp(_swa_attn_fwd, _swa_attn_bwd)


def _f32dot(a, b, dims):
    return lax.dot_general(a, b, dims, precision=lax.Precision.HIGHEST, preferred_element_type=F32)


DN_SOLVE_BLOCK = 16


def _unit_lower_inverse(a, a_t, transposed):
    g, c, _ = a.shape
    nb = DN_SOLVE_BLOCK
    row = lax.broadcasted_iota(jnp.int32, (1, c, c), 1)
    col = lax.broadcasted_iota(jnp.int32, (1, c, c), 2)
    src, off = (a, a_t) if transposed else (a_t, a)
    coef = jnp.zeros((g, c, nb), F32)
    for b in range(c // nb):
        in_block = (lax.broadcasted_iota(jnp.int32, (1, c, nb), 1) // nb) == b
        coef = coef + jnp.where(in_block, src[:, :, b * nb:(b + 1) * nb], 0.0)
    sub = lax.broadcasted_iota(jnp.int32, (1, c // nb, nb, c), 2)
    x = jnp.broadcast_to((row == col).astype(F32), a.shape)
    for i in (range(nb - 2, -1, -1) if transposed else range(1, nb)):
        prod = (coef[:, :, i:i + 1] * x).reshape(g, c // nb, nb, c)
        new_rows = -jnp.sum(prod, axis=2, keepdims=True)
        x = x + jnp.where(sub == i, new_rows, 0.0).reshape(g, c, c)
    width = nb
    while width < c:
        joins = ((row // (2 * width)) == (col // (2 * width))) & ((row // width) != (col // width))
        x = x - _f32dot(x, _f32dot(jnp.where(joins, off, 0.0), x, BNN), BNN)
        width *= 2
    return x


def _dn_masks(c):
    row = lax.broadcasted_iota(jnp.int32, (1, c, c), 1)
    col = lax.broadcasted_iota(jnp.int32, (1, c, c), 2)
    return row, col


def _dn_fwd_call(q, k, k_t, v, gc, bb, gr):
    g, n, c, _ = q.shape

    def body(q_ref, k_ref, kt_ref, v_ref, gc_ref, bb_ref, gr_ref, o_ref, vn_ref, sall_ref, w_ref, u_ref, s_scr):
        i = pl.program_id(0)

        @pl.when(i == 0)
        def _():
            s_scr[...] = jnp.zeros_like(s_scr)

        qv, kv, ktv, vv, gcv, bv, grv = (q_ref[:, 0], k_ref[:, 0], kt_ref[:, 0], v_ref[:, 0], gc_ref[:, 0],
                                          bb_ref[:, 0], gr_ref[:, 0])
        row, col = _dn_masks(c)
        e = jnp.exp(gcv)
        kb = kv * bv
        decay = jnp.exp(jnp.where(row >= col, gcv - grv, -jnp.inf))
        decay_ts = jnp.exp(jnp.where(row < col, grv - gcv, -jnp.inf))
        a_mat = _bdot(kb, kv, BNT) * jnp.where(row > col, decay, 0.0)
        t = _unit_lower_inverse(a_mat, _bdot(kv, kb, BNT) * decay_ts, False)
        w = _f32dot(t, kb * e, BNN)
        u = _f32dot(t, vv * bv, BNN)
        glast = grv[:, :, c - 1:c]
        s = s_scr[...]
        sall_ref[:, 0] = s
        vnew = u - _bdot(w, s, BNN)
        o_ref[:, 0] = _bdot(qv * e, s, BNN) + _bdot(_bdot(qv, kv, BNT) * decay, vnew, BNN)
        vn_ref[:, 0] = vnew
        w_ref[:, 0] = w
        u_ref[:, 0] = u
        s_scr[...] = s * jnp.exp(glast) + _bdot(ktv * jnp.exp(glast - grv), vnew, BNN)

    blk = pl.BlockSpec((g, 1, c, c), lambda i: (0, i, 0, 0))
    rblk = pl.BlockSpec((g, 1, 1, c), lambda i: (0, i, 0, 0))
    big = jax.ShapeDtypeStruct((g, n, c, c), F32)
    return _pcall(
        body, name="dn_fwd", grid=(n,),
        in_specs=[blk, blk, blk, blk, blk, blk, rblk],
        out_specs=[blk] * 5, out_shape=[big] * 5,
        scratch_shapes=[pltpu.VMEM((g, c, c), F32)],
        compiler_params=pltpu.CompilerParams(dimension_semantics=("arbitrary",)),
    )(q, k, k_t, v, gc, bb, gr)


def _dn_bwd_call(q, k, q_t, k_t, v, gc, bb, gr, br, sall, vn, w, u, do):
    g, n, c, _ = q.shape

    def body(q_ref, k_ref, qt_ref, kt_ref, v_ref, gc_ref, bb_ref, gr_ref, br_ref, sall_ref, vn_ref, w_ref, u_ref, do_ref,
             dq_ref, dk_ref, dv_ref, dgc_ref, dbb_ref, dgr_ref, ds_scr):
        i = pl.program_id(0)

        @pl.when(i == 0)
        def _():
            ds_scr[...] = jnp.zeros_like(ds_scr)

        qv, kv, qtv, ktv, vv, gcv, bv, grv, brv = (q_ref[:, 0], k_ref[:, 0], qt_ref[:, 0], kt_ref[:, 0], v_ref[:, 0],
                                                    gc_ref[:, 0], bb_ref[:, 0], gr_ref[:, 0], br_ref[:, 0])
        s, vnew, w, u, dov = sall_ref[:, 0], vn_ref[:, 0], w_ref[:, 0], u_ref[:, 0], do_ref[:, 0]
        dsn = ds_scr[...]
        row, col = _dn_masks(c)
        e = jnp.exp(gcv)
        er = jnp.exp(grv)
        kb = kv * bv
        decay = jnp.exp(jnp.where(row >= col, gcv - grv, -jnp.inf))
        decay_s = jnp.where(row > col, decay, 0.0)
        decay_t = jnp.exp(jnp.where(row <= col, grv - gcv, -jnp.inf))
        decay_ts = jnp.where(row < col, decay_t, 0.0)
        kk = _bdot(kb, kv, BNT)
        tt = _unit_lower_inverse(kk * decay_s, _bdot(kv, kb, BNT) * decay_ts, True)
        glast = grv[:, :, c - 1:c]
        eg = jnp.exp(glast)
        x = jnp.exp(glast - gcv)
        kt = kv * x
        qk_raw = _bdot(qv, kv, BNT)
        w_t = _f32dot(ktv * (brv * er), tt, BNN)
        dvn = _bdot(_bdot(kv, qv, BNT) * decay_t, dov, BNN) + _bdot(kt, dsn, BNN)
        dqk = _bdot(dov, vnew, BNT)
        dqk_t = _bdot(vnew, dov, BNT)
        dqd = _bdot(dov, s, BNT)
        dkt = _bdot(vnew, dsn, BNT)
        deg = jnp.sum(jnp.sum(dsn * s, axis=2, keepdims=True), axis=1, keepdims=True)
        dw = -_bdot(dvn, s, BNT)
        ds_scr[...] = dsn * eg + _bdot(qtv * er, dov, BNN) - _bdot(w_t, dvn, BNN)
        dwp = _f32dot(tt, dw, BNN)
        dup = _f32dot(tt, dvn, BNN)
        d_a = -(_bdot(dwp, w, BNT) + _bdot(dup, u, BNT))
        d_at = -(_bdot(w, dwp, BNT) + _bdot(u, dup, BNT))
        dkb = _bdot(d_a * decay_s, kv, BNN) + dwp * e
        dkx = dkt * kv * x
        dq_ref[:, 0] = dqd * e + _bdot(dqk * decay, kv, BNN)
        dk_ref[:, 0] = _bdot(d_at * decay_ts, kb, BNN) + dkb * bv + dkt * x + _bdot(dqk_t * decay_t, qv, BNN)
        dv_ref[:, 0] = dup * bv
        dbb_ref[:, 0] = dkb * kv + dup * vv
        ddiff = dqk * qk_raw * decay + d_a * kk * decay_s
        dgc_ref[:, 0] = ddiff + (dwp * kb + dqd * qv) * e - dkx
        dglast = jnp.sum(jnp.sum(dkx, axis=2, keepdims=True), axis=1, keepdims=True) + deg * eg
        lane = lax.broadcasted_iota(jnp.int32, (1, 1, c), 2)
        dgr_ref[:, 0] = jnp.where(lane == c - 1, dglast, 0.0) - jnp.sum(ddiff, axis=1, keepdims=True)

    blk = pl.BlockSpec((g, 1, c, c), lambda i: (0, n - 1 - i, 0, 0))
    rblk = pl.BlockSpec((g, 1, 1, c), lambda i: (0, n - 1 - i, 0, 0))
    big = jax.ShapeDtypeStruct((g, n, c, c), F32)
    return _pcall(
        body, name="dn_bwd", grid=(n,),
        in_specs=[blk] * 7 + [rblk, rblk] + [blk] * 5,
        out_specs=[blk] * 5 + [rblk],
        out_shape=[big] * 5 + [jax.ShapeDtypeStruct((g, n, 1, c), F32)],
        scratch_shapes=[pltpu.VMEM((g, c, c), F32)],
        compiler_params=pltpu.CompilerParams(dimension_semantics=("arbitrary",)),
    )(q, k, q_t, k_t, v, gc, bb, gr, br, sall, vn, w, u, do)


_t = lambda a: jnp.swapaxes(a, -1, -2)


def _dn_forms(gcum, beta, d):
    lanes = lambda t: jnp.broadcast_to(t[..., None], t.shape + (d,))
    return lanes(gcum), lanes(beta), gcum[:, :, None, :], beta[:, :, None, :]


@jax.custom_vjp
def dn_chunked(q, k, v, gcum, beta):
    gc, bb, gr, _ = _dn_forms(gcum, beta, q.shape[-1])
    return _dn_fwd_call(q, k, _t(k), v, gc, bb, gr)[0]


def _dn_chunked_fwd(q, k, v, gcum, beta):
    gc, bb, gr, _ = _dn_forms(gcum, beta, q.shape[-1])
    o, vn, sall, w, u = _dn_fwd_call(q, k, _t(k), v, gc, bb, gr)
    return o, (q, k, v, gcum, beta, vn, sall, w, u)


def _dn_chunked_bwd(res, do):
    q, k, v, gcum, beta, vn, sall, w, u = res
    gc, bb, gr, br = _dn_forms(gcum, beta, q.shape[-1])
    dq, dk, dv, dgc, dbb, dgr = _dn_bwd_call(q, k, _t(q), _t(k), v, gc, bb, gr, br, sall, vn, w, u, do)
    return dq, dk, dv, jnp.sum(dgc, axis=-1) + dgr[:, :, 0, :], jnp.sum(dbb, axis=-1)


dn_chunked.defvjp(_dn_chunked_fwd, _dn_chunked_bwd)


def _ret_fwd_call(q, k, k_t, v, dmat, xi_b, zeta_r, gm):
    g, n, c, dk = q.shape
    dv = v.shape[-1]

    def body(q_ref, k_ref, kt_ref, v_ref, d_ref, xib_ref, zr_ref, gm_ref, o_ref, starts_ref, s_scr):
        i = pl.program_id(0)

        @pl.when(i == 0)
        def _():
            s_scr[...] = jnp.zeros_like(s_scr)

        qv, vv = q_ref[:, 0], v_ref[:, 0]
        s = s_scr[...]
        starts_ref[:, 0] = s
        o_ref[:, 0] = _bdot(_bdot(qv, k_ref[:, 0], BNT) * d_ref[...], vv, BNN) + _bdot(qv * xib_ref[...], s, BNN)
        s_scr[...] = s * gm_ref[...] + _bdot(kt_ref[:, 0] * zr_ref[...], vv, BNN)

    tok = lambda d: pl.BlockSpec((g, 1, c, d), lambda i: (0, i, 0, 0))
    const = lambda a, b: pl.BlockSpec((g, a, b), lambda i: (0, 0, 0))
    return _pcall(
        body, name="ret_fwd", grid=(n,),
        in_specs=[tok(dk), tok(dk), pl.BlockSpec((g, 1, dk, c), lambda i: (0, i, 0, 0)), tok(dv),
                  const(c, c), const(c, dk), const(1, c), const(dk, dv)],
        out_specs=[tok(dv), pl.BlockSpec((g, 1, dk, dv), lambda i: (0, i, 0, 0))],
        out_shape=[jax.ShapeDtypeStruct((g, n, c, dv), F32), jax.ShapeDtypeStruct((g, n, dk, dv), F32)],
        scratch_shapes=[pltpu.VMEM((g, dk, dv), F32)],
        compiler_params=pltpu.CompilerParams(dimension_semantics=("arbitrary",)),
    )(q, k, k_t, v, dmat, xi_b, zeta_r, gm)


def _ret_bwd_call(q, k, q_t, k_t, v, dmat, dmat_t, xi_b, xi_r, zeta_b, gm, starts, do):
    g, n, c, dk = q.shape
    dv = v.shape[-1]

    def body(q_ref, k_ref, qt_ref, kt_ref, v_ref, d_ref, dt_ref, xib_ref, xr_ref, zb_ref, gm_ref, starts_ref, do_ref,
             dq_ref, dk_ref, dv_ref, dd_ref, dxib_ref, dzb_ref, dgm_ref, ds_scr):
        i = pl.program_id(0)

        @pl.when(i == 0)
        def _():
            ds_scr[...] = jnp.zeros_like(ds_scr)
            dd_ref[...] = jnp.zeros_like(dd_ref)
            dxib_ref[...] = jnp.zeros_like(dxib_ref)
            dzb_ref[...] = jnp.zeros_like(dzb_ref)
            dgm_ref[...] = jnp.zeros_like(dgm_ref)

        qv, kv, vv, dov = q_ref[:, 0], k_ref[:, 0], v_ref[:, 0], do_ref[:, 0]
        s, dsn = starts_ref[:, 0], ds_scr[...]
        dm, dmt, zb = d_ref[...], dt_ref[...], zb_ref[...]
        qk_raw = _bdot(qv, kv, BNT)
        dqkd = _bdot(dov, vv, BNT)
        do_s = _bdot(dov, s, BNT)
        dkz = _bdot(vv, dsn, BNT)
        dq_ref[:, 0] = _bdot(dqkd * dm, kv, BNN) + do_s * xib_ref[...]
        dk_ref[:, 0] = _bdot(_bdot(vv, dov, BNT) * dmt, qv, BNN) + dkz * zb
        dv_ref[:, 0] = _bdot(_bdot(kv, qv, BNT) * dmt, dov, BNN) + _bdot(kv * zb, dsn, BNN)
        dd_ref[...] += dqkd * qk_raw
        dxib_ref[...] += do_s * qv
        dzb_ref[...] += dkz * kv
        dgm_ref[...] += dsn * s
        ds_scr[...] = dsn * gm_ref[...] + _bdot(qt_ref[:, 0] * xr_ref[...], dov, BNN)

    tok = lambda d: pl.BlockSpec((g, 1, c, d), lambda i: (0, n - 1 - i, 0, 0))
    tok_t = pl.BlockSpec((g, 1, dk, c), lambda i: (0, n - 1 - i, 0, 0))
    const = lambda a, b: pl.BlockSpec((g, a, b), lambda i: (0, 0, 0))
    sds = lambda *s: jax.ShapeDtypeStruct(s, F32)
    return _pcall(
        body, name="ret_bwd", grid=(n,),
        in_specs=[tok(dk), tok(dk), tok_t, tok_t, tok(dv), const(c, c), const(c, c), const(c, dk), const(1, c),
                  const(c, dk), const(dk, dv), pl.BlockSpec((g, 1, dk, dv), lambda i: (0, n - 1 - i, 0, 0)), tok(dv)],
        out_specs=[tok(dk), tok(dk), tok(dv), const(c, c), const(c, dk), const(c, dk), const(dk, dv)],
        out_shape=[sds(g, n, c, dk), sds(g, n, c, dk), sds(g, n, c, dv), sds(g, c, c), sds(g, c, dk), sds(g, c, dk),
                   sds(g, dk, dv)],
        scratch_shapes=[pltpu.VMEM((g, dk, dv), F32)],
        compiler_params=pltpu.CompilerParams(dimension_semantics=("arbitrary",)),
    )(q, k, q_t, k_t, v, dmat, dmat_t, xi_b, xi_r, zeta_b, gm, starts, do)


def _ret_forms(xi, zeta, gm, dk, dv):
    lanes = lambda t: jnp.broadcast_to(t[..., None], t.shape + (dk,))
    return lanes(xi), xi[:, None, :], lanes(zeta), zeta[:, None, :], jnp.broadcast_to(gm[:, None, None], gm.shape + (dk, dv))


@jax.custom_vjp
def ret_chunked(q, k, v, dmat, xi, zeta, gm):
    xi_b, _, _, zeta_r, gm_f = _ret_forms(xi, zeta, gm, q.shape[-1], v.shape[-1])
    return _ret_fwd_call(q, k, _t(k), v, dmat, xi_b, zeta_r, gm_f)[0]


def _ret_chunked_fwd(q, k, v, dmat, xi, zeta, gm):
    xi_b, _, _, zeta_r, gm_f = _ret_forms(xi, zeta, gm, q.shape[-1], v.shape[-1])
    o, starts = _ret_fwd_call(q, k, _t(k), v, dmat, xi_b, zeta_r, gm_f)
    return o, (q, k, v, dmat, xi, zeta, gm, starts)


def _ret_chunked_bwd(res, do):
    q, k, v, dmat, xi, zeta, gm, starts = res
    xi_b, xi_r, zeta_b, _, gm_f = _ret_forms(xi, zeta, gm, q.shape[-1], v.shape[-1])
    dq, dk, dv, dd, dxib, dzb, dgm = _ret_bwd_call(q, k, _t(q), _t(k), v, dmat, _t(dmat), xi_b, xi_r, zeta_b, gm_f, starts, do)
    return dq, dk, dv, dd, jnp.sum(dxib, axis=-1), jnp.sum(dzb, axis=-1), jnp.sum(dgm, axis=(1, 2))


ret_chunked.defvjp(_ret_chunked_fwd, _ret_chunked_bwd)


def _peer(k):
    mx, my, mc = lax.axis_index("x"), lax.axis_index("y"), lax.axis_index("c")
    px = 1 - mx if k & 4 else mx
    py = 1 - my if k & 2 else my
    pc = 1 - mc if k & 1 else mc
    return (px, py, pc), 4 * px + 2 * py + pc


def _exchange_call(xs, all_to_all, name):
    n_arr = len(xs)
    n_peer = N_DEV - 1

    def body(*refs):
        x_refs, out_refs = refs[:n_arr], refs[n_arr:2 * n_arr]
        send_sems, recv_sems, local_sems = refs[2 * n_arr:]
        me = 4 * lax.axis_index("x") + 2 * lax.axis_index("y") + lax.axis_index("c")

        def copy(j, k, dst_idx):
            dev, idx = _peer(k)
            return pltpu.make_async_remote_copy(
                src_ref=x_refs[j].at[idx] if all_to_all else x_refs[j], dst_ref=out_refs[j].at[dst_idx],
                send_sem=send_sems.at[j * n_peer + k - 1], recv_sem=recv_sems.at[j * n_peer + k - 1],
                device_id=dev, device_id_type=pl.DeviceIdType.MESH)

        mine = [pltpu.make_async_copy(x_refs[j].at[me] if all_to_all else x_refs[j], out_refs[j].at[me], local_sems.at[j])
                for j in range(n_arr)]
        for cp in mine:
            cp.start()
        sends = [copy(j, k, me) for j in range(n_arr) for k in range(1, N_DEV)]
        for cp in sends:
            cp.start()
        for j in range(n_arr):
            for k in range(1, N_DEV):
                copy(j, k, _peer(k)[1]).wait_recv()
        for cp in sends:
            cp.wait_send()
        for cp in mine:
            cp.wait()

    return _pcall(
        body, name=name,
        in_specs=[pl.BlockSpec(memory_space=pl.ANY)] * n_arr, out_specs=[pl.BlockSpec(memory_space=pl.ANY)] * n_arr,
        out_shape=[jax.ShapeDtypeStruct((N_DEV,) + tuple(x.shape[1:] if all_to_all else x.shape), x.dtype) for x in xs],
        scratch_shapes=[pltpu.SemaphoreType.DMA((n_arr * n_peer,)), pltpu.SemaphoreType.DMA((n_arr * n_peer,)),
                        pltpu.SemaphoreType.DMA((n_arr,))],
    )(*xs)


def _sum8_call(x, name):
    _, r, c = x.shape
    tr = _pick(r, (256, 160, 128, 72, 64, 32, 16, 8))

    def body(x_ref, o_ref):
        acc = x_ref[0].astype(F32)
        for d in range(1, N_DEV):
            acc = acc + x_ref[d].astype(F32)
        o_ref[...] = acc

    return _pcall(
        body, name=name, grid=(r // tr,),
        in_specs=[pl.BlockSpec((N_DEV, tr, c), lambda i: (0, i, 0))],
        out_specs=pl.BlockSpec((tr, c), lambda i: (i, 0)),
        out_shape=jax.ShapeDtypeStruct((r, c), F32),
        compiler_params=pltpu.CompilerParams(dimension_semantics=("parallel",)),
    )(x)


def _adamw_call(w, g, m, v, name):
    r, c = w.shape
    tr = _pick(r, (256, 128, 64, 32, 16, 8))
    bc1 = 1.0 - ADAM_B1 ** ADAM_STEP
    bc2 = 1.0 - ADAM_B2 ** ADAM_STEP

    def body(w_ref, g_ref, m_ref, v_ref, d_ref, nm_ref, nv_ref):
        gv = g_ref[...]
        nm = ADAM_B1 * m_ref[...] + (1.0 - ADAM_B1) * gv
        nv = ADAM_B2 * v_ref[...] + (1.0 - ADAM_B2) * jnp.square(gv)
        d_ref[...] = -ADAM_LR * ((nm / bc1) / (jnp.sqrt(nv / bc2) + ADAM_EPS) + ADAM_WD * w_ref[...])
        nm_ref[...] = nm
        nv_ref[...] = nv

    spec = pl.BlockSpec((tr, c), lambda i: (i, 0))
    sds = jax.ShapeDtypeStruct((r, c), F32)
    return _pcall(
        body, name=name, grid=(r // tr,), in_specs=[spec] * 4, out_specs=[spec] * 3, out_shape=[sds] * 3,
        compiler_params=pltpu.CompilerParams(dimension_semantics=("parallel",)),
    )(w, g, m, v)


def layer_norm(x, g, b):
    mu = jnp.mean(x, axis=-1, keepdims=True)
    var = jnp.mean(jnp.square(x - mu), axis=-1, keepdims=True)
    return (x - mu) * lax.rsqrt(var + LN_EPS) * g + b


def rms_norm(x, g):
    return x * lax.rsqrt(jnp.mean(x * x, axis=-1, keepdims=True) + NORM_EPS) * g


def head_layer_norm(o, g):
    b_, t_, h_, d_ = o.shape
    mu = jnp.mean(o, axis=-1, keepdims=True)
    var = jnp.mean(jnp.square(o - mu), axis=-1, keepdims=True)
    return ((o - mu) * lax.rsqrt(var + NORM_EPS)).reshape(b_, t_, h_ * d_) * g


def l2norm(t):
    return t * lax.rsqrt(jnp.sum(t * t, axis=-1, keepdims=True) + NORM_EPS)


def rope_freqs(dim):
    return ROPE_BASE ** (-jnp.arange(0, dim, 2, dtype=F32) / dim)


def axial_rope(rows, rot_dim):
    row = jnp.broadcast_to(jnp.arange(rows, dtype=F32)[:, None], (rows, GRID_W)).reshape(-1)
    col = jnp.broadcast_to(jnp.arange(GRID_W, dtype=F32)[None, :], (rows, GRID_W)).reshape(-1)
    inv = rope_freqs(rot_dim // 2)
    ang = jnp.concatenate([row[:, None] * inv, col[:, None] * inv], axis=-1)
    return jnp.cos(ang), jnp.sin(ang)


def sequence_rope(n_tok, rot_dim):
    ang = jnp.arange(n_tok, dtype=F32)[:, None] * rope_freqs(rot_dim)
    return jnp.cos(ang), jnp.sin(ang)


def apply_rope(x, cos, sin):
    x1, x2 = jnp.split(x, 2, axis=-1)
    c = cos[:, None, :]
    s = sin[:, None, :]
    return jnp.concatenate([x1 * c - x2 * s, x1 * s + x2 * c], axis=-1)


def _flip_t(t):
    return jnp.flip(t, axis=2)


def _split_columns(z):
    idx = np.cumsum(np.array(IN_SPLITS))[:-1].tolist()
    return jnp.split(z, idx, axis=-1)


def short_conv(x, w):
    k_width, ch = w.shape
    pad = k_width // 2
    return lax.conv_general_dilated(x, w[:, None, :], window_strides=(1,), padding=[(pad, pad)],
                                    dimension_numbers=('NWC', 'WIO', 'NWC'), feature_group_count=ch)


def _to_heads(t, h, d):
    b_, t_, _ = t.shape
    return t.reshape(b_, t_, h, d).transpose(0, 2, 1, 3).reshape(b_ * h, t_, d)


def _from_heads(t, b_):
    g, t_, d = t.shape
    return t.reshape(b_, g // b_, t_, d).transpose(0, 2, 1, 3).reshape(b_, t_, (g // b_) * d)


def _lane_scalar(vals):
    return jnp.broadcast_to(vals[:, None, None], (vals.shape[0], 1, 128))


def swa_group(q, k, v, qc, kc, vc, sink, cos, sin, with_ctx_out):
    b_, s_, _ = q.shape
    l_ = kc.shape[1]
    grp = SWA_HEADS // SWA_KV_HEADS
    d = SWA_HEAD_DIM
    w_ = SWA_BLOCK
    scale = d ** -0.5
    qh = apply_rope(q.reshape(b_, s_, SWA_HEADS, d), cos, sin).transpose(0, 2, 1, 3).reshape(b_ * SWA_HEADS, s_, d)
    kh = apply_rope(k.reshape(b_, s_, SWA_KV_HEADS, d), cos, sin).transpose(0, 2, 1, 3).reshape(b_ * SWA_KV_HEADS, s_, d)
    vh = _to_heads(v, SWA_KV_HEADS, d)
    kch = _to_heads(kc, SWA_KV_HEADS, d)
    vch = _to_heads(vc, SWA_KV_HEADS, d)
    padk = lambda t: jnp.pad(t, ((0, 0), (w_, w_), (0, 0)))
    sink_g = _lane_scalar(jnp.tile(sink, b_))
    y = _from_heads(swa_attn(qh, padk(kh), padk(vh), kch, vch, sink_g, scale), b_)
    yc = None
    if with_ctx_out:
        qch = _to_heads(qc, SWA_HEADS, d)
        rep = lambda t: jnp.repeat(t.reshape(b_, SWA_KV_HEADS, l_, d), grp, axis=1).reshape(b_ * SWA_HEADS, l_, d)
        yc = _from_heads(attn_full(qch, rep(kch), rep(vch), sink_g, scale, True), b_)
    return y, yc


def gated_delta_chunked(q, k, v, log_g, beta):
    g_, t_, dk = k.shape
    dv = v.shape[-1]
    c_ = DN_CHUNK
    n = t_ // c_
    assert dk == c_ and dv == c_
    g_cum = jnp.cumsum(log_g.reshape(g_, n, c_), axis=-1)
    o = dn_chunked(q.reshape(g_, n, c_, dk), k.reshape(g_, n, c_, dk), v.reshape(g_, n, c_, dv), g_cum,
                   beta.reshape(g_, n, c_))
    return o.reshape(g_, t_, dv)


def deltanet_group(qkv, z, ab, qkv_c, z_c, ab_c, conv_w, a_log, dt_bias, norm_g, with_ctx_out):
    def prep(qkv_, ab_):
        b_, t_, _ = qkv_.shape
        y = jax.nn.silu(short_conv(qkv_, conv_w))
        q, k, v = [t.reshape(b_, t_, DN_HEADS, DN_HEAD_DIM).transpose(0, 2, 1, 3) for t in jnp.split(y, 3, axis=-1)]
        q = l2norm(q) * DN_HEAD_DIM ** -0.5
        k = l2norm(k)
        ab_ = ab_.reshape(b_, t_, 2, 2, DN_HEADS)
        log_g = -jnp.exp(a_log) * jax.nn.softplus(ab_[:, :, :, 0] + dt_bias)
        beta = jax.nn.sigmoid(ab_[:, :, :, 1])
        return q, k, v, log_g.transpose(2, 0, 3, 1), beta.transpose(2, 0, 3, 1)

    def out(o, z_):
        b_, t_, _ = z_.shape
        o = rms_norm(o.transpose(0, 2, 1, 3), norm_g) * jax.nn.silu(z_).reshape(b_, t_, DN_HEADS, DN_HEAD_DIM)
        return o.reshape(b_, t_, DN_W)

    qc, kc, vc, lgc, bc = prep(qkv_c, ab_c)
    q, k, v, lg, bt = prep(qkv, ab)
    b_, l_, s_ = qkv.shape[0], qkv_c.shape[1], qkv.shape[1]
    seq = lambda tc, tl: jnp.stack([jnp.concatenate([tc, tl], axis=2),
                                    jnp.concatenate([_flip_t(tc), _flip_t(tl)], axis=2)])
    seq_g = lambda tc, tl: jnp.stack([jnp.concatenate([tc[0], tl[0]], axis=2),
                                      jnp.concatenate([_flip_t(tc[1]), _flip_t(tl[1])], axis=2)])
    flat = lambda t: t.reshape((2 * b_ * DN_HEADS,) + t.shape[3:])
    o = gated_delta_chunked(flat(seq(qc, q)), flat(seq(kc, k)), flat(seq(vc, v)), flat(seq_g(lgc, lg)), flat(seq_g(bc, bt)))
    o = o.reshape(2, b_, DN_HEADS, l_ + s_, DN_HEAD_DIM)
    y = out(o[0][:, :, l_:] + _flip_t(o[1][:, :, l_:]), z)
    yc = out(o[0][:, :, :l_] + _flip_t(o[1][:, :, :l_]), z_c) if with_ctx_out else None
    return y, yc


def retention_core(q, k, v, log_gamma):
    _, b_, h_, t_, dk = q.shape
    dv = v.shape[-1]
    c_ = RET_CHUNK
    n = t_ // c_
    g_ = 2 * b_ * h_
    pos = jnp.arange(c_, dtype=F32)
    per_g = lambda t: jnp.broadcast_to(t[:, None], (2, b_) + t.shape[1:]).reshape((g_,) + t.shape[2:])
    zeta = jnp.exp((c_ - 1 - pos) * log_gamma[..., None])
    xi = jnp.exp((pos + 1.0) * log_gamma[..., None])
    rel = pos[:, None] - pos[None, :]
    dmat = jnp.where(rel >= 0, jnp.exp(jnp.maximum(rel, 0.0) * log_gamma[..., None, None]), 0.0)
    gm = jnp.exp(c_ * log_gamma)
    o = ret_chunked(q.reshape(g_, n, c_, dk), k.reshape(g_, n, c_, dk), v.reshape(g_, n, c_, dv),
                    per_g(dmat), per_g(xi), per_g(zeta), per_g(gm))
    return o.reshape(2, b_, h_, t_, dv)


def retention_group(q, k, v, g, qc, kc, vc, gc, log1m_gamma, norm_g, cos, sin, with_ctx_out):
    log_gamma = jnp.log1p(-jnp.exp(log1m_gamma))
    heads = lambda t, dh: t.reshape(t.shape[0], t.shape[1], RET_HEADS, dh)
    bhtd = lambda t: t.transpose(0, 2, 1, 3)
    sc = RET_QK_DIM ** -0.5
    l_ = kc.shape[1]
    q = bhtd(apply_rope(heads(q, RET_QK_DIM), cos, sin)) * sc
    k = bhtd(apply_rope(heads(k, RET_QK_DIM), cos, sin))
    v = bhtd(heads(v, RET_V_DIM))
    kc = bhtd(heads(kc, RET_QK_DIM))
    vc = bhtd(heads(vc, RET_V_DIM))
    qcs = bhtd(heads(qc, RET_QK_DIM)) * sc

    def out(o, g_):
        return head_layer_norm(o.transpose(0, 2, 1, 3), norm_g) * jax.nn.silu(g_)

    seq = lambda tc, tl: jnp.stack([jnp.concatenate([tc, tl], axis=2),
                                    jnp.concatenate([_flip_t(tc), _flip_t(tl)], axis=2)])
    o = retention_core(seq(qcs, q), seq(kc, k), seq(vc, v), log_gamma)
    o_f, o_b = o[0], o[1]
    y = out(o_f[:, :, l_:] + _flip_t(o_b[:, :, l_:]), g)
    yc = out(o_f[:, :, :l_] + _flip_t(o_b[:, :, :l_]), gc) if with_ctx_out else None
    return y, yc


def mla_group(cq, ckv, kr, cq_c, ckv_c, kr_c, q_norm, w_uq, kv_norm, w_ukv, cos, sin, with_ctx_out):
    b_, s_, _ = cq.shape
    l_ = cq_c.shape[1]
    dqk = MLA_NOPE_DIM + MLA_ROPE_DIM
    rows = lambda tl, tc: jnp.concatenate([tl.reshape(b_ * s_, -1), tc.reshape(b_ * l_, -1)], axis=0)
    qa = matmul(rms_norm(rows(cq, cq_c), q_norm), w_uq)
    kva = matmul(rms_norm(rows(ckv, ckv_c), kv_norm), w_ukv)
    q = qa[:b_ * s_].reshape(b_, s_, MLA_HEADS, dqk)
    qc = qa[b_ * s_:].reshape(b_, l_, MLA_HEADS, dqk)
    q = jnp.concatenate([q[..., :MLA_NOPE_DIM], apply_rope(q[..., MLA_NOPE_DIM:], cos, sin)], axis=-1)
    kv = kva[:b_ * s_].reshape(b_, s_, MLA_HEADS, MLA_NOPE_DIM + MLA_V_DIM)
    kvc = kva[b_ * s_:].reshape(b_, l_, MLA_HEADS, MLA_NOPE_DIM + MLA_V_DIM)
    kr = apply_rope(kr[:, :, None, :], cos, sin)
    k = jnp.concatenate([kv[..., :MLA_NOPE_DIM], jnp.broadcast_to(kr, (b_, s_, MLA_HEADS, MLA_ROPE_DIM))], axis=-1)
    kc = jnp.concatenate([kvc[..., :MLA_NOPE_DIM],
                          jnp.broadcast_to(kr_c[:, :, None, :], (b_, l_, MLA_HEADS, MLA_ROPE_DIM))], axis=-1)
    v, vc = kv[..., MLA_NOPE_DIM:], kvc[..., MLA_NOPE_DIM:]
    hd = lambda t: t.transpose(0, 2, 1, 3).reshape(b_ * MLA_HEADS, t.shape[1], t.shape[3])
    scale = dqk ** -0.5
    no_sink = jnp.zeros((b_ * MLA_HEADS, 1, 128), F32)
    kch, vch = hd(kc), hd(vc)
    y = attn_full(hd(q), jnp.concatenate([hd(k), kch], axis=1), jnp.concatenate([hd(v), vch], axis=1), no_sink, scale, False)
    y = _from_heads(y, b_)
    yc = _from_heads(attn_full(hd(qc), kch, vch, no_sink, scale, False), b_) if with_ctx_out else None
    return y, yc


def token_mixers(zl, zc, p, layer, rope, with_ctx_out):
    (a_q, a_k, a_v, b_qkv, b_z, b_ab, c_q, c_k, c_v, c_g, d_cq, d_ckv, d_kr) = _split_columns(zl)
    (a_qc, a_kc, a_vc, b_qkvc, b_zc, b_abc, c_qc, c_kc, c_vc, c_gc, d_cqc, d_ckvc, d_krc) = _split_columns(zc)
    swa_cos, swa_sin, ret_cos, ret_sin, mla_cos, mla_sin = rope
    ya, yac = swa_group(a_q, a_k, a_v, a_qc, a_kc, a_vc, p['swa_sink'][layer], swa_cos, swa_sin, with_ctx_out)
    yb, ybc = deltanet_group(b_qkv, b_z, b_ab, b_qkvc, b_zc, b_abc, p['dn_conv_w'][layer], p['dn_a_log'][layer],
                             p['dn_dt_bias'][layer], p['dn_norm_g'][layer], with_ctx_out)
    yr, yrc = retention_group(c_q, c_k, c_v, c_g, c_qc, c_kc, c_vc, c_gc, p['ret_log1m_gamma'][layer],
                              p['ret_norm_g'][layer], ret_cos, ret_sin, with_ctx_out)
    yd, ydc = mla_group(d_cq, d_ckv, d_kr, d_cqc, d_ckvc, d_krc, p['mla_q_norm'][layer], p['mla_w_uq'][layer],
                        p['mla_kv_norm'][layer], p['mla_w_ukv'][layer], mla_cos, mla_sin, with_ctx_out)
    y = jnp.concatenate([ya, yb, yr, yd], axis=-1)
    yc = jnp.concatenate([yac, ybc, yrc, ydc], axis=-1) if with_ctx_out else None
    return y, yc


def local_loss(p, x, ctx, loss_target):
    b_, n_tok, d_ = x.shape
    l_ = ctx.shape[1]
    rows = n_tok // GRID_W
    rope = (*axial_rope(rows, SWA_HEAD_DIM), *sequence_rope(n_tok, RET_QK_DIM), *axial_rope(rows, MLA_ROPE_DIM))
    rl, rc = b_ * n_tok, b_ * l_
    mods = [jnp.concatenate([p['mod'][layer], p['cmod'][layer][None]], axis=0) for layer in range(DEPTH)]
    part = lambda layer, j: mods[layer][:, j * d_:(j + 1) * d_][:, None, :]
    vec = lambda name, layer: p[name][layer][None, :]
    xr = jnp.concatenate([x.reshape(rl, d_), ctx.reshape(rc, d_)], axis=0)
    sh1, sc1 = part(0, 0), part(0, 1)
    h = jnp.concatenate([(x * (1 + sc1[:b_]) + sh1[:b_]).reshape(rl, d_), (ctx * (1 + sc1[b_]) + sh1[b_]).reshape(rc, d_)],
                        axis=0)
    for layer in range(DEPTH):
        with_ctx_out = layer < DEPTH - 1
        g1, sh2, sc2, g2 = part(layer, 2), part(layer, 3), part(layer, 4), part(layer, 5)
        z = matmul(h, p['w_in'][layer])
        zl = z[:rl, :IN_WIDTH].reshape(b_, n_tok, IN_WIDTH)
        zc = z[rl:, :IN_WIDTH].reshape(b_, l_, IN_WIDTH)
        y, yc = token_mixers(zl, zc, p, layer, rope, with_ctx_out)
        if with_ctx_out:
            yo = matmul(jnp.concatenate([y.reshape(rl, d_), yc.reshape(rc, d_)], axis=0), p['w_out'][layer])
            xr, h2 = ln_mod(xr, yo, g1, vec('ln1_g', layer), vec('ln1_b', layer), sc2, sh2, n_tok)
            f = matmul_relu2(matmul(h2, p['w_ff1'][layer]), p['w_ff2'][layer])
            xr, h = ln_mod(xr, f, g2, vec('ln2_g', layer), vec('ln2_b', layer), part(layer + 1, 1), part(layer + 1, 0), n_tok)
        else:
            lat = lambda t: t[:b_]
            yo = matmul(y.reshape(rl, d_), p['w_out'][layer])
            xl, h2 = ln_mod(xr[:rl], yo, lat(g1), vec('ln1_g', layer), vec('ln1_b', layer), lat(sc2), lat(sh2), n_tok)
            f = matmul_relu2(matmul(h2, p['w_ff1'][layer]), p['w_ff2'][layer])
            none = jnp.zeros((b_, 1, d_), F32)
            xl, _ = ln_mod(xl, f, lat(g2), vec('ln2_g', layer), vec('ln2_b', layer), none, none, n_tok)
    err = jnp.square(xl - loss_target.reshape(rl, d_))
    return 0.5 * jnp.sum(jnp.mean(err, axis=-1))


def _shard_shape(shape, axis):
    s = list(shape)
    s[axis] //= N_DEV
    return tuple(s)


def _join_shards(pieces, axis):
    _, _, r, c = pieces.shape
    if axis == 0:
        full = pieces.transpose(1, 0, 2, 3).reshape(DEPTH, N_DEV * r, c)
    else:
        full = pieces.transpose(1, 2, 0, 3).reshape(DEPTH, r, N_DEV * c)
    return full.astype(F32)


def _split_shards(g, shape, axis):
    r, c = _shard_shape(shape, axis)
    if axis == 0:
        pieces = g.reshape(DEPTH, N_DEV, r, c).transpose(1, 0, 2, 3)
    else:
        pieces = g.reshape(DEPTH, r, N_DEV, c).transpose(2, 0, 1, 3)
    return pieces.astype(BF16)


def _pad_vec(vec, rows_multiple=8):
    n = vec.shape[0]
    rows = -(-n // (128 * rows_multiple)) * rows_multiple
    return jnp.pad(vec, (0, rows * 128 - n)).reshape(rows, 128)


def _adamw(w, g, m, v, name):
    shape = w.shape
    if w.ndim >= 2 and shape[-1] >= 128:
        as2 = lambda t: t.reshape(-1, shape[-1])
        d, nm, nv = _adamw_call(as2(w), as2(g), as2(m), as2(v), name)
        return d.reshape(shape), nm.reshape(shape), nv.reshape(shape)
    n = int(np.prod(shape))
    as2 = lambda t: _pad_vec(t.reshape(-1))
    d, nm, nv = _adamw_call(as2(w), as2(g), as2(m), as2(v), name)
    un = lambda t: t.reshape(-1)[:n].reshape(shape)
    return un(d), un(nm), un(nv)


def kernel(x, c, ctx, c_ctx, ada_w, ada_b, w_in, swa_sink, dn_conv_w, dn_a_log, dn_dt_bias, dn_norm_g, ret_log1m_gamma, ret_norm_g, mla_q_norm, mla_w_uq, mla_kv_norm, mla_w_ukv, w_out, ln1_g, ln1_b, w_ff1, w_ff2, ln2_g, ln2_b, loss_target, m_c_ctx, m_ada_w, m_ada_b, m_w_in, m_swa_sink, m_dn_conv_w, m_dn_a_log, m_dn_dt_bias, m_dn_norm_g, m_ret_log1m_gamma, m_ret_norm_g, m_mla_q_norm, m_mla_w_uq, m_mla_kv_norm, m_mla_w_ukv, m_w_out, m_ln1_g, m_ln1_b, m_w_ff1, m_w_ff2, m_ln2_g, m_ln2_b, v_c_ctx, v_ada_w, v_ada_b, v_w_in, v_swa_sink, v_dn_conv_w, v_dn_a_log, v_dn_dt_bias, v_dn_norm_g, v_ret_log1m_gamma, v_ret_norm_g, v_mla_q_norm, v_mla_w_uq, v_mla_kv_norm, v_mla_w_ukv, v_w_out, v_ln1_g, v_ln1_b, v_w_ff1, v_w_ff2, v_ln2_g, v_ln2_b):
    a = dict(zip(ARG_NAMES, (x, c, ctx, c_ctx, ada_w, ada_b, w_in, swa_sink, dn_conv_w, dn_a_log, dn_dt_bias, dn_norm_g, ret_log1m_gamma, ret_norm_g, mla_q_norm, mla_w_uq, mla_kv_norm, mla_w_ukv, w_out, ln1_g, ln1_b, w_ff1, w_ff2, ln2_g, ln2_b, loss_target, m_c_ctx, m_ada_w, m_ada_b, m_w_in, m_swa_sink, m_dn_conv_w, m_dn_a_log, m_dn_dt_bias, m_dn_norm_g, m_ret_log1m_gamma, m_ret_norm_g, m_mla_q_norm, m_mla_w_uq, m_mla_kv_norm, m_mla_w_ukv, m_w_out, m_ln1_g, m_ln1_b, m_w_ff1, m_w_ff2, m_ln2_g, m_ln2_b, v_c_ctx, v_ada_w, v_ada_b, v_w_in, v_swa_sink, v_dn_conv_w, v_dn_a_log, v_dn_dt_bias, v_dn_norm_g, v_ret_log1m_gamma, v_ret_norm_g, v_mla_q_norm, v_mla_w_uq, v_mla_kv_norm, v_mla_w_ukv, v_w_out, v_ln1_g, v_ln1_b, v_w_ff1, v_w_ff2, v_ln2_g, v_ln2_b)))
    me = 4 * lax.axis_index("x") + 2 * lax.axis_index("y") + lax.axis_index("c")
    b_loc = x.shape[0]
    n_ex = N_DEV * b_loc
    conv_k, conv_c = dn_conv_w.shape[1], dn_conv_w.shape[2]
    ada_cols = ada_w.shape[2]

    small_in = jnp.concatenate([c.reshape(-1), dn_conv_w.reshape(-1)])
    gathered = _exchange_call([_pad_vec(small_in)] + [a[name].astype(BF16) for name, _, _ in BIG], False, "gather_weights")
    small_all = gathered[0].reshape(N_DEV, -1)
    c_all = small_all[:, :b_loc * D_MODEL].reshape(n_ex, D_MODEL)
    conv_all = small_all[:, b_loc * D_MODEL:b_loc * D_MODEL + DEPTH * conv_k * conv_c].reshape(N_DEV, DEPTH, conv_k, conv_c)
    conv_full = conv_all.transpose(1, 2, 0, 3).reshape(DEPTH, conv_k, N_DEV * conv_c)
    big = {name: _join_shards(pieces, axis) for (name, _, axis), pieces in zip(BIG, gathered[1:])}
    big['w_in'] = jnp.pad(big['w_in'], ((0, 0), (0, 0), (0, IN_WIDTH_PAD - IN_WIDTH)))

    n_rows = -(-(n_ex + 1) // 16) * 16
    silu_cc = jax.nn.silu(c_ctx)
    a_rows = jnp.concatenate([jax.nn.silu(c_all), silu_cc[None], jnp.zeros((n_rows - n_ex - 1, D_MODEL), F32)], axis=0)
    m_loc = jnp.concatenate([_mm_call(a_rows, ada_w[l], False, "ada_fwd") for l in range(DEPTH)], axis=0)
    m_all = _exchange_call([m_loc], False, "gather_mod")[0].reshape(N_DEV, DEPTH, n_rows, ada_cols)
    mod_full = m_all.transpose(1, 2, 0, 3).reshape(DEPTH, n_rows, N_DEV * ada_cols) + ada_b[:, None, :]
    mod = lax.dynamic_slice_in_dim(mod_full, me * b_loc, b_loc, axis=1)
    cmod = mod_full[:, n_ex]

    p = dict(big)
    p.update(mod=mod, cmod=cmod, dn_conv_w=conv_full)
    for name in SMALL:
        p[name] = a[name]
    loss_loc, (gp, gx) = jax.value_and_grad(local_loss, argnums=(0, 1))(p, x, ctx, loss_target)
    loss = lax.psum(loss_loc, MESH_AXES)

    gp['w_in'] = gp['w_in'][:, :, :IN_WIDTH]
    arrived = _exchange_call([_split_shards(gp[name], shape, axis) for name, shape, axis in BIG], True, "scatter_grads")
    g_big = {}
    for (name, shape, axis), part in zip(BIG, arrived):
        r, c_ = _shard_shape(shape, axis)
        g_big[name] = _sum8_call(part.reshape(N_DEV, DEPTH * r, c_), "sum_" + name).reshape(DEPTH, r, c_)

    d_loc = jnp.concatenate([gp['mod'], gp['cmod'][:, None, :]], axis=1).reshape(DEPTH * (b_loc + 1), -1)
    d_loc = jnp.pad(d_loc, ((0, 8 - DEPTH * (b_loc + 1)), (0, 0)))
    d_all = _exchange_call([d_loc], False, "gather_dmod")[0][:, :DEPTH * (b_loc + 1)].reshape(N_DEV, DEPTH, b_loc + 1, -1)
    d_rows = d_all[:, :, :b_loc].transpose(1, 0, 2, 3).reshape(DEPTH, n_ex, -1)
    d_crow = d_all[0, :, b_loc]
    for d in range(1, N_DEV):
        d_crow = d_crow + d_all[d, :, b_loc]
    dm_full = jnp.concatenate([d_rows, d_crow[:, None, :], jnp.zeros((DEPTH, n_rows - n_ex - 1, d_rows.shape[-1]), F32)], axis=1)
    g_ada_b = jnp.sum(dm_full, axis=1)
    dm_mine = lax.dynamic_slice_in_dim(dm_full, me * ada_cols, ada_cols, axis=2)
    g_ada_w = jnp.stack([_mm_call(a_rows, dm_mine[l], True, "ada_bwd_w") for l in range(DEPTH)])
    crow8 = jnp.concatenate([dm_mine[:, n_ex:n_ex + 1], jnp.zeros((DEPTH, 15, ada_cols), F32)], axis=1)
    dsilu_part = sum(_mm_call(crow8[l], jnp.transpose(ada_w[l]), False, "ada_bwd_c")[0] for l in range(DEPTH))

    small_g = jnp.concatenate([gp[name].reshape(-1) for name in SMALL] + [gp['dn_conv_w'].reshape(-1), dsilu_part])
    small_sum = _sum8_call(_exchange_call([_pad_vec(small_g)], False, "gather_small_grads")[0], "sum_small_grads").reshape(-1)
    g_all, off = {}, 0
    for name in SMALL:
        n = int(np.prod(a[name].shape))
        g_all[name] = small_sum[off:off + n].reshape(a[name].shape)
        off += n
    n = DEPTH * conv_k * N_DEV * conv_c
    g_conv_full = small_sum[off:off + n].reshape(DEPTH, conv_k, N_DEV * conv_c)
    g_all['dn_conv_w'] = lax.dynamic_slice_in_dim(g_conv_full, me * conv_c, conv_c, axis=2)
    off += n
    dsilu = small_sum[off:off + D_MODEL]
    sig = jax.nn.sigmoid(c_ctx)
    g_all['c_ctx'] = dsilu * (sig * (1 + c_ctx * (1 - sig)))
    g_all['ada_w'] = g_ada_w
    g_all['ada_b'] = g_ada_b
    g_all.update(g_big)

    delta, new_m, new_v = {}, {}, {}
    for name in WEIGHTS:
        delta[name], new_m[name], new_v[name] = _adamw(a[name], g_all[name], a['m_' + name], a['v_' + name], "adamw_" + name)
    return (loss, gx, *[g_all[n] for n in WEIGHTS], *[delta[n] for n in WEIGHTS],
            *[new_m[n] for n in WEIGHTS], *[new_v[n] for n in WEIGHTS])
```

```python
import functools
import math

import jax
import jax.numpy as jnp
import numpy as np
from jax import lax
from jax.experimental import pallas as pl
from jax.experimental.pallas import tpu as pltpu

F32 = jnp.float32
BF16 = jnp.bfloat16
N_DEV = 8
MESH_AXES = ("x", "y", "c")

D_MODEL = 1024
DEPTH = 2
GRID_W = 64
SWA_HEADS, SWA_KV_HEADS, SWA_HEAD_DIM, SWA_WINDOW, SWA_BLOCK = 4, 2, 64, 128, 128
DN_HEADS, DN_HEAD_DIM, DN_CHUNK = 4, 64, 64
RET_HEADS, RET_QK_DIM, RET_V_DIM, RET_CHUNK = 4, 32, 64, 64
MLA_HEADS, MLA_Q_RANK, MLA_KV_RANK, MLA_NOPE_DIM, MLA_ROPE_DIM, MLA_V_DIM = 4, 256, 128, 64, 32, 64
D_FF = 4 * D_MODEL
ROPE_BASE = 10000.0
NORM_EPS = 1e-6
LN_EPS = 1e-5
DEEPNORM_ALPHA = (2 * DEPTH) ** 0.25
SWA_Q = SWA_HEADS * SWA_HEAD_DIM
SWA_KV = SWA_KV_HEADS * SWA_HEAD_DIM
DN_W = DN_HEADS * DN_HEAD_DIM
RET_QK = RET_HEADS * RET_QK_DIM
RET_V = RET_HEADS * RET_V_DIM
IN_SPLITS = (SWA_Q, SWA_KV, SWA_KV, 3 * DN_W, DN_W, 4 * DN_HEADS, RET_QK, RET_QK, RET_V, RET_V,
             MLA_Q_RANK, MLA_KV_RANK, MLA_ROPE_DIM)
IN_WIDTH = sum(IN_SPLITS)
IN_WIDTH_PAD = -(-IN_WIDTH // 128) * 128

ADAM_LR, ADAM_B1, ADAM_B2, ADAM_EPS, ADAM_WD, ADAM_STEP = 0.001, 0.9, 0.999, 1e-08, 0.01, 10

WEIGHTS = ['c_ctx', 'ada_w', 'ada_b', 'w_in', 'swa_sink', 'dn_conv_w', 'dn_a_log', 'dn_dt_bias', 'dn_norm_g',
           'ret_log1m_gamma', 'ret_norm_g', 'mla_q_norm', 'mla_w_uq', 'mla_kv_norm', 'mla_w_ukv', 'w_out', 'ln1_g',
           'ln1_b', 'w_ff1', 'w_ff2', 'ln2_g', 'ln2_b']
FWD_INPUTS = ['x', 'c', 'ctx'] + WEIGHTS
ARG_NAMES = FWD_INPUTS + ['loss_target'] + ['m_' + n for n in WEIGHTS] + ['v_' + n for n in WEIGHTS]

BIG = (('w_in', (D_MODEL, IN_WIDTH), 1), ('w_out', (D_MODEL, D_MODEL), 0), ('w_ff1', (D_MODEL, D_FF), 1),
       ('w_ff2', (D_FF, D_MODEL), 0), ('mla_w_uq', (MLA_Q_RANK, MLA_HEADS * (MLA_NOPE_DIM + MLA_ROPE_DIM)), 1),
       ('mla_w_ukv', (MLA_KV_RANK, MLA_HEADS * (MLA_NOPE_DIM + MLA_V_DIM)), 1))
SMALL = ('swa_sink', 'dn_a_log', 'dn_dt_bias', 'dn_norm_g', 'ret_log1m_gamma', 'ret_norm_g', 'mla_q_norm',
         'mla_kv_norm', 'ln1_g', 'ln1_b', 'ln2_g', 'ln2_b')


def _pcall(body, **kw):
    return pl.pallas_call(body, **kw)


def _pick(n, cands):
    for cand in cands:
        if n % cand == 0:
            return cand
    return n


def _bdot(a, b, dims):
    return lax.dot_general(a.astype(BF16), b.astype(BF16), dims, preferred_element_type=F32)


def _lane0(t):
    return jnp.where(lax.broadcasted_iota(jnp.int32, t.shape, t.ndim - 1) == 0, t, 0.0)


NN = (((1,), (0,)), ((), ()))
NT = (((1,), (1,)), ((), ()))
TN = (((0,), (0,)), ((), ()))
BNN = (((2,), (1,)), ((0,), (0,)))
BNT = (((2,), (2,)), ((0,), (0,)))


MM_ROW_TILE_MAX = 1088
MM_COL_TILE_MAX = 1408
MM_TOKEN_TILE_MAX = 544
VMEM_LIMIT_MAX = 60 * 1024 * 1024


def _tile(n, cap, align):
    best = None
    for t in range(align, min(n, cap) + 1, align):
        if n % t == 0:
            best = t
    return best or n


def _relu2(t):
    return jnp.square(jnp.maximum(t, 0.0))


def _mm_call(a, b, trans_a, name, act_a=False, epi=None):
    if trans_a:
        kdim, m = a.shape
        tk = _tile(kdim, MM_TOKEN_TILE_MAX, 8)
        tm = _tile(m, 1024, 128)
    else:
        m, kdim = a.shape
        tk = _tile(kdim, MM_COL_TILE_MAX, 128)
        tm = _tile(m, MM_ROW_TILE_MAX, 8)
    n = b.shape[1]
    assert b.shape[0] == kdim
    tn = _tile(n, MM_COL_TILE_MAX, 128)
    nk = kdim // tk

    def body(*refs):
        a_ref, b_ref = refs[0], refs[1]
        e_ref = refs[2] if epi is not None else None
        o_ref = refs[-1]
        k = pl.program_id(2)
        av = a_ref[...]
        if act_a:
            av = _relu2(av)
        part = _bdot(av, b_ref[...], TN if trans_a else NN)

        def finish(t):
            return t * (2.0 * jnp.maximum(e_ref[...], 0.0)) if epi is not None else t

        if nk == 1:
            o_ref[...] = finish(part)
        else:
            @pl.when(k == 0)
            def _():
                o_ref[...] = part

            @pl.when((k > 0) & (k < nk - 1))
            def _():
                o_ref[...] += part

            @pl.when(k == nk - 1)
            def _():
                o_ref[...] = finish(o_ref[...] + part)

    if trans_a:
        a_spec = pl.BlockSpec((tk, tm), lambda i, j, k: (k, i))
    else:
        a_spec = pl.BlockSpec((tm, tk), lambda i, j, k: (i, k))
    o_spec = pl.BlockSpec((tm, tn), lambda i, j, k: (i, j))
    in_specs = [a_spec, pl.BlockSpec((tk, tn), lambda i, j, k: (k, j))] + ([o_spec] if epi is not None else [])
    tiles = tm * tk * a.dtype.itemsize + tk * tn * b.dtype.itemsize + tm * tn * 4 * (2 if epi is not None else 1)
    temps = tm * tk * (2 + (4 if act_a else 0)) + tk * tn * 2 + 2 * tm * tn * 4
    return _pcall(
        body, name=name, grid=(m // tm, n // tn, nk), in_specs=in_specs, out_specs=o_spec,
        out_shape=jax.ShapeDtypeStruct((m, n), F32),
        compiler_params=pltpu.CompilerParams(dimension_semantics=("parallel", "parallel", "arbitrary"),
                                             vmem_limit_bytes=min(2 * tiles + temps + (4 << 20), VMEM_LIMIT_MAX)),
    )(*((a, b) + ((epi,) if epi is not None else ())))


@jax.custom_vjp
def matmul(a, b):
    return _mm_call(a, b.astype(BF16), False, "mm_fwd")


def _matmul_fwd(a, b):
    bb = b.astype(BF16)
    return _mm_call(a, bb, False, "mm_fwd"), (a, bb)


def _matmul_bwd(res, g):
    a, bb = res
    da = _mm_call(g, jnp.transpose(bb), False, "mm_bwd_da")
    db = _mm_call(a, g, True, "mm_bwd_db")
    return da, db


matmul.defvjp(_matmul_fwd, _matmul_bwd)


@jax.custom_vjp
def matmul_relu2(a, b):
    return _mm_call(a, b.astype(BF16), False, "mm_act_fwd", act_a=True)


def _matmul_relu2_fwd(a, b):
    bb = b.astype(BF16)
    return _mm_call(a, bb, False, "mm_act_fwd", act_a=True), (a, bb)


def _matmul_relu2_bwd(res, g):
    a, bb = res
    da = _mm_call(g, jnp.transpose(bb), False, "mm_act_bwd_da", epi=a)
    db = _mm_call(a, g, True, "mm_act_bwd_db", act_a=True)
    return da, db


matmul_relu2.defvjp(_matmul_relu2_fwd, _matmul_relu2_bwd)


LN_ROW_TILE = 256


def _ln_group_map(group_rows, n_groups):
    per = group_rows // LN_ROW_TILE
    return lambda i: (jnp.minimum(i // per, n_groups - 1), 0, 0)


def _ln_stats(x, y, gate):
    pre = DEEPNORM_ALPHA * x + gate * y
    mu = jnp.mean(pre, axis=-1, keepdims=True)
    cen = pre - mu
    rstd = lax.rsqrt(jnp.mean(jnp.square(cen), axis=-1, keepdims=True) + LN_EPS)
    return cen * rstd, rstd


def _ln_mod_fwd_call(x, y, gate, gamma, beta, sc, sh, group_rows):
    r, d = x.shape
    ng = gate.shape[0]
    gmap = _ln_group_map(group_rows, ng)

    def body(x_ref, y_ref, gate_ref, gamma_ref, beta_ref, sc_ref, sh_ref, xn_ref, h_ref):
        xh, _ = _ln_stats(x_ref[...], y_ref[...], gate_ref[0])
        xn = xh * gamma_ref[...] + beta_ref[...]
        xn_ref[...] = xn
        h_ref[...] = xn * (1.0 + sc_ref[0]) + sh_ref[0]

    row = pl.BlockSpec((LN_ROW_TILE, d), lambda i: (i, 0))
    grp = pl.BlockSpec((1, 1, d), gmap)
    vec = pl.BlockSpec((1, d), lambda i: (0, 0))
    return _pcall(
        body, name="ln_mod_fwd", grid=(r // LN_ROW_TILE,),
        in_specs=[row, row, grp, vec, vec, grp, grp], out_specs=[row, row],
        out_shape=[jax.ShapeDtypeStruct((r, d), F32)] * 2,
        compiler_params=pltpu.CompilerParams(dimension_semantics=("parallel",)),
    )(x, y, gate, gamma, beta, sc, sh)


def _ln_mod_bwd_call(x, y, gate, gamma, beta, sc, dxn, dh, group_rows):
    r, d = x.shape
    ng = gate.shape[0]
    gmap = _ln_group_map(group_rows, ng)
    per = group_rows // LN_ROW_TILE

    def body(x_ref, y_ref, gate_ref, gamma_ref, beta_ref, sc_ref, dxn_ref, dh_ref,
             dx_ref, dy_ref, dgate_ref, dgamma_ref, dbeta_ref, dsc_ref, dsh_ref):
        i = pl.program_id(0)
        yv, gate_v, gamma_v = y_ref[...], gate_ref[0], gamma_ref[...]
        xh, rstd = _ln_stats(x_ref[...], yv, gate_v)
        dhv = dh_ref[...]
        dtot = dxn_ref[...] + dhv * (1.0 + sc_ref[0])
        dxh = dtot * gamma_v
        dpre = rstd * (dxh - jnp.mean(dxh, axis=-1, keepdims=True) - xh * jnp.mean(dxh * xh, axis=-1, keepdims=True))
        dx_ref[...] = DEEPNORM_ALPHA * dpre
        dy_ref[...] = gate_v * dpre
        col = lambda t: jnp.sum(t, axis=0, keepdims=True)

        @pl.when(i == 0)
        def _():
            dgamma_ref[...] = jnp.zeros_like(dgamma_ref)
            dbeta_ref[...] = jnp.zeros_like(dbeta_ref)

        first_of_group = (i % per == 0) | (i == (ng - 1) * per)

        @pl.when(first_of_group & (i <= (ng - 1) * per))
        def _():
            dgate_ref[...] = jnp.zeros_like(dgate_ref)
            dsc_ref[...] = jnp.zeros_like(dsc_ref)
            dsh_ref[...] = jnp.zeros_like(dsh_ref)

        dgamma_ref[...] += col(dtot * xh)
        dbeta_ref[...] += col(dtot)
        dgate_ref[0] += col(dpre * yv)
        dsc_ref[0] += col(dhv * (xh * gamma_v + beta_ref[...]))
        dsh_ref[0] += col(dhv)

    row = pl.BlockSpec((LN_ROW_TILE, d), lambda i: (i, 0))
    grp = pl.BlockSpec((1, 1, d), gmap)
    vec = pl.BlockSpec((1, d), lambda i: (0, 0))
    big = jax.ShapeDtypeStruct((r, d), F32)
    gs = jax.ShapeDtypeStruct((ng, 1, d), F32)
    vs = jax.ShapeDtypeStruct((1, d), F32)
    return _pcall(
        body, name="ln_mod_bwd", grid=(r // LN_ROW_TILE,),
        in_specs=[row, row, grp, vec, vec, grp, row, row],
        out_specs=[row, row, grp, vec, vec, grp, grp],
        out_shape=[big, big, gs, vs, vs, gs, gs],
        compiler_params=pltpu.CompilerParams(dimension_semantics=("arbitrary",)),
    )(x, y, gate, gamma, beta, sc, dxn, dh)


@functools.partial(jax.custom_vjp, nondiff_argnums=(7,))
def ln_mod(x, y, gate, gamma, beta, sc, sh, group_rows):
    return tuple(_ln_mod_fwd_call(x, y, gate, gamma, beta, sc, sh, group_rows))


def _ln_mod_fwd(x, y, gate, gamma, beta, sc, sh, group_rows):
    xn, h = _ln_mod_fwd_call(x, y, gate, gamma, beta, sc, sh, group_rows)
    return (xn, h), (x, y, gate, gamma, beta, sc)


def _ln_mod_bwd(group_rows, res, cts):
    x, y, gate, gamma, beta, sc = res
    dxn, dh = cts
    return tuple(_ln_mod_bwd_call(x, y, gate, gamma, beta, sc, dxn, dh, group_rows))


ln_mod.defvjp(_ln_mod_fwd, _ln_mod_bwd)


def _attn_probs(q, k, sink, scale, has_sink):
    s = _bdot(q, k, NT) * scale
    m = jnp.max(s, axis=-1, keepdims=True)
    if has_sink:
        m = jnp.maximum(m, sink)
    p = jnp.exp(s - m)
    den = jnp.sum(p, axis=-1, keepdims=True)
    p_sink = None
    if has_sink:
        p_sink = jnp.exp(sink - m)
        den = den + p_sink
    inv = 1.0 / den
    if has_sink:
        p_sink = p_sink * inv
    return p * inv, p_sink


def _attn_full_fwd_call(q, k, v, sink, scale, has_sink):
    g, sq, dq = q.shape
    nk, dv = k.shape[1], v.shape[2]
    bq = _pick(sq, (256, 128))

    def body(q_ref, k_ref, v_ref, sink_ref, o_ref):
        p, _ = _attn_probs(q_ref[0], k_ref[0], sink_ref[0, :, 0:1], scale, has_sink)
        o_ref[0] = _bdot(p, v_ref[0], NN)

    return _pcall(
        body, name="attn_full_fwd", grid=(g, sq // bq),
        in_specs=[pl.BlockSpec((1, bq, dq), lambda b, i: (b, i, 0)), pl.BlockSpec((1, nk, dq), lambda b, i: (b, 0, 0)),
                  pl.BlockSpec((1, nk, dv), lambda b, i: (b, 0, 0)), pl.BlockSpec((1, 1, 128), lambda b, i: (b, 0, 0))],
        out_specs=pl.BlockSpec((1, bq, dv), lambda b, i: (b, i, 0)),
        out_shape=jax.ShapeDtypeStruct((g, sq, dv), F32),
        compiler_params=pltpu.CompilerParams(dimension_semantics=("parallel", "arbitrary")),
    )(q, k, v, sink)


def _attn_full_bwd_call(q, k, v, sink, o, do, scale, has_sink):
    g, sq, dq = q.shape
    nk, dv = k.shape[1], v.shape[2]
    bq = _pick(sq, (256, 128))

    def body(q_ref, k_ref, v_ref, sink_ref, o_ref, do_ref, dq_ref, dk_ref, dv_ref, dsink_ref):
        i = pl.program_id(1)
        qv, kv, vv, dov = q_ref[0], k_ref[0], v_ref[0], do_ref[0]
        p, p_sink = _attn_probs(qv, kv, sink_ref[0, :, 0:1], scale, has_sink)
        delta = jnp.sum(dov * o_ref[0], axis=-1, keepdims=True)
        dv_part = _bdot(p, dov, TN)
        dp = _bdot(dov, vv, NT)
        ds = p * (dp - delta) * scale
        dq_ref[0] = _bdot(ds, kv, NN)
        dk_part = _bdot(ds, qv, TN)
        if has_sink:
            dsk = jnp.broadcast_to(-jnp.sum(p_sink * delta, axis=0, keepdims=True), (1, 128))
        else:
            dsk = jnp.zeros((1, 128), F32)

        @pl.when(i == 0)
        def _():
            dk_ref[0] = dk_part
            dv_ref[0] = dv_part
            dsink_ref[0] = dsk

        @pl.when(i > 0)
        def _():
            dk_ref[0] += dk_part
            dv_ref[0] += dv_part
            dsink_ref[0] += dsk

    qspec = pl.BlockSpec((1, bq, dq), lambda b, i: (b, i, 0))
    kspec = pl.BlockSpec((1, nk, dq), lambda b, i: (b, 0, 0))
    vspec = pl.BlockSpec((1, nk, dv), lambda b, i: (b, 0, 0))
    ospec = pl.BlockSpec((1, bq, dv), lambda b, i: (b, i, 0))
    sspec = pl.BlockSpec((1, 1, 128), lambda b, i: (b, 0, 0))
    return _pcall(
        body, name="attn_full_bwd", grid=(g, sq // bq),
        in_specs=[qspec, kspec, vspec, sspec, ospec, ospec],
        out_specs=[qspec, kspec, vspec, sspec],
        out_shape=[jax.ShapeDtypeStruct(q.shape, F32), jax.ShapeDtypeStruct(k.shape, F32),
                   jax.ShapeDtypeStruct(v.shape, F32), jax.ShapeDtypeStruct(sink.shape, F32)],
        compiler_params=pltpu.CompilerParams(dimension_semantics=("parallel", "arbitrary")),
    )(q, k, v, sink, o, do)


@functools.partial(jax.custom_vjp, nondiff_argnums=(4, 5))
def attn_full(q, k, v, sink, scale, has_sink):
    return _attn_full_fwd_call(q, k, v, sink, scale, has_sink)


def _attn_full_fwd(q, k, v, sink, scale, has_sink):
    o = _attn_full_fwd_call(q, k, v, sink, scale, has_sink)
    return o, (q, k, v, sink, o)


def _attn_full_bwd(scale, has_sink, res, do):
    q, k, v, sink, o = res
    dq, dk, dv, dsink = _attn_full_bwd_call(q, k, v, sink, o, do, scale, has_sink)
    return dq, dk, dv, _lane0(dsink)


attn_full.defvjp(_attn_full_fwd, _attn_full_bwd)


def _swa_probs(q, kw, kc, sink, i, s_len, scale):
    w = SWA_BLOCK
    s_loc = _bdot(q, kw, NT) * scale
    qpos = i * w + lax.broadcasted_iota(jnp.int32, (w, 3 * w), 0)
    kpos = (i - 1) * w + lax.broadcasted_iota(jnp.int32, (w, 3 * w), 1)
    valid = (jnp.abs(kpos - qpos) <= SWA_WINDOW) & (kpos >= 0) & (kpos < s_len)
    s_loc = jnp.where(valid, s_loc, -jnp.inf)
    s_ctx = _bdot(q, kc, NT) * scale
    m = jnp.maximum(jnp.maximum(jnp.max(s_loc, axis=-1, keepdims=True), jnp.max(s_ctx, axis=-1, keepdims=True)), sink)
    p_loc = jnp.exp(s_loc - m)
    p_ctx = jnp.exp(s_ctx - m)
    p_sink = jnp.exp(sink - m)
    inv = 1.0 / (jnp.sum(p_loc, axis=-1, keepdims=True) + jnp.sum(p_ctx, axis=-1, keepdims=True) + p_sink)
    return p_loc * inv, p_ctx * inv, p_sink * inv


def _swa_fwd_call(q, kp, vp, kc, vc, sink, scale):
    g, s_len, d = q.shape
    l_ctx = kc.shape[1]
    w = SWA_BLOCK
    grp = SWA_HEADS // SWA_KV_HEADS

    def body(q_ref, kp_ref, vp_ref, kc_ref, vc_ref, sink_ref, o_ref):
        i = pl.program_id(1)
        start = pl.multiple_of(i * w, w)
        kw = kp_ref[0, pl.ds(start, 3 * w), :]
        vw = vp_ref[0, pl.ds(start, 3 * w), :]
        p_loc, p_ctx, _ = _swa_probs(q_ref[0], kw, kc_ref[0], sink_ref[0, :, 0:1], i, s_len, scale)
        o_ref[0] = _bdot(p_loc, vw, NN) + _bdot(p_ctx, vc_ref[0], NN)

    return _pcall(
        body, name="swa_fwd", grid=(g, s_len // w),
        in_specs=[pl.BlockSpec((1, w, d), lambda b, i: (b, i, 0)),
                  pl.BlockSpec((1, s_len + 2 * w, d), lambda b, i: (b // grp, 0, 0)),
                  pl.BlockSpec((1, s_len + 2 * w, d), lambda b, i: (b // grp, 0, 0)),
                  pl.BlockSpec((1, l_ctx, d), lambda b, i: (b // grp, 0, 0)),
                  pl.BlockSpec((1, l_ctx, d), lambda b, i: (b // grp, 0, 0)),
                  pl.BlockSpec((1, 1, 128), lambda b, i: (b, 0, 0))],
        out_specs=pl.BlockSpec((1, w, d), lambda b, i: (b, i, 0)),
        out_shape=jax.ShapeDtypeStruct(q.shape, F32),
        compiler_params=pltpu.CompilerParams(dimension_semantics=("parallel", "arbitrary")),
    )(q, kp, vp, kc, vc, sink)


def _swa_bwd_call(q, kp, vp, kc, vc, sink, o, do, scale):
    g, s_len, d = q.shape
    l_ctx = kc.shape[1]
    w = SWA_BLOCK
    grp = SWA_HEADS // SWA_KV_HEADS
    sp = s_len + 2 * w

    def body(q_ref, kp_ref, vp_ref, kc_ref, vc_ref, sink_ref, o_ref, do_ref,
             dq_ref, dkp_ref, dvp_ref, dkc_ref, dvc_ref, dsink_ref):
        i = pl.program_id(1)
        start = pl.multiple_of(i * w, w)
        qv, dov = q_ref[0], do_ref[0]
        kw = kp_ref[0, pl.ds(start, 3 * w), :]
        vw = vp_ref[0, pl.ds(start, 3 * w), :]
        kcv, vcv = kc_ref[0], vc_ref[0]
        p_loc, p_ctx, p_sink = _swa_probs(qv, kw, kcv, sink_ref[0, :, 0:1], i, s_len, scale)
        delta = jnp.sum(dov * o_ref[0], axis=-1, keepdims=True)
        ds_loc = p_loc * (_bdot(dov, vw, NT) - delta) * scale
        ds_ctx = p_ctx * (_bdot(dov, vcv, NT) - delta) * scale
        dq_ref[0] = _bdot(ds_loc, kw, NN) + _bdot(ds_ctx, kcv, NN)
        dsk = jnp.broadcast_to(-jnp.sum(p_sink * delta, axis=0, keepdims=True), (1, 128))

        @pl.when(i == 0)
        def _():
            dkp_ref[...] = jnp.zeros_like(dkp_ref)
            dvp_ref[...] = jnp.zeros_like(dvp_ref)
            dkc_ref[...] = jnp.zeros_like(dkc_ref)
            dvc_ref[...] = jnp.zeros_like(dvc_ref)
            dsink_ref[...] = jnp.zeros_like(dsink_ref)

        dkp_ref[0, pl.ds(start, 3 * w), :] += _bdot(ds_loc, qv, TN)
        dvp_ref[0, pl.ds(start, 3 * w), :] += _bdot(p_loc, dov, TN)
        dkc_ref[0] += _bdot(ds_ctx, qv, TN)
        dvc_ref[0] += _bdot(p_ctx, dov, TN)
        dsink_ref[0] += dsk

    qspec = pl.BlockSpec((1, w, d), lambda b, i: (b, i, 0))
    kin = pl.BlockSpec((1, sp, d), lambda b, i: (b // grp, 0, 0))
    cin = pl.BlockSpec((1, l_ctx, d), lambda b, i: (b // grp, 0, 0))
    kout = pl.BlockSpec((1, sp, d), lambda b, i: (b, 0, 0))
    cout = pl.BlockSpec((1, l_ctx, d), lambda b, i: (b, 0, 0))
    sspec = pl.BlockSpec((1, 1, 128), lambda b, i: (b, 0, 0))
    return _pcall(
        body, name="swa_bwd", grid=(g, s_len // w),
        in_specs=[qspec, kin, kin, cin, cin, sspec, qspec, qspec],
        out_specs=[qspec, kout, kout, cout, cout, sspec],
        out_shape=[jax.ShapeDtypeStruct(q.shape, F32), jax.ShapeDtypeStruct((g, sp, d), F32),
                   jax.ShapeDtypeStruct((g, sp, d), F32), jax.ShapeDtypeStruct((g, l_ctx, d), F32),
                   jax.ShapeDtypeStruct((g, l_ctx, d), F32), jax.ShapeDtypeStruct(sink.shape, F32)],
        compiler_params=pltpu.CompilerParams(dimension_semantics=("parallel", "arbitrary")),
    )(q, kp, vp, kc, vc, sink, o, do)


@functools.partial(jax.custom_vjp, nondiff_argnums=(6,))
def swa_attn(q, kp, vp, kc, vc, sink, scale):
    return _swa_fwd_call(q, kp, vp, kc, vc, sink, scale)


def _swa_attn_fwd(q, kp, vp, kc, vc, sink, scale):
    o = _swa_fwd_call(q, kp, vp, kc, vc, sink, scale)
    return o, (q, kp, vp, kc, vc, sink, o)


def _swa_attn_bwd(scale, res, do):
    q, kp, vp, kc, vc, sink, o = res
    grp = SWA_HEADS // SWA_KV_HEADS
    dq, dkp, dvp, dkc, dvc, dsink = _swa_bwd_call(q, kp, vp, kc, vc, sink, o, do, scale)
    pair = lambda t: t.reshape(t.shape[0] // grp, grp, *t.shape[1:]).sum(axis=1)
    return dq, pair(dkp), pair(dvp), pair(dkc), pair(dvc), _lane0(dsink)


swa_attn.defvjp(_swa_attn_fwd, _swa_attn_bwd)


def _f32dot(a, b, dims):
    return lax.dot_general(a, b, dims, precision=lax.Precision.HIGHEST, preferred_element_type=F32)


DN_SOLVE_BLOCK = 16


def _unit_lower_inverse(a, a_t, transposed):
    g, c, _ = a.shape
    nb = DN_SOLVE_BLOCK
    row = lax.broadcasted_iota(jnp.int32, (1, c, c), 1)
    col = lax.broadcasted_iota(jnp.int32, (1, c, c), 2)
    src, off = (a, a_t) if transposed else (a_t, a)
    coef = jnp.zeros((g, c, nb), F32)
    for b in range(c // nb):
        in_block = (lax.broadcasted_iota(jnp.int32, (1, c, nb), 1) // nb) == b
        coef = coef + jnp.where(in_block, src[:, :, b * nb:(b + 1) * nb], 0.0)
    sub = lax.broadcasted_iota(jnp.int32, (1, c // nb, nb, c), 2)
    x = jnp.broadcast_to((row == col).astype(F32), a.shape)
    for i in (range(nb - 2, -1, -1) if transposed else range(1, nb)):
        prod = (coef[:, :, i:i + 1] * x).reshape(g, c // nb, nb, c)
        new_rows = -jnp.sum(prod, axis=2, keepdims=True)
        x = x + jnp.where(sub == i, new_rows, 0.0).reshape(g, c, c)
    width = nb
    while width < c:
        joins = ((row // (2 * width)) == (col // (2 * width))) & ((row // width) != (col // width))
        x = x - _f32dot(x, _f32dot(jnp.where(joins, off, 0.0), x, BNN), BNN)
        width *= 2
    return x


def _dn_masks(c):
    row = lax.broadcasted_iota(jnp.int32, (1, c, c), 1)
    col = lax.broadcasted_iota(jnp.int32, (1, c, c), 2)
    return row, col


def _sweep_chunks(n, n_ctx):
    return (lambda i: i), (lambda i: jnp.where(i < n_ctx, n_ctx - 1 - i, n + n_ctx - 1 - i))


def _half_spec(gh, tail, half, chunk_of):
    return pl.BlockSpec((gh, 1) + tail, lambda i: (half, chunk_of(i), 0, 0))


def _both(ref_f, ref_b):
    return jnp.concatenate([ref_f[:, 0], ref_b[:, 0]], axis=0)


def _dn_direction_masks(g, c):
    backward = lax.broadcasted_iota(jnp.int32, (g, 1, 1), 0) >= g // 2
    row, col = _dn_masks(c)
    return backward, jnp.where(backward, c - 1 - row, row), jnp.where(backward, c - 1 - col, col)


def _dn_inverse(a_mat, a_t, transposed):
    h = a_mat.shape[0] // 2
    return jnp.concatenate([_unit_lower_inverse(a_mat[:h], a_t[:h], transposed),
                            _unit_lower_inverse(a_t[h:], a_mat[h:], not transposed)], axis=0)


def _dn_fwd_call(q, k, k_t, v, gc, bb, gr, n_ctx):
    gh, n, c, _ = q.shape
    g = 2 * gh

    def body(qf, qb, kf, kb_, ktf, ktb, vf, vb, gcf, gcb, bbf, bbb, grf, grb,
             of_ref, ob_ref, vn_ref, sall_ref, w_ref, u_ref, s_scr):
        i = pl.program_id(0)

        @pl.when(i == 0)
        def _():
            s_scr[...] = jnp.zeros_like(s_scr)

        qv, kv, ktv, vv, gcv, bv, grv = (_both(qf, qb), _both(kf, kb_), _both(ktf, ktb), _both(vf, vb), _both(gcf, gcb),
                                          _both(bbf, bbb), _both(grf, grb))
        backward, row, col = _dn_direction_masks(g, c)
        e = jnp.exp(gcv)
        kb = kv * bv
        decay = jnp.exp(jnp.where(row >= col, gcv - grv, -jnp.inf))
        decay_ts = jnp.exp(jnp.where(row < col, grv - gcv, -jnp.inf))
        a_mat = _bdot(kb, kv, BNT) * jnp.where(row > col, decay, 0.0)
        t = _dn_inverse(a_mat, _bdot(kv, kb, BNT) * decay_ts, False)
        w = _f32dot(t, kb * e, BNN)
        u = _f32dot(t, vv * bv, BNN)
        glast = jnp.where(backward, grv[:, :, 0:1], grv[:, :, c - 1:c])
        s = s_scr[...]
        sall_ref[:, 0] = s
        vnew = u - _bdot(w, s, BNN)
        o = _bdot(qv * e, s, BNN) + _bdot(_bdot(qv, kv, BNT) * decay, vnew, BNN)
        of_ref[:, 0] = o[:gh]
        ob_ref[:, 0] = o[gh:]
        vn_ref[:, 0] = vnew
        w_ref[:, 0] = w
        u_ref[:, 0] = u
        s_scr[...] = s * jnp.exp(glast) + _bdot(ktv * jnp.exp(glast - grv), vnew, BNN)

    cf, cb = _sweep_chunks(n, n_ctx)
    tok = lambda half, chunk_of: _half_spec(gh, (c, c), half, chunk_of)
    rowv = lambda half, chunk_of: _half_spec(gh, (1, c), half, chunk_of)
    step = pl.BlockSpec((g, 1, c, c), lambda i: (0, i, 0, 0))
    shared = [tok(0, cf), tok(0, cb)]
    split = [tok(0, cf), tok(1, cb)]
    return _pcall(
        body, name="dn_fwd", grid=(n,),
        in_specs=shared * 4 + split * 2 + [rowv(0, cf), rowv(1, cb)],
        out_specs=[tok(0, cf), tok(0, cb)] + [step] * 4,
        out_shape=[jax.ShapeDtypeStruct((gh, n, c, c), F32)] * 2 + [jax.ShapeDtypeStruct((g, n, c, c), F32)] * 4,
        scratch_shapes=[pltpu.VMEM((g, c, c), F32)],
        compiler_params=pltpu.CompilerParams(dimension_semantics=("arbitrary",)),
    )(q, q, k, k, k_t, k_t, v, v, gc, gc, bb, bb, gr, gr)


def _dn_bwd_call(q, k, q_t, k_t, v, gc, bb, gr, br, sall, vn, w, u, do_f, do_b, n_ctx):
    gh, n, c, _ = q.shape
    g = 2 * gh

    def body(qf, qb, kf, kb_, qtf, qtb, ktf, ktb, vf, vb, gcf, gcb, bbf, bbb, grf, grb, brf, brb,
             sall_ref, vn_ref, w_ref, u_ref, dof, dob,
             dqf, dqb, dkf, dkb_, dvf, dvb, dgcf, dgcb, dbbf, dbbb, dgrf, dgrb, ds_scr):
        i = pl.program_id(0)

        @pl.when(i == 0)
        def _():
            ds_scr[...] = jnp.zeros_like(ds_scr)

        qv, kv, qtv, ktv, vv = _both(qf, qb), _both(kf, kb_), _both(qtf, qtb), _both(ktf, ktb), _both(vf, vb)
        gcv, bv, grv, brv, dov = _both(gcf, gcb), _both(bbf, bbb), _both(grf, grb), _both(brf, brb), _both(dof, dob)
        s, vnew, w, u = sall_ref[:, 0], vn_ref[:, 0], w_ref[:, 0], u_ref[:, 0]
        dsn = ds_scr[...]
        backward, row, col = _dn_direction_masks(g, c)
        e = jnp.exp(gcv)
        er = jnp.exp(grv)
        kb = kv * bv
        decay = jnp.exp(jnp.where(row >= col, gcv - grv, -jnp.inf))
        decay_s = jnp.where(row > col, decay, 0.0)
        decay_t = jnp.exp(jnp.where(row <= col, grv - gcv, -jnp.inf))
        decay_ts = jnp.where(row < col, decay_t, 0.0)
        kk = _bdot(kb, kv, BNT)
        tt = _dn_inverse(kk * decay_s, _bdot(kv, kb, BNT) * decay_ts, True)
        glast = jnp.where(backward, grv[:, :, 0:1], grv[:, :, c - 1:c])
        eg = jnp.exp(glast)
        x = jnp.exp(glast - gcv)
        kt = kv * x
        qk_raw = _bdot(qv, kv, BNT)
        w_t = _f32dot(ktv * (brv * er), tt, BNN)
        dvn = _bdot(_bdot(kv, qv, BNT) * decay_t, dov, BNN) + _bdot(kt, dsn, BNN)
        dqk = _bdot(dov, vnew, BNT)
        dqk_t = _bdot(vnew, dov, BNT)
        dqd = _bdot(dov, s, BNT)
        dkt = _bdot(vnew, dsn, BNT)
        deg = jnp.sum(jnp.sum(dsn * s, axis=2, keepdims=True), axis=1, keepdims=True)
        dw = -_bdot(dvn, s, BNT)
        ds_scr[...] = dsn * eg + _bdot(qtv * er, dov, BNN) - _bdot(w_t, dvn, BNN)
        dwp = _f32dot(tt, dw, BNN)
        dup = _f32dot(tt, dvn, BNN)
        d_a = -(_bdot(dwp, w, BNT) + _bdot(dup, u, BNT))
        d_at = -(_bdot(w, dwp, BNT) + _bdot(u, dup, BNT))
        dkb = _bdot(d_a * decay_s, kv, BNN) + dwp * e
        dkx = dkt * kv * x
        ddiff = dqk * qk_raw * decay + d_a * kk * decay_s
        dglast = jnp.sum(jnp.sum(dkx, axis=2, keepdims=True), axis=1, keepdims=True) + deg * eg
        lane = lax.broadcasted_iota(jnp.int32, (1, 1, c), 2)
        last_lane = jnp.where(backward, 0, c - 1)
        results = (
            (dqf, dqb, dqd * e + _bdot(dqk * decay, kv, BNN)),
            (dkf, dkb_, _bdot(d_at * decay_ts, kb, BNN) + dkb * bv + dkt * x + _bdot(dqk_t * decay_t, qv, BNN)),
            (dvf, dvb, dup * bv),
            (dgcf, dgcb, ddiff + (dwp * kb + dqd * qv) * e - dkx),
            (dbbf, dbbb, dkb * kv + dup * vv),
            (dgrf, dgrb, jnp.where(lane == last_lane, dglast, 0.0) - jnp.sum(ddiff, axis=1, keepdims=True)),
        )
        for ref_f, ref_b, val in results:
            ref_f[:, 0] = val[:gh]
            ref_b[:, 0] = val[gh:]

    cf, cb = _sweep_chunks(n, n_ctx)
    rf, rb = (lambda i: cf(n - 1 - i)), (lambda i: cb(n - 1 - i))
    tok = lambda half, chunk_of: _half_spec(gh, (c, c), half, chunk_of)
    rowv = lambda half, chunk_of: _half_spec(gh, (1, c), half, chunk_of)
    step = pl.BlockSpec((g, 1, c, c), lambda i: (0, n - 1 - i, 0, 0))
    shared = [tok(0, rf), tok(0, rb)]
    split = [tok(0, rf), tok(1, rb)]
    split_row = [rowv(0, rf), rowv(1, rb)]
    big = jax.ShapeDtypeStruct((gh, n, c, c), F32)
    return _pcall(
        body, name="dn_bwd", grid=(n,),
        in_specs=shared * 5 + split * 2 + split_row * 2 + [step] * 4 + shared,
        out_specs=shared * 5 + [rowv(0, rf), rowv(0, rb)],
        out_shape=[big] * 10 + [jax.ShapeDtypeStruct((gh, n, 1, c), F32)] * 2,
        scratch_shapes=[pltpu.VMEM((g, c, c), F32)],
        compiler_params=pltpu.CompilerParams(dimension_semantics=("arbitrary",)),
    )(q, q, k, k, q_t, q_t, k_t, k_t, v, v, gc, gc, bb, bb, gr, gr, br, br, sall, vn, w, u, do_f, do_b)


_t = lambda a: jnp.swapaxes(a, -1, -2)


def _dn_forms(gcum, beta, d):
    lanes = lambda t: jnp.broadcast_to(t[..., None], t.shape + (d,))
    return lanes(gcum), lanes(beta), gcum[:, :, None, :], beta[:, :, None, :]


@functools.partial(jax.custom_vjp, nondiff_argnums=(5,))
def dn_chunked(q, k, v, gcum, beta, n_ctx):
    gc, bb, gr, _ = _dn_forms(gcum, beta, q.shape[-1])
    return tuple(_dn_fwd_call(q, k, _t(k), v, gc, bb, gr, n_ctx)[:2])


def _dn_chunked_fwd(q, k, v, gcum, beta, n_ctx):
    gc, bb, gr, _ = _dn_forms(gcum, beta, q.shape[-1])
    o_f, o_b, vn, sall, w, u = _dn_fwd_call(q, k, _t(k), v, gc, bb, gr, n_ctx)
    return (o_f, o_b), (q, k, v, gcum, beta, vn, sall, w, u)


def _dn_chunked_bwd(n_ctx, res, cts):
    q, k, v, gcum, beta, vn, sall, w, u = res
    gc, bb, gr, br = _dn_forms(gcum, beta, q.shape[-1])
    (dq_f, dq_b, dk_f, dk_b, dv_f, dv_b, dgc_f, dgc_b, dbb_f, dbb_b, dgr_f, dgr_b) = _dn_bwd_call(
        q, k, _t(q), _t(k), v, gc, bb, gr, br, sall, vn, w, u, cts[0], cts[1], n_ctx)
    dgcum = jnp.concatenate([jnp.sum(dgc_f, axis=-1) + dgr_f[:, :, 0, :], jnp.sum(dgc_b, axis=-1) + dgr_b[:, :, 0, :]], axis=0)
    dbeta = jnp.concatenate([jnp.sum(dbb_f, axis=-1), jnp.sum(dbb_b, axis=-1)], axis=0)
    return dq_f + dq_b, dk_f + dk_b, dv_f + dv_b, dgcum, dbeta


dn_chunked.defvjp(_dn_chunked_fwd, _dn_chunked_bwd)


def _ret_fwd_call(q, k, k_t, v, dmat, xi_b, zeta_r, gm, n_ctx):
    gh, n, c, dk = q.shape
    dv = v.shape[-1]
    g = 2 * gh

    def body(qf, qb, kf, kb_, ktf, ktb, vf, vb, d_ref, xib_ref, zr_ref, gm_ref, of_ref, ob_ref, starts_ref, s_scr):
        i = pl.program_id(0)

        @pl.when(i == 0)
        def _():
            s_scr[...] = jnp.zeros_like(s_scr)

        qv, kv, ktv, vv = _both(qf, qb), _both(kf, kb_), _both(ktf, ktb), _both(vf, vb)
        s = s_scr[...]
        starts_ref[:, 0] = s
        o = _bdot(_bdot(qv, kv, BNT) * d_ref[...], vv, BNN) + _bdot(qv * xib_ref[...], s, BNN)
        of_ref[:, 0] = o[:gh]
        ob_ref[:, 0] = o[gh:]
        s_scr[...] = s * gm_ref[...] + _bdot(ktv * zr_ref[...], vv, BNN)

    cf, cb = _sweep_chunks(n, n_ctx)
    pair = lambda tail: [_half_spec(gh, tail, 0, cf), _half_spec(gh, tail, 0, cb)]
    const = lambda a, b: pl.BlockSpec((g, a, b), lambda i: (0, 0, 0))
    return _pcall(
        body, name="ret_fwd", grid=(n,),
        in_specs=pair((c, dk)) * 2 + pair((dk, c)) + pair((c, dv)) + [const(c, c), const(c, dk), const(1, c), const(dk, dv)],
        out_specs=pair((c, dv)) + [pl.BlockSpec((g, 1, dk, dv), lambda i: (0, i, 0, 0))],
        out_shape=[jax.ShapeDtypeStruct((gh, n, c, dv), F32)] * 2 + [jax.ShapeDtypeStruct((g, n, dk, dv), F32)],
        scratch_shapes=[pltpu.VMEM((g, dk, dv), F32)],
        compiler_params=pltpu.CompilerParams(dimension_semantics=("arbitrary",)),
    )(q, q, k, k, k_t, k_t, v, v, dmat, xi_b, zeta_r, gm)


def _ret_bwd_call(q, k, q_t, k_t, v, dmat, dmat_t, xi_b, xi_r, zeta_b, gm, starts, do_f, do_b, n_ctx):
    gh, n, c, dk = q.shape
    dv = v.shape[-1]
    g = 2 * gh

    def body(qf, qb, kf, kb_, qtf, qtb, ktf, ktb, vf, vb, d_ref, dt_ref, xib_ref, xr_ref, zb_ref, gm_ref, starts_ref,
             dof, dob, dqf, dqb, dkf, dkb_, dvf, dvb, dd_ref, dxib_ref, dzb_ref, dgm_ref, ds_scr):
        i = pl.program_id(0)

        @pl.when(i == 0)
        def _():
            ds_scr[...] = jnp.zeros_like(ds_scr)
            dd_ref[...] = jnp.zeros_like(dd_ref)
            dxib_ref[...] = jnp.zeros_like(dxib_ref)
            dzb_ref[...] = jnp.zeros_like(dzb_ref)
            dgm_ref[...] = jnp.zeros_like(dgm_ref)

        qv, kv, qtv, vv, dov = _both(qf, qb), _both(kf, kb_), _both(qtf, qtb), _both(vf, vb), _both(dof, dob)
        s, dsn = starts_ref[:, 0], ds_scr[...]
        dm, dmt, zb = d_ref[...], dt_ref[...], zb_ref[...]
        qk_raw = _bdot(qv, kv, BNT)
        dqkd = _bdot(dov, vv, BNT)
        do_s = _bdot(dov, s, BNT)
        dkz = _bdot(vv, dsn, BNT)
        results = ((dqf, dqb, _bdot(dqkd * dm, kv, BNN) + do_s * xib_ref[...]),
                   (dkf, dkb_, _bdot(_bdot(vv, dov, BNT) * dmt, qv, BNN) + dkz * zb),
                   (dvf, dvb, _bdot(_bdot(kv, qv, BNT) * dmt, dov, BNN) + _bdot(kv * zb, dsn, BNN)))
        for ref_f, ref_b, val in results:
            ref_f[:, 0] = val[:gh]
            ref_b[:, 0] = val[gh:]
        dd_ref[...] += dqkd * qk_raw
        dxib_ref[...] += do_s * qv
        dzb_ref[...] += dkz * kv
        dgm_ref[...] += dsn * s
        ds_scr[...] = dsn * gm_ref[...] + _bdot(qtv * xr_ref[...], dov, BNN)

    cf, cb = _sweep_chunks(n, n_ctx)
    rf, rb = (lambda i: cf(n - 1 - i)), (lambda i: cb(n - 1 - i))
    pair = lambda tail: [_half_spec(gh, tail, 0, rf), _half_spec(gh, tail, 0, rb)]
    const = lambda a, b: pl.BlockSpec((g, a, b), lambda i: (0, 0, 0))
    sds = lambda *s: jax.ShapeDtypeStruct(s, F32)
    return _pcall(
        body, name="ret_bwd", grid=(n,),
        in_specs=pair((c, dk)) * 2 + pair((dk, c)) * 2 + pair((c, dv))
        + [const(c, c), const(c, c), const(c, dk), const(1, c), const(c, dk), const(dk, dv),
           pl.BlockSpec((g, 1, dk, dv), lambda i: (0, n - 1 - i, 0, 0))] + pair((c, dv)),
        out_specs=pair((c, dk)) * 2 + pair((c, dv)) + [const(c, c), const(c, dk), const(c, dk), const(dk, dv)],
        out_shape=[sds(gh, n, c, dk)] * 4 + [sds(gh, n, c, dv)] * 2 + [sds(g, c, c), sds(g, c, dk), sds(g, c, dk), sds(g, dk, dv)],
        scratch_shapes=[pltpu.VMEM((g, dk, dv), F32)],
        compiler_params=pltpu.CompilerParams(dimension_semantics=("arbitrary",)),
    )(q, q, k, k, q_t, q_t, k_t, k_t, v, v, dmat, dmat_t, xi_b, xi_r, zeta_b, gm, starts, do_f, do_b)


def _ret_forms(xi, zeta, gm, dk, dv):
    lanes = lambda t: jnp.broadcast_to(t[..., None], t.shape + (dk,))
    return lanes(xi), xi[:, None, :], lanes(zeta), zeta[:, None, :], jnp.broadcast_to(gm[:, None, None], gm.shape + (dk, dv))


@functools.partial(jax.custom_vjp, nondiff_argnums=(7,))
def ret_chunked(q, k, v, dmat, xi, zeta, gm, n_ctx):
    xi_b, _, _, zeta_r, gm_f = _ret_forms(xi, zeta, gm, q.shape[-1], v.shape[-1])
    return tuple(_ret_fwd_call(q, k, _t(k), v, dmat, xi_b, zeta_r, gm_f, n_ctx)[:2])


def _ret_chunked_fwd(q, k, v, dmat, xi, zeta, gm, n_ctx):
    xi_b, _, _, zeta_r, gm_f = _ret_forms(xi, zeta, gm, q.shape[-1], v.shape[-1])
    o_f, o_b, starts = _ret_fwd_call(q, k, _t(k), v, dmat, xi_b, zeta_r, gm_f, n_ctx)
    return (o_f, o_b), (q, k, v, dmat, xi, zeta, gm, starts)


def _ret_chunked_bwd(n_ctx, res, cts):
    q, k, v, dmat, xi, zeta, gm, starts = res
    xi_b, xi_r, zeta_b, _, gm_f = _ret_forms(xi, zeta, gm, q.shape[-1], v.shape[-1])
    dq_f, dq_b, dk_f, dk_b, dv_f, dv_b, dd, dxib, dzb, dgm = _ret_bwd_call(
        q, k, _t(q), _t(k), v, dmat, _t(dmat), xi_b, xi_r, zeta_b, gm_f, starts, cts[0], cts[1], n_ctx)
    return (dq_f + dq_b, dk_f + dk_b, dv_f + dv_b, dd, jnp.sum(dxib, axis=-1), jnp.sum(dzb, axis=-1),
            jnp.sum(dgm, axis=(1, 2)))


ret_chunked.defvjp(_ret_chunked_fwd, _ret_chunked_bwd)


def _peer(k):
    mx, my, mc = lax.axis_index("x"), lax.axis_index("y"), lax.axis_index("c")
    px = 1 - mx if k & 4 else mx
    py = 1 - my if k & 2 else my
    pc = 1 - mc if k & 1 else mc
    return (px, py, pc), 4 * px + 2 * py + pc


def _exchange_call(xs, all_to_all, name):
    n_arr = len(xs)
    n_peer = N_DEV - 1

    def body(*refs):
        x_refs, out_refs = refs[:n_arr], refs[n_arr:2 * n_arr]
        send_sems, recv_sems, local_sems = refs[2 * n_arr:]
        me = 4 * lax.axis_index("x") + 2 * lax.axis_index("y") + lax.axis_index("c")

        def copy(j, k, dst_idx):
            dev, idx = _peer(k)
            return pltpu.make_async_remote_copy(
                src_ref=x_refs[j].at[idx] if all_to_all else x_refs[j], dst_ref=out_refs[j].at[dst_idx],
                send_sem=send_sems.at[j * n_peer + k - 1], recv_sem=recv_sems.at[j * n_peer + k - 1],
                device_id=dev, device_id_type=pl.DeviceIdType.MESH)

        mine = [pltpu.make_async_copy(x_refs[j].at[me] if all_to_all else x_refs[j], out_refs[j].at[me], local_sems.at[j])
                for j in range(n_arr)]
        for cp in mine:
            cp.start()
        sends = [copy(j, k, me) for j in range(n_arr) for k in range(1, N_DEV)]
        for cp in sends:
            cp.start()
        for j in range(n_arr):
            for k in range(1, N_DEV):
                copy(j, k, _peer(k)[1]).wait_recv()
        for cp in sends:
            cp.wait_send()
        for cp in mine:
            cp.wait()

    return _pcall(
        body, name=name,
        in_specs=[pl.BlockSpec(memory_space=pl.ANY)] * n_arr, out_specs=[pl.BlockSpec(memory_space=pl.ANY)] * n_arr,
        out_shape=[jax.ShapeDtypeStruct((N_DEV,) + tuple(x.shape[1:] if all_to_all else x.shape), x.dtype) for x in xs],
        scratch_shapes=[pltpu.SemaphoreType.DMA((n_arr * n_peer,)), pltpu.SemaphoreType.DMA((n_arr * n_peer,)),
                        pltpu.SemaphoreType.DMA((n_arr,))],
    )(*xs)


def _sum8_call(x, name):
    _, r, c = x.shape
    tr = _pick(r, (256, 160, 128, 72, 64, 32, 16, 8))

    def body(x_ref, o_ref):
        acc = x_ref[0].astype(F32)
        for d in range(1, N_DEV):
            acc = acc + x_ref[d].astype(F32)
        o_ref[...] = acc

    return _pcall(
        body, name=name, grid=(r // tr,),
        in_specs=[pl.BlockSpec((N_DEV, tr, c), lambda i: (0, i, 0))],
        out_specs=pl.BlockSpec((tr, c), lambda i: (i, 0)),
        out_shape=jax.ShapeDtypeStruct((r, c), F32),
        compiler_params=pltpu.CompilerParams(dimension_semantics=("parallel",)),
    )(x)


def _adamw_call(w, g, m, v, name):
    r, c = w.shape
    tr = _pick(r, (256, 128, 64, 32, 16, 8))
    bc1 = 1.0 - ADAM_B1 ** ADAM_STEP
    bc2 = 1.0 - ADAM_B2 ** ADAM_STEP

    def body(w_ref, g_ref, m_ref, v_ref, d_ref, nm_ref, nv_ref):
        gv = g_ref[...]
        nm = ADAM_B1 * m_ref[...] + (1.0 - ADAM_B1) * gv
        nv = ADAM_B2 * v_ref[...] + (1.0 - ADAM_B2) * jnp.square(gv)
        d_ref[...] = -ADAM_LR * ((nm / bc1) / (jnp.sqrt(nv / bc2) + ADAM_EPS) + ADAM_WD * w_ref[...])
        nm_ref[...] = nm
        nv_ref[...] = nv

    spec = pl.BlockSpec((tr, c), lambda i: (i, 0))
    sds = jax.ShapeDtypeStruct((r, c), F32)
    return _pcall(
        body, name=name, grid=(r // tr,), in_specs=[spec] * 4, out_specs=[spec] * 3, out_shape=[sds] * 3,
        compiler_params=pltpu.CompilerParams(dimension_semantics=("parallel",)),
    )(w, g, m, v)


def layer_norm(x, g, b):
    mu = jnp.mean(x, axis=-1, keepdims=True)
    var = jnp.mean(jnp.square(x - mu), axis=-1, keepdims=True)
    return (x - mu) * lax.rsqrt(var + LN_EPS) * g + b


def rms_norm(x, g):
    return x * lax.rsqrt(jnp.mean(x * x, axis=-1, keepdims=True) + NORM_EPS) * g


def head_layer_norm(o, g):
    b_, t_, h_, d_ = o.shape
    mu = jnp.mean(o, axis=-1, keepdims=True)
    var = jnp.mean(jnp.square(o - mu), axis=-1, keepdims=True)
    return ((o - mu) * lax.rsqrt(var + NORM_EPS)).reshape(b_, t_, h_ * d_) * g


def l2norm(t):
    return t * lax.rsqrt(jnp.sum(t * t, axis=-1, keepdims=True) + NORM_EPS)


def rope_freqs(dim):
    return ROPE_BASE ** (-jnp.arange(0, dim, 2, dtype=F32) / dim)


def axial_rope(rows, rot_dim):
    row = jnp.broadcast_to(jnp.arange(rows, dtype=F32)[:, None], (rows, GRID_W)).reshape(-1)
    col = jnp.broadcast_to(jnp.arange(GRID_W, dtype=F32)[None, :], (rows, GRID_W)).reshape(-1)
    inv = rope_freqs(rot_dim // 2)
    ang = jnp.concatenate([row[:, None] * inv, col[:, None] * inv], axis=-1)
    return jnp.cos(ang), jnp.sin(ang)


def sequence_rope(n_tok, rot_dim):
    ang = jnp.arange(n_tok, dtype=F32)[:, None] * rope_freqs(rot_dim)
    return jnp.cos(ang), jnp.sin(ang)


def apply_rope(x, cos, sin):
    x1, x2 = jnp.split(x, 2, axis=-1)
    c = cos[:, None, :]
    s = sin[:, None, :]
    return jnp.concatenate([x1 * c - x2 * s, x1 * s + x2 * c], axis=-1)


def _flip_t(t):
    return jnp.flip(t, axis=2)


def _split_columns(z):
    idx = np.cumsum(np.array(IN_SPLITS))[:-1].tolist()
    return jnp.split(z, idx, axis=-1)


def short_conv(x, w):
    k_width, ch = w.shape
    pad = k_width // 2
    return lax.conv_general_dilated(x, w[:, None, :], window_strides=(1,), padding=[(pad, pad)],
                                    dimension_numbers=('NWC', 'WIO', 'NWC'), feature_group_count=ch)


def _to_heads(t, h, d):
    b_, t_, _ = t.shape
    return t.reshape(b_, t_, h, d).transpose(0, 2, 1, 3).reshape(b_ * h, t_, d)


def _from_heads(t, b_):
    g, t_, d = t.shape
    return t.reshape(b_, g // b_, t_, d).transpose(0, 2, 1, 3).reshape(b_, t_, (g // b_) * d)


def _lane_scalar(vals):
    return jnp.broadcast_to(vals[:, None, None], (vals.shape[0], 1, 128))


def swa_group(q, k, v, qc, kc, vc, sink, cos, sin, with_ctx_out):
    b_, s_, _ = q.shape
    l_ = kc.shape[1]
    grp = SWA_HEADS // SWA_KV_HEADS
    d = SWA_HEAD_DIM
    w_ = SWA_BLOCK
    scale = d ** -0.5
    qh = apply_rope(q.reshape(b_, s_, SWA_HEADS, d), cos, sin).transpose(0, 2, 1, 3).reshape(b_ * SWA_HEADS, s_, d)
    kh = apply_rope(k.reshape(b_, s_, SWA_KV_HEADS, d), cos, sin).transpose(0, 2, 1, 3).reshape(b_ * SWA_KV_HEADS, s_, d)
    vh = _to_heads(v, SWA_KV_HEADS, d)
    kch = _to_heads(kc, SWA_KV_HEADS, d)
    vch = _to_heads(vc, SWA_KV_HEADS, d)
    padk = lambda t: jnp.pad(t, ((0, 0), (w_, w_), (0, 0)))
    sink_g = _lane_scalar(jnp.tile(sink, b_))
    y = _from_heads(swa_attn(qh, padk(kh), padk(vh), kch, vch, sink_g, scale), b_)
    yc = None
    if with_ctx_out:
        qch = _to_heads(qc, SWA_HEADS, d)
        rep = lambda t: jnp.repeat(t.reshape(b_, SWA_KV_HEADS, l_, d), grp, axis=1).reshape(b_ * SWA_HEADS, l_, d)
        yc = _from_heads(attn_full(qch, rep(kch), rep(vch), sink_g, scale, True), b_)
    return y, yc


def gated_delta_chunked(q, k, v, log_g, beta, n_ctx):
    g_, t_, dk = k.shape
    dv = v.shape[-1]
    c_ = DN_CHUNK
    n = t_ // c_
    assert dk == c_ and dv == c_
    lg = log_g.reshape(2, g_, n, c_)
    g_cum = jnp.concatenate([jnp.cumsum(lg[0], axis=-1), jnp.flip(jnp.cumsum(jnp.flip(lg[1], axis=-1), axis=-1), axis=-1)],
                            axis=0)
    o_f, o_b = dn_chunked(q.reshape(g_, n, c_, dk), k.reshape(g_, n, c_, dk), v.reshape(g_, n, c_, dv), g_cum,
                          beta.reshape(2 * g_, n, c_), n_ctx)
    return (o_f + o_b).reshape(g_, t_, dv)


def deltanet_group(qkv, z, ab, qkv_c, z_c, ab_c, conv_w, a_log, dt_bias, norm_g, with_ctx_out):
    def prep(qkv_, ab_):
        b_, t_, _ = qkv_.shape
        y = jax.nn.silu(short_conv(qkv_, conv_w))
        q, k, v = [t.reshape(b_, t_, DN_HEADS, DN_HEAD_DIM).transpose(0, 2, 1, 3) for t in jnp.split(y, 3, axis=-1)]
        q = l2norm(q) * DN_HEAD_DIM ** -0.5
        k = l2norm(k)
        ab_ = ab_.reshape(b_, t_, 2, 2, DN_HEADS)
        log_g = -jnp.exp(a_log) * jax.nn.softplus(ab_[:, :, :, 0] + dt_bias)
        beta = jax.nn.sigmoid(ab_[:, :, :, 1])
        return q, k, v, log_g.transpose(2, 0, 3, 1), beta.transpose(2, 0, 3, 1)

    def out(o, z_):
        b_, t_, _ = z_.shape
        o = rms_norm(o.transpose(0, 2, 1, 3), norm_g) * jax.nn.silu(z_).reshape(b_, t_, DN_HEADS, DN_HEAD_DIM)
        return o.reshape(b_, t_, DN_W)

    qc, kc, vc, lgc, bc = prep(qkv_c, ab_c)
    q, k, v, lg, bt = prep(qkv, ab)
    b_, l_, s_ = qkv.shape[0], qkv_c.shape[1], qkv.shape[1]
    seq = lambda tc, tl: jnp.concatenate([tc, tl], axis=2).reshape((b_ * DN_HEADS, l_ + s_) + tc.shape[3:])
    seq_g = lambda tc, tl: jnp.concatenate([tc, tl], axis=3).reshape(2, b_ * DN_HEADS, l_ + s_)
    o = gated_delta_chunked(seq(qc, q), seq(kc, k), seq(vc, v), seq_g(lgc, lg), seq_g(bc, bt), l_ // DN_CHUNK)
    o = o.reshape(b_, DN_HEADS, l_ + s_, DN_HEAD_DIM)
    y = out(o[:, :, l_:], z)
    yc = out(o[:, :, :l_], z_c) if with_ctx_out else None
    return y, yc


def retention_core(q, k, v, log_gamma, n_ctx):
    b_, h_, t_, dk = q.shape
    dv = v.shape[-1]
    c_ = RET_CHUNK
    n = t_ // c_
    gh = b_ * h_
    fwd_rank = jnp.arange(c_, dtype=F32)
    rank = jnp.stack([fwd_rank, c_ - 1 - fwd_rank])[:, None, :]
    lg = log_gamma[..., None]
    per_g = lambda t: jnp.broadcast_to(t[:, None], (2, b_) + t.shape[1:]).reshape((2 * gh,) + t.shape[2:])
    zeta = jnp.exp((c_ - 1 - rank) * lg)
    xi = jnp.exp((rank + 1.0) * lg)
    rel = rank[..., :, None] - rank[..., None, :]
    dmat = jnp.where(rel >= 0, jnp.exp(jnp.maximum(rel, 0.0) * lg[..., None]), 0.0)
    gm = jnp.exp(c_ * log_gamma)
    o_f, o_b = ret_chunked(q.reshape(gh, n, c_, dk), k.reshape(gh, n, c_, dk), v.reshape(gh, n, c_, dv),
                           per_g(dmat), per_g(xi), per_g(zeta), per_g(gm), n_ctx)
    return (o_f + o_b).reshape(b_, h_, t_, dv)


def retention_group(q, k, v, g, qc, kc, vc, gc, log1m_gamma, norm_g, cos, sin, with_ctx_out):
    log_gamma = jnp.log1p(-jnp.exp(log1m_gamma))
    heads = lambda t, dh: t.reshape(t.shape[0], t.shape[1], RET_HEADS, dh)
    bhtd = lambda t: t.transpose(0, 2, 1, 3)
    sc = RET_QK_DIM ** -0.5
    l_ = kc.shape[1]
    q = bhtd(apply_rope(heads(q, RET_QK_DIM), cos, sin)) * sc
    k = bhtd(apply_rope(heads(k, RET_QK_DIM), cos, sin))
    v = bhtd(heads(v, RET_V_DIM))
    kc = bhtd(heads(kc, RET_QK_DIM))
    vc = bhtd(heads(vc, RET_V_DIM))
    qcs = bhtd(heads(qc, RET_QK_DIM)) * sc

    def out(o, g_):
        return head_layer_norm(o.transpose(0, 2, 1, 3), norm_g) * jax.nn.silu(g_)

    seq = lambda tc, tl: jnp.concatenate([tc, tl], axis=2)
    o = retention_core(seq(qcs, q), seq(kc, k), seq(vc, v), log_gamma, l_ // RET_CHUNK)
    y = out(o[:, :, l_:], g)
    yc = out(o[:, :, :l_], gc) if with_ctx_out else None
    return y, yc


def mla_group(cq, ckv, kr, cq_c, ckv_c, kr_c, q_norm, w_uq, kv_norm, w_ukv, cos, sin, with_ctx_out):
    b_, s_, _ = cq.shape
    l_ = cq_c.shape[1]
    dqk = MLA_NOPE_DIM + MLA_ROPE_DIM
    rows = lambda tl, tc: jnp.concatenate([tl.reshape(b_ * s_, -1), tc.reshape(b_ * l_, -1)], axis=0)
    qa = matmul(rms_norm(rows(cq, cq_c), q_norm), w_uq)
    kva = matmul(rms_norm(rows(ckv, ckv_c), kv_norm), w_ukv)
    q = qa[:b_ * s_].reshape(b_, s_, MLA_HEADS, dqk)
    qc = qa[b_ * s_:].reshape(b_, l_, MLA_HEADS, dqk)
    q = jnp.concatenate([q[..., :MLA_NOPE_DIM], apply_rope(q[..., MLA_NOPE_DIM:], cos, sin)], axis=-1)
    kv = kva[:b_ * s_].reshape(b_, s_, MLA_HEADS, MLA_NOPE_DIM + MLA_V_DIM)
    kvc = kva[b_ * s_:].reshape(b_, l_, MLA_HEADS, MLA_NOPE_DIM + MLA_V_DIM)
    kr = apply_rope(kr[:, :, None, :], cos, sin)
    k = jnp.concatenate([kv[..., :MLA_NOPE_DIM], jnp.broadcast_to(kr, (b_, s_, MLA_HEADS, MLA_ROPE_DIM))], axis=-1)
    kc = jnp.concatenate([kvc[..., :MLA_NOPE_DIM],
                          jnp.broadcast_to(kr_c[:, :, None, :], (b_, l_, MLA_HEADS, MLA_ROPE_DIM))], axis=-1)
    v, vc = kv[..., MLA_NOPE_DIM:], kvc[..., MLA_NOPE_DIM:]
    hd = lambda t: t.transpose(0, 2, 1, 3).reshape(b_ * MLA_HEADS, t.shape[1], t.shape[3])
    scale = dqk ** -0.5
    no_sink = jnp.zeros((b_ * MLA_HEADS, 1, 128), F32)
    kch, vch = hd(kc), hd(vc)
    y = attn_full(hd(q), jnp.concatenate([hd(k), kch], axis=1), jnp.concatenate([hd(v), vch], axis=1), no_sink, scale, False)
    y = _from_heads(y, b_)
    yc = _from_heads(attn_full(hd(qc), kch, vch, no_sink, scale, False), b_) if with_ctx_out else None
    return y, yc


def token_mixers(zl, zc, p, layer, rope, with_ctx_out):
    (a_q, a_k, a_v, b_qkv, b_z, b_ab, c_q, c_k, c_v, c_g, d_cq, d_ckv, d_kr) = _split_columns(zl)
    (a_qc, a_kc, a_vc, b_qkvc, b_zc, b_abc, c_qc, c_kc, c_vc, c_gc, d_cqc, d_ckvc, d_krc) = _split_columns(zc)
    swa_cos, swa_sin, ret_cos, ret_sin, mla_cos, mla_sin = rope
    ya, yac = swa_group(a_q, a_k, a_v, a_qc, a_kc, a_vc, p['swa_sink'][layer], swa_cos, swa_sin, with_ctx_out)
    yb, ybc = deltanet_group(b_qkv, b_z, b_ab, b_qkvc, b_zc, b_abc, p['dn_conv_w'][layer], p['dn_a_log'][layer],
                             p['dn_dt_bias'][layer], p['dn_norm_g'][layer], with_ctx_out)
    yr, yrc = retention_group(c_q, c_k, c_v, c_g, c_qc, c_kc, c_vc, c_gc, p['ret_log1m_gamma'][layer],
                              p['ret_norm_g'][layer], ret_cos, ret_sin, with_ctx_out)
    yd, ydc = mla_group(d_cq, d_ckv, d_kr, d_cqc, d_ckvc, d_krc, p['mla_q_norm'][layer], p['mla_w_uq'][layer],
                        p['mla_kv_norm'][layer], p['mla_w_ukv'][layer], mla_cos, mla_sin, with_ctx_out)
    y = jnp.concatenate([ya, yb, yr, yd], axis=-1)
    yc = jnp.concatenate([yac, ybc, yrc, ydc], axis=-1) if with_ctx_out else None
    return y, yc


def local_loss(p, x, ctx, loss_target):
    b_, n_tok, d_ = x.shape
    l_ = ctx.shape[1]
    rows = n_tok // GRID_W
    rope = (*axial_rope(rows, SWA_HEAD_DIM), *sequence_rope(n_tok, RET_QK_DIM), *axial_rope(rows, MLA_ROPE_DIM))
    rl, rc = b_ * n_tok, b_ * l_
    mods = [jnp.concatenate([p['mod'][layer], p['cmod'][layer][None]], axis=0) for layer in range(DEPTH)]
    part = lambda layer, j: mods[layer][:, j * d_:(j + 1) * d_][:, None, :]
    vec = lambda name, layer: p[name][layer][None, :]
    xr = jnp.concatenate([x.reshape(rl, d_), ctx.reshape(rc, d_)], axis=0)
    sh1, sc1 = part(0, 0), part(0, 1)
    h = jnp.concatenate([(x * (1 + sc1[:b_]) + sh1[:b_]).reshape(rl, d_), (ctx * (1 + sc1[b_]) + sh1[b_]).reshape(rc, d_)],
                        axis=0)
    for layer in range(DEPTH):
        with_ctx_out = layer < DEPTH - 1
        g1, sh2, sc2, g2 = part(layer, 2), part(layer, 3), part(layer, 4), part(layer, 5)
        z = matmul(h, p['w_in'][layer])
        zl = z[:rl, :IN_WIDTH].reshape(b_, n_tok, IN_WIDTH)
        zc = z[rl:, :IN_WIDTH].reshape(b_, l_, IN_WIDTH)
        y, yc = token_mixers(zl, zc, p, layer, rope, with_ctx_out)
        if with_ctx_out:
            yo = matmul(jnp.concatenate([y.reshape(rl, d_), yc.reshape(rc, d_)], axis=0), p['w_out'][layer])
            xr, h2 = ln_mod(xr, yo, g1, vec('ln1_g', layer), vec('ln1_b', layer), sc2, sh2, n_tok)
            f = matmul_relu2(matmul(h2, p['w_ff1'][layer]), p['w_ff2'][layer])
            xr, h = ln_mod(xr, f, g2, vec('ln2_g', layer), vec('ln2_b', layer), part(layer + 1, 1), part(layer + 1, 0), n_tok)
        else:
            lat = lambda t: t[:b_]
            yo = matmul(y.reshape(rl, d_), p['w_out'][layer])
            xl, h2 = ln_mod(xr[:rl], yo, lat(g1), vec('ln1_g', layer), vec('ln1_b', layer), lat(sc2), lat(sh2), n_tok)
            f = matmul_relu2(matmul(h2, p['w_ff1'][layer]), p['w_ff2'][layer])
            none = jnp.zeros((b_, 1, d_), F32)
            xl, _ = ln_mod(xl, f, lat(g2), vec('ln2_g', layer), vec('ln2_b', layer), none, none, n_tok)
    err = jnp.square(xl - loss_target.reshape(rl, d_))
    return 0.5 * jnp.sum(jnp.mean(err, axis=-1))


def _shard_shape(shape, axis):
    s = list(shape)
    s[axis] //= N_DEV
    return tuple(s)


def _join_shards(pieces, axis):
    _, _, r, c = pieces.shape
    if axis == 0:
        full = pieces.transpose(1, 0, 2, 3).reshape(DEPTH, N_DEV * r, c)
    else:
        full = pieces.transpose(1, 2, 0, 3).reshape(DEPTH, r, N_DEV * c)
    return full.astype(F32)


def _split_shards(g, shape, axis):
    r, c = _shard_shape(shape, axis)
    if axis == 0:
        pieces = g.reshape(DEPTH, N_DEV, r, c).transpose(1, 0, 2, 3)
    else:
        pieces = g.reshape(DEPTH, r, N_DEV, c).transpose(2, 0, 1, 3)
    return pieces.astype(BF16)


def _pad_vec(vec, rows_multiple=8):
    n = vec.shape[0]
    rows = -(-n // (128 * rows_multiple)) * rows_multiple
    return jnp.pad(vec, (0, rows * 128 - n)).reshape(rows, 128)


def _adamw(w, g, m, v, name):
    shape = w.shape
    if w.ndim >= 2 and shape[-1] >= 128:
        as2 = lambda t: t.reshape(-1, shape[-1])
        d, nm, nv = _adamw_call(as2(w), as2(g), as2(m), as2(v), name)
        return d.reshape(shape), nm.reshape(shape), nv.reshape(shape)
    n = int(np.prod(shape))
    as2 = lambda t: _pad_vec(t.reshape(-1))
    d, nm, nv = _adamw_call(as2(w), as2(g), as2(m), as2(v), name)
    un = lambda t: t.reshape(-1)[:n].reshape(shape)
    return un(d), un(nm), un(nv)


def kernel(x, c, ctx, c_ctx, ada_w, ada_b, w_in, swa_sink, dn_conv_w, dn_a_log, dn_dt_bias, dn_norm_g, ret_log1m_gamma, ret_norm_g, mla_q_norm, mla_w_uq, mla_kv_norm, mla_w_ukv, w_out, ln1_g, ln1_b, w_ff1, w_ff2, ln2_g, ln2_b, loss_target, m_c_ctx, m_ada_w, m_ada_b, m_w_in, m_swa_sink, m_dn_conv_w, m_dn_a_log, m_dn_dt_bias, m_dn_norm_g, m_ret_log1m_gamma, m_ret_norm_g, m_mla_q_norm, m_mla_w_uq, m_mla_kv_norm, m_mla_w_ukv, m_w_out, m_ln1_g, m_ln1_b, m_w_ff1, m_w_ff2, m_ln2_g, m_ln2_b, v_c_ctx, v_ada_w, v_ada_b, v_w_in, v_swa_sink, v_dn_conv_w, v_dn_a_log, v_dn_dt_bias, v_dn_norm_g, v_ret_log1m_gamma, v_ret_norm_g, v_mla_q_norm, v_mla_w_uq, v_mla_kv_norm, v_mla_w_ukv, v_w_out, v_ln1_g, v_ln1_b, v_w_ff1, v_w_ff2, v_ln2_g, v_ln2_b):
    a = dict(zip(ARG_NAMES, (x, c, ctx, c_ctx, ada_w, ada_b, w_in, swa_sink, dn_conv_w, dn_a_log, dn_dt_bias, dn_norm_g, ret_log1m_gamma, ret_norm_g, mla_q_norm, mla_w_uq, mla_kv_norm, mla_w_ukv, w_out, ln1_g, ln1_b, w_ff1, w_ff2, ln2_g, ln2_b, loss_target, m_c_ctx, m_ada_w, m_ada_b, m_w_in, m_swa_sink, m_dn_conv_w, m_dn_a_log, m_dn_dt_bias, m_dn_norm_g, m_ret_log1m_gamma, m_ret_norm_g, m_mla_q_norm, m_mla_w_uq, m_mla_kv_norm, m_mla_w_ukv, m_w_out, m_ln1_g, m_ln1_b, m_w_ff1, m_w_ff2, m_ln2_g, m_ln2_b, v_c_ctx, v_ada_w, v_ada_b, v_w_in, v_swa_sink, v_dn_conv_w, v_dn_a_log, v_dn_dt_bias, v_dn_norm_g, v_ret_log1m_gamma, v_ret_norm_g, v_mla_q_norm, v_mla_w_uq, v_mla_kv_norm, v_mla_w_ukv, v_w_out, v_ln1_g, v_ln1_b, v_w_ff1, v_w_ff2, v_ln2_g, v_ln2_b)))
    me = 4 * lax.axis_index("x") + 2 * lax.axis_index("y") + lax.axis_index("c")
    b_loc = x.shape[0]
    n_ex = N_DEV * b_loc
    conv_k, conv_c = dn_conv_w.shape[1], dn_conv_w.shape[2]
    ada_cols = ada_w.shape[2]

    small_in = jnp.concatenate([c.reshape(-1), dn_conv_w.reshape(-1)])
    gathered = _exchange_call([_pad_vec(small_in)] + [a[name].astype(BF16) for name, _, _ in BIG], False, "gather_weights")
    small_all = gathered[0].reshape(N_DEV, -1)
    c_all = small_all[:, :b_loc * D_MODEL].reshape(n_ex, D_MODEL)
    conv_all = small_all[:, b_loc * D_MODEL:b_loc * D_MODEL + DEPTH * conv_k * conv_c].reshape(N_DEV, DEPTH, conv_k, conv_c)
    conv_full = conv_all.transpose(1, 2, 0, 3).reshape(DEPTH, conv_k, N_DEV * conv_c)
    big = {name: _join_shards(pieces, axis) for (name, _, axis), pieces in zip(BIG, gathered[1:])}
    big['w_in'] = jnp.pad(big['w_in'], ((0, 0), (0, 0), (0, IN_WIDTH_PAD - IN_WIDTH)))

    n_rows = -(-(n_ex + 1) // 16) * 16
    silu_cc = jax.nn.silu(c_ctx)
    a_rows = jnp.concatenate([jax.nn.silu(c_all), silu_cc[None], jnp.zeros((n_rows - n_ex - 1, D_MODEL), F32)], axis=0)
    m_loc = jnp.concatenate([_mm_call(a_rows, ada_w[l], False, "ada_fwd") for l in range(DEPTH)], axis=0)
    m_all = _exchange_call([m_loc], False, "gather_mod")[0].reshape(N_DEV, DEPTH, n_rows, ada_cols)
    mod_full = m_all.transpose(1, 2, 0, 3).reshape(DEPTH, n_rows, N_DEV * ada_cols) + ada_b[:, None, :]
    mod = lax.dynamic_slice_in_dim(mod_full, me * b_loc, b_loc, axis=1)
    cmod = mod_full[:, n_ex]

    p = dict(big)
    p.update(mod=mod, cmod=cmod, dn_conv_w=conv_full)
    for name in SMALL:
        p[name] = a[name]
    loss_loc, (gp, gx) = jax.value_and_grad(local_loss, argnums=(0, 1))(p, x, ctx, loss_target)
    loss = lax.psum(loss_loc, MESH_AXES)

    gp['w_in'] = gp['w_in'][:, :, :IN_WIDTH]
    arrived = _exchange_call([_split_shards(gp[name], shape, axis) for name, shape, axis in BIG], True, "scatter_grads")
    g_big = {}
    for (name, shape, axis), part in zip(BIG, arrived):
        r, c_ = _shard_shape(shape, axis)
        g_big[name] = _sum8_call(part.reshape(N_DEV, DEPTH * r, c_), "sum_" + name).reshape(DEPTH, r, c_)

    d_loc = jnp.concatenate([gp['mod'], gp['cmod'][:, None, :]], axis=1).reshape(DEPTH * (b_loc + 1), -1)
    d_loc = jnp.pad(d_loc, ((0, 8 - DEPTH * (b_loc + 1)), (0, 0)))
    d_all = _exchange_call([d_loc], False, "gather_dmod")[0][:, :DEPTH * (b_loc + 1)].reshape(N_DEV, DEPTH, b_loc + 1, -1)
    d_rows = d_all[:, :, :b_loc].transpose(1, 0, 2, 3).reshape(DEPTH, n_ex, -1)
    d_crow = d_all[0, :, b_loc]
    for d in range(1, N_DEV):
        d_crow = d_crow + d_all[d, :, b_loc]
    dm_full = jnp.concatenate([d_rows, d_crow[:, None, :], jnp.zeros((DEPTH, n_rows - n_ex - 1, d_rows.shape[-1]), F32)], axis=1)
    g_ada_b = jnp.sum(dm_full, axis=1)
    dm_mine = lax.dynamic_slice_in_dim(dm_full, me * ada_cols, ada_cols, axis=2)
    g_ada_w = jnp.stack([_mm_call(a_rows, dm_mine[l], True, "ada_bwd_w") for l in range(DEPTH)])
    crow8 = jnp.concatenate([dm_mine[:, n_ex:n_ex + 1], jnp.zeros((DEPTH, 15, ada_cols), F32)], axis=1)
    dsilu_part = sum(_mm_call(crow8[l], jnp.transpose(ada_w[l]), False, "ada_bwd_c")[0] for l in range(DEPTH))

    small_g = jnp.concatenate([gp[name].reshape(-1) for name in SMALL] + [gp['dn_conv_w'].reshape(-1), dsilu_part])
    small_sum = _sum8_call(_exchange_call([_pad_vec(small_g)], False, "gather_small_grads")[0], "sum_small_grads").reshape(-1)
    g_all, off = {}, 0
    for name in SMALL:
        n = int(np.prod(a[name].shape))
        g_all[name] = small_sum[off:off + n].reshape(a[name].shape)
        off += n
    n = DEPTH * conv_k * N_DEV * conv_c
    g_conv_full = small_sum[off:off + n].reshape(DEPTH, conv_k, N_DEV * conv_c)
    g_all['dn_conv_w'] = lax.dynamic_slice_in_dim(g_conv_full, me * conv_c, conv_c, axis=2)
    off += n
    dsilu = small_sum[off:off + D_MODEL]
    sig = jax.nn.sigmoid(c_ctx)
    g_all['c_ctx'] = dsilu * (sig * (1 + c_ctx * (1 - sig)))
    g_all['ada_w'] = g_ada_w
    g_all['ada_b'] = g_ada_b
    g_all.update(g_big)

    delta, new_m, new_v = {}, {}, {}
    for name in WEIGHTS:
        delta[name], new_m[name], new_v[name] = _adamw(a[name], g_all[name], a['m_' + name], a['v_' + name], "adamw_" + name)
    return (loss, gx, *[g_all[n] for n in WEIGHTS], *[delta[n] for n in WEIGHTS],
            *[new_m[n] for n in WEIGHTS], *[new_v[n] for n in WEIGHTS])
```

```python
import functools
import math

import jax
import jax.numpy as jnp
import numpy as np
from jax import lax
from jax.experimental import pallas as pl
from jax.experimental.pallas import tpu as pltpu

F32 = jnp.float32
BF16 = jnp.bfloat16
N_DEV = 8
MESH_AXES = ("x", "y", "c")

D_MODEL = 1024
DEPTH = 2
GRID_W = 64
SWA_HEADS, SWA_KV_HEADS, SWA_HEAD_DIM, SWA_WINDOW, SWA_BLOCK = 4, 2, 64, 128, 128
DN_HEADS, DN_HEAD_DIM, DN_CHUNK = 4, 64, 64
RET_HEADS, RET_QK_DIM, RET_V_DIM, RET_CHUNK = 4, 32, 64, 64
MLA_HEADS, MLA_Q_RANK, MLA_KV_RANK, MLA_NOPE_DIM, MLA_ROPE_DIM, MLA_V_DIM = 4, 256, 128, 64, 32, 64
D_FF = 4 * D_MODEL
ROPE_BASE = 10000.0
NORM_EPS = 1e-6
LN_EPS = 1e-5
DEEPNORM_ALPHA = (2 * DEPTH) ** 0.25
SWA_Q = SWA_HEADS * SWA_HEAD_DIM
SWA_KV = SWA_KV_HEADS * SWA_HEAD_DIM
DN_W = DN_HEADS * DN_HEAD_DIM
RET_QK = RET_HEADS * RET_QK_DIM
RET_V = RET_HEADS * RET_V_DIM
IN_SPLITS = (SWA_Q, SWA_KV, SWA_KV, 3 * DN_W, DN_W, 4 * DN_HEADS, RET_QK, RET_QK, RET_V, RET_V,
             MLA_Q_RANK, MLA_KV_RANK, MLA_ROPE_DIM)
IN_WIDTH = sum(IN_SPLITS)
IN_WIDTH_PAD = -(-IN_WIDTH // 128) * 128

ADAM_LR, ADAM_B1, ADAM_B2, ADAM_EPS, ADAM_WD, ADAM_STEP = 0.001, 0.9, 0.999, 1e-08, 0.01, 10

WEIGHTS = ['c_ctx', 'ada_w', 'ada_b', 'w_in', 'swa_sink', 'dn_conv_w', 'dn_a_log', 'dn_dt_bias', 'dn_norm_g',
           'ret_log1m_gamma', 'ret_norm_g', 'mla_q_norm', 'mla_w_uq', 'mla_kv_norm', 'mla_w_ukv', 'w_out', 'ln1_g',
           'ln1_b', 'w_ff1', 'w_ff2', 'ln2_g', 'ln2_b']
FWD_INPUTS = ['x', 'c', 'ctx'] + WEIGHTS
ARG_NAMES = FWD_INPUTS + ['loss_target'] + ['m_' + n for n in WEIGHTS] + ['v_' + n for n in WEIGHTS]

BIG = (('w_in', (D_MODEL, IN_WIDTH), 1), ('w_out', (D_MODEL, D_MODEL), 0), ('w_ff1', (D_MODEL, D_FF), 1),
       ('w_ff2', (D_FF, D_MODEL), 0), ('mla_w_uq', (MLA_Q_RANK, MLA_HEADS * (MLA_NOPE_DIM + MLA_ROPE_DIM)), 1),
       ('mla_w_ukv', (MLA_KV_RANK, MLA_HEADS * (MLA_NOPE_DIM + MLA_V_DIM)), 1))
SMALL = ('swa_sink', 'dn_a_log', 'dn_dt_bias', 'dn_norm_g', 'ret_log1m_gamma', 'ret_norm_g', 'mla_q_norm',
         'mla_kv_norm', 'ln1_g', 'ln1_b', 'ln2_g', 'ln2_b')


def _pcall(body, **kw):
    return pl.pallas_call(body, **kw)


def _pick(n, cands):
    for cand in cands:
        if n % cand == 0:
            return cand
    return n


def _bdot(a, b, dims):
    return lax.dot_general(a.astype(BF16), b.astype(BF16), dims, preferred_element_type=F32)


def _lane0(t):
    return jnp.where(lax.broadcasted_iota(jnp.int32, t.shape, t.ndim - 1) == 0, t, 0.0)


NN = (((1,), (0,)), ((), ()))
NT = (((1,), (1,)), ((), ()))
TN = (((0,), (0,)), ((), ()))
BNN = (((2,), (1,)), ((0,), (0,)))
BNT = (((2,), (2,)), ((0,), (0,)))


MM_ROW_TILE_MAX = 1088
MM_COL_TILE_MAX = 1408
MM_TOKEN_TILE_MAX = 1088
VMEM_LIMIT_MAX = 60 * 1024 * 1024


def _tile(n, cap, align):
    best = None
    for t in range(align, min(n, cap) + 1, align):
        if n % t == 0:
            best = t
    return best or n


def _relu2(t):
    return jnp.square(jnp.maximum(t, 0.0))


def _mm_call(a, b, trans_a, name, act_a=False, epi=None):
    if trans_a:
        kdim, m = a.shape
        tk = _tile(kdim, MM_TOKEN_TILE_MAX, 8)
        tm = _tile(m, 1024, 128)
    else:
        m, kdim = a.shape
        tk = _tile(kdim, MM_COL_TILE_MAX, 128)
        tm = _tile(m, MM_ROW_TILE_MAX, 8)
    n = b.shape[1]
    assert b.shape[0] == kdim
    tn = _tile(n, MM_COL_TILE_MAX, 128)
    nk = kdim // tk

    def body(*refs):
        a_ref, b_ref = refs[0], refs[1]
        e_ref = refs[2] if epi is not None else None
        o_ref = refs[-1]
        k = pl.program_id(2)
        av = a_ref[...]
        if act_a:
            av = _relu2(av)
        part = _bdot(av, b_ref[...], TN if trans_a else NN)

        def finish(t):
            return t * (2.0 * jnp.maximum(e_ref[...], 0.0)) if epi is not None else t

        if nk == 1:
            o_ref[...] = finish(part)
        else:
            @pl.when(k == 0)
            def _():
                o_ref[...] = part

            @pl.when((k > 0) & (k < nk - 1))
            def _():
                o_ref[...] += part

            @pl.when(k == nk - 1)
            def _():
                o_ref[...] = finish(o_ref[...] + part)

    if trans_a:
        a_spec = pl.BlockSpec((tk, tm), lambda i, j, k: (k, i))
    else:
        a_spec = pl.BlockSpec((tm, tk), lambda i, j, k: (i, k))
    o_spec = pl.BlockSpec((tm, tn), lambda i, j, k: (i, j))
    in_specs = [a_spec, pl.BlockSpec((tk, tn), lambda i, j, k: (k, j))] + ([o_spec] if epi is not None else [])
    tiles = tm * tk * a.dtype.itemsize + tk * tn * b.dtype.itemsize + tm * tn * 4 * (2 if epi is not None else 1)
    temps = tm * tk * (2 + (4 if act_a else 0)) + tk * tn * 2 + 2 * tm * tn * 4
    return _pcall(
        body, name=name, grid=(m // tm, n // tn, nk), in_specs=in_specs, out_specs=o_spec,
        out_shape=jax.ShapeDtypeStruct((m, n), F32),
        compiler_params=pltpu.CompilerParams(dimension_semantics=("parallel", "parallel", "arbitrary"),
                                             vmem_limit_bytes=min(2 * tiles + temps + (4 << 20), VMEM_LIMIT_MAX)),
    )(*((a, b) + ((epi,) if epi is not None else ())))


@jax.custom_vjp
def matmul(a, b):
    return _mm_call(a, b.astype(BF16), False, "mm_fwd")


def _matmul_fwd(a, b):
    bb = b.astype(BF16)
    return _mm_call(a, bb, False, "mm_fwd"), (a, bb)


def _matmul_bwd(res, g):
    a, bb = res
    da = _mm_call(g, jnp.transpose(bb), False, "mm_bwd_da")
    db = _mm_call(a, g, True, "mm_bwd_db")
    return da, db


matmul.defvjp(_matmul_fwd, _matmul_bwd)


@jax.custom_vjp
def matmul_relu2(a, b):
    return _mm_call(a, b.astype(BF16), False, "mm_act_fwd", act_a=True)


def _matmul_relu2_fwd(a, b):
    bb = b.astype(BF16)
    return _mm_call(a, bb, False, "mm_act_fwd", act_a=True), (a, bb)


def _matmul_relu2_bwd(res, g):
    a, bb = res
    da = _mm_call(g, jnp.transpose(bb), False, "mm_act_bwd_da", epi=a)
    db = _mm_call(a, g, True, "mm_act_bwd_db", act_a=True)
    return da, db


matmul_relu2.defvjp(_matmul_relu2_fwd, _matmul_relu2_bwd)


LN_ROW_TILE = 256


def _ln_group_map(group_rows, n_groups):
    per = group_rows // LN_ROW_TILE
    return lambda i: (jnp.minimum(i // per, n_groups - 1), 0, 0)


def _ln_stats(x, y, gate):
    pre = DEEPNORM_ALPHA * x + gate * y
    mu = jnp.mean(pre, axis=-1, keepdims=True)
    cen = pre - mu
    rstd = lax.rsqrt(jnp.mean(jnp.square(cen), axis=-1, keepdims=True) + LN_EPS)
    return cen * rstd, rstd


def _ln_mod_fwd_call(x, y, gate, gamma, beta, sc, sh, group_rows):
    r, d = x.shape
    ng = gate.shape[0]
    gmap = _ln_group_map(group_rows, ng)

    def body(x_ref, y_ref, gate_ref, gamma_ref, beta_ref, sc_ref, sh_ref, xn_ref, h_ref):
        xh, _ = _ln_stats(x_ref[...], y_ref[...], gate_ref[0])
        xn = xh * gamma_ref[...] + beta_ref[...]
        xn_ref[...] = xn
        h_ref[...] = xn * (1.0 + sc_ref[0]) + sh_ref[0]

    row = pl.BlockSpec((LN_ROW_TILE, d), lambda i: (i, 0))
    grp = pl.BlockSpec((1, 1, d), gmap)
    vec = pl.BlockSpec((1, d), lambda i: (0, 0))
    return _pcall(
        body, name="ln_mod_fwd", grid=(r // LN_ROW_TILE,),
        in_specs=[row, row, grp, vec, vec, grp, grp], out_specs=[row, row],
        out_shape=[jax.ShapeDtypeStruct((r, d), F32)] * 2,
        compiler_params=pltpu.CompilerParams(dimension_semantics=("parallel",)),
    )(x, y, gate, gamma, beta, sc, sh)


def _ln_mod_bwd_call(x, y, gate, gamma, beta, sc, dxn, dh, group_rows):
    r, d = x.shape
    ng = gate.shape[0]
    gmap = _ln_group_map(group_rows, ng)
    per = group_rows // LN_ROW_TILE

    def body(x_ref, y_ref, gate_ref, gamma_ref, beta_ref, sc_ref, dxn_ref, dh_ref,
             dx_ref, dy_ref, dgate_ref, dgamma_ref, dbeta_ref, dsc_ref, dsh_ref):
        i = pl.program_id(0)
        yv, gate_v, gamma_v = y_ref[...], gate_ref[0], gamma_ref[...]
        xh, rstd = _ln_stats(x_ref[...], yv, gate_v)
        dhv = dh_ref[...]
        dtot = dxn_ref[...] + dhv * (1.0 + sc_ref[0])
        dxh = dtot * gamma_v
        dpre = rstd * (dxh - jnp.mean(dxh, axis=-1, keepdims=True) - xh * jnp.mean(dxh * xh, axis=-1, keepdims=True))
        dx_ref[...] = DEEPNORM_ALPHA * dpre
        dy_ref[...] = gate_v * dpre
        col = lambda t: jnp.sum(t, axis=0, keepdims=True)

        @pl.when(i == 0)
        def _():
            dgamma_ref[...] = jnp.zeros_like(dgamma_ref)
            dbeta_ref[...] = jnp.zeros_like(dbeta_ref)

        first_of_group = (i % per == 0) | (i == (ng - 1) * per)

        @pl.when(first_of_group & (i <= (ng - 1) * per))
        def _():
            dgate_ref[...] = jnp.zeros_like(dgate_ref)
            dsc_ref[...] = jnp.zeros_like(dsc_ref)
            dsh_ref[...] = jnp.zeros_like(dsh_ref)

        dgamma_ref[...] += col(dtot * xh)
        dbeta_ref[...] += col(dtot)
        dgate_ref[0] += col(dpre * yv)
        dsc_ref[0] += col(dhv * (xh * gamma_v + beta_ref[...]))
        dsh_ref[0] += col(dhv)

    row = pl.BlockSpec((LN_ROW_TILE, d), lambda i: (i, 0))
    grp = pl.BlockSpec((1, 1, d), gmap)
    vec = pl.BlockSpec((1, d), lambda i: (0, 0))
    big = jax.ShapeDtypeStruct((r, d), F32)
    gs = jax.ShapeDtypeStruct((ng, 1, d), F32)
    vs = jax.ShapeDtypeStruct((1, d), F32)
    return _pcall(
        body, name="ln_mod_bwd", grid=(r // LN_ROW_TILE,),
        in_specs=[row, row, grp, vec, vec, grp, row, row],
        out_specs=[row, row, grp, vec, vec, grp, grp],
        out_shape=[big, big, gs, vs, vs, gs, gs],
        compiler_params=pltpu.CompilerParams(dimension_semantics=("arbitrary",)),
    )(x, y, gate, gamma, beta, sc, dxn, dh)


@functools.partial(jax.custom_vjp, nondiff_argnums=(7,))
def ln_mod(x, y, gate, gamma, beta, sc, sh, group_rows):
    return tuple(_ln_mod_fwd_call(x, y, gate, gamma, beta, sc, sh, group_rows))


def _ln_mod_fwd(x, y, gate, gamma, beta, sc, sh, group_rows):
    xn, h = _ln_mod_fwd_call(x, y, gate, gamma, beta, sc, sh, group_rows)
    return (xn, h), (x, y, gate, gamma, beta, sc)


def _ln_mod_bwd(group_rows, res, cts):
    x, y, gate, gamma, beta, sc = res
    dxn, dh = cts
    return tuple(_ln_mod_bwd_call(x, y, gate, gamma, beta, sc, dxn, dh, group_rows))


ln_mod.defvjp(_ln_mod_fwd, _ln_mod_bwd)


def _attn_probs(q, k, sink, scale, has_sink):
    s = _bdot(q, k, NT) * scale
    m = jnp.max(s, axis=-1, keepdims=True)
    if has_sink:
        m = jnp.maximum(m, sink)
    p = jnp.exp(s - m)
    den = jnp.sum(p, axis=-1, keepdims=True)
    p_sink = None
    if has_sink:
        p_sink = jnp.exp(sink - m)
        den = den + p_sink
    inv = 1.0 / den
    if has_sink:
        p_sink = p_sink * inv
    return p * inv, p_sink


def _attn_full_fwd_call(q, k, v, sink, scale, has_sink):
    g, sq, dq = q.shape
    nk, dv = k.shape[1], v.shape[2]
    bq = _pick(sq, (256, 128))

    def body(q_ref, k_ref, v_ref, sink_ref, o_ref):
        p, _ = _attn_probs(q_ref[0], k_ref[0], sink_ref[0, :, 0:1], scale, has_sink)
        o_ref[0] = _bdot(p, v_ref[0], NN)

    return _pcall(
        body, name="attn_full_fwd", grid=(g, sq // bq),
        in_specs=[pl.BlockSpec((1, bq, dq), lambda b, i: (b, i, 0)), pl.BlockSpec((1, nk, dq), lambda b, i: (b, 0, 0)),
                  pl.BlockSpec((1, nk, dv), lambda b, i: (b, 0, 0)), pl.BlockSpec((1, 1, 128), lambda b, i: (b, 0, 0))],
        out_specs=pl.BlockSpec((1, bq, dv), lambda b, i: (b, i, 0)),
        out_shape=jax.ShapeDtypeStruct((g, sq, dv), F32),
        compiler_params=pltpu.CompilerParams(dimension_semantics=("parallel", "arbitrary")),
    )(q, k, v, sink)


def _attn_full_bwd_call(q, k, v, sink, o, do, scale, has_sink):
    g, sq, dq = q.shape
    nk, dv = k.shape[1], v.shape[2]
    bq = _pick(sq, (256, 128))

    def body(q_ref, k_ref, v_ref, sink_ref, o_ref, do_ref, dq_ref, dk_ref, dv_ref, dsink_ref):
        i = pl.program_id(1)
        qv, kv, vv, dov = q_ref[0], k_ref[0], v_ref[0], do_ref[0]
        p, p_sink = _attn_probs(qv, kv, sink_ref[0, :, 0:1], scale, has_sink)
        delta = jnp.sum(dov * o_ref[0], axis=-1, keepdims=True)
        dv_part = _bdot(p, dov, TN)
        dp = _bdot(dov, vv, NT)
        ds = p * (dp - delta) * scale
        dq_ref[0] = _bdot(ds, kv, NN)
        dk_part = _bdot(ds, qv, TN)
        if has_sink:
            dsk = jnp.broadcast_to(-jnp.sum(p_sink * delta, axis=0, keepdims=True), (1, 128))
        else:
            dsk = jnp.zeros((1, 128), F32)

        @pl.when(i == 0)
        def _():
            dk_ref[0] = dk_part
            dv_ref[0] = dv_part
            dsink_ref[0] = dsk

        @pl.when(i > 0)
        def _():
            dk_ref[0] += dk_part
            dv_ref[0] += dv_part
            dsink_ref[0] += dsk

    qspec = pl.BlockSpec((1, bq, dq), lambda b, i: (b, i, 0))
    kspec = pl.BlockSpec((1, nk, dq), lambda b, i: (b, 0, 0))
    vspec = pl.BlockSpec((1, nk, dv), lambda b, i: (b, 0, 0))
    ospec = pl.BlockSpec((1, bq, dv), lambda b, i: (b, i, 0))
    sspec = pl.BlockSpec((1, 1, 128), lambda b, i: (b, 0, 0))
    return _pcall(
        body, name="attn_full_bwd", grid=(g, sq // bq),
        in_specs=[qspec, kspec, vspec, sspec, ospec, ospec],
        out_specs=[qspec, kspec, vspec, sspec],
        out_shape=[jax.ShapeDtypeStruct(q.shape, F32), jax.ShapeDtypeStruct(k.shape, F32),
                   jax.ShapeDtypeStruct(v.shape, F32), jax.ShapeDtypeStruct(sink.shape, F32)],
        compiler_params=pltpu.CompilerParams(dimension_semantics=("parallel", "arbitrary")),
    )(q, k, v, sink, o, do)


@functools.partial(jax.custom_vjp, nondiff_argnums=(4, 5))
def attn_full(q, k, v, sink, scale, has_sink):
    return _attn_full_fwd_call(q, k, v, sink, scale, has_sink)


def _attn_full_fwd(q, k, v, sink, scale, has_sink):
    o = _attn_full_fwd_call(q, k, v, sink, scale, has_sink)
    return o, (q, k, v, sink, o)


def _attn_full_bwd(scale, has_sink, res, do):
    q, k, v, sink, o = res
    dq, dk, dv, dsink = _attn_full_bwd_call(q, k, v, sink, o, do, scale, has_sink)
    return dq, dk, dv, _lane0(dsink)


attn_full.defvjp(_attn_full_fwd, _attn_full_bwd)


SWA_GROUP = SWA_HEADS // SWA_KV_HEADS


def _swa_rows(ref):
    return ref[...].reshape(SWA_GROUP * SWA_BLOCK, ref.shape[-1])


def _swa_sink_rows(sink_ref):
    head = lax.broadcasted_iota(jnp.int32, (SWA_GROUP * SWA_BLOCK, 1), 0) // SWA_BLOCK
    out = jnp.zeros((SWA_GROUP * SWA_BLOCK, 1), F32)
    for j in range(SWA_GROUP):
        out = jnp.where(head == j, sink_ref[j, :, 0:1], out)
    return out


def _swa_probs(q, kw, kc, sink, i, s_len, scale):
    w = SWA_BLOCK
    rows = q.shape[0]
    s_loc = _bdot(q, kw, NT) * scale
    qpos = i * w + lax.broadcasted_iota(jnp.int32, (rows, 3 * w), 0) % w
    kpos = (i - 1) * w + lax.broadcasted_iota(jnp.int32, (rows, 3 * w), 1)
    valid = (jnp.abs(kpos - qpos) <= SWA_WINDOW) & (kpos >= 0) & (kpos < s_len)
    s_loc = jnp.where(valid, s_loc, -jnp.inf)
    s_ctx = _bdot(q, kc, NT) * scale
    m = jnp.maximum(jnp.maximum(jnp.max(s_loc, axis=-1, keepdims=True), jnp.max(s_ctx, axis=-1, keepdims=True)), sink)
    p_loc = jnp.exp(s_loc - m)
    p_ctx = jnp.exp(s_ctx - m)
    p_sink = jnp.exp(sink - m)
    inv = 1.0 / (jnp.sum(p_loc, axis=-1, keepdims=True) + jnp.sum(p_ctx, axis=-1, keepdims=True) + p_sink)
    return p_loc * inv, p_ctx * inv, p_sink * inv


def _swa_fwd_call(q, kp, vp, kc, vc, sink, scale):
    g, s_len, d = q.shape
    l_ctx = kc.shape[1]
    w = SWA_BLOCK
    grp = SWA_GROUP

    def body(q_ref, kp_ref, vp_ref, kc_ref, vc_ref, sink_ref, o_ref):
        i = pl.program_id(1)
        start = pl.multiple_of(i * w, w)
        kw = kp_ref[0, pl.ds(start, 3 * w), :]
        vw = vp_ref[0, pl.ds(start, 3 * w), :]
        p_loc, p_ctx, _ = _swa_probs(_swa_rows(q_ref), kw, kc_ref[0], _swa_sink_rows(sink_ref), i, s_len, scale)
        o_ref[...] = (_bdot(p_loc, vw, NN) + _bdot(p_ctx, vc_ref[0], NN)).reshape(grp, w, d)

    return _pcall(
        body, name="swa_fwd", grid=(g // grp, s_len // w),
        in_specs=[pl.BlockSpec((grp, w, d), lambda b, i: (b, i, 0)),
                  pl.BlockSpec((1, s_len + 2 * w, d), lambda b, i: (b, 0, 0)),
                  pl.BlockSpec((1, s_len + 2 * w, d), lambda b, i: (b, 0, 0)),
                  pl.BlockSpec((1, l_ctx, d), lambda b, i: (b, 0, 0)),
                  pl.BlockSpec((1, l_ctx, d), lambda b, i: (b, 0, 0)),
                  pl.BlockSpec((grp, 1, 128), lambda b, i: (b, 0, 0))],
        out_specs=pl.BlockSpec((grp, w, d), lambda b, i: (b, i, 0)),
        out_shape=jax.ShapeDtypeStruct(q.shape, F32),
        compiler_params=pltpu.CompilerParams(dimension_semantics=("parallel", "arbitrary")),
    )(q, kp, vp, kc, vc, sink)


def _swa_bwd_call(q, kp, vp, kc, vc, sink, o, do, scale):
    g, s_len, d = q.shape
    l_ctx = kc.shape[1]
    w = SWA_BLOCK
    grp = SWA_GROUP
    sp = s_len + 2 * w

    def body(q_ref, kp_ref, vp_ref, kc_ref, vc_ref, sink_ref, o_ref, do_ref,
             dq_ref, dkp_ref, dvp_ref, dkc_ref, dvc_ref, dsink_ref):
        i = pl.program_id(1)
        start = pl.multiple_of(i * w, w)
        qv, dov = _swa_rows(q_ref), _swa_rows(do_ref)
        kw = kp_ref[0, pl.ds(start, 3 * w), :]
        vw = vp_ref[0, pl.ds(start, 3 * w), :]
        kcv, vcv = kc_ref[0], vc_ref[0]
        p_loc, p_ctx, p_sink = _swa_probs(qv, kw, kcv, _swa_sink_rows(sink_ref), i, s_len, scale)
        delta = jnp.sum(dov * _swa_rows(o_ref), axis=-1, keepdims=True)
        ds_loc = p_loc * (_bdot(dov, vw, NT) - delta) * scale
        ds_ctx = p_ctx * (_bdot(dov, vcv, NT) - delta) * scale
        dq_ref[...] = (_bdot(ds_loc, kw, NN) + _bdot(ds_ctx, kcv, NN)).reshape(grp, w, d)
        dsk = jnp.broadcast_to(-jnp.sum((p_sink * delta).reshape(grp, w, 1), axis=1, keepdims=True), (grp, 1, 128))

        @pl.when(i == 0)
        def _():
            dkp_ref[...] = jnp.zeros_like(dkp_ref)
            dvp_ref[...] = jnp.zeros_like(dvp_ref)
            dkc_ref[...] = jnp.zeros_like(dkc_ref)
            dvc_ref[...] = jnp.zeros_like(dvc_ref)
            dsink_ref[...] = jnp.zeros_like(dsink_ref)

        dkp_ref[0, pl.ds(start, 3 * w), :] += _bdot(ds_loc, qv, TN)
        dvp_ref[0, pl.ds(start, 3 * w), :] += _bdot(p_loc, dov, TN)
        dkc_ref[0] += _bdot(ds_ctx, qv, TN)
        dvc_ref[0] += _bdot(p_ctx, dov, TN)
        dsink_ref[...] += dsk

    qspec = pl.BlockSpec((grp, w, d), lambda b, i: (b, i, 0))
    kspec = pl.BlockSpec((1, sp, d), lambda b, i: (b, 0, 0))
    cspec = pl.BlockSpec((1, l_ctx, d), lambda b, i: (b, 0, 0))
    sspec = pl.BlockSpec((grp, 1, 128), lambda b, i: (b, 0, 0))
    return _pcall(
        body, name="swa_bwd", grid=(g // grp, s_len // w),
        in_specs=[qspec, kspec, kspec, cspec, cspec, sspec, qspec, qspec],
        out_specs=[qspec, kspec, kspec, cspec, cspec, sspec],
        out_shape=[jax.ShapeDtypeStruct(q.shape, F32), jax.ShapeDtypeStruct(kp.shape, F32),
                   jax.ShapeDtypeStruct(vp.shape, F32), jax.ShapeDtypeStruct(kc.shape, F32),
                   jax.ShapeDtypeStruct(vc.shape, F32), jax.ShapeDtypeStruct(sink.shape, F32)],
        compiler_params=pltpu.CompilerParams(dimension_semantics=("parallel", "arbitrary")),
    )(q, kp, vp, kc, vc, sink, o, do)


@functools.partial(jax.custom_vjp, nondiff_argnums=(6,))
def swa_attn(q, kp, vp, kc, vc, sink, scale):
    return _swa_fwd_call(q, kp, vp, kc, vc, sink, scale)


def _swa_attn_fwd(q, kp, vp, kc, vc, sink, scale):
    o = _swa_fwd_call(q, kp, vp, kc, vc, sink, scale)
    return o, (q, kp, vp, kc, vc, sink, o)


def _swa_attn_bwd(scale, res, do):
    q, kp, vp, kc, vc, sink, o = res
    dq, dkp, dvp, dkc, dvc, dsink = _swa_bwd_call(q, kp, vp, kc, vc, sink, o, do, scale)
    return dq, dkp, dvp, dkc, dvc, _lane0(dsink)


swa_attn.defvjp(_swa_attn_fwd, _swa_attn_bwd)


def _f32dot(a, b, dims):
    return lax.dot_general(a, b, dims, precision=lax.Precision.HIGHEST, preferred_element_type=F32)


DN_SOLVE_BLOCK = 16


def _unit_lower_inverse(a, a_t, transposed):
    g, c, _ = a.shape
    nb = DN_SOLVE_BLOCK
    row = lax.broadcasted_iota(jnp.int32, (1, c, c), 1)
    col = lax.broadcasted_iota(jnp.int32, (1, c, c), 2)
    src, off = (a, a_t) if transposed else (a_t, a)
    coef = jnp.zeros((g, c, nb), F32)
    for b in range(c // nb):
        in_block = (lax.broadcasted_iota(jnp.int32, (1, c, nb), 1) // nb) == b
        coef = coef + jnp.where(in_block, src[:, :, b * nb:(b + 1) * nb], 0.0)
    sub = lax.broadcasted_iota(jnp.int32, (1, c // nb, nb, c), 2)
    x = jnp.broadcast_to((row == col).astype(F32), a.shape)
    for i in (range(nb - 2, -1, -1) if transposed else range(1, nb)):
        prod = (coef[:, :, i:i + 1] * x).reshape(g, c // nb, nb, c)
        new_rows = -jnp.sum(prod, axis=2, keepdims=True)
        x = x + jnp.where(sub == i, new_rows, 0.0).reshape(g, c, c)
    width = nb
    while width < c:
        joins = ((row // (2 * width)) == (col // (2 * width))) & ((row // width) != (col // width))
        x = x - _f32dot(x, _f32dot(jnp.where(joins, off, 0.0), x, BNN), BNN)
        width *= 2
    return x


def _dn_masks(c):
    row = lax.broadcasted_iota(jnp.int32, (1, c, c), 1)
    col = lax.broadcasted_iota(jnp.int32, (1, c, c), 2)
    return row, col


def _sweep_chunks(n, n_ctx):
    return (lambda i: i), (lambda i: jnp.where(i < n_ctx, n_ctx - 1 - i, n + n_ctx - 1 - i))


def _half_spec(gh, tail, half, chunk_of):
    return pl.BlockSpec((gh, 1) + tail, lambda i: (half, chunk_of(i), 0, 0))


def _both(ref_f, ref_b):
    return jnp.concatenate([ref_f[:, 0], ref_b[:, 0]], axis=0)


def _dn_direction_masks(g, c):
    backward = lax.broadcasted_iota(jnp.int32, (g, 1, 1), 0) >= g // 2
    row, col = _dn_masks(c)
    return backward, jnp.where(backward, c - 1 - row, row), jnp.where(backward, c - 1 - col, col)


def _dn_inverse(a_mat, a_t, transposed):
    h = a_mat.shape[0] // 2
    return jnp.concatenate([_unit_lower_inverse(a_mat[:h], a_t[:h], transposed),
                            _unit_lower_inverse(a_t[h:], a_mat[h:], not transposed)], axis=0)


def _dn_fwd_call(q, k, k_t, v, gc, bb, gr, n_ctx):
    gh, n, c, _ = q.shape
    g = 2 * gh

    def body(qf, qb, kf, kb_, ktf, ktb, vf, vb, gcf, gcb, bbf, bbb, grf, grb,
             of_ref, ob_ref, vn_ref, sall_ref, w_ref, u_ref, s_scr):
        i = pl.program_id(0)

        @pl.when(i == 0)
        def _():
            s_scr[...] = jnp.zeros_like(s_scr)

        qv, kv, ktv, vv, gcv, bv, grv = (_both(qf, qb), _both(kf, kb_), _both(ktf, ktb), _both(vf, vb), _both(gcf, gcb),
                                          _both(bbf, bbb), _both(grf, grb))
        backward, row, col = _dn_direction_masks(g, c)
        e = jnp.exp(gcv)
        kb = kv * bv
        decay = jnp.exp(jnp.where(row >= col, gcv - grv, -jnp.inf))
        decay_ts = jnp.exp(jnp.where(row < col, grv - gcv, -jnp.inf))
        a_mat = _bdot(kb, kv, BNT) * jnp.where(row > col, decay, 0.0)
        t = _dn_inverse(a_mat, _bdot(kv, kb, BNT) * decay_ts, False)
        w = _f32dot(t, kb * e, BNN)
        u = _f32dot(t, vv * bv, BNN)
        glast = jnp.where(backward, grv[:, :, 0:1], grv[:, :, c - 1:c])
        s = s_scr[...]
        sall_ref[:, 0] = s
        vnew = u - _bdot(w, s, BNN)
        o = _bdot(qv * e, s, BNN) + _bdot(_bdot(qv, kv, BNT) * decay, vnew, BNN)
        of_ref[:, 0] = o[:gh]
        ob_ref[:, 0] = o[gh:]
        vn_ref[:, 0] = vnew
        w_ref[:, 0] = w
        u_ref[:, 0] = u
        s_scr[...] = s * jnp.exp(glast) + _bdot(ktv * jnp.exp(glast - grv), vnew, BNN)

    cf, cb = _sweep_chunks(n, n_ctx)
    tok = lambda half, chunk_of: _half_spec(gh, (c, c), half, chunk_of)
    rowv = lambda half, chunk_of: _half_spec(gh, (1, c), half, chunk_of)
    step = pl.BlockSpec((g, 1, c, c), lambda i: (0, i, 0, 0))
    shared = [tok(0, cf), tok(0, cb)]
    split = [tok(0, cf), tok(1, cb)]
    return _pcall(
        body, name="dn_fwd", grid=(n,),
        in_specs=shared * 4 + split * 2 + [rowv(0, cf), rowv(1, cb)],
        out_specs=[tok(0, cf), tok(0, cb)] + [step] * 4,
        out_shape=[jax.ShapeDtypeStruct((gh, n, c, c), F32)] * 2 + [jax.ShapeDtypeStruct((g, n, c, c), F32)] * 4,
        scratch_shapes=[pltpu.VMEM((g, c, c), F32)],
        compiler_params=pltpu.CompilerParams(dimension_semantics=("arbitrary",)),
    )(q, q, k, k, k_t, k_t, v, v, gc, gc, bb, bb, gr, gr)


def _dn_bwd_call(q, k, q_t, k_t, v, gc, bb, gr, br, sall, vn, w, u, do_f, do_b, n_ctx):
    gh, n, c, _ = q.shape
    g = 2 * gh

    def body(qf, qb, kf, kb_, qtf, qtb, ktf, ktb, vf, vb, gcf, gcb, bbf, bbb, grf, grb, brf, brb,
             sall_ref, vn_ref, w_ref, u_ref, dof, dob,
             dqf, dqb, dkf, dkb_, dvf, dvb, dgcf, dgcb, dbbf, dbbb, dgrf, dgrb, ds_scr):
        i = pl.program_id(0)

        @pl.when(i == 0)
        def _():
            ds_scr[...] = jnp.zeros_like(ds_scr)

        qv, kv, qtv, ktv, vv = _both(qf, qb), _both(kf, kb_), _both(qtf, qtb), _both(ktf, ktb), _both(vf, vb)
        gcv, bv, grv, brv, dov = _both(gcf, gcb), _both(bbf, bbb), _both(grf, grb), _both(brf, brb), _both(dof, dob)
        s, vnew, w, u = sall_ref[:, 0], vn_ref[:, 0], w_ref[:, 0], u_ref[:, 0]
        dsn = ds_scr[...]
        backward, row, col = _dn_direction_masks(g, c)
        e = jnp.exp(gcv)
        er = jnp.exp(grv)
        kb = kv * bv
        decay = jnp.exp(jnp.where(row >= col, gcv - grv, -jnp.inf))
        decay_s = jnp.where(row > col, decay, 0.0)
        decay_t = jnp.exp(jnp.where(row <= col, grv - gcv, -jnp.inf))
        decay_ts = jnp.where(row < col, decay_t, 0.0)
        kk = _bdot(kb, kv, BNT)
        tt = _dn_inverse(kk * decay_s, _bdot(kv, kb, BNT) * decay_ts, True)
        glast = jnp.where(backward, grv[:, :, 0:1], grv[:, :, c - 1:c])
        eg = jnp.exp(glast)
        x = jnp.exp(glast - gcv)
        kt = kv * x
        qk_raw = _bdot(qv, kv, BNT)
        w_t = _f32dot(ktv * (brv * er), tt, BNN)
        dvn = _bdot(_bdot(kv, qv, BNT) * decay_t, dov, BNN) + _bdot(kt, dsn, BNN)
        dqk = _bdot(dov, vnew, BNT)
        dqk_t = _bdot(vnew, dov, BNT)
        dqd = _bdot(dov, s, BNT)
        dkt = _bdot(vnew, dsn, BNT)
        deg = jnp.sum(jnp.sum(dsn * s, axis=2, keepdims=True), axis=1, keepdims=True)
        dw = -_bdot(dvn, s, BNT)
        ds_scr[...] = dsn * eg + _bdot(qtv * er, dov, BNN) - _bdot(w_t, dvn, BNN)
        dwp = _f32dot(tt, dw, BNN)
        dup = _f32dot(tt, dvn, BNN)
        d_a = -(_bdot(dwp, w, BNT) + _bdot(dup, u, BNT))
        d_at = -(_bdot(w, dwp, BNT) + _bdot(u, dup, BNT))
        dkb = _bdot(d_a * decay_s, kv, BNN) + dwp * e
        dkx = dkt * kv * x
        ddiff = dqk * qk_raw * decay + d_a * kk * decay_s
        dglast = jnp.sum(jnp.sum(dkx, axis=2, keepdims=True), axis=1, keepdims=True) + deg * eg
        lane = lax.broadcasted_iota(jnp.int32, (1, 1, c), 2)
        last_lane = jnp.where(backward, 0, c - 1)
        results = (
            (dqf, dqb, dqd * e + _bdot(dqk * decay, kv, BNN)),
            (dkf, dkb_, _bdot(d_at * decay_ts, kb, BNN) + dkb * bv + dkt * x + _bdot(dqk_t * decay_t, qv, BNN)),
            (dvf, dvb, dup * bv),
            (dgcf, dgcb, ddiff + (dwp * kb + dqd * qv) * e - dkx),
            (dbbf, dbbb, dkb * kv + dup * vv),
            (dgrf, dgrb, jnp.where(lane == last_lane, dglast, 0.0) - jnp.sum(ddiff, axis=1, keepdims=True)),
        )
        for ref_f, ref_b, val in results:
            ref_f[:, 0] = val[:gh]
            ref_b[:, 0] = val[gh:]

    cf, cb = _sweep_chunks(n, n_ctx)
    rf, rb = (lambda i: cf(n - 1 - i)), (lambda i: cb(n - 1 - i))
    tok = lambda half, chunk_of: _half_spec(gh, (c, c), half, chunk_of)
    rowv = lambda half, chunk_of: _half_spec(gh, (1, c), half, chunk_of)
    step = pl.BlockSpec((g, 1, c, c), lambda i: (0, n - 1 - i, 0, 0))
    shared = [tok(0, rf), tok(0, rb)]
    split = [tok(0, rf), tok(1, rb)]
    split_row = [rowv(0, rf), rowv(1, rb)]
    big = jax.ShapeDtypeStruct((gh, n, c, c), F32)
    return _pcall(
        body, name="dn_bwd", grid=(n,),
        in_specs=shared * 5 + split * 2 + split_row * 2 + [step] * 4 + shared,
        out_specs=shared * 5 + [rowv(0, rf), rowv(0, rb)],
        out_shape=[big] * 10 + [jax.ShapeDtypeStruct((gh, n, 1, c), F32)] * 2,
        scratch_shapes=[pltpu.VMEM((g, c, c), F32)],
        compiler_params=pltpu.CompilerParams(dimension_semantics=("arbitrary",)),
    )(q, q, k, k, q_t, q_t, k_t, k_t, v, v, gc, gc, bb, bb, gr, gr, br, br, sall, vn, w, u, do_f, do_b)


_t = lambda a: jnp.swapaxes(a, -1, -2)


def _dn_forms(gcum, beta, d):
    lanes = lambda t: jnp.broadcast_to(t[..., None], t.shape + (d,))
    return lanes(gcum), lanes(beta), gcum[:, :, None, :], beta[:, :, None, :]


@functools.partial(jax.custom_vjp, nondiff_argnums=(5,))
def dn_chunked(q, k, v, gcum, beta, n_ctx):
    gc, bb, gr, _ = _dn_forms(gcum, beta, q.shape[-1])
    return tuple(_dn_fwd_call(q, k, _t(k), v, gc, bb, gr, n_ctx)[:2])


def _dn_chunked_fwd(q, k, v, gcum, beta, n_ctx):
    gc, bb, gr, _ = _dn_forms(gcum, beta, q.shape[-1])
    o_f, o_b, vn, sall, w, u = _dn_fwd_call(q, k, _t(k), v, gc, bb, gr, n_ctx)
    return (o_f, o_b), (q, k, v, gcum, beta, vn, sall, w, u)


def _dn_chunked_bwd(n_ctx, res, cts):
    q, k, v, gcum, beta, vn, sall, w, u = res
    gc, bb, gr, br = _dn_forms(gcum, beta, q.shape[-1])
    (dq_f, dq_b, dk_f, dk_b, dv_f, dv_b, dgc_f, dgc_b, dbb_f, dbb_b, dgr_f, dgr_b) = _dn_bwd_call(
        q, k, _t(q), _t(k), v, gc, bb, gr, br, sall, vn, w, u, cts[0], cts[1], n_ctx)
    dgcum = jnp.concatenate([jnp.sum(dgc_f, axis=-1) + dgr_f[:, :, 0, :], jnp.sum(dgc_b, axis=-1) + dgr_b[:, :, 0, :]], axis=0)
    dbeta = jnp.concatenate([jnp.sum(dbb_f, axis=-1), jnp.sum(dbb_b, axis=-1)], axis=0)
    return dq_f + dq_b, dk_f + dk_b, dv_f + dv_b, dgcum, dbeta


dn_chunked.defvjp(_dn_chunked_fwd, _dn_chunked_bwd)


def _ret_fwd_call(q, k, k_t, v, dmat, xi_b, zeta_r, gm, n_ctx):
    gh, n, c, dk = q.shape
    dv = v.shape[-1]
    g = 2 * gh

    def body(qf, qb, kf, kb_, ktf, ktb, vf, vb, d_ref, xib_ref, zr_ref, gm_ref, of_ref, ob_ref, starts_ref, s_scr):
        i = pl.program_id(0)

        @pl.when(i == 0)
        def _():
            s_scr[...] = jnp.zeros_like(s_scr)

        qv, kv, ktv, vv = _both(qf, qb), _both(kf, kb_), _both(ktf, ktb), _both(vf, vb)
        s = s_scr[...]
        starts_ref[:, 0] = s
        o = _bdot(_bdot(qv, kv, BNT) * d_ref[...], vv, BNN) + _bdot(qv * xib_ref[...], s, BNN)
        of_ref[:, 0] = o[:gh]
        ob_ref[:, 0] = o[gh:]
        s_scr[...] = s * gm_ref[...] + _bdot(ktv * zr_ref[...], vv, BNN)

    cf, cb = _sweep_chunks(n, n_ctx)
    pair = lambda tail: [_half_spec(gh, tail, 0, cf), _half_spec(gh, tail, 0, cb)]
    const = lambda a, b: pl.BlockSpec((g, a, b), lambda i: (0, 0, 0))
    return _pcall(
        body, name="ret_fwd", grid=(n,),
        in_specs=pair((c, dk)) * 2 + pair((dk, c)) + pair((c, dv)) + [const(c, c), const(c, dk), const(1, c), const(dk, dv)],
        out_specs=pair((c, dv)) + [pl.BlockSpec((g, 1, dk, dv), lambda i: (0, i, 0, 0))],
        out_shape=[jax.ShapeDtypeStruct((gh, n, c, dv), F32)] * 2 + [jax.ShapeDtypeStruct((g, n, dk, dv), F32)],
        scratch_shapes=[pltpu.VMEM((g, dk, dv), F32)],
        compiler_params=pltpu.CompilerParams(dimension_semantics=("arbitrary",)),
    )(q, q, k, k, k_t, k_t, v, v, dmat, xi_b, zeta_r, gm)


def _ret_bwd_call(q, k, q_t, k_t, v, dmat, dmat_t, xi_b, xi_r, zeta_b, gm, starts, do_f, do_b, n_ctx):
    gh, n, c, dk = q.shape
    dv = v.shape[-1]
    g = 2 * gh

    def body(qf, qb, kf, kb_, qtf, qtb, ktf, ktb, vf, vb, d_ref, dt_ref, xib_ref, xr_ref, zb_ref, gm_ref, starts_ref,
             dof, dob, dqf, dqb, dkf, dkb_, dvf, dvb, dd_ref, dxib_ref, dzb_ref, dgm_ref, ds_scr):
        i = pl.program_id(0)

        @pl.when(i == 0)
        def _():
            ds_scr[...] = jnp.zeros_like(ds_scr)
            dd_ref[...] = jnp.zeros_like(dd_ref)
            dxib_ref[...] = jnp.zeros_like(dxib_ref)
            dzb_ref[...] = jnp.zeros_like(dzb_ref)
            dgm_ref[...] = jnp.zeros_like(dgm_ref)

        qv, kv, qtv, vv, dov = _both(qf, qb), _both(kf, kb_), _both(qtf, qtb), _both(vf, vb), _both(dof, dob)
        s, dsn = starts_ref[:, 0], ds_scr[...]
        dm, dmt, zb = d_ref[...], dt_ref[...], zb_ref[...]
        qk_raw = _bdot(qv, kv, BNT)
        dqkd = _bdot(dov, vv, BNT)
        do_s = _bdot(dov, s, BNT)
        dkz = _bdot(vv, dsn, BNT)
        results = ((dqf, dqb, _bdot(dqkd * dm, kv, BNN) + do_s * xib_ref[...]),
                   (dkf, dkb_, _bdot(_bdot(vv, dov, BNT) * dmt, qv, BNN) + dkz * zb),
                   (dvf, dvb, _bdot(_bdot(kv, qv, BNT) * dmt, dov, BNN) + _bdot(kv * zb, dsn, BNN)))
        for ref_f, ref_b, val in results:
            ref_f[:, 0] = val[:gh]
            ref_b[:, 0] = val[gh:]
        dd_ref[...] += dqkd * qk_raw
        dxib_ref[...] += do_s * qv
        dzb_ref[...] += dkz * kv
        dgm_ref[...] += dsn * s
        ds_scr[...] = dsn * gm_ref[...] + _bdot(qtv * xr_ref[...], dov, BNN)

    cf, cb = _sweep_chunks(n, n_ctx)
    rf, rb = (lambda i: cf(n - 1 - i)), (lambda i: cb(n - 1 - i))
    pair = lambda tail: [_half_spec(gh, tail, 0, rf), _half_spec(gh, tail, 0, rb)]
    const = lambda a, b: pl.BlockSpec((g, a, b), lambda i: (0, 0, 0))
    sds = lambda *s: jax.ShapeDtypeStruct(s, F32)
    return _pcall(
        body, name="ret_bwd", grid=(n,),
        in_specs=pair((c, dk)) * 2 + pair((dk, c)) * 2 + pair((c, dv))
        + [const(c, c), const(c, c), const(c, dk), const(1, c), const(c, dk), const(dk, dv),
           pl.BlockSpec((g, 1, dk, dv), lambda i: (0, n - 1 - i, 0, 0))] + pair((c, dv)),
        out_specs=pair((c, dk)) * 2 + pair((c, dv)) + [const(c, c), const(c, dk), const(c, dk), const(dk, dv)],
        out_shape=[sds(gh, n, c, dk)] * 4 + [sds(gh, n, c, dv)] * 2 + [sds(g, c, c), sds(g, c, dk), sds(g, c, dk), sds(g, dk, dv)],
        scratch_shapes=[pltpu.VMEM((g, dk, dv), F32)],
        compiler_params=pltpu.CompilerParams(dimension_semantics=("arbitrary",)),
    )(q, q, k, k, q_t, q_t, k_t, k_t, v, v, dmat, dmat_t, xi_b, xi_r, zeta_b, gm, starts, do_f, do_b)


def _ret_forms(xi, zeta, gm, dk, dv):
    lanes = lambda t: jnp.broadcast_to(t[..., None], t.shape + (dk,))
    return lanes(xi), xi[:, None, :], lanes(zeta), zeta[:, None, :], jnp.broadcast_to(gm[:, None, None], gm.shape + (dk, dv))


@functools.partial(jax.custom_vjp, nondiff_argnums=(7,))
def ret_chunked(q, k, v, dmat, xi, zeta, gm, n_ctx):
    xi_b, _, _, zeta_r, gm_f = _ret_forms(xi, zeta, gm, q.shape[-1], v.shape[-1])
    return tuple(_ret_fwd_call(q, k, _t(k), v, dmat, xi_b, zeta_r, gm_f, n_ctx)[:2])


def _ret_chunked_fwd(q, k, v, dmat, xi, zeta, gm, n_ctx):
    xi_b, _, _, zeta_r, gm_f = _ret_forms(xi, zeta, gm, q.shape[-1], v.shape[-1])
    o_f, o_b, starts = _ret_fwd_call(q, k, _t(k), v, dmat, xi_b, zeta_r, gm_f, n_ctx)
    return (o_f, o_b), (q, k, v, dmat, xi, zeta, gm, starts)


def _ret_chunked_bwd(n_ctx, res, cts):
    q, k, v, dmat, xi, zeta, gm, starts = res
    xi_b, xi_r, zeta_b, _, gm_f = _ret_forms(xi, zeta, gm, q.shape[-1], v.shape[-1])
    dq_f, dq_b, dk_f, dk_b, dv_f, dv_b, dd, dxib, dzb, dgm = _ret_bwd_call(
        q, k, _t(q), _t(k), v, dmat, _t(dmat), xi_b, xi_r, zeta_b, gm_f, starts, cts[0], cts[1], n_ctx)
    return (dq_f + dq_b, dk_f + dk_b, dv_f + dv_b, dd, jnp.sum(dxib, axis=-1), jnp.sum(dzb, axis=-1),
            jnp.sum(dgm, axis=(1, 2)))


ret_chunked.defvjp(_ret_chunked_fwd, _ret_chunked_bwd)


def _peer(k):
    mx, my, mc = lax.axis_index("x"), lax.axis_index("y"), lax.axis_index("c")
    px = 1 - mx if k & 4 else mx
    py = 1 - my if k & 2 else my
    pc = 1 - mc if k & 1 else mc
    return (px, py, pc), 4 * px + 2 * py + pc


def _all_to_all_call(xs, name):
    n_arr = len(xs)
    n_peer = N_DEV - 1

    def body(*refs):
        x_refs, out_refs = refs[:n_arr], refs[n_arr:2 * n_arr]
        send_sems, recv_sems, local_sems = refs[2 * n_arr:]
        me = 4 * lax.axis_index("x") + 2 * lax.axis_index("y") + lax.axis_index("c")

        def copy(j, k, dst_idx):
            dev, idx = _peer(k)
            return pltpu.make_async_remote_copy(
                src_ref=x_refs[j].at[idx], dst_ref=out_refs[j].at[dst_idx],
                send_sem=send_sems.at[j * n_peer + k - 1], recv_sem=recv_sems.at[j * n_peer + k - 1],
                device_id=dev, device_id_type=pl.DeviceIdType.MESH)

        mine = [pltpu.make_async_copy(x_refs[j].at[me], out_refs[j].at[me], local_sems.at[j]) for j in range(n_arr)]
        for cp in mine:
            cp.start()
        sends = [copy(j, k, me) for j in range(n_arr) for k in range(1, N_DEV)]
        for cp in sends:
            cp.start()
        for j in range(n_arr):
            for k in range(1, N_DEV):
                copy(j, k, _peer(k)[1]).wait_recv()
        for cp in sends:
            cp.wait_send()
        for cp in mine:
            cp.wait()

    return _pcall(
        body, name=name,
        in_specs=[pl.BlockSpec(memory_space=pl.ANY)] * n_arr, out_specs=[pl.BlockSpec(memory_space=pl.ANY)] * n_arr,
        out_shape=[jax.ShapeDtypeStruct(x.shape, x.dtype) for x in xs],
        scratch_shapes=[pltpu.SemaphoreType.DMA((n_arr * n_peer,)), pltpu.SemaphoreType.DMA((n_arr * n_peer,)),
                        pltpu.SemaphoreType.DMA((n_arr,))],
    )(*xs)


def _gather_call(xs, name):
    n_arr = len(xs)
    per = N_DEV - 1
    chips = (2, 4, 6)

    def body(*refs):
        x_refs, out_refs = refs[:n_arr], refs[n_arr:2 * n_arr]
        send_sems, recv_sems, local_sems = refs[2 * n_arr:]
        me = 4 * lax.axis_index("x") + 2 * lax.axis_index("y") + lax.axis_index("c")
        sib_dev, sib_idx = _peer(1)

        def copy(j, s, src, slot, dev):
            return pltpu.make_async_remote_copy(
                src_ref=src, dst_ref=out_refs[j].at[slot],
                send_sem=send_sems.at[j * per + s], recv_sem=recv_sems.at[j * per + s],
                device_id=dev, device_id_type=pl.DeviceIdType.MESH)

        mine = [pltpu.make_async_copy(x_refs[j], out_refs[j].at[me], local_sems.at[j]) for j in range(n_arr)]
        for cp in mine:
            cp.start()
        sends = []
        for j in range(n_arr):
            sends.append(copy(j, 0, x_refs[j], me, sib_dev))
            for t, k in enumerate(chips):
                sends.append(copy(j, 1 + t, x_refs[j], me, _peer(k)[0]))
        for cp in sends:
            cp.start()
        for j in range(n_arr):
            for t, k in enumerate(chips):
                dev, idx = _peer(k)
                copy(j, 1 + t, x_refs[j], idx, dev).wait_recv()
                forward = copy(j, 4 + t, out_refs[j].at[idx], idx, sib_dev)
                forward.start()
                sends.append(forward)
        for j in range(n_arr):
            copy(j, 0, x_refs[j], sib_idx, sib_dev).wait_recv()
            for t, k in enumerate(chips):
                idx = _peer(k + 1)[1]
                copy(j, 4 + t, out_refs[j].at[idx], idx, sib_dev).wait_recv()
        for cp in sends:
            cp.wait_send()
        for cp in mine:
            cp.wait()

    return _pcall(
        body, name=name,
        in_specs=[pl.BlockSpec(memory_space=pl.ANY)] * n_arr, out_specs=[pl.BlockSpec(memory_space=pl.ANY)] * n_arr,
        out_shape=[jax.ShapeDtypeStruct((N_DEV,) + tuple(x.shape), x.dtype) for x in xs],
        scratch_shapes=[pltpu.SemaphoreType.DMA((n_arr * per,)), pltpu.SemaphoreType.DMA((n_arr * per,)),
                        pltpu.SemaphoreType.DMA((n_arr,))],
    )(*xs)


def _sum8_call(x, name):
    _, r, c = x.shape
    tr = _pick(r, (256, 160, 128, 72, 64, 32, 16, 8))

    def body(x_ref, o_ref):
        acc = x_ref[0].astype(F32)
        for d in range(1, N_DEV):
            acc = acc + x_ref[d].astype(F32)
        o_ref[...] = acc

    return _pcall(
        body, name=name, grid=(r // tr,),
        in_specs=[pl.BlockSpec((N_DEV, tr, c), lambda i: (0, i, 0))],
        out_specs=pl.BlockSpec((tr, c), lambda i: (i, 0)),
        out_shape=jax.ShapeDtypeStruct((r, c), F32),
        compiler_params=pltpu.CompilerParams(dimension_semantics=("parallel",)),
    )(x)


def _adamw_call(w, g, m, v, name):
    r, c = w.shape
    tr = _pick(r, (256, 128, 64, 32, 16, 8))
    bc1 = 1.0 - ADAM_B1 ** ADAM_STEP
    bc2 = 1.0 - ADAM_B2 ** ADAM_STEP

    def body(w_ref, g_ref, m_ref, v_ref, d_ref, nm_ref, nv_ref):
        gv = g_ref[...]
        nm = ADAM_B1 * m_ref[...] + (1.0 - ADAM_B1) * gv
        nv = ADAM_B2 * v_ref[...] + (1.0 - ADAM_B2) * jnp.square(gv)
        d_ref[...] = -ADAM_LR * ((nm / bc1) / (jnp.sqrt(nv / bc2) + ADAM_EPS) + ADAM_WD * w_ref[...])
        nm_ref[...] = nm
        nv_ref[...] = nv

    spec = pl.BlockSpec((tr, c), lambda i: (i, 0))
    sds = jax.ShapeDtypeStruct((r, c), F32)
    return _pcall(
        body, name=name, grid=(r // tr,), in_specs=[spec] * 4, out_specs=[spec] * 3, out_shape=[sds] * 3,
        compiler_params=pltpu.CompilerParams(dimension_semantics=("parallel",)),
    )(w, g, m, v)


def layer_norm(x, g, b):
    mu = jnp.mean(x, axis=-1, keepdims=True)
    var = jnp.mean(jnp.square(x - mu), axis=-1, keepdims=True)
    return (x - mu) * lax.rsqrt(var + LN_EPS) * g + b


def rms_norm(x, g):
    return x * lax.rsqrt(jnp.mean(x * x, axis=-1, keepdims=True) + NORM_EPS) * g


def head_layer_norm(o, g):
    b_, t_, h_, d_ = o.shape
    mu = jnp.mean(o, axis=-1, keepdims=True)
    var = jnp.mean(jnp.square(o - mu), axis=-1, keepdims=True)
    return ((o - mu) * lax.rsqrt(var + NORM_EPS)).reshape(b_, t_, h_ * d_) * g


def l2norm(t):
    return t * lax.rsqrt(jnp.sum(t * t, axis=-1, keepdims=True) + NORM_EPS)


def rope_freqs(dim):
    return ROPE_BASE ** (-jnp.arange(0, dim, 2, dtype=F32) / dim)


def axial_rope(rows, rot_dim):
    row = jnp.broadcast_to(jnp.arange(rows, dtype=F32)[:, None], (rows, GRID_W)).reshape(-1)
    col = jnp.broadcast_to(jnp.arange(GRID_W, dtype=F32)[None, :], (rows, GRID_W)).reshape(-1)
    inv = rope_freqs(rot_dim // 2)
    ang = jnp.concatenate([row[:, None] * inv, col[:, None] * inv], axis=-1)
    return jnp.cos(ang), jnp.sin(ang)


def sequence_rope(n_tok, rot_dim):
    ang = jnp.arange(n_tok, dtype=F32)[:, None] * rope_freqs(rot_dim)
    return jnp.cos(ang), jnp.sin(ang)


def apply_rope(x, cos, sin):
    x1, x2 = jnp.split(x, 2, axis=-1)
    c = cos[:, None, :]
    s = sin[:, None, :]
    return jnp.concatenate([x1 * c - x2 * s, x1 * s + x2 * c], axis=-1)


def _flip_t(t):
    return jnp.flip(t, axis=2)


def _split_columns(z):
    idx = np.cumsum(np.array(IN_SPLITS))[:-1].tolist()
    return jnp.split(z, idx, axis=-1)


def short_conv(x, w):
    k_width, ch = w.shape
    pad = k_width // 2
    return lax.conv_general_dilated(x, w[:, None, :], window_strides=(1,), padding=[(pad, pad)],
                                    dimension_numbers=('NWC', 'WIO', 'NWC'), feature_group_count=ch)


def _to_heads(t, h, d):
    b_, t_, _ = t.shape
    return t.reshape(b_, t_, h, d).transpose(0, 2, 1, 3).reshape(b_ * h, t_, d)


def _from_heads(t, b_):
    g, t_, d = t.shape
    return t.reshape(b_, g // b_, t_, d).transpose(0, 2, 1, 3).reshape(b_, t_, (g // b_) * d)


def _lane_scalar(vals):
    return jnp.broadcast_to(vals[:, None, None], (vals.shape[0], 1, 128))


def swa_group(q, k, v, qc, kc, vc, sink, cos, sin, with_ctx_out):
    b_, s_, _ = q.shape
    l_ = kc.shape[1]
    grp = SWA_HEADS // SWA_KV_HEADS
    d = SWA_HEAD_DIM
    w_ = SWA_BLOCK
    scale = d ** -0.5
    qh = apply_rope(q.reshape(b_, s_, SWA_HEADS, d), cos, sin).transpose(0, 2, 1, 3).reshape(b_ * SWA_HEADS, s_, d)
    kh = apply_rope(k.reshape(b_, s_, SWA_KV_HEADS, d), cos, sin).transpose(0, 2, 1, 3).reshape(b_ * SWA_KV_HEADS, s_, d)
    vh = _to_heads(v, SWA_KV_HEADS, d)
    kch = _to_heads(kc, SWA_KV_HEADS, d)
    vch = _to_heads(vc, SWA_KV_HEADS, d)
    padk = lambda t: jnp.pad(t, ((0, 0), (w_, w_), (0, 0)))
    sink_g = _lane_scalar(jnp.tile(sink, b_))
    y = _from_heads(swa_attn(qh, padk(kh), padk(vh), kch, vch, sink_g, scale), b_)
    yc = None
    if with_ctx_out:
        qch = _to_heads(qc, SWA_HEADS, d)
        rep = lambda t: jnp.repeat(t.reshape(b_, SWA_KV_HEADS, l_, d), grp, axis=1).reshape(b_ * SWA_HEADS, l_, d)
        yc = _from_heads(attn_full(qch, rep(kch), rep(vch), sink_g, scale, True), b_)
    return y, yc


def gated_delta_chunked(q, k, v, log_g, beta, n_ctx):
    g_, t_, dk = k.shape
    dv = v.shape[-1]
    c_ = DN_CHUNK
    n = t_ // c_
    assert dk == c_ and dv == c_
    lg = log_g.reshape(2, g_, n, c_)
    g_cum = jnp.concatenate([jnp.cumsum(lg[0], axis=-1), jnp.flip(jnp.cumsum(jnp.flip(lg[1], axis=-1), axis=-1), axis=-1)],
                            axis=0)
    o_f, o_b = dn_chunked(q.reshape(g_, n, c_, dk), k.reshape(g_, n, c_, dk), v.reshape(g_, n, c_, dv), g_cum,
                          beta.reshape(2 * g_, n, c_), n_ctx)
    return (o_f + o_b).reshape(g_, t_, dv)


def deltanet_group(qkv, z, ab, qkv_c, z_c, ab_c, conv_w, a_log, dt_bias, norm_g, with_ctx_out):
    def prep(qkv_, ab_):
        b_, t_, _ = qkv_.shape
        y = jax.nn.silu(short_conv(qkv_, conv_w))
        q, k, v = [t.reshape(b_, t_, DN_HEADS, DN_HEAD_DIM).transpose(0, 2, 1, 3) for t in jnp.split(y, 3, axis=-1)]
        q = l2norm(q) * DN_HEAD_DIM ** -0.5
        k = l2norm(k)
        ab_ = ab_.reshape(b_, t_, 2, 2, DN_HEADS)
        log_g = -jnp.exp(a_log) * jax.nn.softplus(ab_[:, :, :, 0] + dt_bias)
        beta = jax.nn.sigmoid(ab_[:, :, :, 1])
        return q, k, v, log_g.transpose(2, 0, 3, 1), beta.transpose(2, 0, 3, 1)

    def out(o, z_):
        b_, t_, _ = z_.shape
        o = rms_norm(o.transpose(0, 2, 1, 3), norm_g) * jax.nn.silu(z_).reshape(b_, t_, DN_HEADS, DN_HEAD_DIM)
        return o.reshape(b_, t_, DN_W)

    qc, kc, vc, lgc, bc = prep(qkv_c, ab_c)
    q, k, v, lg, bt = prep(qkv, ab)
    b_, l_, s_ = qkv.shape[0], qkv_c.shape[1], qkv.shape[1]
    seq = lambda tc, tl: jnp.concatenate([tc, tl], axis=2).reshape((b_ * DN_HEADS, l_ + s_) + tc.shape[3:])
    seq_g = lambda tc, tl: jnp.concatenate([tc, tl], axis=3).reshape(2, b_ * DN_HEADS, l_ + s_)
    o = gated_delta_chunked(seq(qc, q), seq(kc, k), seq(vc, v), seq_g(lgc, lg), seq_g(bc, bt), l_ // DN_CHUNK)
    o = o.reshape(b_, DN_HEADS, l_ + s_, DN_HEAD_DIM)
    y = out(o[:, :, l_:], z)
    yc = out(o[:, :, :l_], z_c) if with_ctx_out else None
    return y, yc


def retention_core(q, k, v, log_gamma, n_ctx):
    b_, h_, t_, dk = q.shape
    dv = v.shape[-1]
    c_ = RET_CHUNK
    n = t_ // c_
    gh = b_ * h_
    fwd_rank = jnp.arange(c_, dtype=F32)
    rank = jnp.stack([fwd_rank, c_ - 1 - fwd_rank])[:, None, :]
    lg = log_gamma[..., None]
    per_g = lambda t: jnp.broadcast_to(t[:, None], (2, b_) + t.shape[1:]).reshape((2 * gh,) + t.shape[2:])
    zeta = jnp.exp((c_ - 1 - rank) * lg)
    xi = jnp.exp((rank + 1.0) * lg)
    rel = rank[..., :, None] - rank[..., None, :]
    dmat = jnp.where(rel >= 0, jnp.exp(jnp.maximum(rel, 0.0) * lg[..., None]), 0.0)
    gm = jnp.exp(c_ * log_gamma)
    o_f, o_b = ret_chunked(q.reshape(gh, n, c_, dk), k.reshape(gh, n, c_, dk), v.reshape(gh, n, c_, dv),
                           per_g(dmat), per_g(xi), per_g(zeta), per_g(gm), n_ctx)
    return (o_f + o_b).reshape(b_, h_, t_, dv)


def retention_group(q, k, v, g, qc, kc, vc, gc, log1m_gamma, norm_g, cos, sin, with_ctx_out):
    log_gamma = jnp.log1p(-jnp.exp(log1m_gamma))
    heads = lambda t, dh: t.reshape(t.shape[0], t.shape[1], RET_HEADS, dh)
    bhtd = lambda t: t.transpose(0, 2, 1, 3)
    sc = RET_QK_DIM ** -0.5
    l_ = kc.shape[1]
    q = bhtd(apply_rope(heads(q, RET_QK_DIM), cos, sin)) * sc
    k = bhtd(apply_rope(heads(k, RET_QK_DIM), cos, sin))
    v = bhtd(heads(v, RET_V_DIM))
    kc = bhtd(heads(kc, RET_QK_DIM))
    vc = bhtd(heads(vc, RET_V_DIM))
    qcs = bhtd(heads(qc, RET_QK_DIM)) * sc

    def out(o, g_):
        return head_layer_norm(o.transpose(0, 2, 1, 3), norm_g) * jax.nn.silu(g_)

    seq = lambda tc, tl: jnp.concatenate([tc, tl], axis=2)
    o = retention_core(seq(qcs, q), seq(kc, k), seq(vc, v), log_gamma, l_ // RET_CHUNK)
    y = out(o[:, :, l_:], g)
    yc = out(o[:, :, :l_], gc) if with_ctx_out else None
    return y, yc


def mla_group(cq, ckv, kr, cq_c, ckv_c, kr_c, q_norm, w_uq, kv_norm, w_ukv, cos, sin, with_ctx_out):
    b_, s_, _ = cq.shape
    l_ = cq_c.shape[1]
    dqk = MLA_NOPE_DIM + MLA_ROPE_DIM
    rows = lambda tl, tc: jnp.concatenate([tl.reshape(b_ * s_, -1), tc.reshape(b_ * l_, -1)], axis=0)
    qa = matmul(rms_norm(rows(cq, cq_c), q_norm), w_uq)
    kva = matmul(rms_norm(rows(ckv, ckv_c), kv_norm), w_ukv)
    q = qa[:b_ * s_].reshape(b_, s_, MLA_HEADS, dqk)
    qc = qa[b_ * s_:].reshape(b_, l_, MLA_HEADS, dqk)
    q = jnp.concatenate([q[..., :MLA_NOPE_DIM], apply_rope(q[..., MLA_NOPE_DIM:], cos, sin)], axis=-1)
    kv = kva[:b_ * s_].reshape(b_, s_, MLA_HEADS, MLA_NOPE_DIM + MLA_V_DIM)
    kvc = kva[b_ * s_:].reshape(b_, l_, MLA_HEADS, MLA_NOPE_DIM + MLA_V_DIM)
    kr = apply_rope(kr[:, :, None, :], cos, sin)
    k = jnp.concatenate([kv[..., :MLA_NOPE_DIM], jnp.broadcast_to(kr, (b_, s_, MLA_HEADS, MLA_ROPE_DIM))], axis=-1)
    kc = jnp.concatenate([kvc[..., :MLA_NOPE_DIM],
                          jnp.broadcast_to(kr_c[:, :, None, :], (b_, l_, MLA_HEADS, MLA_ROPE_DIM))], axis=-1)
    v, vc = kv[..., MLA_NOPE_DIM:], kvc[..., MLA_NOPE_DIM:]
    hd = lambda t: t.transpose(0, 2, 1, 3).reshape(b_ * MLA_HEADS, t.shape[1], t.shape[3])
    scale = dqk ** -0.5
    no_sink = jnp.zeros((b_ * MLA_HEADS, 1, 128), F32)
    kch, vch = hd(kc), hd(vc)
    y = attn_full(hd(q), jnp.concatenate([hd(k), kch], axis=1), jnp.concatenate([hd(v), vch], axis=1), no_sink, scale, False)
    y = _from_heads(y, b_)
    yc = _from_heads(attn_full(hd(qc), kch, vch, no_sink, scale, False), b_) if with_ctx_out else None
    return y, yc


def token_mixers(zl, zc, p, layer, rope, with_ctx_out):
    (a_q, a_k, a_v, b_qkv, b_z, b_ab, c_q, c_k, c_v, c_g, d_cq, d_ckv, d_kr) = _split_columns(zl)
    (a_qc, a_kc, a_vc, b_qkvc, b_zc, b_abc, c_qc, c_kc, c_vc, c_gc, d_cqc, d_ckvc, d_krc) = _split_columns(zc)
    swa_cos, swa_sin, ret_cos, ret_sin, mla_cos, mla_sin = rope
    ya, yac = swa_group(a_q, a_k, a_v, a_qc, a_kc, a_vc, p['swa_sink'][layer], swa_cos, swa_sin, with_ctx_out)
    yb, ybc = deltanet_group(b_qkv, b_z, b_ab, b_qkvc, b_zc, b_abc, p['dn_conv_w'][layer], p['dn_a_log'][layer],
                             p['dn_dt_bias'][layer], p['dn_norm_g'][layer], with_ctx_out)
    yr, yrc = retention_group(c_q, c_k, c_v, c_g, c_qc, c_kc, c_vc, c_gc, p['ret_log1m_gamma'][layer],
                              p['ret_norm_g'][layer], ret_cos, ret_sin, with_ctx_out)
    yd, ydc = mla_group(d_cq, d_ckv, d_kr, d_cqc, d_ckvc, d_krc, p['mla_q_norm'][layer], p['mla_w_uq'][layer],
                        p['mla_kv_norm'][layer], p['mla_w_ukv'][layer], mla_cos, mla_sin, with_ctx_out)
    y = jnp.concatenate([ya, yb, yr, yd], axis=-1)
    yc = jnp.concatenate([yac, ybc, yrc, ydc], axis=-1) if with_ctx_out else None
    return y, yc


def local_loss(p, x, ctx, loss_target):
    b_, n_tok, d_ = x.shape
    l_ = ctx.shape[1]
    rows = n_tok // GRID_W
    rope = (*axial_rope(rows, SWA_HEAD_DIM), *sequence_rope(n_tok, RET_QK_DIM), *axial_rope(rows, MLA_ROPE_DIM))
    rl, rc = b_ * n_tok, b_ * l_
    mods = [jnp.concatenate([p['mod'][layer], p['cmod'][layer][None]], axis=0) for layer in range(DEPTH)]
    part = lambda layer, j: mods[layer][:, j * d_:(j + 1) * d_][:, None, :]
    vec = lambda name, layer: p[name][layer][None, :]
    xr = jnp.concatenate([x.reshape(rl, d_), ctx.reshape(rc, d_)], axis=0)
    sh1, sc1 = part(0, 0), part(0, 1)
    h = jnp.concatenate([(x * (1 + sc1[:b_]) + sh1[:b_]).reshape(rl, d_), (ctx * (1 + sc1[b_]) + sh1[b_]).reshape(rc, d_)],
                        axis=0)
    for layer in range(DEPTH):
        with_ctx_out = layer < DEPTH - 1
        g1, sh2, sc2, g2 = part(layer, 2), part(layer, 3), part(layer, 4), part(layer, 5)
        z = matmul(h, p['w_in'][layer])
        zl = z[:rl, :IN_WIDTH].reshape(b_, n_tok, IN_WIDTH)
        zc = z[rl:, :IN_WIDTH].reshape(b_, l_, IN_WIDTH)
        y, yc = token_mixers(zl, zc, p, layer, rope, with_ctx_out)
        if with_ctx_out:
            yo = matmul(jnp.concatenate([y.reshape(rl, d_), yc.reshape(rc, d_)], axis=0), p['w_out'][layer])
            xr, h2 = ln_mod(xr, yo, g1, vec('ln1_g', layer), vec('ln1_b', layer), sc2, sh2, n_tok)
            f = matmul_relu2(matmul(h2, p['w_ff1'][layer]), p['w_ff2'][layer])
            xr, h = ln_mod(xr, f, g2, vec('ln2_g', layer), vec('ln2_b', layer), part(layer + 1, 1), part(layer + 1, 0), n_tok)
        else:
            lat = lambda t: t[:b_]
            yo = matmul(y.reshape(rl, d_), p['w_out'][layer])
            xl, h2 = ln_mod(xr[:rl], yo, lat(g1), vec('ln1_g', layer), vec('ln1_b', layer), lat(sc2), lat(sh2), n_tok)
            f = matmul_relu2(matmul(h2, p['w_ff1'][layer]), p['w_ff2'][layer])
            none = jnp.zeros((b_, 1, d_), F32)
            xl, _ = ln_mod(xl, f, lat(g2), vec('ln2_g', layer), vec('ln2_b', layer), none, none, n_tok)
    err = jnp.square(xl - loss_target.reshape(rl, d_))
    return 0.5 * jnp.sum(jnp.mean(err, axis=-1))


def _shard_shape(shape, axis):
    s = list(shape)
    s[axis] //= N_DEV
    return tuple(s)


def _join_shards(pieces, axis):
    _, _, r, c = pieces.shape
    if axis == 0:
        full = pieces.transpose(1, 0, 2, 3).reshape(DEPTH, N_DEV * r, c)
    else:
        full = pieces.transpose(1, 2, 0, 3).reshape(DEPTH, r, N_DEV * c)
    return full.astype(F32)


def _split_shards(g, shape, axis):
    r, c = _shard_shape(shape, axis)
    if axis == 0:
        pieces = g.reshape(DEPTH, N_DEV, r, c).transpose(1, 0, 2, 3)
    else:
        pieces = g.reshape(DEPTH, r, N_DEV, c).transpose(2, 0, 1, 3)
    return pieces.astype(BF16)


def _pad_vec(vec, rows_multiple=8):
    n = vec.shape[0]
    rows = -(-n // (128 * rows_multiple)) * rows_multiple
    return jnp.pad(vec, (0, rows * 128 - n)).reshape(rows, 128)


def _adamw(w, g, m, v, name):
    shape = w.shape
    if w.ndim >= 2 and shape[-1] >= 128:
        as2 = lambda t: t.reshape(-1, shape[-1])
        d, nm, nv = _adamw_call(as2(w), as2(g), as2(m), as2(v), name)
        return d.reshape(shape), nm.reshape(shape), nv.reshape(shape)
    n = int(np.prod(shape))
    as2 = lambda t: _pad_vec(t.reshape(-1))
    d, nm, nv = _adamw_call(as2(w), as2(g), as2(m), as2(v), name)
    un = lambda t: t.reshape(-1)[:n].reshape(shape)
    return un(d), un(nm), un(nv)


def kernel(x, c, ctx, c_ctx, ada_w, ada_b, w_in, swa_sink, dn_conv_w, dn_a_log, dn_dt_bias, dn_norm_g, ret_log1m_gamma, ret_norm_g, mla_q_norm, mla_w_uq, mla_kv_norm, mla_w_ukv, w_out, ln1_g, ln1_b, w_ff1, w_ff2, ln2_g, ln2_b, loss_target, m_c_ctx, m_ada_w, m_ada_b, m_w_in, m_swa_sink, m_dn_conv_w, m_dn_a_log, m_dn_dt_bias, m_dn_norm_g, m_ret_log1m_gamma, m_ret_norm_g, m_mla_q_norm, m_mla_w_uq, m_mla_kv_norm, m_mla_w_ukv, m_w_out, m_ln1_g, m_ln1_b, m_w_ff1, m_w_ff2, m_ln2_g, m_ln2_b, v_c_ctx, v_ada_w, v_ada_b, v_w_in, v_swa_sink, v_dn_conv_w, v_dn_a_log, v_dn_dt_bias, v_dn_norm_g, v_ret_log1m_gamma, v_ret_norm_g, v_mla_q_norm, v_mla_w_uq, v_mla_kv_norm, v_mla_w_ukv, v_w_out, v_ln1_g, v_ln1_b, v_w_ff1, v_w_ff2, v_ln2_g, v_ln2_b):
    a = dict(zip(ARG_NAMES, (x, c, ctx, c_ctx, ada_w, ada_b, w_in, swa_sink, dn_conv_w, dn_a_log, dn_dt_bias, dn_norm_g, ret_log1m_gamma, ret_norm_g, mla_q_norm, mla_w_uq, mla_kv_norm, mla_w_ukv, w_out, ln1_g, ln1_b, w_ff1, w_ff2, ln2_g, ln2_b, loss_target, m_c_ctx, m_ada_w, m_ada_b, m_w_in, m_swa_sink, m_dn_conv_w, m_dn_a_log, m_dn_dt_bias, m_dn_norm_g, m_ret_log1m_gamma, m_ret_norm_g, m_mla_q_norm, m_mla_w_uq, m_mla_kv_norm, m_mla_w_ukv, m_w_out, m_ln1_g, m_ln1_b, m_w_ff1, m_w_ff2, m_ln2_g, m_ln2_b, v_c_ctx, v_ada_w, v_ada_b, v_w_in, v_swa_sink, v_dn_conv_w, v_dn_a_log, v_dn_dt_bias, v_dn_norm_g, v_ret_log1m_gamma, v_ret_norm_g, v_mla_q_norm, v_mla_w_uq, v_mla_kv_norm, v_mla_w_ukv, v_w_out, v_ln1_g, v_ln1_b, v_w_ff1, v_w_ff2, v_ln2_g, v_ln2_b)))
    me = 4 * lax.axis_index("x") + 2 * lax.axis_index("y") + lax.axis_index("c")
    b_loc = x.shape[0]
    n_ex = N_DEV * b_loc
    conv_k, conv_c = dn_conv_w.shape[1], dn_conv_w.shape[2]
    ada_cols = ada_w.shape[2]

    small_in = jnp.concatenate([c.reshape(-1), dn_conv_w.reshape(-1)])
    gathered = _gather_call([_pad_vec(small_in)] + [a[name].astype(BF16) for name, _, _ in BIG], "gather_weights")
    small_all = gathered[0].reshape(N_DEV, -1)
    c_all = small_all[:, :b_loc * D_MODEL].reshape(n_ex, D_MODEL)
    conv_all = small_all[:, b_loc * D_MODEL:b_loc * D_MODEL + DEPTH * conv_k * conv_c].reshape(N_DEV, DEPTH, conv_k, conv_c)
    conv_full = conv_all.transpose(1, 2, 0, 3).reshape(DEPTH, conv_k, N_DEV * conv_c)
    big = {name: _join_shards(pieces, axis) for (name, _, axis), pieces in zip(BIG, gathered[1:])}
    big['w_in'] = jnp.pad(big['w_in'], ((0, 0), (0, 0), (0, IN_WIDTH_PAD - IN_WIDTH)))

    n_rows = -(-(n_ex + 1) // 16) * 16
    silu_cc = jax.nn.silu(c_ctx)
    a_rows = jnp.concatenate([jax.nn.silu(c_all), silu_cc[None], jnp.zeros((n_rows - n_ex - 1, D_MODEL), F32)], axis=0)
    m_loc = jnp.concatenate([_mm_call(a_rows, ada_w[l], False, "ada_fwd") for l in range(DEPTH)], axis=0)
    m_all = _gather_call([m_loc], "gather_mod")[0].reshape(N_DEV, DEPTH, n_rows, ada_cols)
    mod_full = m_all.transpose(1, 2, 0, 3).reshape(DEPTH, n_rows, N_DEV * ada_cols) + ada_b[:, None, :]
    mod = lax.dynamic_slice_in_dim(mod_full, me * b_loc, b_loc, axis=1)
    cmod = mod_full[:, n_ex]

    p = dict(big)
    p.update(mod=mod, cmod=cmod, dn_conv_w=conv_full)
    for name in SMALL:
        p[name] = a[name]
    loss_loc, (gp, gx) = jax.value_and_grad(local_loss, argnums=(0, 1))(p, x, ctx, loss_target)
    loss = lax.psum(loss_loc, MESH_AXES)

    gp['w_in'] = gp['w_in'][:, :, :IN_WIDTH]
    arrived = _all_to_all_call([_split_shards(gp[name], shape, axis) for name, shape, axis in BIG], "scatter_grads")
    g_big = {}
    for (name, shape, axis), part in zip(BIG, arrived):
        r, c_ = _shard_shape(shape, axis)
        g_big[name] = _sum8_call(part.reshape(N_DEV, DEPTH * r, c_), "sum_" + name).reshape(DEPTH, r, c_)

    d_loc = jnp.concatenate([gp['mod'], gp['cmod'][:, None, :]], axis=1).reshape(DEPTH * (b_loc + 1), -1)
    d_loc = jnp.pad(d_loc, ((0, 8 - DEPTH * (b_loc + 1)), (0, 0)))
    d_all = _gather_call([d_loc], "gather_dmod")[0][:, :DEPTH * (b_loc + 1)].reshape(N_DEV, DEPTH, b_loc + 1, -1)
    d_rows = d_all[:, :, :b_loc].transpose(1, 0, 2, 3).reshape(DEPTH, n_ex, -1)
    d_crow = d_all[0, :, b_loc]
    for d in range(1, N_DEV):
        d_crow = d_crow + d_all[d, :, b_loc]
    dm_full = jnp.concatenate([d_rows, d_crow[:, None, :], jnp.zeros((DEPTH, n_rows - n_ex - 1, d_rows.shape[-1]), F32)], axis=1)
    g_ada_b = jnp.sum(dm_full, axis=1)
    dm_mine = lax.dynamic_slice_in_dim(dm_full, me * ada_cols, ada_cols, axis=2)
    g_ada_w = jnp.stack([_mm_call(a_rows, dm_mine[l], True, "ada_bwd_w") for l in range(DEPTH)])
    crow8 = jnp.concatenate([dm_mine[:, n_ex:n_ex + 1], jnp.zeros((DEPTH, 15, ada_cols), F32)], axis=1)
    dsilu_part = sum(_mm_call(crow8[l], jnp.transpose(ada_w[l]), False, "ada_bwd_c")[0] for l in range(DEPTH))

    small_g = jnp.concatenate([gp[name].reshape(-1) for name in SMALL] + [gp['dn_conv_w'].reshape(-1), dsilu_part])
    small_sum = _sum8_call(_gather_call([_pad_vec(small_g)], "gather_small_grads")[0], "sum_small_grads").reshape(-1)
    g_all, off = {}, 0
    for name in SMALL:
        n = int(np.prod(a[name].shape))
        g_all[name] = small_sum[off:off + n].reshape(a[name].shape)
        off += n
    n = DEPTH * conv_k * N_DEV * conv_c
    g_conv_full = small_sum[off:off + n].reshape(DEPTH, conv_k, N_DEV * conv_c)
    g_all['dn_conv_w'] = lax.dynamic_slice_in_dim(g_conv_full, me * conv_c, conv_c, axis=2)
    off += n
    dsilu = small_sum[off:off + D_MODEL]
    sig = jax.nn.sigmoid(c_ctx)
    g_all['c_ctx'] = dsilu * (sig * (1 + c_ctx * (1 - sig)))
    g_all['ada_w'] = g_ada_w
    g_all['ada_b'] = g_ada_b
    g_all.update(g_big)

    delta, new_m, new_v = {}, {}, {}
    for name in WEIGHTS:
        delta[name], new_m[name], new_v[name] = _adamw(a[name], g_all[name], a['m_' + name], a['v_' + name], "adamw_" + name)
    return (loss, gx, *[g_all[n] for n in WEIGHTS], *[delta[n] for n in WEIGHTS],
            *[new_m[n] for n in WEIGHTS], *[new_v[n] for n in WEIGHTS])
```

```python
import functools
import math

import jax
import jax.numpy as jnp
import numpy as np
from jax import lax
from jax.experimental import pallas as pl
from jax.experimental.pallas import tpu as pltpu

F32 = jnp.float32
BF16 = jnp.bfloat16
N_DEV = 8
MESH_AXES = ("x", "y", "c")

D_MODEL = 1024
DEPTH = 2
GRID_W = 64
SWA_HEADS, SWA_KV_HEADS, SWA_HEAD_DIM, SWA_WINDOW, SWA_BLOCK = 4, 2, 64, 128, 128
DN_HEADS, DN_HEAD_DIM, DN_CHUNK = 4, 64, 64
RET_HEADS, RET_QK_DIM, RET_V_DIM, RET_CHUNK = 4, 32, 64, 64
MLA_HEADS, MLA_Q_RANK, MLA_KV_RANK, MLA_NOPE_DIM, MLA_ROPE_DIM, MLA_V_DIM = 4, 256, 128, 64, 32, 64
D_FF = 4 * D_MODEL
ROPE_BASE = 10000.0
NORM_EPS = 1e-6
LN_EPS = 1e-5
DEEPNORM_ALPHA = (2 * DEPTH) ** 0.25
SWA_Q = SWA_HEADS * SWA_HEAD_DIM
SWA_KV = SWA_KV_HEADS * SWA_HEAD_DIM
DN_W = DN_HEADS * DN_HEAD_DIM
RET_QK = RET_HEADS * RET_QK_DIM
RET_V = RET_HEADS * RET_V_DIM
IN_SPLITS = (SWA_Q, SWA_KV, SWA_KV, 3 * DN_W, DN_W, 4 * DN_HEADS, RET_QK, RET_QK, RET_V, RET_V,
             MLA_Q_RANK, MLA_KV_RANK, MLA_ROPE_DIM)
IN_WIDTH = sum(IN_SPLITS)
IN_WIDTH_PAD = -(-IN_WIDTH // 128) * 128

ADAM_LR, ADAM_B1, ADAM_B2, ADAM_EPS, ADAM_WD, ADAM_STEP = 0.001, 0.9, 0.999, 1e-08, 0.01, 10

WEIGHTS = ['c_ctx', 'ada_w', 'ada_b', 'w_in', 'swa_sink', 'dn_conv_w', 'dn_a_log', 'dn_dt_bias', 'dn_norm_g',
           'ret_log1m_gamma', 'ret_norm_g', 'mla_q_norm', 'mla_w_uq', 'mla_kv_norm', 'mla_w_ukv', 'w_out', 'ln1_g',
           'ln1_b', 'w_ff1', 'w_ff2', 'ln2_g', 'ln2_b']
FWD_INPUTS = ['x', 'c', 'ctx'] + WEIGHTS
ARG_NAMES = FWD_INPUTS + ['loss_target'] + ['m_' + n for n in WEIGHTS] + ['v_' + n for n in WEIGHTS]

BIG = (('w_in', (D_MODEL, IN_WIDTH), 1), ('w_out', (D_MODEL, D_MODEL), 0), ('w_ff1', (D_MODEL, D_FF), 1),
       ('w_ff2', (D_FF, D_MODEL), 0), ('mla_w_uq', (MLA_Q_RANK, MLA_HEADS * (MLA_NOPE_DIM + MLA_ROPE_DIM)), 1),
       ('mla_w_ukv', (MLA_KV_RANK, MLA_HEADS * (MLA_NOPE_DIM + MLA_V_DIM)), 1))
SMALL = ('swa_sink', 'dn_a_log', 'dn_dt_bias', 'dn_norm_g', 'ret_log1m_gamma', 'ret_norm_g', 'mla_q_norm',
         'mla_kv_norm', 'ln1_g', 'ln1_b', 'ln2_g', 'ln2_b')


def _pcall(body, **kw):
    return pl.pallas_call(body, **kw)


def _pick(n, cands):
    for cand in cands:
        if n % cand == 0:
            return cand
    return n


def _bdot(a, b, dims):
    return lax.dot_general(a.astype(BF16), b.astype(BF16), dims, preferred_element_type=F32)


def _lane0(t):
    return jnp.where(lax.broadcasted_iota(jnp.int32, t.shape, t.ndim - 1) == 0, t, 0.0)


NN = (((1,), (0,)), ((), ()))
NT = (((1,), (1,)), ((), ()))
TN = (((0,), (0,)), ((), ()))
BNN = (((2,), (1,)), ((0,), (0,)))
BNT = (((2,), (2,)), ((0,), (0,)))


MM_ROW_TILE_MAX = 1088
MM_COL_TILE_MAX = 1408
MM_TOKEN_TILE_MAX = 1088
VMEM_LIMIT_MAX = 60 * 1024 * 1024


def _tile(n, cap, align):
    best = None
    for t in range(align, min(n, cap) + 1, align):
        if n % t == 0:
            best = t
    return best or n


def _relu2(t):
    return jnp.square(jnp.maximum(t, 0.0))


def _mm_call(a, b, trans_a, name, act_a=False, epi=None):
    if trans_a:
        kdim, m = a.shape
        tk = _tile(kdim, MM_TOKEN_TILE_MAX, 8)
        tm = _tile(m, 1024, 128)
    else:
        m, kdim = a.shape
        tk = _tile(kdim, MM_COL_TILE_MAX, 128)
        tm = _tile(m, MM_ROW_TILE_MAX, 8)
    n = b.shape[1]
    assert b.shape[0] == kdim
    tn = _tile(n, MM_COL_TILE_MAX, 128)
    nk = kdim // tk

    def body(*refs):
        a_ref, b_ref = refs[0], refs[1]
        e_ref = refs[2] if epi is not None else None
        o_ref = refs[-1]
        k = pl.program_id(2)
        av = a_ref[...]
        if act_a:
            av = _relu2(av)
        part = _bdot(av, b_ref[...], TN if trans_a else NN)

        def finish(t):
            return t * (2.0 * jnp.maximum(e_ref[...], 0.0)) if epi is not None else t

        if nk == 1:
            o_ref[...] = finish(part)
        else:
            @pl.when(k == 0)
            def _():
                o_ref[...] = part

            @pl.when((k > 0) & (k < nk - 1))
            def _():
                o_ref[...] += part

            @pl.when(k == nk - 1)
            def _():
                o_ref[...] = finish(o_ref[...] + part)

    if trans_a:
        a_spec = pl.BlockSpec((tk, tm), lambda i, j, k: (k, i))
    else:
        a_spec = pl.BlockSpec((tm, tk), lambda i, j, k: (i, k))
    o_spec = pl.BlockSpec((tm, tn), lambda i, j, k: (i, j))
    in_specs = [a_spec, pl.BlockSpec((tk, tn), lambda i, j, k: (k, j))] + ([o_spec] if epi is not None else [])
    tiles = tm * tk * a.dtype.itemsize + tk * tn * b.dtype.itemsize + tm * tn * 4 * (2 if epi is not None else 1)
    temps = tm * tk * (2 + (4 if act_a else 0)) + tk * tn * 2 + 2 * tm * tn * 4
    return _pcall(
        body, name=name, grid=(m // tm, n // tn, nk), in_specs=in_specs, out_specs=o_spec,
        out_shape=jax.ShapeDtypeStruct((m, n), F32),
        compiler_params=pltpu.CompilerParams(dimension_semantics=("parallel", "parallel", "arbitrary"),
                                             vmem_limit_bytes=min(2 * tiles + temps + (4 << 20), VMEM_LIMIT_MAX)),
    )(*((a, b) + ((epi,) if epi is not None else ())))


@jax.custom_vjp
def matmul(a, b):
    return _mm_call(a, b.astype(BF16), False, "mm_fwd")


def _matmul_fwd(a, b):
    bb = b.astype(BF16)
    return _mm_call(a, bb, False, "mm_fwd"), (a, bb)


def _matmul_bwd(res, g):
    a, bb = res
    da = _mm_call(g, jnp.transpose(bb), False, "mm_bwd_da")
    db = _mm_call(a, g, True, "mm_bwd_db")
    return da, db


matmul.defvjp(_matmul_fwd, _matmul_bwd)


@jax.custom_vjp
def matmul_relu2(a, b):
    return _mm_call(a, b.astype(BF16), False, "mm_act_fwd", act_a=True)


def _matmul_relu2_fwd(a, b):
    bb = b.astype(BF16)
    return _mm_call(a, bb, False, "mm_act_fwd", act_a=True), (a, bb)


def _matmul_relu2_bwd(res, g):
    a, bb = res
    da = _mm_call(g, jnp.transpose(bb), False, "mm_act_bwd_da", epi=a)
    db = _mm_call(a, g, True, "mm_act_bwd_db", act_a=True)
    return da, db


matmul_relu2.defvjp(_matmul_relu2_fwd, _matmul_relu2_bwd)


LN_ROW_TILE = 256


def _ln_group_map(group_rows, n_groups):
    per = group_rows // LN_ROW_TILE
    return lambda i: (jnp.minimum(i // per, n_groups - 1), 0, 0)


def _ln_stats(x, y, gate):
    pre = DEEPNORM_ALPHA * x + gate * y
    mu = jnp.mean(pre, axis=-1, keepdims=True)
    cen = pre - mu
    rstd = lax.rsqrt(jnp.mean(jnp.square(cen), axis=-1, keepdims=True) + LN_EPS)
    return cen * rstd, rstd


def _ln_mod_fwd_call(x, y, gate, gamma, beta, sc, sh, group_rows):
    r, d = x.shape
    ng = gate.shape[0]
    gmap = _ln_group_map(group_rows, ng)

    def body(x_ref, y_ref, gate_ref, gamma_ref, beta_ref, sc_ref, sh_ref, xn_ref, h_ref):
        xh, _ = _ln_stats(x_ref[...], y_ref[...], gate_ref[0])
        xn = xh * gamma_ref[...] + beta_ref[...]
        xn_ref[...] = xn
        h_ref[...] = xn * (1.0 + sc_ref[0]) + sh_ref[0]

    row = pl.BlockSpec((LN_ROW_TILE, d), lambda i: (i, 0))
    grp = pl.BlockSpec((1, 1, d), gmap)
    vec = pl.BlockSpec((1, d), lambda i: (0, 0))
    return _pcall(
        body, name="ln_mod_fwd", grid=(r // LN_ROW_TILE,),
        in_specs=[row, row, grp, vec, vec, grp, grp], out_specs=[row, row],
        out_shape=[jax.ShapeDtypeStruct((r, d), F32)] * 2,
        compiler_params=pltpu.CompilerParams(dimension_semantics=("parallel",)),
    )(x, y, gate, gamma, beta, sc, sh)


def _ln_mod_bwd_call(x, y, gate, gamma, beta, sc, dxn, dh, group_rows):
    r, d = x.shape
    ng = gate.shape[0]
    gmap = _ln_group_map(group_rows, ng)
    per = group_rows // LN_ROW_TILE

    def body(x_ref, y_ref, gate_ref, gamma_ref, beta_ref, sc_ref, dxn_ref, dh_ref,
             dx_ref, dy_ref, dgate_ref, dgamma_ref, dbeta_ref, dsc_ref, dsh_ref):
        i = pl.program_id(0)
        yv, gate_v, gamma_v = y_ref[...], gate_ref[0], gamma_ref[...]
        xh, rstd = _ln_stats(x_ref[...], yv, gate_v)
        dhv = dh_ref[...]
        dtot = dxn_ref[...] + dhv * (1.0 + sc_ref[0])
        dxh = dtot * gamma_v
        dpre = rstd * (dxh - jnp.mean(dxh, axis=-1, keepdims=True) - xh * jnp.mean(dxh * xh, axis=-1, keepdims=True))
        dx_ref[...] = DEEPNORM_ALPHA * dpre
        dy_ref[...] = gate_v * dpre
        col = lambda t: jnp.sum(t, axis=0, keepdims=True)

        @pl.when(i == 0)
        def _():
            dgamma_ref[...] = jnp.zeros_like(dgamma_ref)
            dbeta_ref[...] = jnp.zeros_like(dbeta_ref)

        first_of_group = (i % per == 0) | (i == (ng - 1) * per)

        @pl.when(first_of_group & (i <= (ng - 1) * per))
        def _():
            dgate_ref[...] = jnp.zeros_like(dgate_ref)
            dsc_ref[...] = jnp.zeros_like(dsc_ref)
            dsh_ref[...] = jnp.zeros_like(dsh_ref)

        dgamma_ref[...] += col(dtot * xh)
        dbeta_ref[...] += col(dtot)
        dgate_ref[0] += col(dpre * yv)
        dsc_ref[0] += col(dhv * (xh * gamma_v + beta_ref[...]))
        dsh_ref[0] += col(dhv)

    row = pl.BlockSpec((LN_ROW_TILE, d), lambda i: (i, 0))
    grp = pl.BlockSpec((1, 1, d), gmap)
    vec = pl.BlockSpec((1, d), lambda i: (0, 0))
    big = jax.ShapeDtypeStruct((r, d), F32)
    gs = jax.ShapeDtypeStruct((ng, 1, d), F32)
    vs = jax.ShapeDtypeStruct((1, d), F32)
    return _pcall(
        body, name="ln_mod_bwd", grid=(r // LN_ROW_TILE,),
        in_specs=[row, row, grp, vec, vec, grp, row, row],
        out_specs=[row, row, grp, vec, vec, grp, grp],
        out_shape=[big, big, gs, vs, vs, gs, gs],
        compiler_params=pltpu.CompilerParams(dimension_semantics=("arbitrary",)),
    )(x, y, gate, gamma, beta, sc, dxn, dh)


@functools.partial(jax.custom_vjp, nondiff_argnums=(7,))
def ln_mod(x, y, gate, gamma, beta, sc, sh, group_rows):
    return tuple(_ln_mod_fwd_call(x, y, gate, gamma, beta, sc, sh, group_rows))


def _ln_mod_fwd(x, y, gate, gamma, beta, sc, sh, group_rows):
    xn, h = _ln_mod_fwd_call(x, y, gate, gamma, beta, sc, sh, group_rows)
    return (xn, h), (x, y, gate, gamma, beta, sc)


def _ln_mod_bwd(group_rows, res, cts):
    x, y, gate, gamma, beta, sc = res
    dxn, dh = cts
    return tuple(_ln_mod_bwd_call(x, y, gate, gamma, beta, sc, dxn, dh, group_rows))


ln_mod.defvjp(_ln_mod_fwd, _ln_mod_bwd)


def _attn_probs(q, k, sink, scale, has_sink):
    s = _bdot(q, k, NT) * scale
    m = jnp.max(s, axis=-1, keepdims=True)
    if has_sink:
        m = jnp.maximum(m, sink)
    p = jnp.exp(s - m)
    den = jnp.sum(p, axis=-1, keepdims=True)
    p_sink = None
    if has_sink:
        p_sink = jnp.exp(sink - m)
        den = den + p_sink
    inv = 1.0 / den
    if has_sink:
        p_sink = p_sink * inv
    return p * inv, p_sink


def _attn_full_fwd_call(q, k, v, sink, scale, has_sink):
    g, sq, dq = q.shape
    nk, dv = k.shape[1], v.shape[2]
    bq = _pick(sq, (256, 128))

    def body(q_ref, k_ref, v_ref, sink_ref, o_ref):
        p, _ = _attn_probs(q_ref[0], k_ref[0], sink_ref[0, :, 0:1], scale, has_sink)
        o_ref[0] = _bdot(p, v_ref[0], NN)

    return _pcall(
        body, name="attn_full_fwd", grid=(g, sq // bq),
        in_specs=[pl.BlockSpec((1, bq, dq), lambda b, i: (b, i, 0)), pl.BlockSpec((1, nk, dq), lambda b, i: (b, 0, 0)),
                  pl.BlockSpec((1, nk, dv), lambda b, i: (b, 0, 0)), pl.BlockSpec((1, 1, 128), lambda b, i: (b, 0, 0))],
        out_specs=pl.BlockSpec((1, bq, dv), lambda b, i: (b, i, 0)),
        out_shape=jax.ShapeDtypeStruct((g, sq, dv), F32),
        compiler_params=pltpu.CompilerParams(dimension_semantics=("parallel", "arbitrary")),
    )(q, k, v, sink)


def _attn_full_bwd_call(q, k, v, sink, o, do, scale, has_sink):
    g, sq, dq = q.shape
    nk, dv = k.shape[1], v.shape[2]
    bq = _pick(sq, (256, 128))

    def body(q_ref, k_ref, v_ref, sink_ref, o_ref, do_ref, dq_ref, dk_ref, dv_ref, dsink_ref):
        i = pl.program_id(1)
        qv, kv, vv, dov = q_ref[0], k_ref[0], v_ref[0], do_ref[0]
        p, p_sink = _attn_probs(qv, kv, sink_ref[0, :, 0:1], scale, has_sink)
        delta = jnp.sum(dov * o_ref[0], axis=-1, keepdims=True)
        dv_part = _bdot(p, dov, TN)
        dp = _bdot(dov, vv, NT)
        ds = p * (dp - delta) * scale
        dq_ref[0] = _bdot(ds, kv, NN)
        dk_part = _bdot(ds, qv, TN)
        if has_sink:
            dsk = jnp.broadcast_to(-jnp.sum(p_sink * delta, axis=0, keepdims=True), (1, 128))
        else:
            dsk = jnp.zeros((1, 128), F32)

        @pl.when(i == 0)
        def _():
            dk_ref[0] = dk_part
            dv_ref[0] = dv_part
            dsink_ref[0] = dsk

        @pl.when(i > 0)
        def _():
            dk_ref[0] += dk_part
            dv_ref[0] += dv_part
            dsink_ref[0] += dsk

    qspec = pl.BlockSpec((1, bq, dq), lambda b, i: (b, i, 0))
    kspec = pl.BlockSpec((1, nk, dq), lambda b, i: (b, 0, 0))
    vspec = pl.BlockSpec((1, nk, dv), lambda b, i: (b, 0, 0))
    ospec = pl.BlockSpec((1, bq, dv), lambda b, i: (b, i, 0))
    sspec = pl.BlockSpec((1, 1, 128), lambda b, i: (b, 0, 0))
    return _pcall(
        body, name="attn_full_bwd", grid=(g, sq // bq),
        in_specs=[qspec, kspec, vspec, sspec, ospec, ospec],
        out_specs=[qspec, kspec, vspec, sspec],
        out_shape=[jax.ShapeDtypeStruct(q.shape, F32), jax.ShapeDtypeStruct(k.shape, F32),
                   jax.ShapeDtypeStruct(v.shape, F32), jax.ShapeDtypeStruct(sink.shape, F32)],
        compiler_params=pltpu.CompilerParams(dimension_semantics=("parallel", "arbitrary")),
    )(q, k, v, sink, o, do)


@functools.partial(jax.custom_vjp, nondiff_argnums=(4, 5))
def attn_full(q, k, v, sink, scale, has_sink):
    return _attn_full_fwd_call(q, k, v, sink, scale, has_sink)


def _attn_full_fwd(q, k, v, sink, scale, has_sink):
    o = _attn_full_fwd_call(q, k, v, sink, scale, has_sink)
    return o, (q, k, v, sink, o)


def _attn_full_bwd(scale, has_sink, res, do):
    q, k, v, sink, o = res
    dq, dk, dv, dsink = _attn_full_bwd_call(q, k, v, sink, o, do, scale, has_sink)
    return dq, dk, dv, _lane0(dsink)


attn_full.defvjp(_attn_full_fwd, _attn_full_bwd)


SWA_GROUP = SWA_HEADS // SWA_KV_HEADS


def _swa_rows(ref):
    return ref[...].reshape(SWA_GROUP * SWA_BLOCK, ref.shape[-1])


def _swa_sink_rows(sink_ref):
    head = lax.broadcasted_iota(jnp.int32, (SWA_GROUP * SWA_BLOCK, 1), 0) // SWA_BLOCK
    out = jnp.zeros((SWA_GROUP * SWA_BLOCK, 1), F32)
    for j in range(SWA_GROUP):
        out = jnp.where(head == j, sink_ref[j, :, 0:1], out)
    return out


def _swa_probs(q, kw, kc, sink, i, s_len, scale):
    w = SWA_BLOCK
    rows = q.shape[0]
    s_loc = _bdot(q, kw, NT) * scale
    qpos = i * w + lax.broadcasted_iota(jnp.int32, (rows, 3 * w), 0) % w
    kpos = (i - 1) * w + lax.broadcasted_iota(jnp.int32, (rows, 3 * w), 1)
    valid = (jnp.abs(kpos - qpos) <= SWA_WINDOW) & (kpos >= 0) & (kpos < s_len)
    s_loc = jnp.where(valid, s_loc, -jnp.inf)
    s_ctx = _bdot(q, kc, NT) * scale
    m = jnp.maximum(jnp.maximum(jnp.max(s_loc, axis=-1, keepdims=True), jnp.max(s_ctx, axis=-1, keepdims=True)), sink)
    p_loc = jnp.exp(s_loc - m)
    p_ctx = jnp.exp(s_ctx - m)
    p_sink = jnp.exp(sink - m)
    inv = 1.0 / (jnp.sum(p_loc, axis=-1, keepdims=True) + jnp.sum(p_ctx, axis=-1, keepdims=True) + p_sink)
    return p_loc * inv, p_ctx * inv, p_sink * inv


def _swa_fwd_call(q, kp, vp, kc, vc, sink, scale):
    g, s_len, d = q.shape
    l_ctx = kc.shape[1]
    w = SWA_BLOCK
    grp = SWA_GROUP

    def body(q_ref, kp_ref, vp_ref, kc_ref, vc_ref, sink_ref, o_ref):
        i = pl.program_id(1)
        start = pl.multiple_of(i * w, w)
        kw = kp_ref[0, pl.ds(start, 3 * w), :]
        vw = vp_ref[0, pl.ds(start, 3 * w), :]
        p_loc, p_ctx, _ = _swa_probs(_swa_rows(q_ref), kw, kc_ref[0], _swa_sink_rows(sink_ref), i, s_len, scale)
        o_ref[...] = (_bdot(p_loc, vw, NN) + _bdot(p_ctx, vc_ref[0], NN)).reshape(grp, w, d)

    return _pcall(
        body, name="swa_fwd", grid=(g // grp, s_len // w),
        in_specs=[pl.BlockSpec((grp, w, d), lambda b, i: (b, i, 0)),
                  pl.BlockSpec((1, s_len + 2 * w, d), lambda b, i: (b, 0, 0)),
                  pl.BlockSpec((1, s_len + 2 * w, d), lambda b, i: (b, 0, 0)),
                  pl.BlockSpec((1, l_ctx, d), lambda b, i: (b, 0, 0)),
                  pl.BlockSpec((1, l_ctx, d), lambda b, i: (b, 0, 0)),
                  pl.BlockSpec((grp, 1, 128), lambda b, i: (b, 0, 0))],
        out_specs=pl.BlockSpec((grp, w, d), lambda b, i: (b, i, 0)),
        out_shape=jax.ShapeDtypeStruct(q.shape, F32),
        compiler_params=pltpu.CompilerParams(dimension_semantics=("parallel", "arbitrary")),
    )(q, kp, vp, kc, vc, sink)


def _swa_bwd_call(q, kp, vp, kc, vc, sink, o, do, scale):
    g, s_len, d = q.shape
    l_ctx = kc.shape[1]
    w = SWA_BLOCK
    grp = SWA_GROUP
    sp = s_len + 2 * w

    def body(q_ref, kp_ref, vp_ref, kc_ref, vc_ref, sink_ref, o_ref, do_ref,
             dq_ref, dkp_ref, dvp_ref, dkc_ref, dvc_ref, dsink_ref):
        i = pl.program_id(1)
        start = pl.multiple_of(i * w, w)
        qv, dov = _swa_rows(q_ref), _swa_rows(do_ref)
        kw = kp_ref[0, pl.ds(start, 3 * w), :]
        vw = vp_ref[0, pl.ds(start, 3 * w), :]
        kcv, vcv = kc_ref[0], vc_ref[0]
        p_loc, p_ctx, p_sink = _swa_probs(qv, kw, kcv, _swa_sink_rows(sink_ref), i, s_len, scale)
        delta = jnp.sum(dov * _swa_rows(o_ref), axis=-1, keepdims=True)
        ds_loc = p_loc * (_bdot(dov, vw, NT) - delta) * scale
        ds_ctx = p_ctx * (_bdot(dov, vcv, NT) - delta) * scale
        dq_ref[...] = (_bdot(ds_loc, kw, NN) + _bdot(ds_ctx, kcv, NN)).reshape(grp, w, d)
        dsk = jnp.broadcast_to(-jnp.sum((p_sink * delta).reshape(grp, w, 1), axis=1, keepdims=True), (grp, 1, 128))

        @pl.when(i == 0)
        def _():
            dkp_ref[...] = jnp.zeros_like(dkp_ref)
            dvp_ref[...] = jnp.zeros_like(dvp_ref)
            dkc_ref[...] = jnp.zeros_like(dkc_ref)
            dvc_ref[...] = jnp.zeros_like(dvc_ref)
            dsink_ref[...] = jnp.zeros_like(dsink_ref)

        dkp_ref[0, pl.ds(start, 3 * w), :] += _bdot(ds_loc, qv, TN)
        dvp_ref[0, pl.ds(start, 3 * w), :] += _bdot(p_loc, dov, TN)
        dkc_ref[0] += _bdot(ds_ctx, qv, TN)
        dvc_ref[0] += _bdot(p_ctx, dov, TN)
        dsink_ref[...] += dsk

    qspec = pl.BlockSpec((grp, w, d), lambda b, i: (b, i, 0))
    kspec = pl.BlockSpec((1, sp, d), lambda b, i: (b, 0, 0))
    cspec = pl.BlockSpec((1, l_ctx, d), lambda b, i: (b, 0, 0))
    sspec = pl.BlockSpec((grp, 1, 128), lambda b, i: (b, 0, 0))
    return _pcall(
        body, name="swa_bwd", grid=(g // grp, s_len // w),
        in_specs=[qspec, kspec, kspec, cspec, cspec, sspec, qspec, qspec],
        out_specs=[qspec, kspec, kspec, cspec, cspec, sspec],
        out_shape=[jax.ShapeDtypeStruct(q.shape, F32), jax.ShapeDtypeStruct(kp.shape, F32),
                   jax.ShapeDtypeStruct(vp.shape, F32), jax.ShapeDtypeStruct(kc.shape, F32),
                   jax.ShapeDtypeStruct(vc.shape, F32), jax.ShapeDtypeStruct(sink.shape, F32)],
        compiler_params=pltpu.CompilerParams(dimension_semantics=("parallel", "arbitrary")),
    )(q, kp, vp, kc, vc, sink, o, do)


@functools.partial(jax.custom_vjp, nondiff_argnums=(6,))
def swa_attn(q, kp, vp, kc, vc, sink, scale):
    return _swa_fwd_call(q, kp, vp, kc, vc, sink, scale)


def _swa_attn_fwd(q, kp, vp, kc, vc, sink, scale):
    o = _swa_fwd_call(q, kp, vp, kc, vc, sink, scale)
    return o, (q, kp, vp, kc, vc, sink, o)


def _swa_attn_bwd(scale, res, do):
    q, kp, vp, kc, vc, sink, o = res
    dq, dkp, dvp, dkc, dvc, dsink = _swa_bwd_call(q, kp, vp, kc, vc, sink, o, do, scale)
    return dq, dkp, dvp, dkc, dvc, _lane0(dsink)


swa_attn.defvjp(_swa_attn_fwd, _swa_attn_bwd)


def _f32dot(a, b, dims):
    return lax.dot_general(a, b, dims, precision=lax.Precision.HIGHEST, preferred_element_type=F32)


DN_SOLVE_BLOCK = 16


def _unit_lower_inverse(a, a_t, transposed):
    g, c, _ = a.shape
    nb = DN_SOLVE_BLOCK
    row = lax.broadcasted_iota(jnp.int32, (1, c, c), 1)
    col = lax.broadcasted_iota(jnp.int32, (1, c, c), 2)
    src, off = (a, a_t) if transposed else (a_t, a)
    coef = jnp.zeros((g, c, nb), F32)
    for b in range(c // nb):
        in_block = (lax.broadcasted_iota(jnp.int32, (1, c, nb), 1) // nb) == b
        coef = coef + jnp.where(in_block, src[:, :, b * nb:(b + 1) * nb], 0.0)
    sub = lax.broadcasted_iota(jnp.int32, (1, c // nb, nb, c), 2)
    x = jnp.broadcast_to((row == col).astype(F32), a.shape)
    for i in (range(nb - 2, -1, -1) if transposed else range(1, nb)):
        prod = (coef[:, :, i:i + 1] * x).reshape(g, c // nb, nb, c)
        new_rows = -jnp.sum(prod, axis=2, keepdims=True)
        x = x + jnp.where(sub == i, new_rows, 0.0).reshape(g, c, c)
    width = nb
    while width < c:
        joins = ((row // (2 * width)) == (col // (2 * width))) & ((row // width) != (col // width))
        x = x - _f32dot(x, _f32dot(jnp.where(joins, off, 0.0), x, BNN), BNN)
        width *= 2
    return x


def _dn_masks(c):
    row = lax.broadcasted_iota(jnp.int32, (1, c, c), 1)
    col = lax.broadcasted_iota(jnp.int32, (1, c, c), 2)
    return row, col


def _sweep_chunks(n, n_ctx):
    return (lambda i: i), (lambda i: jnp.where(i < n_ctx, n_ctx - 1 - i, n + n_ctx - 1 - i))


def _half_spec(gh, tail, half, chunk_of):
    return pl.BlockSpec((gh, 1) + tail, lambda i: (half, chunk_of(i), 0, 0))


def _both(ref_f, ref_b):
    return jnp.concatenate([ref_f[:, 0], ref_b[:, 0]], axis=0)


def _dn_direction_masks(g, c):
    backward = lax.broadcasted_iota(jnp.int32, (g, 1, 1), 0) >= g // 2
    row, col = _dn_masks(c)
    return backward, jnp.where(backward, c - 1 - row, row), jnp.where(backward, c - 1 - col, col)


def _dn_inverse(a_mat, a_t, transposed):
    h = a_mat.shape[0] // 2
    return jnp.concatenate([_unit_lower_inverse(a_mat[:h], a_t[:h], transposed),
                            _unit_lower_inverse(a_t[h:], a_mat[h:], not transposed)], axis=0)


def _dn_fwd_call(q, k, k_t, v, gc, bb, gr, n_ctx):
    gh, n, c, _ = q.shape
    g = 2 * gh

    def body(qf, qb, kf, kb_, ktf, ktb, vf, vb, gcf, gcb, bbf, bbb, grf, grb,
             of_ref, ob_ref, vn_ref, sall_ref, w_ref, u_ref, t_ref, s_scr):
        i = pl.program_id(0)

        @pl.when(i == 0)
        def _():
            s_scr[...] = jnp.zeros_like(s_scr)

        qv, kv, ktv, vv, gcv, bv, grv = (_both(qf, qb), _both(kf, kb_), _both(ktf, ktb), _both(vf, vb), _both(gcf, gcb),
                                          _both(bbf, bbb), _both(grf, grb))
        backward, row, col = _dn_direction_masks(g, c)
        e = jnp.exp(gcv)
        kb = kv * bv
        decay = jnp.exp(jnp.where(row >= col, gcv - grv, -jnp.inf))
        decay_ts = jnp.exp(jnp.where(row < col, grv - gcv, -jnp.inf))
        a_mat = _bdot(kb, kv, BNT) * jnp.where(row > col, decay, 0.0)
        t = _dn_inverse(a_mat, _bdot(kv, kb, BNT) * decay_ts, False)
        w = _f32dot(t, kb * e, BNN)
        u = _f32dot(t, vv * bv, BNN)
        glast = jnp.where(backward, grv[:, :, 0:1], grv[:, :, c - 1:c])
        s = s_scr[...]
        sall_ref[:, 0] = s
        vnew = u - _bdot(w, s, BNN)
        o = _bdot(qv * e, s, BNN) + _bdot(_bdot(qv, kv, BNT) * decay, vnew, BNN)
        of_ref[:, 0] = o[:gh]
        ob_ref[:, 0] = o[gh:]
        vn_ref[:, 0] = vnew
        w_ref[:, 0] = w
        u_ref[:, 0] = u
        t_ref[:, 0] = t
        s_scr[...] = s * jnp.exp(glast) + _bdot(ktv * jnp.exp(glast - grv), vnew, BNN)

    cf, cb = _sweep_chunks(n, n_ctx)
    tok = lambda half, chunk_of: _half_spec(gh, (c, c), half, chunk_of)
    rowv = lambda half, chunk_of: _half_spec(gh, (1, c), half, chunk_of)
    step = pl.BlockSpec((g, 1, c, c), lambda i: (0, i, 0, 0))
    shared = [tok(0, cf), tok(0, cb)]
    split = [tok(0, cf), tok(1, cb)]
    return _pcall(
        body, name="dn_fwd", grid=(n,),
        in_specs=shared * 4 + split * 2 + [rowv(0, cf), rowv(1, cb)],
        out_specs=[tok(0, cf), tok(0, cb)] + [step] * 5,
        out_shape=[jax.ShapeDtypeStruct((gh, n, c, c), F32)] * 2 + [jax.ShapeDtypeStruct((g, n, c, c), F32)] * 5,
        scratch_shapes=[pltpu.VMEM((g, c, c), F32)],
        compiler_params=pltpu.CompilerParams(dimension_semantics=("arbitrary",)),
    )(q, q, k, k, k_t, k_t, v, v, gc, gc, bb, bb, gr, gr)


def _dn_bwd_call(q, k, q_t, k_t, v, gc, bb, gr, br, sall, vn, w, u, t_t, do_f, do_b, n_ctx):
    gh, n, c, _ = q.shape
    g = 2 * gh

    def body(qf, qb, kf, kb_, qtf, qtb, ktf, ktb, vf, vb, gcf, gcb, bbf, bbb, grf, grb, brf, brb,
             sall_ref, vn_ref, w_ref, u_ref, tt_ref, dof, dob,
             dqf, dqb, dkf, dkb_, dvf, dvb, dgcf, dgcb, dbbf, dbbb, dgrf, dgrb, ds_scr):
        i = pl.program_id(0)

        @pl.when(i == 0)
        def _():
            ds_scr[...] = jnp.zeros_like(ds_scr)

        qv, kv, qtv, ktv, vv = _both(qf, qb), _both(kf, kb_), _both(qtf, qtb), _both(ktf, ktb), _both(vf, vb)
        gcv, bv, grv, brv, dov = _both(gcf, gcb), _both(bbf, bbb), _both(grf, grb), _both(brf, brb), _both(dof, dob)
        s, vnew, w, u = sall_ref[:, 0], vn_ref[:, 0], w_ref[:, 0], u_ref[:, 0]
        dsn = ds_scr[...]
        backward, row, col = _dn_direction_masks(g, c)
        e = jnp.exp(gcv)
        er = jnp.exp(grv)
        kb = kv * bv
        decay = jnp.exp(jnp.where(row >= col, gcv - grv, -jnp.inf))
        decay_s = jnp.where(row > col, decay, 0.0)
        decay_t = jnp.exp(jnp.where(row <= col, grv - gcv, -jnp.inf))
        decay_ts = jnp.where(row < col, decay_t, 0.0)
        kk = _bdot(kb, kv, BNT)
        tt = tt_ref[:, 0]
        glast = jnp.where(backward, grv[:, :, 0:1], grv[:, :, c - 1:c])
        eg = jnp.exp(glast)
        x = jnp.exp(glast - gcv)
        kt = kv * x
        qk_raw = _bdot(qv, kv, BNT)
        w_t = _f32dot(ktv * (brv * er), tt, BNN)
        dvn = _bdot(_bdot(kv, qv, BNT) * decay_t, dov, BNN) + _bdot(kt, dsn, BNN)
        dqk = _bdot(dov, vnew, BNT)
        dqk_t = _bdot(vnew, dov, BNT)
        dqd = _bdot(dov, s, BNT)
        dkt = _bdot(vnew, dsn, BNT)
        deg = jnp.sum(jnp.sum(dsn * s, axis=2, keepdims=True), axis=1, keepdims=True)
        dw = -_bdot(dvn, s, BNT)
        ds_scr[...] = dsn * eg + _bdot(qtv * er, dov, BNN) - _bdot(w_t, dvn, BNN)
        dwp = _f32dot(tt, dw, BNN)
        dup = _f32dot(tt, dvn, BNN)
        d_a = -(_bdot(dwp, w, BNT) + _bdot(dup, u, BNT))
        d_at = -(_bdot(w, dwp, BNT) + _bdot(u, dup, BNT))
        dkb = _bdot(d_a * decay_s, kv, BNN) + dwp * e
        dkx = dkt * kv * x
        ddiff = dqk * qk_raw * decay + d_a * kk * decay_s
        dglast = jnp.sum(jnp.sum(dkx, axis=2, keepdims=True), axis=1, keepdims=True) + deg * eg
        lane = lax.broadcasted_iota(jnp.int32, (1, 1, c), 2)
        last_lane = jnp.where(backward, 0, c - 1)
        results = (
            (dqf, dqb, dqd * e + _bdot(dqk * decay, kv, BNN)),
            (dkf, dkb_, _bdot(d_at * decay_ts, kb, BNN) + dkb * bv + dkt * x + _bdot(dqk_t * decay_t, qv, BNN)),
            (dvf, dvb, dup * bv),
            (dgcf, dgcb, ddiff + (dwp * kb + dqd * qv) * e - dkx),
            (dbbf, dbbb, dkb * kv + dup * vv),
            (dgrf, dgrb, jnp.where(lane == last_lane, dglast, 0.0) - jnp.sum(ddiff, axis=1, keepdims=True)),
        )
        for ref_f, ref_b, val in results:
            ref_f[:, 0] = val[:gh]
            ref_b[:, 0] = val[gh:]

    cf, cb = _sweep_chunks(n, n_ctx)
    rf, rb = (lambda i: cf(n - 1 - i)), (lambda i: cb(n - 1 - i))
    tok = lambda half, chunk_of: _half_spec(gh, (c, c), half, chunk_of)
    rowv = lambda half, chunk_of: _half_spec(gh, (1, c), half, chunk_of)
    step = pl.BlockSpec((g, 1, c, c), lambda i: (0, n - 1 - i, 0, 0))
    shared = [tok(0, rf), tok(0, rb)]
    split = [tok(0, rf), tok(1, rb)]
    split_row = [rowv(0, rf), rowv(1, rb)]
    big = jax.ShapeDtypeStruct((gh, n, c, c), F32)
    return _pcall(
        body, name="dn_bwd", grid=(n,),
        in_specs=shared * 5 + split * 2 + split_row * 2 + [step] * 5 + shared,
        out_specs=shared * 5 + [rowv(0, rf), rowv(0, rb)],
        out_shape=[big] * 10 + [jax.ShapeDtypeStruct((gh, n, 1, c), F32)] * 2,
        scratch_shapes=[pltpu.VMEM((g, c, c), F32)],
        compiler_params=pltpu.CompilerParams(dimension_semantics=("arbitrary",)),
    )(q, q, k, k, q_t, q_t, k_t, k_t, v, v, gc, gc, bb, bb, gr, gr, br, br, sall, vn, w, u, t_t, do_f, do_b)


_t = lambda a: jnp.swapaxes(a, -1, -2)


def _dn_forms(gcum, beta, d):
    lanes = lambda t: jnp.broadcast_to(t[..., None], t.shape + (d,))
    return lanes(gcum), lanes(beta), gcum[:, :, None, :], beta[:, :, None, :]


@functools.partial(jax.custom_vjp, nondiff_argnums=(5,))
def dn_chunked(q, k, v, gcum, beta, n_ctx):
    gc, bb, gr, _ = _dn_forms(gcum, beta, q.shape[-1])
    return tuple(_dn_fwd_call(q, k, _t(k), v, gc, bb, gr, n_ctx)[:2])


def _dn_chunked_fwd(q, k, v, gcum, beta, n_ctx):
    gc, bb, gr, _ = _dn_forms(gcum, beta, q.shape[-1])
    o_f, o_b, vn, sall, w, u, t = _dn_fwd_call(q, k, _t(k), v, gc, bb, gr, n_ctx)
    return (o_f, o_b), (q, k, v, gcum, beta, vn, sall, w, u, t)


def _dn_chunked_bwd(n_ctx, res, cts):
    q, k, v, gcum, beta, vn, sall, w, u, t = res
    gc, bb, gr, br = _dn_forms(gcum, beta, q.shape[-1])
    (dq_f, dq_b, dk_f, dk_b, dv_f, dv_b, dgc_f, dgc_b, dbb_f, dbb_b, dgr_f, dgr_b) = _dn_bwd_call(
        q, k, _t(q), _t(k), v, gc, bb, gr, br, sall, vn, w, u, _t(t), cts[0], cts[1], n_ctx)
    dgcum = jnp.concatenate([jnp.sum(dgc_f, axis=-1) + dgr_f[:, :, 0, :], jnp.sum(dgc_b, axis=-1) + dgr_b[:, :, 0, :]], axis=0)
    dbeta = jnp.concatenate([jnp.sum(dbb_f, axis=-1), jnp.sum(dbb_b, axis=-1)], axis=0)
    return dq_f + dq_b, dk_f + dk_b, dv_f + dv_b, dgcum, dbeta


dn_chunked.defvjp(_dn_chunked_fwd, _dn_chunked_bwd)


def _ret_fwd_call(q, k, k_t, v, dmat, xi_b, zeta_r, gm, n_ctx):
    gh, n, c, dk = q.shape
    dv = v.shape[-1]
    g = 2 * gh

    def body(qf, qb, kf, kb_, ktf, ktb, vf, vb, d_ref, xib_ref, zr_ref, gm_ref, of_ref, ob_ref, starts_ref, s_scr):
        i = pl.program_id(0)

        @pl.when(i == 0)
        def _():
            s_scr[...] = jnp.zeros_like(s_scr)

        qv, kv, ktv, vv = _both(qf, qb), _both(kf, kb_), _both(ktf, ktb), _both(vf, vb)
        s = s_scr[...]
        starts_ref[:, 0] = s
        o = _bdot(_bdot(qv, kv, BNT) * d_ref[...], vv, BNN) + _bdot(qv * xib_ref[...], s, BNN)
        of_ref[:, 0] = o[:gh]
        ob_ref[:, 0] = o[gh:]
        s_scr[...] = s * gm_ref[...] + _bdot(ktv * zr_ref[...], vv, BNN)

    cf, cb = _sweep_chunks(n, n_ctx)
    pair = lambda tail: [_half_spec(gh, tail, 0, cf), _half_spec(gh, tail, 0, cb)]
    const = lambda a, b: pl.BlockSpec((g, a, b), lambda i: (0, 0, 0))
    return _pcall(
        body, name="ret_fwd", grid=(n,),
        in_specs=pair((c, dk)) * 2 + pair((dk, c)) + pair((c, dv)) + [const(c, c), const(c, dk), const(1, c), const(dk, dv)],
        out_specs=pair((c, dv)) + [pl.BlockSpec((g, 1, dk, dv), lambda i: (0, i, 0, 0))],
        out_shape=[jax.ShapeDtypeStruct((gh, n, c, dv), F32)] * 2 + [jax.ShapeDtypeStruct((g, n, dk, dv), F32)],
        scratch_shapes=[pltpu.VMEM((g, dk, dv), F32)],
        compiler_params=pltpu.CompilerParams(dimension_semantics=("arbitrary",)),
    )(q, q, k, k, k_t, k_t, v, v, dmat, xi_b, zeta_r, gm)


def _ret_bwd_call(q, k, q_t, k_t, v, dmat, dmat_t, xi_b, xi_r, zeta_b, gm, starts, do_f, do_b, n_ctx):
    gh, n, c, dk = q.shape
    dv = v.shape[-1]
    g = 2 * gh

    def body(qf, qb, kf, kb_, qtf, qtb, ktf, ktb, vf, vb, d_ref, dt_ref, xib_ref, xr_ref, zb_ref, gm_ref, starts_ref,
             dof, dob, dqf, dqb, dkf, dkb_, dvf, dvb, dd_ref, dxib_ref, dzb_ref, dgm_ref, ds_scr):
        i = pl.program_id(0)

        @pl.when(i == 0)
        def _():
            ds_scr[...] = jnp.zeros_like(ds_scr)
            dd_ref[...] = jnp.zeros_like(dd_ref)
            dxib_ref[...] = jnp.zeros_like(dxib_ref)
            dzb_ref[...] = jnp.zeros_like(dzb_ref)
            dgm_ref[...] = jnp.zeros_like(dgm_ref)

        qv, kv, qtv, vv, dov = _both(qf, qb), _both(kf, kb_), _both(qtf, qtb), _both(vf, vb), _both(dof, dob)
        s, dsn = starts_ref[:, 0], ds_scr[...]
        dm, dmt, zb = d_ref[...], dt_ref[...], zb_ref[...]
        qk_raw = _bdot(qv, kv, BNT)
        dqkd = _bdot(dov, vv, BNT)
        do_s = _bdot(dov, s, BNT)
        dkz = _bdot(vv, dsn, BNT)
        results = ((dqf, dqb, _bdot(dqkd * dm, kv, BNN) + do_s * xib_ref[...]),
                   (dkf, dkb_, _bdot(_bdot(vv, dov, BNT) * dmt, qv, BNN) + dkz * zb),
                   (dvf, dvb, _bdot(_bdot(kv, qv, BNT) * dmt, dov, BNN) + _bdot(kv * zb, dsn, BNN)))
        for ref_f, ref_b, val in results:
            ref_f[:, 0] = val[:gh]
            ref_b[:, 0] = val[gh:]
        dd_ref[...] += dqkd * qk_raw
        dxib_ref[...] += do_s * qv
        dzb_ref[...] += dkz * kv
        dgm_ref[...] += dsn * s
        ds_scr[...] = dsn * gm_ref[...] + _bdot(qtv * xr_ref[...], dov, BNN)

    cf, cb = _sweep_chunks(n, n_ctx)
    rf, rb = (lambda i: cf(n - 1 - i)), (lambda i: cb(n - 1 - i))
    pair = lambda tail: [_half_spec(gh, tail, 0, rf), _half_spec(gh, tail, 0, rb)]
    const = lambda a, b: pl.BlockSpec((g, a, b), lambda i: (0, 0, 0))
    sds = lambda *s: jax.ShapeDtypeStruct(s, F32)
    return _pcall(
        body, name="ret_bwd", grid=(n,),
        in_specs=pair((c, dk)) * 2 + pair((dk, c)) * 2 + pair((c, dv))
        + [const(c, c), const(c, c), const(c, dk), const(1, c), const(c, dk), const(dk, dv),
           pl.BlockSpec((g, 1, dk, dv), lambda i: (0, n - 1 - i, 0, 0))] + pair((c, dv)),
        out_specs=pair((c, dk)) * 2 + pair((c, dv)) + [const(c, c), const(c, dk), const(c, dk), const(dk, dv)],
        out_shape=[sds(gh, n, c, dk)] * 4 + [sds(gh, n, c, dv)] * 2 + [sds(g, c, c), sds(g, c, dk), sds(g, c, dk), sds(g, dk, dv)],
        scratch_shapes=[pltpu.VMEM((g, dk, dv), F32)],
        compiler_params=pltpu.CompilerParams(dimension_semantics=("arbitrary",)),
    )(q, q, k, k, q_t, q_t, k_t, k_t, v, v, dmat, dmat_t, xi_b, xi_r, zeta_b, gm, starts, do_f, do_b)


def _ret_forms(xi, zeta, gm, dk, dv):
    lanes = lambda t: jnp.broadcast_to(t[..., None], t.shape + (dk,))
    return lanes(xi), xi[:, None, :], lanes(zeta), zeta[:, None, :], jnp.broadcast_to(gm[:, None, None], gm.shape + (dk, dv))


@functools.partial(jax.custom_vjp, nondiff_argnums=(7,))
def ret_chunked(q, k, v, dmat, xi, zeta, gm, n_ctx):
    xi_b, _, _, zeta_r, gm_f = _ret_forms(xi, zeta, gm, q.shape[-1], v.shape[-1])
    return tuple(_ret_fwd_call(q, k, _t(k), v, dmat, xi_b, zeta_r, gm_f, n_ctx)[:2])


def _ret_chunked_fwd(q, k, v, dmat, xi, zeta, gm, n_ctx):
    xi_b, _, _, zeta_r, gm_f = _ret_forms(xi, zeta, gm, q.shape[-1], v.shape[-1])
    o_f, o_b, starts = _ret_fwd_call(q, k, _t(k), v, dmat, xi_b, zeta_r, gm_f, n_ctx)
    return (o_f, o_b), (q, k, v, dmat, xi, zeta, gm, starts)


def _ret_chunked_bwd(n_ctx, res, cts):
    q, k, v, dmat, xi, zeta, gm, starts = res
    xi_b, xi_r, zeta_b, _, gm_f = _ret_forms(xi, zeta, gm, q.shape[-1], v.shape[-1])
    dq_f, dq_b, dk_f, dk_b, dv_f, dv_b, dd, dxib, dzb, dgm = _ret_bwd_call(
        q, k, _t(q), _t(k), v, dmat, _t(dmat), xi_b, xi_r, zeta_b, gm_f, starts, cts[0], cts[1], n_ctx)
    return (dq_f + dq_b, dk_f + dk_b, dv_f + dv_b, dd, jnp.sum(dxib, axis=-1), jnp.sum(dzb, axis=-1),
            jnp.sum(dgm, axis=(1, 2)))


ret_chunked.defvjp(_ret_chunked_fwd, _ret_chunked_bwd)


def _peer(k):
    mx, my, mc = lax.axis_index("x"), lax.axis_index("y"), lax.axis_index("c")
    px = 1 - mx if k & 4 else mx
    py = 1 - my if k & 2 else my
    pc = 1 - mc if k & 1 else mc
    return (px, py, pc), 4 * px + 2 * py + pc


N_CHIP = N_DEV // 2


def _transfer_call(xs, name, plan, n_transfers, n_out):
    n_arr = len(xs)

    def body(*refs):
        x_refs, out_refs = refs[:n_arr], refs[n_arr:2 * n_arr]
        send_sems, recv_sems, local_sems = refs[2 * n_arr:]
        transfers, local = plan()
        n_tr = n_transfers
        assert len(transfers) == n_tr

        def copy(j, s, dst_slot):
            flip, src_slot, _, _ = transfers[s]
            return pltpu.make_async_remote_copy(
                src_ref=x_refs[j].at[src_slot], dst_ref=out_refs[j].at[dst_slot],
                send_sem=send_sems.at[j * n_tr + s], recv_sem=recv_sems.at[j * n_tr + s],
                device_id=_peer(flip)[0], device_id_type=pl.DeviceIdType.MESH)

        mine = []
        if local is not None:
            mine = [pltpu.make_async_copy(x_refs[j].at[local[0]], out_refs[j].at[local[1]], local_sems.at[j])
                    for j in range(n_arr)]
        for cp in mine:
            cp.start()
        sends = [copy(j, s, transfers[s][2]) for j in range(n_arr) for s in range(n_tr)]
        for cp in sends:
            cp.start()
        for j in range(n_arr):
            for s in range(n_tr):
                copy(j, s, transfers[s][3]).wait_recv()
        for cp in sends:
            cp.wait_send()
        for cp in mine:
            cp.wait()

    n_sem = n_arr * n_transfers
    return _pcall(
        body, name=name,
        in_specs=[pl.BlockSpec(memory_space=pl.ANY)] * n_arr, out_specs=[pl.BlockSpec(memory_space=pl.ANY)] * n_arr,
        out_shape=[jax.ShapeDtypeStruct((n_out,) + tuple(x.shape[1:]), x.dtype) for x in xs],
        scratch_shapes=[pltpu.SemaphoreType.DMA((n_sem,)), pltpu.SemaphoreType.DMA((n_sem,)),
                        pltpu.SemaphoreType.DMA((n_arr,))],
    )(*xs)


def _sibling_plan():
    mc = lax.axis_index("c")
    return [(1, 2 * t + (1 - mc), t, t) for t in range(N_CHIP)], None


def _chip_plan():
    my_chip = 2 * lax.axis_index("x") + lax.axis_index("y")
    transfers = []
    for flip in (2, 4, 6):
        peer_chip = _peer(flip)[1] // 2
        transfers.append((flip, peer_chip, my_chip, peer_chip))
    return transfers, (my_chip, my_chip)


def _gather_call(xs, name):
    n_arr = len(xs)
    per = N_DEV - 1
    chips = (2, 4, 6)

    def body(*refs):
        x_refs, out_refs = refs[:n_arr], refs[n_arr:2 * n_arr]
        send_sems, recv_sems, local_sems = refs[2 * n_arr:]
        me = 4 * lax.axis_index("x") + 2 * lax.axis_index("y") + lax.axis_index("c")
        sib_dev, sib_idx = _peer(1)

        def copy(j, s, src, slot, dev):
            return pltpu.make_async_remote_copy(
                src_ref=src, dst_ref=out_refs[j].at[slot],
                send_sem=send_sems.at[j * per + s], recv_sem=recv_sems.at[j * per + s],
                device_id=dev, device_id_type=pl.DeviceIdType.MESH)

        mine = [pltpu.make_async_copy(x_refs[j], out_refs[j].at[me], local_sems.at[j]) for j in range(n_arr)]
        for cp in mine:
            cp.start()
        sends = []
        for j in range(n_arr):
            sends.append(copy(j, 0, x_refs[j], me, sib_dev))
            for t, k in enumerate(chips):
                sends.append(copy(j, 1 + t, x_refs[j], me, _peer(k)[0]))
        for cp in sends:
            cp.start()
        for j in range(n_arr):
            for t, k in enumerate(chips):
                dev, idx = _peer(k)
                copy(j, 1 + t, x_refs[j], idx, dev).wait_recv()
                forward = copy(j, 4 + t, out_refs[j].at[idx], idx, sib_dev)
                forward.start()
                sends.append(forward)
        for j in range(n_arr):
            copy(j, 0, x_refs[j], sib_idx, sib_dev).wait_recv()
            for t, k in enumerate(chips):
                idx = _peer(k + 1)[1]
                copy(j, 4 + t, out_refs[j].at[idx], idx, sib_dev).wait_recv()
        for cp in sends:
            cp.wait_send()
        for cp in mine:
            cp.wait()

    return _pcall(
        body, name=name,
        in_specs=[pl.BlockSpec(memory_space=pl.ANY)] * n_arr, out_specs=[pl.BlockSpec(memory_space=pl.ANY)] * n_arr,
        out_shape=[jax.ShapeDtypeStruct((N_DEV,) + tuple(x.shape), x.dtype) for x in xs],
        scratch_shapes=[pltpu.SemaphoreType.DMA((n_arr * per,)), pltpu.SemaphoreType.DMA((n_arr * per,)),
                        pltpu.SemaphoreType.DMA((n_arr,))],
    )(*xs)


def _sum8_call(x, name):
    n_slots, r, c = x.shape
    tr = _pick(r, (256, 160, 128, 72, 64, 32, 16, 8))

    def body(x_ref, o_ref):
        acc = x_ref[0].astype(F32)
        for d in range(1, n_slots):
            acc = acc + x_ref[d].astype(F32)
        o_ref[...] = acc

    return _pcall(
        body, name=name, grid=(r // tr,),
        in_specs=[pl.BlockSpec((n_slots, tr, c), lambda i: (0, i, 0))],
        out_specs=pl.BlockSpec((tr, c), lambda i: (i, 0)),
        out_shape=jax.ShapeDtypeStruct((r, c), F32),
        compiler_params=pltpu.CompilerParams(dimension_semantics=("parallel",)),
    )(x)


def _pair_add_call(a, b, name):
    n_slots, r, c = a.shape
    tr = _pick(r, (256, 160, 128, 64, 32, 16))

    def body(a_ref, b_ref, o_ref):
        o_ref[...] = (a_ref[...].astype(F32) + b_ref[...].astype(F32)).astype(BF16)

    spec = pl.BlockSpec((n_slots, tr, c), lambda i: (0, i, 0))
    return _pcall(
        body, name=name, grid=(r // tr,), in_specs=[spec, spec], out_specs=spec,
        out_shape=jax.ShapeDtypeStruct(a.shape, BF16),
        compiler_params=pltpu.CompilerParams(dimension_semantics=("parallel",)),
    )(a, b)


def _adamw_call(w, g, m, v, name):
    r, c = w.shape
    tr = _pick(r, (256, 128, 64, 32, 16, 8))
    bc1 = 1.0 - ADAM_B1 ** ADAM_STEP
    bc2 = 1.0 - ADAM_B2 ** ADAM_STEP

    def body(w_ref, g_ref, m_ref, v_ref, d_ref, nm_ref, nv_ref):
        gv = g_ref[...]
        nm = ADAM_B1 * m_ref[...] + (1.0 - ADAM_B1) * gv
        nv = ADAM_B2 * v_ref[...] + (1.0 - ADAM_B2) * jnp.square(gv)
        d_ref[...] = -ADAM_LR * ((nm / bc1) / (jnp.sqrt(nv / bc2) + ADAM_EPS) + ADAM_WD * w_ref[...])
        nm_ref[...] = nm
        nv_ref[...] = nv

    spec = pl.BlockSpec((tr, c), lambda i: (i, 0))
    sds = jax.ShapeDtypeStruct((r, c), F32)
    return _pcall(
        body, name=name, grid=(r // tr,), in_specs=[spec] * 4, out_specs=[spec] * 3, out_shape=[sds] * 3,
        compiler_params=pltpu.CompilerParams(dimension_semantics=("parallel",)),
    )(w, g, m, v)


def layer_norm(x, g, b):
    mu = jnp.mean(x, axis=-1, keepdims=True)
    var = jnp.mean(jnp.square(x - mu), axis=-1, keepdims=True)
    return (x - mu) * lax.rsqrt(var + LN_EPS) * g + b


def rms_norm(x, g):
    return x * lax.rsqrt(jnp.mean(x * x, axis=-1, keepdims=True) + NORM_EPS) * g


def head_layer_norm(o, g):
    b_, t_, h_, d_ = o.shape
    mu = jnp.mean(o, axis=-1, keepdims=True)
    var = jnp.mean(jnp.square(o - mu), axis=-1, keepdims=True)
    return ((o - mu) * lax.rsqrt(var + NORM_EPS)).reshape(b_, t_, h_ * d_) * g


def l2norm(t):
    return t * lax.rsqrt(jnp.sum(t * t, axis=-1, keepdims=True) + NORM_EPS)


def rope_freqs(dim):
    return ROPE_BASE ** (-jnp.arange(0, dim, 2, dtype=F32) / dim)


def axial_rope(rows, rot_dim):
    row = jnp.broadcast_to(jnp.arange(rows, dtype=F32)[:, None], (rows, GRID_W)).reshape(-1)
    col = jnp.broadcast_to(jnp.arange(GRID_W, dtype=F32)[None, :], (rows, GRID_W)).reshape(-1)
    inv = rope_freqs(rot_dim // 2)
    ang = jnp.concatenate([row[:, None] * inv, col[:, None] * inv], axis=-1)
    return jnp.cos(ang), jnp.sin(ang)


def sequence_rope(n_tok, rot_dim):
    ang = jnp.arange(n_tok, dtype=F32)[:, None] * rope_freqs(rot_dim)
    return jnp.cos(ang), jnp.sin(ang)


def apply_rope(x, cos, sin):
    x1, x2 = jnp.split(x, 2, axis=-1)
    c = cos[:, None, :]
    s = sin[:, None, :]
    return jnp.concatenate([x1 * c - x2 * s, x1 * s + x2 * c], axis=-1)


def _flip_t(t):
    return jnp.flip(t, axis=2)


def _split_columns(z):
    idx = np.cumsum(np.array(IN_SPLITS))[:-1].tolist()
    return jnp.split(z, idx, axis=-1)


CONV_ROW_TILE = 512
CONV_HALO = 8


def _conv_windows(prev_ref, x_ref, next_ref, n_tiles, taps):
    i = pl.program_id(1)
    tr = x_ref.shape[1]
    prev = jnp.where(i == 0, 0.0, prev_ref[0])
    nxt = jnp.where(i == n_tiles - 1, 0.0, next_ref[0])
    xx = jnp.concatenate([prev, x_ref[0], nxt], axis=0)
    rows = tr + 2 * CONV_HALO
    pad = taps // 2
    return [pltpu.roll(xx, (pad - k) % rows, 0)[CONV_HALO:CONV_HALO + tr] for k in range(taps)]


def _conv_specs(t_len, tr, ch):
    per = tr // CONV_HALO
    last = t_len // CONV_HALO - 1
    return [pl.BlockSpec((1, CONV_HALO, ch), lambda b, i: (b, jnp.maximum(i * per - 1, 0), 0)),
            pl.BlockSpec((1, tr, ch), lambda b, i: (b, i, 0)),
            pl.BlockSpec((1, CONV_HALO, ch), lambda b, i: (b, jnp.minimum((i + 1) * per, last), 0))]


def _conv_fwd_call(x, w, taps):
    b_, t_len, ch = x.shape
    tr = min(CONV_ROW_TILE, t_len)
    n_tiles = t_len // tr

    def body(prev_ref, x_ref, next_ref, w_ref, o_ref):
        wins = _conv_windows(prev_ref, x_ref, next_ref, n_tiles, taps)
        acc = wins[0] * w_ref[0:1, :]
        for k in range(1, taps):
            acc = acc + wins[k] * w_ref[k:k + 1, :]
        o_ref[0] = acc

    return _pcall(
        body, name="conv_fwd", grid=(b_, n_tiles),
        in_specs=_conv_specs(t_len, tr, ch) + [pl.BlockSpec((8, ch), lambda b, i: (0, 0))],
        out_specs=pl.BlockSpec((1, tr, ch), lambda b, i: (b, i, 0)),
        out_shape=jax.ShapeDtypeStruct(x.shape, F32),
        compiler_params=pltpu.CompilerParams(dimension_semantics=("parallel", "parallel")),
    )(x, x, x, w)


def _conv_dw_call(x, du, taps):
    b_, t_len, ch = x.shape
    tr = min(CONV_ROW_TILE, t_len)
    n_tiles = t_len // tr

    def body(prev_ref, x_ref, next_ref, du_ref, dw_ref):
        wins = _conv_windows(prev_ref, x_ref, next_ref, n_tiles, taps)
        duv = du_ref[0]
        rows = [jnp.sum(duv * wins[k], axis=0, keepdims=True) for k in range(taps)]
        part = jnp.concatenate(rows + [jnp.zeros((8 - taps, ch), F32)], axis=0)

        @pl.when((pl.program_id(0) == 0) & (pl.program_id(1) == 0))
        def _():
            dw_ref[...] = jnp.zeros_like(dw_ref)

        dw_ref[...] += part

    return _pcall(
        body, name="conv_dw", grid=(b_, n_tiles),
        in_specs=_conv_specs(t_len, tr, ch) + [pl.BlockSpec((1, tr, ch), lambda b, i: (b, i, 0))],
        out_specs=pl.BlockSpec((8, ch), lambda b, i: (0, 0)),
        out_shape=jax.ShapeDtypeStruct((8, ch), F32),
        compiler_params=pltpu.CompilerParams(dimension_semantics=("arbitrary", "arbitrary")),
    )(x, x, x, du)


def _pad_taps(w):
    return jnp.concatenate([w, jnp.zeros((8 - w.shape[0], w.shape[1]), F32)], axis=0)


@jax.custom_vjp
def short_conv(x, w):
    return _conv_fwd_call(x, _pad_taps(w), w.shape[0])


def _short_conv_fwd(x, w):
    return _conv_fwd_call(x, _pad_taps(w), w.shape[0]), (x, w)


def _short_conv_bwd(res, du):
    x, w = res
    taps = w.shape[0]
    return _conv_fwd_call(du, _pad_taps(jnp.flip(w, axis=0)), taps), _conv_dw_call(x, du, taps)[:taps]


short_conv.defvjp(_short_conv_fwd, _short_conv_bwd)


def _to_heads(t, h, d):
    b_, t_, _ = t.shape
    return t.reshape(b_, t_, h, d).transpose(0, 2, 1, 3).reshape(b_ * h, t_, d)


def _from_heads(t, b_):
    g, t_, d = t.shape
    return t.reshape(b_, g // b_, t_, d).transpose(0, 2, 1, 3).reshape(b_, t_, (g // b_) * d)


def _lane_scalar(vals):
    return jnp.broadcast_to(vals[:, None, None], (vals.shape[0], 1, 128))


def swa_group(q, k, v, qc, kc, vc, sink, cos, sin, with_ctx_out):
    b_, s_, _ = q.shape
    l_ = kc.shape[1]
    grp = SWA_HEADS // SWA_KV_HEADS
    d = SWA_HEAD_DIM
    w_ = SWA_BLOCK
    scale = d ** -0.5
    qh = apply_rope(q.reshape(b_, s_, SWA_HEADS, d), cos, sin).transpose(0, 2, 1, 3).reshape(b_ * SWA_HEADS, s_, d)
    kh = apply_rope(k.reshape(b_, s_, SWA_KV_HEADS, d), cos, sin).transpose(0, 2, 1, 3).reshape(b_ * SWA_KV_HEADS, s_, d)
    vh = _to_heads(v, SWA_KV_HEADS, d)
    kch = _to_heads(kc, SWA_KV_HEADS, d)
    vch = _to_heads(vc, SWA_KV_HEADS, d)
    padk = lambda t: jnp.pad(t, ((0, 0), (w_, w_), (0, 0)))
    sink_g = _lane_scalar(jnp.tile(sink, b_))
    y = _from_heads(swa_attn(qh, padk(kh), padk(vh), kch, vch, sink_g, scale), b_)
    yc = None
    if with_ctx_out:
        qch = _to_heads(qc, SWA_HEADS, d)
        rep = lambda t: jnp.repeat(t.reshape(b_, SWA_KV_HEADS, l_, d), grp, axis=1).reshape(b_ * SWA_HEADS, l_, d)
        yc = _from_heads(attn_full(qch, rep(kch), rep(vch), sink_g, scale, True), b_)
    return y, yc


def gated_delta_chunked(q, k, v, log_g, beta, n_ctx):
    g_, t_, dk = k.shape
    dv = v.shape[-1]
    c_ = DN_CHUNK
    n = t_ // c_
    assert dk == c_ and dv == c_
    lg = log_g.reshape(2, g_, n, c_)
    g_cum = jnp.concatenate([jnp.cumsum(lg[0], axis=-1), jnp.flip(jnp.cumsum(jnp.flip(lg[1], axis=-1), axis=-1), axis=-1)],
                            axis=0)
    o_f, o_b = dn_chunked(q.reshape(g_, n, c_, dk), k.reshape(g_, n, c_, dk), v.reshape(g_, n, c_, dv), g_cum,
                          beta.reshape(2 * g_, n, c_), n_ctx)
    return (o_f + o_b).reshape(g_, t_, dv)


def deltanet_group(qkv, z, ab, qkv_c, z_c, ab_c, conv_w, a_log, dt_bias, norm_g, with_ctx_out):
    def prep(qkv_, ab_):
        b_, t_, _ = qkv_.shape
        y = jax.nn.silu(short_conv(qkv_, conv_w))
        q, k, v = [t.reshape(b_, t_, DN_HEADS, DN_HEAD_DIM).transpose(0, 2, 1, 3) for t in jnp.split(y, 3, axis=-1)]
        q = l2norm(q) * DN_HEAD_DIM ** -0.5
        k = l2norm(k)
        ab_ = ab_.reshape(b_, t_, 2, 2, DN_HEADS)
        log_g = -jnp.exp(a_log) * jax.nn.softplus(ab_[:, :, :, 0] + dt_bias)
        beta = jax.nn.sigmoid(ab_[:, :, :, 1])
        return q, k, v, log_g.transpose(2, 0, 3, 1), beta.transpose(2, 0, 3, 1)

    def out(o, z_):
        b_, t_, _ = z_.shape
        o = rms_norm(o.transpose(0, 2, 1, 3), norm_g) * jax.nn.silu(z_).reshape(b_, t_, DN_HEADS, DN_HEAD_DIM)
        return o.reshape(b_, t_, DN_W)

    qc, kc, vc, lgc, bc = prep(qkv_c, ab_c)
    q, k, v, lg, bt = prep(qkv, ab)
    b_, l_, s_ = qkv.shape[0], qkv_c.shape[1], qkv.shape[1]
    seq = lambda tc, tl: jnp.concatenate([tc, tl], axis=2).reshape((b_ * DN_HEADS, l_ + s_) + tc.shape[3:])
    seq_g = lambda tc, tl: jnp.concatenate([tc, tl], axis=3).reshape(2, b_ * DN_HEADS, l_ + s_)
    o = gated_delta_chunked(seq(qc, q), seq(kc, k), seq(vc, v), seq_g(lgc, lg), seq_g(bc, bt), l_ // DN_CHUNK)
    o = o.reshape(b_, DN_HEADS, l_ + s_, DN_HEAD_DIM)
    y = out(o[:, :, l_:], z)
    yc = out(o[:, :, :l_], z_c) if with_ctx_out else None
    return y, yc


def retention_core(q, k, v, log_gamma, n_ctx):
    b_, h_, t_, dk = q.shape
    dv = v.shape[-1]
    c_ = RET_CHUNK
    n = t_ // c_
    gh = b_ * h_
    fwd_rank = jnp.arange(c_, dtype=F32)
    rank = jnp.stack([fwd_rank, c_ - 1 - fwd_rank])[:, None, :]
    lg = log_gamma[..., None]
    per_g = lambda t: jnp.broadcast_to(t[:, None], (2, b_) + t.shape[1:]).reshape((2 * gh,) + t.shape[2:])
    zeta = jnp.exp((c_ - 1 - rank) * lg)
    xi = jnp.exp((rank + 1.0) * lg)
    rel = rank[..., :, None] - rank[..., None, :]
    dmat = jnp.where(rel >= 0, jnp.exp(jnp.maximum(rel, 0.0) * lg[..., None]), 0.0)
    gm = jnp.exp(c_ * log_gamma)
    o_f, o_b = ret_chunked(q.reshape(gh, n, c_, dk), k.reshape(gh, n, c_, dk), v.reshape(gh, n, c_, dv),
                           per_g(dmat), per_g(xi), per_g(zeta), per_g(gm), n_ctx)
    return (o_f + o_b).reshape(b_, h_, t_, dv)


def retention_group(q, k, v, g, qc, kc, vc, gc, log1m_gamma, norm_g, cos, sin, with_ctx_out):
    log_gamma = jnp.log1p(-jnp.exp(log1m_gamma))
    heads = lambda t, dh: t.reshape(t.shape[0], t.shape[1], RET_HEADS, dh)
    bhtd = lambda t: t.transpose(0, 2, 1, 3)
    sc = RET_QK_DIM ** -0.5
    l_ = kc.shape[1]
    q = bhtd(apply_rope(heads(q, RET_QK_DIM), cos, sin)) * sc
    k = bhtd(apply_rope(heads(k, RET_QK_DIM), cos, sin))
    v = bhtd(heads(v, RET_V_DIM))
    kc = bhtd(heads(kc, RET_QK_DIM))
    vc = bhtd(heads(vc, RET_V_DIM))
    qcs = bhtd(heads(qc, RET_QK_DIM)) * sc

    def out(o, g_):
        return head_layer_norm(o.transpose(0, 2, 1, 3), norm_g) * jax.nn.silu(g_)

    seq = lambda tc, tl: jnp.concatenate([tc, tl], axis=2)
    o = retention_core(seq(qcs, q), seq(kc, k), seq(vc, v), log_gamma, l_ // RET_CHUNK)
    y = out(o[:, :, l_:], g)
    yc = out(o[:, :, :l_], gc) if with_ctx_out else None
    return y, yc


def mla_group(cq, ckv, kr, cq_c, ckv_c, kr_c, q_norm, w_uq, kv_norm, w_ukv, cos, sin, with_ctx_out):
    b_, s_, _ = cq.shape
    l_ = cq_c.shape[1]
    dqk = MLA_NOPE_DIM + MLA_ROPE_DIM
    rows = lambda tl, tc: jnp.concatenate([tl.reshape(b_ * s_, -1), tc.reshape(b_ * l_, -1)], axis=0)
    qa = matmul(rms_norm(rows(cq, cq_c), q_norm), w_uq)
    kva = matmul(rms_norm(rows(ckv, ckv_c), kv_norm), w_ukv)
    q = qa[:b_ * s_].reshape(b_, s_, MLA_HEADS, dqk)
    qc = qa[b_ * s_:].reshape(b_, l_, MLA_HEADS, dqk)
    q = jnp.concatenate([q[..., :MLA_NOPE_DIM], apply_rope(q[..., MLA_NOPE_DIM:], cos, sin)], axis=-1)
    kv = kva[:b_ * s_].reshape(b_, s_, MLA_HEADS, MLA_NOPE_DIM + MLA_V_DIM)
    kvc = kva[b_ * s_:].reshape(b_, l_, MLA_HEADS, MLA_NOPE_DIM + MLA_V_DIM)
    kr = apply_rope(kr[:, :, None, :], cos, sin)
    k = jnp.concatenate([kv[..., :MLA_NOPE_DIM], jnp.broadcast_to(kr, (b_, s_, MLA_HEADS, MLA_ROPE_DIM))], axis=-1)
    kc = jnp.concatenate([kvc[..., :MLA_NOPE_DIM],
                          jnp.broadcast_to(kr_c[:, :, None, :], (b_, l_, MLA_HEADS, MLA_ROPE_DIM))], axis=-1)
    v, vc = kv[..., MLA_NOPE_DIM:], kvc[..., MLA_NOPE_DIM:]
    hd = lambda t: t.transpose(0, 2, 1, 3).reshape(b_ * MLA_HEADS, t.shape[1], t.shape[3])
    scale = dqk ** -0.5
    no_sink = jnp.zeros((b_ * MLA_HEADS, 1, 128), F32)
    kch, vch = hd(kc), hd(vc)
    y = attn_full(hd(q), jnp.concatenate([hd(k), kch], axis=1), jnp.concatenate([hd(v), vch], axis=1), no_sink, scale, False)
    y = _from_heads(y, b_)
    yc = _from_heads(attn_full(hd(qc), kch, vch, no_sink, scale, False), b_) if with_ctx_out else None
    return y, yc


def token_mixers(zl, zc, p, layer, rope, with_ctx_out):
    (a_q, a_k, a_v, b_qkv, b_z, b_ab, c_q, c_k, c_v, c_g, d_cq, d_ckv, d_kr) = _split_columns(zl)
    (a_qc, a_kc, a_vc, b_qkvc, b_zc, b_abc, c_qc, c_kc, c_vc, c_gc, d_cqc, d_ckvc, d_krc) = _split_columns(zc)
    swa_cos, swa_sin, ret_cos, ret_sin, mla_cos, mla_sin = rope
    ya, yac = swa_group(a_q, a_k, a_v, a_qc, a_kc, a_vc, p['swa_sink'][layer], swa_cos, swa_sin, with_ctx_out)
    yb, ybc = deltanet_group(b_qkv, b_z, b_ab, b_qkvc, b_zc, b_abc, p['dn_conv_w'][layer], p['dn_a_log'][layer],
                             p['dn_dt_bias'][layer], p['dn_norm_g'][layer], with_ctx_out)
    yr, yrc = retention_group(c_q, c_k, c_v, c_g, c_qc, c_kc, c_vc, c_gc, p['ret_log1m_gamma'][layer],
                              p['ret_norm_g'][layer], ret_cos, ret_sin, with_ctx_out)
    yd, ydc = mla_group(d_cq, d_ckv, d_kr, d_cqc, d_ckvc, d_krc, p['mla_q_norm'][layer], p['mla_w_uq'][layer],
                        p['mla_kv_norm'][layer], p['mla_w_ukv'][layer], mla_cos, mla_sin, with_ctx_out)
    y = jnp.concatenate([ya, yb, yr, yd], axis=-1)
    yc = jnp.concatenate([yac, ybc, yrc, ydc], axis=-1) if with_ctx_out else None
    return y, yc


def local_loss(p, x, ctx, loss_target):
    b_, n_tok, d_ = x.shape
    l_ = ctx.shape[1]
    rows = n_tok // GRID_W
    rope = (*axial_rope(rows, SWA_HEAD_DIM), *sequence_rope(n_tok, RET_QK_DIM), *axial_rope(rows, MLA_ROPE_DIM))
    rl, rc = b_ * n_tok, b_ * l_
    mods = [jnp.concatenate([p['mod'][layer], p['cmod'][layer][None]], axis=0) for layer in range(DEPTH)]
    part = lambda layer, j: mods[layer][:, j * d_:(j + 1) * d_][:, None, :]
    vec = lambda name, layer: p[name][layer][None, :]
    xr = jnp.concatenate([x.reshape(rl, d_), ctx.reshape(rc, d_)], axis=0)
    sh1, sc1 = part(0, 0), part(0, 1)
    h = jnp.concatenate([(x * (1 + sc1[:b_]) + sh1[:b_]).reshape(rl, d_), (ctx * (1 + sc1[b_]) + sh1[b_]).reshape(rc, d_)],
                        axis=0)
    for layer in range(DEPTH):
        with_ctx_out = layer < DEPTH - 1
        g1, sh2, sc2, g2 = part(layer, 2), part(layer, 3), part(layer, 4), part(layer, 5)
        z = matmul(h, p['w_in'][layer])
        zl = z[:rl, :IN_WIDTH].reshape(b_, n_tok, IN_WIDTH)
        zc = z[rl:, :IN_WIDTH].reshape(b_, l_, IN_WIDTH)
        y, yc = token_mixers(zl, zc, p, layer, rope, with_ctx_out)
        if with_ctx_out:
            yo = matmul(jnp.concatenate([y.reshape(rl, d_), yc.reshape(rc, d_)], axis=0), p['w_out'][layer])
            xr, h2 = ln_mod(xr, yo, g1, vec('ln1_g', layer), vec('ln1_b', layer), sc2, sh2, n_tok)
            f = matmul_relu2(matmul(h2, p['w_ff1'][layer]), p['w_ff2'][layer])
            xr, h = ln_mod(xr, f, g2, vec('ln2_g', layer), vec('ln2_b', layer), part(layer + 1, 1), part(layer + 1, 0), n_tok)
        else:
            lat = lambda t: t[:b_]
            yo = matmul(y.reshape(rl, d_), p['w_out'][layer])
            xl, h2 = ln_mod(xr[:rl], yo, lat(g1), vec('ln1_g', layer), vec('ln1_b', layer), lat(sc2), lat(sh2), n_tok)
            f = matmul_relu2(matmul(h2, p['w_ff1'][layer]), p['w_ff2'][layer])
            none = jnp.zeros((b_, 1, d_), F32)
            xl, _ = ln_mod(xl, f, lat(g2), vec('ln2_g', layer), vec('ln2_b', layer), none, none, n_tok)
    err = jnp.square(xl - loss_target.reshape(rl, d_))
    return 0.5 * jnp.sum(jnp.mean(err, axis=-1))


def _shard_shape(shape, axis):
    s = list(shape)
    s[axis] //= N_DEV
    return tuple(s)


def _join_shards(pieces, axis):
    _, _, r, c = pieces.shape
    if axis == 0:
        full = pieces.transpose(1, 0, 2, 3).reshape(DEPTH, N_DEV * r, c)
    else:
        full = pieces.transpose(1, 2, 0, 3).reshape(DEPTH, r, N_DEV * c)
    return full.astype(F32)


def _split_shards(g, shape, axis):
    r, c = _shard_shape(shape, axis)
    if axis == 0:
        pieces = g.reshape(DEPTH, N_DEV, r, c).transpose(1, 0, 2, 3)
    else:
        pieces = g.reshape(DEPTH, r, N_DEV, c).transpose(2, 0, 1, 3)
    return pieces.astype(BF16)


def _pad_vec(vec, rows_multiple=8):
    n = vec.shape[0]
    rows = -(-n // (128 * rows_multiple)) * rows_multiple
    return jnp.pad(vec, (0, rows * 128 - n)).reshape(rows, 128)


def _adamw(w, g, m, v, name):
    shape = w.shape
    if w.ndim >= 2 and shape[-1] >= 128:
        as2 = lambda t: t.reshape(-1, shape[-1])
        d, nm, nv = _adamw_call(as2(w), as2(g), as2(m), as2(v), name)
        return d.reshape(shape), nm.reshape(shape), nv.reshape(shape)
    n = int(np.prod(shape))
    as2 = lambda t: _pad_vec(t.reshape(-1))
    d, nm, nv = _adamw_call(as2(w), as2(g), as2(m), as2(v), name)
    un = lambda t: t.reshape(-1)[:n].reshape(shape)
    return un(d), un(nm), un(nv)


def kernel(x, c, ctx, c_ctx, ada_w, ada_b, w_in, swa_sink, dn_conv_w, dn_a_log, dn_dt_bias, dn_norm_g, ret_log1m_gamma, ret_norm_g, mla_q_norm, mla_w_uq, mla_kv_norm, mla_w_ukv, w_out, ln1_g, ln1_b, w_ff1, w_ff2, ln2_g, ln2_b, loss_target, m_c_ctx, m_ada_w, m_ada_b, m_w_in, m_swa_sink, m_dn_conv_w, m_dn_a_log, m_dn_dt_bias, m_dn_norm_g, m_ret_log1m_gamma, m_ret_norm_g, m_mla_q_norm, m_mla_w_uq, m_mla_kv_norm, m_mla_w_ukv, m_w_out, m_ln1_g, m_ln1_b, m_w_ff1, m_w_ff2, m_ln2_g, m_ln2_b, v_c_ctx, v_ada_w, v_ada_b, v_w_in, v_swa_sink, v_dn_conv_w, v_dn_a_log, v_dn_dt_bias, v_dn_norm_g, v_ret_log1m_gamma, v_ret_norm_g, v_mla_q_norm, v_mla_w_uq, v_mla_kv_norm, v_mla_w_ukv, v_w_out, v_ln1_g, v_ln1_b, v_w_ff1, v_w_ff2, v_ln2_g, v_ln2_b):
    a = dict(zip(ARG_NAMES, (x, c, ctx, c_ctx, ada_w, ada_b, w_in, swa_sink, dn_conv_w, dn_a_log, dn_dt_bias, dn_norm_g, ret_log1m_gamma, ret_norm_g, mla_q_norm, mla_w_uq, mla_kv_norm, mla_w_ukv, w_out, ln1_g, ln1_b, w_ff1, w_ff2, ln2_g, ln2_b, loss_target, m_c_ctx, m_ada_w, m_ada_b, m_w_in, m_swa_sink, m_dn_conv_w, m_dn_a_log, m_dn_dt_bias, m_dn_norm_g, m_ret_log1m_gamma, m_ret_norm_g, m_mla_q_norm, m_mla_w_uq, m_mla_kv_norm, m_mla_w_ukv, m_w_out, m_ln1_g, m_ln1_b, m_w_ff1, m_w_ff2, m_ln2_g, m_ln2_b, v_c_ctx, v_ada_w, v_ada_b, v_w_in, v_swa_sink, v_dn_conv_w, v_dn_a_log, v_dn_dt_bias, v_dn_norm_g, v_ret_log1m_gamma, v_ret_norm_g, v_mla_q_norm, v_mla_w_uq, v_mla_kv_norm, v_mla_w_ukv, v_w_out, v_ln1_g, v_ln1_b, v_w_ff1, v_w_ff2, v_ln2_g, v_ln2_b)))
    me = 4 * lax.axis_index("x") + 2 * lax.axis_index("y") + lax.axis_index("c")
    b_loc = x.shape[0]
    n_ex = N_DEV * b_loc
    conv_k, conv_c = dn_conv_w.shape[1], dn_conv_w.shape[2]
    ada_cols = ada_w.shape[2]

    small_in = jnp.concatenate([c.reshape(-1), dn_conv_w.reshape(-1)])
    gathered = _gather_call([_pad_vec(small_in)] + [a[name].astype(BF16) for name, _, _ in BIG], "gather_weights")
    small_all = gathered[0].reshape(N_DEV, -1)
    c_all = small_all[:, :b_loc * D_MODEL].reshape(n_ex, D_MODEL)
    conv_all = small_all[:, b_loc * D_MODEL:b_loc * D_MODEL + DEPTH * conv_k * conv_c].reshape(N_DEV, DEPTH, conv_k, conv_c)
    conv_full = conv_all.transpose(1, 2, 0, 3).reshape(DEPTH, conv_k, N_DEV * conv_c)
    big = {name: _join_shards(pieces, axis) for (name, _, axis), pieces in zip(BIG, gathered[1:])}
    big['w_in'] = jnp.pad(big['w_in'], ((0, 0), (0, 0), (0, IN_WIDTH_PAD - IN_WIDTH)))

    n_rows = -(-(n_ex + 1) // 16) * 16
    silu_cc = jax.nn.silu(c_ctx)
    a_rows = jnp.concatenate([jax.nn.silu(c_all), silu_cc[None], jnp.zeros((n_rows - n_ex - 1, D_MODEL), F32)], axis=0)
    m_loc = jnp.concatenate([_mm_call(a_rows, ada_w[l], False, "ada_fwd") for l in range(DEPTH)], axis=0)
    m_all = _gather_call([m_loc], "gather_mod")[0].reshape(N_DEV, DEPTH, n_rows, ada_cols)
    mod_full = m_all.transpose(1, 2, 0, 3).reshape(DEPTH, n_rows, N_DEV * ada_cols) + ada_b[:, None, :]
    mod = lax.dynamic_slice_in_dim(mod_full, me * b_loc, b_loc, axis=1)
    cmod = mod_full[:, n_ex]

    p = dict(big)
    p.update(mod=mod, cmod=cmod, dn_conv_w=conv_full)
    for name in SMALL:
        p[name] = a[name]
    loss_loc, (gp, gx) = jax.value_and_grad(local_loss, argnums=(0, 1))(p, x, ctx, loss_target)
    loss = lax.psum(loss_loc, MESH_AXES)

    gp['w_in'] = gp['w_in'][:, :, :IN_WIDTH]
    my_core = lax.axis_index("c")
    parts = [_split_shards(gp[name], shape, axis) for name, shape, axis in BIG]
    from_sibling = _transfer_call(parts, "scatter_sibling", _sibling_plan, N_CHIP, N_CHIP)
    chip_parts = []
    for (name, shape, axis), part, other in zip(BIG, parts, from_sibling):
        r, c_ = _shard_shape(shape, axis)
        mine = lax.dynamic_index_in_dim(part.reshape(N_CHIP, 2, DEPTH * r, c_), my_core, axis=1, keepdims=False)
        chip_parts.append(_pair_add_call(mine, other.reshape(N_CHIP, DEPTH * r, c_), "pair_" + name))
    arrived = _transfer_call(chip_parts, "scatter_chips", _chip_plan, N_CHIP - 1, N_CHIP)
    g_big = {}
    for (name, shape, axis), part in zip(BIG, arrived):
        r, c_ = _shard_shape(shape, axis)
        g_big[name] = _sum8_call(part, "sum_" + name).reshape(DEPTH, r, c_)

    d_loc = jnp.concatenate([gp['mod'], gp['cmod'][:, None, :]], axis=1).reshape(DEPTH * (b_loc + 1), -1)
    d_loc = jnp.pad(d_loc, ((0, 8 - DEPTH * (b_loc + 1)), (0, 0)))
    d_all = _gather_call([d_loc], "gather_dmod")[0][:, :DEPTH * (b_loc + 1)].reshape(N_DEV, DEPTH, b_loc + 1, -1)
    d_rows = d_all[:, :, :b_loc].transpose(1, 0, 2, 3).reshape(DEPTH, n_ex, -1)
    d_crow = d_all[0, :, b_loc]
    for d in range(1, N_DEV):
        d_crow = d_crow + d_all[d, :, b_loc]
    dm_full = jnp.concatenate([d_rows, d_crow[:, None, :], jnp.zeros((DEPTH, n_rows - n_ex - 1, d_rows.shape[-1]), F32)], axis=1)
    g_ada_b = jnp.sum(dm_full, axis=1)
    dm_mine = lax.dynamic_slice_in_dim(dm_full, me * ada_cols, ada_cols, axis=2)
    g_ada_w = jnp.stack([_mm_call(a_rows, dm_mine[l], True, "ada_bwd_w") for l in range(DEPTH)])
    crow8 = jnp.concatenate([dm_mine[:, n_ex:n_ex + 1], jnp.zeros((DEPTH, 15, ada_cols), F32)], axis=1)
    dsilu_part = sum(_mm_call(crow8[l], jnp.transpose(ada_w[l]), False, "ada_bwd_c")[0] for l in range(DEPTH))

    small_g = jnp.concatenate([gp[name].reshape(-1) for name in SMALL] + [gp['dn_conv_w'].reshape(-1), dsilu_part])
    small_sum = _sum8_call(_gather_call([_pad_vec(small_g)], "gather_small_grads")[0], "sum_small_grads").reshape(-1)
    g_all, off = {}, 0
    for name in SMALL:
        n = int(np.prod(a[name].shape))
        g_all[name] = small_sum[off:off + n].reshape(a[name].shape)
        off += n
    n = DEPTH * conv_k * N_DEV * conv_c
    g_conv_full = small_sum[off:off + n].reshape(DEPTH, conv_k, N_DEV * conv_c)
    g_all['dn_conv_w'] = lax.dynamic_slice_in_dim(g_conv_full, me * conv_c, conv_c, axis=2)
    off += n
    dsilu = small_sum[off:off + D_MODEL]
    sig = jax.nn.sigmoid(c_ctx)
    g_all['c_ctx'] = dsilu * (sig * (1 + c_ctx * (1 - sig)))
    g_all['ada_w'] = g_ada_w
    g_all['ada_b'] = g_ada_b
    g_all.update(g_big)

    delta, new_m, new_v = {}, {}, {}
    for name in WEIGHTS:
        delta[name], new_m[name], new_v[name] = _adamw(a[name], g_all[name], a['m_' + name], a['v_' + name], "adamw_" + name)
    return (loss, gx, *[g_all[n] for n in WEIGHTS], *[delta[n] for n in WEIGHTS],
            *[new_m[n] for n in WEIGHTS], *[new_v[n] for n in WEIGHTS])
```

```python
import functools

import jax
import jax.numpy as jnp
import numpy as np
from jax import lax
from jax.experimental import pallas as pl
from jax.experimental.pallas import tpu as pltpu

F32 = jnp.float32
BF16 = jnp.bfloat16
N_DEV = 8
MESH_AXES = ("x", "y", "c")

D_MODEL = 1024
DEPTH = 2
GRID_W = 64
SWA_HEADS, SWA_KV_HEADS, SWA_HEAD_DIM, SWA_WINDOW, SWA_BLOCK = 4, 2, 64, 128, 128
DN_HEADS, DN_HEAD_DIM, DN_CHUNK = 4, 64, 64
RET_HEADS, RET_QK_DIM, RET_V_DIM, RET_CHUNK = 4, 32, 64, 64
MLA_HEADS, MLA_Q_RANK, MLA_KV_RANK, MLA_NOPE_DIM, MLA_ROPE_DIM, MLA_V_DIM = 4, 256, 128, 64, 32, 64
D_FF = 4 * D_MODEL
ROPE_BASE = 10000.0
NORM_EPS = 1e-6
LN_EPS = 1e-5
DEEPNORM_ALPHA = (2 * DEPTH) ** 0.25
SWA_Q = SWA_HEADS * SWA_HEAD_DIM
SWA_KV = SWA_KV_HEADS * SWA_HEAD_DIM
DN_W = DN_HEADS * DN_HEAD_DIM
RET_QK = RET_HEADS * RET_QK_DIM
RET_V = RET_HEADS * RET_V_DIM
IN_SPLITS = (SWA_Q, SWA_KV, SWA_KV, 3 * DN_W, DN_W, 4 * DN_HEADS, RET_QK, RET_QK, RET_V, RET_V,
             MLA_Q_RANK, MLA_KV_RANK, MLA_ROPE_DIM)
IN_WIDTH = sum(IN_SPLITS)
IN_WIDTH_PAD = -(-IN_WIDTH // 128) * 128

ADAM_LR, ADAM_B1, ADAM_B2, ADAM_EPS, ADAM_WD, ADAM_STEP = 0.001, 0.9, 0.999, 1e-08, 0.01, 10

WEIGHTS = ['c_ctx', 'ada_w', 'ada_b', 'w_in', 'swa_sink', 'dn_conv_w', 'dn_a_log', 'dn_dt_bias', 'dn_norm_g',
           'ret_log1m_gamma', 'ret_norm_g', 'mla_q_norm', 'mla_w_uq', 'mla_kv_norm', 'mla_w_ukv', 'w_out', 'ln1_g',
           'ln1_b', 'w_ff1', 'w_ff2', 'ln2_g', 'ln2_b']
FWD_INPUTS = ['x', 'c', 'ctx'] + WEIGHTS
ARG_NAMES = FWD_INPUTS + ['loss_target'] + ['m_' + n for n in WEIGHTS] + ['v_' + n for n in WEIGHTS]

BIG = (('w_in', (D_MODEL, IN_WIDTH), 1), ('w_out', (D_MODEL, D_MODEL), 0), ('w_ff1', (D_MODEL, D_FF), 1),
       ('w_ff2', (D_FF, D_MODEL), 0), ('mla_w_uq', (MLA_Q_RANK, MLA_HEADS * (MLA_NOPE_DIM + MLA_ROPE_DIM)), 1),
       ('mla_w_ukv', (MLA_KV_RANK, MLA_HEADS * (MLA_NOPE_DIM + MLA_V_DIM)), 1))
SMALL = ('swa_sink', 'dn_a_log', 'dn_dt_bias', 'dn_norm_g', 'ret_log1m_gamma', 'ret_norm_g', 'mla_q_norm',
         'mla_kv_norm', 'ln1_g', 'ln1_b', 'ln2_g', 'ln2_b')


def _pcall(body, **kw):
    return pl.pallas_call(body, **kw)


def _pick(n, cands):
    for cand in cands:
        if n % cand == 0:
            return cand
    return n


def _bdot(a, b, dims):
    return lax.dot_general(a.astype(BF16), b.astype(BF16), dims, preferred_element_type=F32)


def _lane0(t):
    return jnp.where(lax.broadcasted_iota(jnp.int32, t.shape, t.ndim - 1) == 0, t, 0.0)


NN = (((1,), (0,)), ((), ()))
NT = (((1,), (1,)), ((), ()))
TN = (((0,), (0,)), ((), ()))
BNN = (((2,), (1,)), ((0,), (0,)))
BNT = (((2,), (2,)), ((0,), (0,)))


MM_ROW_TILE_MAX = 1088
MM_COL_TILE_MAX = 1408
MM_TOKEN_TILE_MAX = 1088
VMEM_LIMIT_MAX = 60 * 1024 * 1024


def _tile(n, cap, align):
    best = None
    for t in range(align, min(n, cap) + 1, align):
        if n % t == 0:
            best = t
    return best or n


def _relu2(t):
    return jnp.square(jnp.maximum(t, 0.0))


def _mm_call(a, b, trans_a, name, act_a=False, epi=None):
    if trans_a:
        kdim, m = a.shape
        tk = _tile(kdim, MM_TOKEN_TILE_MAX, 8)
        tm = _tile(m, 1024, 128)
    else:
        m, kdim = a.shape
        tk = _tile(kdim, MM_COL_TILE_MAX, 128)
        tm = _tile(m, MM_ROW_TILE_MAX, 8)
    n = b.shape[1]
    assert b.shape[0] == kdim
    tn = _tile(n, MM_COL_TILE_MAX, 128)
    nk = kdim // tk

    def body(*refs):
        a_ref, b_ref = refs[0], refs[1]
        e_ref = refs[2] if epi is not None else None
        o_ref = refs[-1]
        k = pl.program_id(2)
        av = a_ref[...]
        if act_a:
            av = _relu2(av)
        part = _bdot(av, b_ref[...], TN if trans_a else NN)

        def finish(t):
            return t * (2.0 * jnp.maximum(e_ref[...], 0.0)) if epi is not None else t

        if nk == 1:
            o_ref[...] = finish(part)
        else:
            @pl.when(k == 0)
            def _():
                o_ref[...] = part

            @pl.when((k > 0) & (k < nk - 1))
            def _():
                o_ref[...] += part

            @pl.when(k == nk - 1)
            def _():
                o_ref[...] = finish(o_ref[...] + part)

    if trans_a:
        a_spec = pl.BlockSpec((tk, tm), lambda i, j, k: (k, i))
    else:
        a_spec = pl.BlockSpec((tm, tk), lambda i, j, k: (i, k))
    o_spec = pl.BlockSpec((tm, tn), lambda i, j, k: (i, j))
    in_specs = [a_spec, pl.BlockSpec((tk, tn), lambda i, j, k: (k, j))] + ([o_spec] if epi is not None else [])
    tiles = tm * tk * a.dtype.itemsize + tk * tn * b.dtype.itemsize + tm * tn * 4 * (2 if epi is not None else 1)
    temps = tm * tk * (2 + (4 if act_a else 0)) + tk * tn * 2 + 2 * tm * tn * 4
    return _pcall(
        body, name=name, grid=(m // tm, n // tn, nk), in_specs=in_specs, out_specs=o_spec,
        out_shape=jax.ShapeDtypeStruct((m, n), F32),
        compiler_params=pltpu.CompilerParams(dimension_semantics=("parallel", "parallel", "arbitrary"),
                                             vmem_limit_bytes=min(2 * tiles + temps + (4 << 20), VMEM_LIMIT_MAX)),
    )(*((a, b) + ((epi,) if epi is not None else ())))


@jax.custom_vjp
def matmul(a, b):
    return _mm_call(a, b.astype(BF16), False, "mm_fwd")


def _matmul_fwd(a, b):
    bb = b.astype(BF16)
    return _mm_call(a, bb, False, "mm_fwd"), (a, bb)


def _matmul_bwd(res, g):
    a, bb = res
    da = _mm_call(g, jnp.transpose(bb), False, "mm_bwd_da")
    db = _mm_call(a, g, True, "mm_bwd_db")
    return da, db


matmul.defvjp(_matmul_fwd, _matmul_bwd)


@jax.custom_vjp
def matmul_relu2(a, b):
    return _mm_call(a, b.astype(BF16), False, "mm_act_fwd", act_a=True)


def _matmul_relu2_fwd(a, b):
    bb = b.astype(BF16)
    return _mm_call(a, bb, False, "mm_act_fwd", act_a=True), (a, bb)


def _matmul_relu2_bwd(res, g):
    a, bb = res
    da = _mm_call(g, jnp.transpose(bb), False, "mm_act_bwd_da", epi=a)
    db = _mm_call(a, g, True, "mm_act_bwd_db", act_a=True)
    return da, db


matmul_relu2.defvjp(_matmul_relu2_fwd, _matmul_relu2_bwd)


LN_ROW_TILE = 256


def _ln_group_map(group_rows, n_groups):
    per = group_rows // LN_ROW_TILE
    return lambda i: (jnp.minimum(i // per, n_groups - 1), 0, 0)


def _ln_stats(x, y, gate):
    pre = DEEPNORM_ALPHA * x + gate * y
    mu = jnp.mean(pre, axis=-1, keepdims=True)
    cen = pre - mu
    rstd = lax.rsqrt(jnp.mean(jnp.square(cen), axis=-1, keepdims=True) + LN_EPS)
    return cen * rstd, rstd


def _ln_mod_fwd_call(x, y, gate, gamma, beta, sc, sh, group_rows):
    r, d = x.shape
    ng = gate.shape[0]
    gmap = _ln_group_map(group_rows, ng)

    def body(x_ref, y_ref, gate_ref, gamma_ref, beta_ref, sc_ref, sh_ref, xn_ref, h_ref):
        xh, _ = _ln_stats(x_ref[...], y_ref[...], gate_ref[0])
        xn = xh * gamma_ref[...] + beta_ref[...]
        xn_ref[...] = xn
        h_ref[...] = xn * (1.0 + sc_ref[0]) + sh_ref[0]

    row = pl.BlockSpec((LN_ROW_TILE, d), lambda i: (i, 0))
    grp = pl.BlockSpec((1, 1, d), gmap)
    vec = pl.BlockSpec((1, d), lambda i: (0, 0))
    return _pcall(
        body, name="ln_mod_fwd", grid=(r // LN_ROW_TILE,),
        in_specs=[row, row, grp, vec, vec, grp, grp], out_specs=[row, row],
        out_shape=[jax.ShapeDtypeStruct((r, d), F32)] * 2,
        compiler_params=pltpu.CompilerParams(dimension_semantics=("parallel",)),
    )(x, y, gate, gamma, beta, sc, sh)


def _ln_mod_bwd_call(x, y, gate, gamma, beta, sc, dxn, dh, group_rows):
    r, d = x.shape
    ng = gate.shape[0]
    gmap = _ln_group_map(group_rows, ng)
    per = group_rows // LN_ROW_TILE

    def body(x_ref, y_ref, gate_ref, gamma_ref, beta_ref, sc_ref, dxn_ref, dh_ref,
             dx_ref, dy_ref, dgate_ref, dgamma_ref, dbeta_ref, dsc_ref, dsh_ref):
        i = pl.program_id(0)
        yv, gate_v, gamma_v = y_ref[...], gate_ref[0], gamma_ref[...]
        xh, rstd = _ln_stats(x_ref[...], yv, gate_v)
        dhv = dh_ref[...]
        dtot = dxn_ref[...] + dhv * (1.0 + sc_ref[0])
        dxh = dtot * gamma_v
        dpre = rstd * (dxh - jnp.mean(dxh, axis=-1, keepdims=True) - xh * jnp.mean(dxh * xh, axis=-1, keepdims=True))
        dx_ref[...] = DEEPNORM_ALPHA * dpre
        dy_ref[...] = gate_v * dpre
        col = lambda t: jnp.sum(t, axis=0, keepdims=True)

        @pl.when(i == 0)
        def _():
            dgamma_ref[...] = jnp.zeros_like(dgamma_ref)
            dbeta_ref[...] = jnp.zeros_like(dbeta_ref)

        first_of_group = (i % per == 0) | (i == (ng - 1) * per)

        @pl.when(first_of_group & (i <= (ng - 1) * per))
        def _():
            dgate_ref[...] = jnp.zeros_like(dgate_ref)
            dsc_ref[...] = jnp.zeros_like(dsc_ref)
            dsh_ref[...] = jnp.zeros_like(dsh_ref)

        dgamma_ref[...] += col(dtot * xh)
        dbeta_ref[...] += col(dtot)
        dgate_ref[0] += col(dpre * yv)
        dsc_ref[0] += col(dhv * (xh * gamma_v + beta_ref[...]))
        dsh_ref[0] += col(dhv)

    row = pl.BlockSpec((LN_ROW_TILE, d), lambda i: (i, 0))
    grp = pl.BlockSpec((1, 1, d), gmap)
    vec = pl.BlockSpec((1, d), lambda i: (0, 0))
    big = jax.ShapeDtypeStruct((r, d), F32)
    gs = jax.ShapeDtypeStruct((ng, 1, d), F32)
    vs = jax.ShapeDtypeStruct((1, d), F32)
    return _pcall(
        body, name="ln_mod_bwd", grid=(r // LN_ROW_TILE,),
        in_specs=[row, row, grp, vec, vec, grp, row, row],
        out_specs=[row, row, grp, vec, vec, grp, grp],
        out_shape=[big, big, gs, vs, vs, gs, gs],
        compiler_params=pltpu.CompilerParams(dimension_semantics=("arbitrary",)),
    )(x, y, gate, gamma, beta, sc, dxn, dh)


@functools.partial(jax.custom_vjp, nondiff_argnums=(7,))
def ln_mod(x, y, gate, gamma, beta, sc, sh, group_rows):
    return tuple(_ln_mod_fwd_call(x, y, gate, gamma, beta, sc, sh, group_rows))


def _ln_mod_fwd(x, y, gate, gamma, beta, sc, sh, group_rows):
    xn, h = _ln_mod_fwd_call(x, y, gate, gamma, beta, sc, sh, group_rows)
    return (xn, h), (x, y, gate, gamma, beta, sc)


def _ln_mod_bwd(group_rows, res, cts):
    x, y, gate, gamma, beta, sc = res
    dxn, dh = cts
    return tuple(_ln_mod_bwd_call(x, y, gate, gamma, beta, sc, dxn, dh, group_rows))


ln_mod.defvjp(_ln_mod_fwd, _ln_mod_bwd)


def _attn_probs(q, k, sink, scale, has_sink):
    s = _bdot(q, k, NT) * scale
    m = jnp.max(s, axis=-1, keepdims=True)
    if has_sink:
        m = jnp.maximum(m, sink)
    p = jnp.exp(s - m)
    den = jnp.sum(p, axis=-1, keepdims=True)
    p_sink = None
    if has_sink:
        p_sink = jnp.exp(sink - m)
        den = den + p_sink
    inv = 1.0 / den
    if has_sink:
        p_sink = p_sink * inv
    return p * inv, p_sink


def _attn_full_fwd_call(q, k, v, sink, scale, has_sink):
    g, sq, dq = q.shape
    nk, dv = k.shape[1], v.shape[2]
    bq = _pick(sq, (256, 128))

    def body(q_ref, k_ref, v_ref, sink_ref, o_ref):
        p, _ = _attn_probs(q_ref[0], k_ref[0], sink_ref[0, :, 0:1], scale, has_sink)
        o_ref[0] = _bdot(p, v_ref[0], NN)

    return _pcall(
        body, name="attn_full_fwd", grid=(g, sq // bq),
        in_specs=[pl.BlockSpec((1, bq, dq), lambda b, i: (b, i, 0)), pl.BlockSpec((1, nk, dq), lambda b, i: (b, 0, 0)),
                  pl.BlockSpec((1, nk, dv), lambda b, i: (b, 0, 0)), pl.BlockSpec((1, 1, 128), lambda b, i: (b, 0, 0))],
        out_specs=pl.BlockSpec((1, bq, dv), lambda b, i: (b, i, 0)),
        out_shape=jax.ShapeDtypeStruct((g, sq, dv), F32),
        compiler_params=pltpu.CompilerParams(dimension_semantics=("parallel", "arbitrary")),
    )(q, k, v, sink)


def _attn_full_bwd_call(q, k, v, sink, o, do, scale, has_sink):
    g, sq, dq = q.shape
    nk, dv = k.shape[1], v.shape[2]
    bq = _pick(sq, (256, 128))

    def body(q_ref, k_ref, v_ref, sink_ref, o_ref, do_ref, dq_ref, dk_ref, dv_ref, dsink_ref):
        i = pl.program_id(1)
        qv, kv, vv, dov = q_ref[0], k_ref[0], v_ref[0], do_ref[0]
        p, p_sink = _attn_probs(qv, kv, sink_ref[0, :, 0:1], scale, has_sink)
        delta = jnp.sum(dov * o_ref[0], axis=-1, keepdims=True)
        dv_part = _bdot(p, dov, TN)
        dp = _bdot(dov, vv, NT)
        ds = p * (dp - delta) * scale
        dq_ref[0] = _bdot(ds, kv, NN)
        dk_part = _bdot(ds, qv, TN)
        if has_sink:
            dsk = jnp.broadcast_to(-jnp.sum(p_sink * delta, axis=0, keepdims=True), (1, 128))
        else:
            dsk = jnp.zeros((1, 128), F32)

        @pl.when(i == 0)
        def _():
            dk_ref[0] = dk_part
            dv_ref[0] = dv_part
            dsink_ref[0] = dsk

        @pl.when(i > 0)
        def _():
            dk_ref[0] += dk_part
            dv_ref[0] += dv_part
            dsink_ref[0] += dsk

    qspec = pl.BlockSpec((1, bq, dq), lambda b, i: (b, i, 0))
    kspec = pl.BlockSpec((1, nk, dq), lambda b, i: (b, 0, 0))
    vspec = pl.BlockSpec((1, nk, dv), lambda b, i: (b, 0, 0))
    ospec = pl.BlockSpec((1, bq, dv), lambda b, i: (b, i, 0))
    sspec = pl.BlockSpec((1, 1, 128), lambda b, i: (b, 0, 0))
    return _pcall(
        body, name="attn_full_bwd", grid=(g, sq // bq),
        in_specs=[qspec, kspec, vspec, sspec, ospec, ospec],
        out_specs=[qspec, kspec, vspec, sspec],
        out_shape=[jax.ShapeDtypeStruct(q.shape, F32), jax.ShapeDtypeStruct(k.shape, F32),
                   jax.ShapeDtypeStruct(v.shape, F32), jax.ShapeDtypeStruct(sink.shape, F32)],
        compiler_params=pltpu.CompilerParams(dimension_semantics=("parallel", "arbitrary")),
    )(q, k, v, sink, o, do)


@functools.partial(jax.custom_vjp, nondiff_argnums=(4, 5))
def attn_full(q, k, v, sink, scale, has_sink):
    return _attn_full_fwd_call(q.astype(BF16), k.astype(BF16), v.astype(BF16), sink, scale, has_sink)


def _attn_full_fwd(q, k, v, sink, scale, has_sink):
    q, k, v = q.astype(BF16), k.astype(BF16), v.astype(BF16)
    o = _attn_full_fwd_call(q, k, v, sink, scale, has_sink)
    return o, (q, k, v, sink, o)


def _attn_full_bwd(scale, has_sink, res, do):
    q, k, v, sink, o = res
    dq, dk, dv, dsink = _attn_full_bwd_call(q, k, v, sink, o, do, scale, has_sink)
    return dq, dk, dv, _lane0(dsink)


attn_full.defvjp(_attn_full_fwd, _attn_full_bwd)


SWA_GROUP = SWA_HEADS // SWA_KV_HEADS


def _swa_rows(ref):
    return ref[...].reshape(SWA_GROUP * SWA_BLOCK, ref.shape[-1])


def _swa_sink_rows(sink_ref):
    head = lax.broadcasted_iota(jnp.int32, (SWA_GROUP * SWA_BLOCK, 1), 0) // SWA_BLOCK
    out = jnp.zeros((SWA_GROUP * SWA_BLOCK, 1), F32)
    for j in range(SWA_GROUP):
        out = jnp.where(head == j, sink_ref[j, :, 0:1], out)
    return out


def _swa_probs(q, kw, kc, sink, i, s_len, scale):
    w = SWA_BLOCK
    rows = q.shape[0]
    s_loc = _bdot(q, kw, NT) * scale
    qpos = i * w + lax.broadcasted_iota(jnp.int32, (rows, 3 * w), 0) % w
    kpos = (i - 1) * w + lax.broadcasted_iota(jnp.int32, (rows, 3 * w), 1)
    valid = (jnp.abs(kpos - qpos) <= SWA_WINDOW) & (kpos >= 0) & (kpos < s_len)
    s_loc = jnp.where(valid, s_loc, -jnp.inf)
    s_ctx = _bdot(q, kc, NT) * scale
    m = jnp.maximum(jnp.maximum(jnp.max(s_loc, axis=-1, keepdims=True), jnp.max(s_ctx, axis=-1, keepdims=True)), sink)
    p_loc = jnp.exp(s_loc - m)
    p_ctx = jnp.exp(s_ctx - m)
    p_sink = jnp.exp(sink - m)
    inv = 1.0 / (jnp.sum(p_loc, axis=-1, keepdims=True) + jnp.sum(p_ctx, axis=-1, keepdims=True) + p_sink)
    return p_loc * inv, p_ctx * inv, p_sink * inv


def _swa_fwd_call(q, kp, vp, kc, vc, sink, scale):
    g, s_len, d = q.shape
    l_ctx = kc.shape[1]
    w = SWA_BLOCK
    grp = SWA_GROUP

    def body(q_ref, kp_ref, vp_ref, kc_ref, vc_ref, sink_ref, o_ref):
        i = pl.program_id(1)
        start = pl.multiple_of(i * w, w)
        kw = kp_ref[0, pl.ds(start, 3 * w), :]
        vw = vp_ref[0, pl.ds(start, 3 * w), :]
        p_loc, p_ctx, _ = _swa_probs(_swa_rows(q_ref), kw, kc_ref[0], _swa_sink_rows(sink_ref), i, s_len, scale)
        o_ref[...] = (_bdot(p_loc, vw, NN) + _bdot(p_ctx, vc_ref[0], NN)).reshape(grp, w, d)

    return _pcall(
        body, name="swa_fwd", grid=(g // grp, s_len // w),
        in_specs=[pl.BlockSpec((grp, w, d), lambda b, i: (b, i, 0)),
                  pl.BlockSpec((1, s_len + 2 * w, d), lambda b, i: (b, 0, 0)),
                  pl.BlockSpec((1, s_len + 2 * w, d), lambda b, i: (b, 0, 0)),
                  pl.BlockSpec((1, l_ctx, d), lambda b, i: (b, 0, 0)),
                  pl.BlockSpec((1, l_ctx, d), lambda b, i: (b, 0, 0)),
                  pl.BlockSpec((grp, 1, 128), lambda b, i: (b, 0, 0))],
        out_specs=pl.BlockSpec((grp, w, d), lambda b, i: (b, i, 0)),
        out_shape=jax.ShapeDtypeStruct(q.shape, F32),
        compiler_params=pltpu.CompilerParams(dimension_semantics=("parallel", "arbitrary")),
    )(q, kp, vp, kc, vc, sink)


def _swa_bwd_call(q, kp, vp, kc, vc, sink, o, do, scale):
    g, s_len, d = q.shape
    l_ctx = kc.shape[1]
    w = SWA_BLOCK
    grp = SWA_GROUP
    sp = s_len + 2 * w

    def body(q_ref, kp_ref, vp_ref, kc_ref, vc_ref, sink_ref, o_ref, do_ref,
             dq_ref, dkp_ref, dvp_ref, dkc_ref, dvc_ref, dsink_ref):
        i = pl.program_id(1)
        start = pl.multiple_of(i * w, w)
        qv, dov = _swa_rows(q_ref), _swa_rows(do_ref)
        kw = kp_ref[0, pl.ds(start, 3 * w), :]
        vw = vp_ref[0, pl.ds(start, 3 * w), :]
        kcv, vcv = kc_ref[0], vc_ref[0]
        p_loc, p_ctx, p_sink = _swa_probs(qv, kw, kcv, _swa_sink_rows(sink_ref), i, s_len, scale)
        delta = jnp.sum(dov * _swa_rows(o_ref), axis=-1, keepdims=True)
        ds_loc = p_loc * (_bdot(dov, vw, NT) - delta) * scale
        ds_ctx = p_ctx * (_bdot(dov, vcv, NT) - delta) * scale
        dq_ref[...] = (_bdot(ds_loc, kw, NN) + _bdot(ds_ctx, kcv, NN)).reshape(grp, w, d)
        dsk = jnp.broadcast_to(-jnp.sum((p_sink * delta).reshape(grp, w, 1), axis=1, keepdims=True), (grp, 1, 128))

        @pl.when(i == 0)
        def _():
            dkp_ref[...] = jnp.zeros_like(dkp_ref)
            dvp_ref[...] = jnp.zeros_like(dvp_ref)
            dkc_ref[...] = jnp.zeros_like(dkc_ref)
            dvc_ref[...] = jnp.zeros_like(dvc_ref)
            dsink_ref[...] = jnp.zeros_like(dsink_ref)

        dkp_ref[0, pl.ds(start, 3 * w), :] += _bdot(ds_loc, qv, TN)
        dvp_ref[0, pl.ds(start, 3 * w), :] += _bdot(p_loc, dov, TN)
        dkc_ref[0] += _bdot(ds_ctx, qv, TN)
        dvc_ref[0] += _bdot(p_ctx, dov, TN)
        dsink_ref[...] += dsk

    qspec = pl.BlockSpec((grp, w, d), lambda b, i: (b, i, 0))
    kspec = pl.BlockSpec((1, sp, d), lambda b, i: (b, 0, 0))
    cspec = pl.BlockSpec((1, l_ctx, d), lambda b, i: (b, 0, 0))
    sspec = pl.BlockSpec((grp, 1, 128), lambda b, i: (b, 0, 0))
    return _pcall(
        body, name="swa_bwd", grid=(g // grp, s_len // w),
        in_specs=[qspec, kspec, kspec, cspec, cspec, sspec, qspec, qspec],
        out_specs=[qspec, kspec, kspec, cspec, cspec, sspec],
        out_shape=[jax.ShapeDtypeStruct(q.shape, F32), jax.ShapeDtypeStruct(kp.shape, F32),
                   jax.ShapeDtypeStruct(vp.shape, F32), jax.ShapeDtypeStruct(kc.shape, F32),
                   jax.ShapeDtypeStruct(vc.shape, F32), jax.ShapeDtypeStruct(sink.shape, F32)],
        compiler_params=pltpu.CompilerParams(dimension_semantics=("parallel", "arbitrary")),
    )(q, kp, vp, kc, vc, sink, o, do)


@functools.partial(jax.custom_vjp, nondiff_argnums=(6,))
def swa_attn(q, kp, vp, kc, vc, sink, scale):
    return _swa_fwd_call(*(t.astype(BF16) for t in (q, kp, vp, kc, vc)), sink, scale)


def _swa_attn_fwd(q, kp, vp, kc, vc, sink, scale):
    q, kp, vp, kc, vc = (t.astype(BF16) for t in (q, kp, vp, kc, vc))
    o = _swa_fwd_call(q, kp, vp, kc, vc, sink, scale)
    return o, (q, kp, vp, kc, vc, sink, o)


def _swa_attn_bwd(scale, res, do):
    q, kp, vp, kc, vc, sink, o = res
    dq, dkp, dvp, dkc, dvc, dsink = _swa_bwd_call(q, kp, vp, kc, vc, sink, o, do, scale)
    return dq, dkp, dvp, dkc, dvc, _lane0(dsink)


swa_attn.defvjp(_swa_attn_fwd, _swa_attn_bwd)


def _f32dot(a, b, dims):
    return lax.dot_general(a, b, dims, precision=lax.Precision.HIGHEST, preferred_element_type=F32)


DN_SOLVE_BLOCK = 16


def _unit_lower_inverse(a, a_t, transposed):
    g, c, _ = a.shape
    nb = DN_SOLVE_BLOCK
    row = lax.broadcasted_iota(jnp.int32, (1, c, c), 1)
    col = lax.broadcasted_iota(jnp.int32, (1, c, c), 2)
    src, off = (a, a_t) if transposed else (a_t, a)
    coef = jnp.zeros((g, c, nb), F32)
    for b in range(c // nb):
        in_block = (lax.broadcasted_iota(jnp.int32, (1, c, nb), 1) // nb) == b
        coef = coef + jnp.where(in_block, src[:, :, b * nb:(b + 1) * nb], 0.0)
    sub = lax.broadcasted_iota(jnp.int32, (1, c // nb, nb, c), 2)
    x = jnp.broadcast_to((row == col).astype(F32), a.shape)
    for i in (range(nb - 2, -1, -1) if transposed else range(1, nb)):
        prod = (coef[:, :, i:i + 1] * x).reshape(g, c // nb, nb, c)
        new_rows = -jnp.sum(prod, axis=2, keepdims=True)
        x = x + jnp.where(sub == i, new_rows, 0.0).reshape(g, c, c)
    width = nb
    while width < c:
        joins = ((row // (2 * width)) == (col // (2 * width))) & ((row // width) != (col // width))
        x = x - _f32dot(x, _f32dot(jnp.where(joins, off, 0.0), x, BNN), BNN)
        width *= 2
    return x


def _dn_masks(c):
    row = lax.broadcasted_iota(jnp.int32, (1, c, c), 1)
    col = lax.broadcasted_iota(jnp.int32, (1, c, c), 2)
    return row, col


def _sweep_chunks(n, n_ctx):
    return (lambda i: i), (lambda i: jnp.where(i < n_ctx, n_ctx - 1 - i, n + n_ctx - 1 - i))


def _half_spec(gh, tail, half, chunk_of):
    return pl.BlockSpec((gh, 1) + tail, lambda i: (half, chunk_of(i), 0, 0))


def _both(ref_f, ref_b):
    return jnp.concatenate([ref_f[:, 0], ref_b[:, 0]], axis=0)


def _dn_direction_masks(g, c):
    backward = lax.broadcasted_iota(jnp.int32, (g, 1, 1), 0) >= g // 2
    row, col = _dn_masks(c)
    return backward, jnp.where(backward, c - 1 - row, row), jnp.where(backward, c - 1 - col, col)


def _dn_inverse(a_mat, a_t, transposed):
    h = a_mat.shape[0] // 2
    return jnp.concatenate([_unit_lower_inverse(a_mat[:h], a_t[:h], transposed),
                            _unit_lower_inverse(a_t[h:], a_mat[h:], not transposed)], axis=0)


def _dn_fwd_call(q, k, k_t, v, gc, bb, gr, n_ctx):
    gh, n, c, _ = q.shape
    g = 2 * gh

    def body(qf, qb, kf, kb_, ktf, ktb, vf, vb, gcf, gcb, bbf, bbb, grf, grb,
             of_ref, ob_ref, vn_ref, sall_ref, w_ref, u_ref, t_ref, s_scr):
        i = pl.program_id(0)

        @pl.when(i == 0)
        def _():
            s_scr[...] = jnp.zeros_like(s_scr)

        qv, kv, ktv, vv, gcv, bv, grv = (_both(qf, qb), _both(kf, kb_), _both(ktf, ktb), _both(vf, vb), _both(gcf, gcb),
                                          _both(bbf, bbb), _both(grf, grb))
        backward, row, col = _dn_direction_masks(g, c)
        e = jnp.exp(gcv)
        kb = kv * bv
        decay = jnp.exp(jnp.where(row >= col, gcv - grv, -jnp.inf))
        decay_ts = jnp.exp(jnp.where(row < col, grv - gcv, -jnp.inf))
        a_mat = _bdot(kb, kv, BNT) * jnp.where(row > col, decay, 0.0)
        t = _dn_inverse(a_mat, _bdot(kv, kb, BNT) * decay_ts, False)
        w = _f32dot(t, kb * e, BNN)
        u = _f32dot(t, vv * bv, BNN)
        glast = jnp.where(backward, grv[:, :, 0:1], grv[:, :, c - 1:c])
        s = s_scr[...]
        sall_ref[:, 0] = s
        vnew = u - _bdot(w, s, BNN)
        o = _bdot(qv * e, s, BNN) + _bdot(_bdot(qv, kv, BNT) * decay, vnew, BNN)
        of_ref[:, 0] = o[:gh]
        ob_ref[:, 0] = o[gh:]
        vn_ref[:, 0] = vnew
        w_ref[:, 0] = w
        u_ref[:, 0] = u
        t_ref[:, 0] = t
        s_scr[...] = s * jnp.exp(glast) + _bdot(ktv * jnp.exp(glast - grv), vnew, BNN)

    cf, cb = _sweep_chunks(n, n_ctx)
    tok = lambda half, chunk_of: _half_spec(gh, (c, c), half, chunk_of)
    rowv = lambda half, chunk_of: _half_spec(gh, (1, c), half, chunk_of)
    step = pl.BlockSpec((g, 1, c, c), lambda i: (0, i, 0, 0))
    shared = [tok(0, cf), tok(0, cb)]
    split = [tok(0, cf), tok(1, cb)]
    return _pcall(
        body, name="dn_fwd", grid=(n,),
        in_specs=shared * 4 + split * 2 + [rowv(0, cf), rowv(1, cb)],
        out_specs=[tok(0, cf), tok(0, cb)] + [step] * 5,
        out_shape=[jax.ShapeDtypeStruct((gh, n, c, c), F32)] * 2 + [jax.ShapeDtypeStruct((g, n, c, c), F32)] * 5,
        scratch_shapes=[pltpu.VMEM((g, c, c), F32)],
        compiler_params=pltpu.CompilerParams(dimension_semantics=("arbitrary",)),
    )(q, q, k, k, k_t, k_t, v, v, gc, gc, bb, bb, gr, gr)


def _dn_bwd_call(q, k, q_t, k_t, v, gc, bb, gr, br, sall, vn, w, u, t_t, do_f, do_b, n_ctx):
    gh, n, c, _ = q.shape
    g = 2 * gh

    def body(qf, qb, kf, kb_, qtf, qtb, ktf, ktb, vf, vb, gcf, gcb, bbf, bbb, grf, grb, brf, brb,
             sall_ref, vn_ref, w_ref, u_ref, tt_ref, dof, dob,
             dqf, dqb, dkf, dkb_, dvf, dvb, dgcf, dgcb, dbbf, dbbb, dgrf, dgrb, ds_scr):
        i = pl.program_id(0)

        @pl.when(i == 0)
        def _():
            ds_scr[...] = jnp.zeros_like(ds_scr)

        qv, kv, qtv, ktv, vv = _both(qf, qb), _both(kf, kb_), _both(qtf, qtb), _both(ktf, ktb), _both(vf, vb)
        gcv, bv, grv, brv, dov = _both(gcf, gcb), _both(bbf, bbb), _both(grf, grb), _both(brf, brb), _both(dof, dob)
        s, vnew, w, u = sall_ref[:, 0], vn_ref[:, 0], w_ref[:, 0], u_ref[:, 0]
        dsn = ds_scr[...]
        backward, row, col = _dn_direction_masks(g, c)
        e = jnp.exp(gcv)
        er = jnp.exp(grv)
        kb = kv * bv
        decay = jnp.exp(jnp.where(row >= col, gcv - grv, -jnp.inf))
        decay_s = jnp.where(row > col, decay, 0.0)
        decay_t = jnp.exp(jnp.where(row <= col, grv - gcv, -jnp.inf))
        decay_ts = jnp.where(row < col, decay_t, 0.0)
        kk = _bdot(kb, kv, BNT)
        tt = tt_ref[:, 0]
        glast = jnp.where(backward, grv[:, :, 0:1], grv[:, :, c - 1:c])
        eg = jnp.exp(glast)
        x = jnp.exp(glast - gcv)
        kt = kv * x
        qk_raw = _bdot(qv, kv, BNT)
        w_t = _f32dot(ktv * (brv * er), tt, BNN)
        dvn = _bdot(_bdot(kv, qv, BNT) * decay_t, dov, BNN) + _bdot(kt, dsn, BNN)
        dqk = _bdot(dov, vnew, BNT)
        dqk_t = _bdot(vnew, dov, BNT)
        dqd = _bdot(dov, s, BNT)
        dkt = _bdot(vnew, dsn, BNT)
        deg = jnp.sum(jnp.sum(dsn * s, axis=2, keepdims=True), axis=1, keepdims=True)
        dw = -_bdot(dvn, s, BNT)
        ds_scr[...] = dsn * eg + _bdot(qtv * er, dov, BNN) - _bdot(w_t, dvn, BNN)
        dwp = _f32dot(tt, dw, BNN)
        dup = _f32dot(tt, dvn, BNN)
        d_a = -(_bdot(dwp, w, BNT) + _bdot(dup, u, BNT))
        d_at = -(_bdot(w, dwp, BNT) + _bdot(u, dup, BNT))
        dkb = _bdot(d_a * decay_s, kv, BNN) + dwp * e
        dkx = dkt * kv * x
        ddiff = dqk * qk_raw * decay + d_a * kk * decay_s
        dglast = jnp.sum(jnp.sum(dkx, axis=2, keepdims=True), axis=1, keepdims=True) + deg * eg
        lane = lax.broadcasted_iota(jnp.int32, (1, 1, c), 2)
        last_lane = jnp.where(backward, 0, c - 1)
        results = (
            (dqf, dqb, dqd * e + _bdot(dqk * decay, kv, BNN)),
            (dkf, dkb_, _bdot(d_at * decay_ts, kb, BNN) + dkb * bv + dkt * x + _bdot(dqk_t * decay_t, qv, BNN)),
            (dvf, dvb, dup * bv),
            (dgcf, dgcb, ddiff + (dwp * kb + dqd * qv) * e - dkx),
            (dbbf, dbbb, dkb * kv + dup * vv),
            (dgrf, dgrb, jnp.where(lane == last_lane, dglast, 0.0) - jnp.sum(ddiff, axis=1, keepdims=True)),
        )
        for ref_f, ref_b, val in results:
            ref_f[:, 0] = val[:gh]
            ref_b[:, 0] = val[gh:]

    cf, cb = _sweep_chunks(n, n_ctx)
    rf, rb = (lambda i: cf(n - 1 - i)), (lambda i: cb(n - 1 - i))
    tok = lambda half, chunk_of: _half_spec(gh, (c, c), half, chunk_of)
    rowv = lambda half, chunk_of: _half_spec(gh, (1, c), half, chunk_of)
    step = pl.BlockSpec((g, 1, c, c), lambda i: (0, n - 1 - i, 0, 0))
    shared = [tok(0, rf), tok(0, rb)]
    split = [tok(0, rf), tok(1, rb)]
    split_row = [rowv(0, rf), rowv(1, rb)]
    big = jax.ShapeDtypeStruct((gh, n, c, c), F32)
    return _pcall(
        body, name="dn_bwd", grid=(n,),
        in_specs=shared * 5 + split * 2 + split_row * 2 + [step] * 5 + shared,
        out_specs=shared * 5 + [rowv(0, rf), rowv(0, rb)],
        out_shape=[big] * 10 + [jax.ShapeDtypeStruct((gh, n, 1, c), F32)] * 2,
        scratch_shapes=[pltpu.VMEM((g, c, c), F32)],
        compiler_params=pltpu.CompilerParams(dimension_semantics=("arbitrary",)),
    )(q, q, k, k, q_t, q_t, k_t, k_t, v, v, gc, gc, bb, bb, gr, gr, br, br, sall, vn, w, u, t_t, do_f, do_b)


_t = lambda a: jnp.swapaxes(a, -1, -2)


def _dn_forms(gcum, beta, d):
    lanes = lambda t: jnp.broadcast_to(t[..., None], t.shape + (d,))
    return lanes(gcum), lanes(beta), gcum[:, :, None, :], beta[:, :, None, :]


@functools.partial(jax.custom_vjp, nondiff_argnums=(5,))
def dn_chunked(q, k, v, gcum, beta, n_ctx):
    gc, bb, gr, _ = _dn_forms(gcum, beta, q.shape[-1])
    return tuple(_dn_fwd_call(q, k, _t(k), v, gc, bb, gr, n_ctx)[:2])


def _dn_chunked_fwd(q, k, v, gcum, beta, n_ctx):
    gc, bb, gr, _ = _dn_forms(gcum, beta, q.shape[-1])
    o_f, o_b, vn, sall, w, u, t = _dn_fwd_call(q, k, _t(k), v, gc, bb, gr, n_ctx)
    return (o_f, o_b), (q, k, v, gcum, beta, vn, sall, w, u, t)


def _dn_chunked_bwd(n_ctx, res, cts):
    q, k, v, gcum, beta, vn, sall, w, u, t = res
    gc, bb, gr, br = _dn_forms(gcum, beta, q.shape[-1])
    (dq_f, dq_b, dk_f, dk_b, dv_f, dv_b, dgc_f, dgc_b, dbb_f, dbb_b, dgr_f, dgr_b) = _dn_bwd_call(
        q, k, _t(q), _t(k), v, gc, bb, gr, br, sall, vn, w, u, _t(t), cts[0], cts[1], n_ctx)
    dgcum = jnp.concatenate([jnp.sum(dgc_f, axis=-1) + dgr_f[:, :, 0, :], jnp.sum(dgc_b, axis=-1) + dgr_b[:, :, 0, :]], axis=0)
    dbeta = jnp.concatenate([jnp.sum(dbb_f, axis=-1), jnp.sum(dbb_b, axis=-1)], axis=0)
    return dq_f + dq_b, dk_f + dk_b, dv_f + dv_b, dgcum, dbeta


dn_chunked.defvjp(_dn_chunked_fwd, _dn_chunked_bwd)


def _ret_fwd_call(q, k, k_t, v, dmat, xi_b, zeta_r, gm, n_ctx):
    gh, n, c, dk = q.shape
    dv = v.shape[-1]
    g = 2 * gh

    def body(qf, qb, kf, kb_, ktf, ktb, vf, vb, d_ref, xib_ref, zr_ref, gm_ref, of_ref, ob_ref, starts_ref, s_scr):
        i = pl.program_id(0)

        @pl.when(i == 0)
        def _():
            s_scr[...] = jnp.zeros_like(s_scr)

        qv, kv, ktv, vv = _both(qf, qb), _both(kf, kb_), _both(ktf, ktb), _both(vf, vb)
        s = s_scr[...]
        starts_ref[:, 0] = s
        o = _bdot(_bdot(qv, kv, BNT) * d_ref[...], vv, BNN) + _bdot(qv * xib_ref[...], s, BNN)
        of_ref[:, 0] = o[:gh]
        ob_ref[:, 0] = o[gh:]
        s_scr[...] = s * gm_ref[...] + _bdot(ktv * zr_ref[...], vv, BNN)

    cf, cb = _sweep_chunks(n, n_ctx)
    pair = lambda tail: [_half_spec(gh, tail, 0, cf), _half_spec(gh, tail, 0, cb)]
    const = lambda a, b: pl.BlockSpec((g, a, b), lambda i: (0, 0, 0))
    return _pcall(
        body, name="ret_fwd", grid=(n,),
        in_specs=pair((c, dk)) * 2 + pair((dk, c)) + pair((c, dv)) + [const(c, c), const(c, dk), const(1, c), const(dk, dv)],
        out_specs=pair((c, dv)) + [pl.BlockSpec((g, 1, dk, dv), lambda i: (0, i, 0, 0))],
        out_shape=[jax.ShapeDtypeStruct((gh, n, c, dv), F32)] * 2 + [jax.ShapeDtypeStruct((g, n, dk, dv), F32)],
        scratch_shapes=[pltpu.VMEM((g, dk, dv), F32)],
        compiler_params=pltpu.CompilerParams(dimension_semantics=("arbitrary",)),
    )(q, q, k, k, k_t, k_t, v, v, dmat, xi_b, zeta_r, gm)


def _ret_bwd_call(q, k, q_t, k_t, v, dmat, dmat_t, xi_b, xi_r, zeta_b, gm, starts, do_f, do_b, n_ctx):
    gh, n, c, dk = q.shape
    dv = v.shape[-1]
    g = 2 * gh

    def body(qf, qb, kf, kb_, qtf, qtb, ktf, ktb, vf, vb, d_ref, dt_ref, xib_ref, xr_ref, zb_ref, gm_ref, starts_ref,
             dof, dob, dqf, dqb, dkf, dkb_, dvf, dvb, dd_ref, dxib_ref, dzb_ref, dgm_ref, ds_scr):
        i = pl.program_id(0)

        @pl.when(i == 0)
        def _():
            ds_scr[...] = jnp.zeros_like(ds_scr)
            dd_ref[...] = jnp.zeros_like(dd_ref)
            dxib_ref[...] = jnp.zeros_like(dxib_ref)
            dzb_ref[...] = jnp.zeros_like(dzb_ref)
            dgm_ref[...] = jnp.zeros_like(dgm_ref)

        qv, kv, qtv, vv, dov = _both(qf, qb), _both(kf, kb_), _both(qtf, qtb), _both(vf, vb), _both(dof, dob)
        s, dsn = starts_ref[:, 0], ds_scr[...]
        dm, dmt, zb = d_ref[...], dt_ref[...], zb_ref[...]
        qk_raw = _bdot(qv, kv, BNT)
        dqkd = _bdot(dov, vv, BNT)
        do_s = _bdot(dov, s, BNT)
        dkz = _bdot(vv, dsn, BNT)
        results = ((dqf, dqb, _bdot(dqkd * dm, kv, BNN) + do_s * xib_ref[...]),
                   (dkf, dkb_, _bdot(_bdot(vv, dov, BNT) * dmt, qv, BNN) + dkz * zb),
                   (dvf, dvb, _bdot(_bdot(kv, qv, BNT) * dmt, dov, BNN) + _bdot(kv * zb, dsn, BNN)))
        for ref_f, ref_b, val in results:
            ref_f[:, 0] = val[:gh]
            ref_b[:, 0] = val[gh:]
        dd_ref[...] += dqkd * qk_raw
        dxib_ref[...] += do_s * qv
        dzb_ref[...] += dkz * kv
        dgm_ref[...] += dsn * s
        ds_scr[...] = dsn * gm_ref[...] + _bdot(qtv * xr_ref[...], dov, BNN)

    cf, cb = _sweep_chunks(n, n_ctx)
    rf, rb = (lambda i: cf(n - 1 - i)), (lambda i: cb(n - 1 - i))
    pair = lambda tail: [_half_spec(gh, tail, 0, rf), _half_spec(gh, tail, 0, rb)]
    const = lambda a, b: pl.BlockSpec((g, a, b), lambda i: (0, 0, 0))
    sds = lambda *s: jax.ShapeDtypeStruct(s, F32)
    return _pcall(
        body, name="ret_bwd", grid=(n,),
        in_specs=pair((c, dk)) * 2 + pair((dk, c)) * 2 + pair((c, dv))
        + [const(c, c), const(c, c), const(c, dk), const(1, c), const(c, dk), const(dk, dv),
           pl.BlockSpec((g, 1, dk, dv), lambda i: (0, n - 1 - i, 0, 0))] + pair((c, dv)),
        out_specs=pair((c, dk)) * 2 + pair((c, dv)) + [const(c, c), const(c, dk), const(c, dk), const(dk, dv)],
        out_shape=[sds(gh, n, c, dk)] * 4 + [sds(gh, n, c, dv)] * 2 + [sds(g, c, c), sds(g, c, dk), sds(g, c, dk), sds(g, dk, dv)],
        scratch_shapes=[pltpu.VMEM((g, dk, dv), F32)],
        compiler_params=pltpu.CompilerParams(dimension_semantics=("arbitrary",)),
    )(q, q, k, k, q_t, q_t, k_t, k_t, v, v, dmat, dmat_t, xi_b, xi_r, zeta_b, gm, starts, do_f, do_b)


def _ret_forms(xi, zeta, gm, dk, dv):
    lanes = lambda t: jnp.broadcast_to(t[..., None], t.shape + (dk,))
    return lanes(xi), xi[:, None, :], lanes(zeta), zeta[:, None, :], jnp.broadcast_to(gm[:, None, None], gm.shape + (dk, dv))


@functools.partial(jax.custom_vjp, nondiff_argnums=(7,))
def ret_chunked(q, k, v, dmat, xi, zeta, gm, n_ctx):
    xi_b, _, _, zeta_r, gm_f = _ret_forms(xi, zeta, gm, q.shape[-1], v.shape[-1])
    return tuple(_ret_fwd_call(q, k, _t(k), v, dmat, xi_b, zeta_r, gm_f, n_ctx)[:2])


def _ret_chunked_fwd(q, k, v, dmat, xi, zeta, gm, n_ctx):
    xi_b, _, _, zeta_r, gm_f = _ret_forms(xi, zeta, gm, q.shape[-1], v.shape[-1])
    o_f, o_b, starts = _ret_fwd_call(q, k, _t(k), v, dmat, xi_b, zeta_r, gm_f, n_ctx)
    return (o_f, o_b), (q, k, v, dmat, xi, zeta, gm, starts)


def _ret_chunked_bwd(n_ctx, res, cts):
    q, k, v, dmat, xi, zeta, gm, starts = res
    xi_b, xi_r, zeta_b, _, gm_f = _ret_forms(xi, zeta, gm, q.shape[-1], v.shape[-1])
    dq_f, dq_b, dk_f, dk_b, dv_f, dv_b, dd, dxib, dzb, dgm = _ret_bwd_call(
        q, k, _t(q), _t(k), v, dmat, _t(dmat), xi_b, xi_r, zeta_b, gm_f, starts, cts[0], cts[1], n_ctx)
    return (dq_f + dq_b, dk_f + dk_b, dv_f + dv_b, dd, jnp.sum(dxib, axis=-1), jnp.sum(dzb, axis=-1),
            jnp.sum(dgm, axis=(1, 2)))


ret_chunked.defvjp(_ret_chunked_fwd, _ret_chunked_bwd)


def _peer(k):
    mx, my, mc = lax.axis_index("x"), lax.axis_index("y"), lax.axis_index("c")
    px = 1 - mx if k & 4 else mx
    py = 1 - my if k & 2 else my
    pc = 1 - mc if k & 1 else mc
    return (px, py, pc), 4 * px + 2 * py + pc


N_CHIP = N_DEV // 2


def _transfer_call(xs, name, plan, n_transfers, n_out):
    n_arr = len(xs)

    def body(*refs):
        x_refs, out_refs = refs[:n_arr], refs[n_arr:2 * n_arr]
        send_sems, recv_sems, local_sems = refs[2 * n_arr:]
        transfers, local = plan()
        n_tr = n_transfers
        assert len(transfers) == n_tr

        def copy(j, s, dst_slot):
            flip, src_slot, _, _ = transfers[s]
            return pltpu.make_async_remote_copy(
                src_ref=x_refs[j].at[src_slot], dst_ref=out_refs[j].at[dst_slot],
                send_sem=send_sems.at[j * n_tr + s], recv_sem=recv_sems.at[j * n_tr + s],
                device_id=_peer(flip)[0], device_id_type=pl.DeviceIdType.MESH)

        mine = []
        if local is not None:
            mine = [pltpu.make_async_copy(x_refs[j].at[local[0]], out_refs[j].at[local[1]], local_sems.at[j])
                    for j in range(n_arr)]
        for cp in mine:
            cp.start()
        sends = [copy(j, s, transfers[s][2]) for j in range(n_arr) for s in range(n_tr)]
        for cp in sends:
            cp.start()
        for j in range(n_arr):
            for s in range(n_tr):
                copy(j, s, transfers[s][3]).wait_recv()
        for cp in sends:
            cp.wait_send()
        for cp in mine:
            cp.wait()

    n_sem = n_arr * n_transfers
    return _pcall(
        body, name=name,
        in_specs=[pl.BlockSpec(memory_space=pl.ANY)] * n_arr, out_specs=[pl.BlockSpec(memory_space=pl.ANY)] * n_arr,
        out_shape=[jax.ShapeDtypeStruct((n_out,) + tuple(x.shape[1:]), x.dtype) for x in xs],
        scratch_shapes=[pltpu.SemaphoreType.DMA((n_sem,)), pltpu.SemaphoreType.DMA((n_sem,)),
                        pltpu.SemaphoreType.DMA((n_arr,))],
    )(*xs)


def _sibling_plan():
    mc = lax.axis_index("c")
    return [(1, 2 * t + (1 - mc), t, t) for t in range(N_CHIP)], None


def _chip_plan():
    my_chip = 2 * lax.axis_index("x") + lax.axis_index("y")
    transfers = []
    for flip in (2, 4, 6):
        peer_chip = _peer(flip)[1] // 2
        transfers.append((flip, peer_chip, my_chip, peer_chip))
    return transfers, (my_chip, my_chip)


def _gather_call(xs, name):
    n_arr = len(xs)
    per = N_DEV - 1
    chips = (2, 4, 6)

    def body(*refs):
        x_refs, out_refs = refs[:n_arr], refs[n_arr:2 * n_arr]
        send_sems, recv_sems, local_sems = refs[2 * n_arr:]
        me = 4 * lax.axis_index("x") + 2 * lax.axis_index("y") + lax.axis_index("c")
        sib_dev, sib_idx = _peer(1)

        def copy(j, s, src, slot, dev):
            return pltpu.make_async_remote_copy(
                src_ref=src, dst_ref=out_refs[j].at[slot],
                send_sem=send_sems.at[j * per + s], recv_sem=recv_sems.at[j * per + s],
                device_id=dev, device_id_type=pl.DeviceIdType.MESH)

        mine = [pltpu.make_async_copy(x_refs[j], out_refs[j].at[me], local_sems.at[j]) for j in range(n_arr)]
        for cp in mine:
            cp.start()
        sends = []
        for j in range(n_arr):
            sends.append(copy(j, 0, x_refs[j], me, sib_dev))
            for t, k in enumerate(chips):
                sends.append(copy(j, 1 + t, x_refs[j], me, _peer(k)[0]))
        for cp in sends:
            cp.start()
        for j in range(n_arr):
            for t, k in enumerate(chips):
                dev, idx = _peer(k)
                copy(j, 1 + t, x_refs[j], idx, dev).wait_recv()
                forward = copy(j, 4 + t, out_refs[j].at[idx], idx, sib_dev)
                forward.start()
                sends.append(forward)
        for j in range(n_arr):
            copy(j, 0, x_refs[j], sib_idx, sib_dev).wait_recv()
            for t, k in enumerate(chips):
                idx = _peer(k + 1)[1]
                copy(j, 4 + t, out_refs[j].at[idx], idx, sib_dev).wait_recv()
        for cp in sends:
            cp.wait_send()
        for cp in mine:
            cp.wait()

    return _pcall(
        body, name=name,
        in_specs=[pl.BlockSpec(memory_space=pl.ANY)] * n_arr, out_specs=[pl.BlockSpec(memory_space=pl.ANY)] * n_arr,
        out_shape=[jax.ShapeDtypeStruct((N_DEV,) + tuple(x.shape), x.dtype) for x in xs],
        scratch_shapes=[pltpu.SemaphoreType.DMA((n_arr * per,)), pltpu.SemaphoreType.DMA((n_arr * per,)),
                        pltpu.SemaphoreType.DMA((n_arr,))],
    )(*xs)


def _sum_slots_call(x, name):
    n_slots, r, c = x.shape
    tr = _pick(r, (256, 160, 128, 72, 64, 32, 16, 8))

    def body(x_ref, o_ref):
        acc = x_ref[0].astype(F32)
        for d in range(1, n_slots):
            acc = acc + x_ref[d].astype(F32)
        o_ref[...] = acc

    return _pcall(
        body, name=name, grid=(r // tr,),
        in_specs=[pl.BlockSpec((n_slots, tr, c), lambda i: (0, i, 0))],
        out_specs=pl.BlockSpec((tr, c), lambda i: (i, 0)),
        out_shape=jax.ShapeDtypeStruct((r, c), F32),
        compiler_params=pltpu.CompilerParams(dimension_semantics=("parallel",)),
    )(x)


def _pair_add_call(a, b, name):
    n_slots, r, c = a.shape
    tr = _pick(r, (256, 160, 128, 64, 32, 16))

    def body(a_ref, b_ref, o_ref):
        o_ref[...] = (a_ref[...].astype(F32) + b_ref[...].astype(F32)).astype(BF16)

    spec = pl.BlockSpec((n_slots, tr, c), lambda i: (0, i, 0))
    return _pcall(
        body, name=name, grid=(r // tr,), in_specs=[spec, spec], out_specs=spec,
        out_shape=jax.ShapeDtypeStruct(a.shape, BF16),
        compiler_params=pltpu.CompilerParams(dimension_semantics=("parallel",)),
    )(a, b)


def _adamw_call(w, g, m, v, name):
    r, c = w.shape
    tr = _pick(r, (256, 128, 64, 32, 16, 8))
    bc1 = 1.0 - ADAM_B1 ** ADAM_STEP
    bc2 = 1.0 - ADAM_B2 ** ADAM_STEP

    def body(w_ref, g_ref, m_ref, v_ref, d_ref, nm_ref, nv_ref):
        gv = g_ref[...]
        nm = ADAM_B1 * m_ref[...] + (1.0 - ADAM_B1) * gv
        nv = ADAM_B2 * v_ref[...] + (1.0 - ADAM_B2) * jnp.square(gv)
        d_ref[...] = -ADAM_LR * ((nm / bc1) / (jnp.sqrt(nv / bc2) + ADAM_EPS) + ADAM_WD * w_ref[...])
        nm_ref[...] = nm
        nv_ref[...] = nv

    spec = pl.BlockSpec((tr, c), lambda i: (i, 0))
    sds = jax.ShapeDtypeStruct((r, c), F32)
    return _pcall(
        body, name=name, grid=(r // tr,), in_specs=[spec] * 4, out_specs=[spec] * 3, out_shape=[sds] * 3,
        compiler_params=pltpu.CompilerParams(dimension_semantics=("parallel",)),
    )(w, g, m, v)


def rms_norm(x, g):
    return x * lax.rsqrt(jnp.mean(x * x, axis=-1, keepdims=True) + NORM_EPS) * g


def head_layer_norm(o, g):
    b_, t_, h_, d_ = o.shape
    mu = jnp.mean(o, axis=-1, keepdims=True)
    var = jnp.mean(jnp.square(o - mu), axis=-1, keepdims=True)
    return ((o - mu) * lax.rsqrt(var + NORM_EPS)).reshape(b_, t_, h_ * d_) * g


def l2norm(t):
    return t * lax.rsqrt(jnp.sum(t * t, axis=-1, keepdims=True) + NORM_EPS)


def rope_freqs(dim):
    return ROPE_BASE ** (-jnp.arange(0, dim, 2, dtype=F32) / dim)


def axial_rope(rows, rot_dim):
    row = jnp.broadcast_to(jnp.arange(rows, dtype=F32)[:, None], (rows, GRID_W)).reshape(-1)
    col = jnp.broadcast_to(jnp.arange(GRID_W, dtype=F32)[None, :], (rows, GRID_W)).reshape(-1)
    inv = rope_freqs(rot_dim // 2)
    ang = jnp.concatenate([row[:, None] * inv, col[:, None] * inv], axis=-1)
    return jnp.cos(ang), jnp.sin(ang)


def sequence_rope(n_tok, rot_dim):
    ang = jnp.arange(n_tok, dtype=F32)[:, None] * rope_freqs(rot_dim)
    return jnp.cos(ang), jnp.sin(ang)


def apply_rope(x, cos, sin):
    x1, x2 = jnp.split(x, 2, axis=-1)
    c = cos[:, None, :]
    s = sin[:, None, :]
    return jnp.concatenate([x1 * c - x2 * s, x1 * s + x2 * c], axis=-1)


def _split_columns(z):
    idx = np.cumsum(np.array(IN_SPLITS))[:-1].tolist()
    return jnp.split(z, idx, axis=-1)


CONV_ROW_TILE = 512
CONV_HALO = 8


def _conv_windows(prev_ref, x_ref, next_ref, n_tiles, taps):
    i = pl.program_id(1)
    tr = x_ref.shape[1]
    prev = jnp.where(i == 0, 0.0, prev_ref[0])
    nxt = jnp.where(i == n_tiles - 1, 0.0, next_ref[0])
    xx = jnp.concatenate([prev, x_ref[0], nxt], axis=0)
    rows = tr + 2 * CONV_HALO
    pad = taps // 2
    return [pltpu.roll(xx, (pad - k) % rows, 0)[CONV_HALO:CONV_HALO + tr] for k in range(taps)]


def _conv_specs(t_len, tr, ch):
    per = tr // CONV_HALO
    last = t_len // CONV_HALO - 1
    return [pl.BlockSpec((1, CONV_HALO, ch), lambda b, i: (b, jnp.maximum(i * per - 1, 0), 0)),
            pl.BlockSpec((1, tr, ch), lambda b, i: (b, i, 0)),
            pl.BlockSpec((1, CONV_HALO, ch), lambda b, i: (b, jnp.minimum((i + 1) * per, last), 0))]


def _conv_fwd_call(x, w, taps):
    b_, t_len, ch = x.shape
    tr = min(CONV_ROW_TILE, t_len)
    n_tiles = t_len // tr

    def body(prev_ref, x_ref, next_ref, w_ref, o_ref):
        wins = _conv_windows(prev_ref, x_ref, next_ref, n_tiles, taps)
        acc = wins[0] * w_ref[0:1, :]
        for k in range(1, taps):
            acc = acc + wins[k] * w_ref[k:k + 1, :]
        o_ref[0] = acc

    return _pcall(
        body, name="conv_fwd", grid=(b_, n_tiles),
        in_specs=_conv_specs(t_len, tr, ch) + [pl.BlockSpec((8, ch), lambda b, i: (0, 0))],
        out_specs=pl.BlockSpec((1, tr, ch), lambda b, i: (b, i, 0)),
        out_shape=jax.ShapeDtypeStruct(x.shape, F32),
        compiler_params=pltpu.CompilerParams(dimension_semantics=("parallel", "parallel")),
    )(x, x, x, w)


def _conv_dw_call(x, du, taps):
    b_, t_len, ch = x.shape
    tr = min(CONV_ROW_TILE, t_len)
    n_tiles = t_len // tr

    def body(prev_ref, x_ref, next_ref, du_ref, dw_ref):
        wins = _conv_windows(prev_ref, x_ref, next_ref, n_tiles, taps)
        duv = du_ref[0]
        rows = [jnp.sum(duv * wins[k], axis=0, keepdims=True) for k in range(taps)]
        part = jnp.concatenate(rows + [jnp.zeros((8 - taps, ch), F32)], axis=0)

        @pl.when((pl.program_id(0) == 0) & (pl.program_id(1) == 0))
        def _():
            dw_ref[...] = jnp.zeros_like(dw_ref)

        dw_ref[...] += part

    return _pcall(
        body, name="conv_dw", grid=(b_, n_tiles),
        in_specs=_conv_specs(t_len, tr, ch) + [pl.BlockSpec((1, tr, ch), lambda b, i: (b, i, 0))],
        out_specs=pl.BlockSpec((8, ch), lambda b, i: (0, 0)),
        out_shape=jax.ShapeDtypeStruct((8, ch), F32),
        compiler_params=pltpu.CompilerParams(dimension_semantics=("arbitrary", "arbitrary")),
    )(x, x, x, du)


def _pad_taps(w):
    return jnp.concatenate([w, jnp.zeros((8 - w.shape[0], w.shape[1]), F32)], axis=0)


@jax.custom_vjp
def short_conv(x, w):
    return _conv_fwd_call(x, _pad_taps(w), w.shape[0])


def _short_conv_fwd(x, w):
    return _conv_fwd_call(x, _pad_taps(w), w.shape[0]), (x, w)


def _short_conv_bwd(res, du):
    x, w = res
    taps = w.shape[0]
    return _conv_fwd_call(du, _pad_taps(jnp.flip(w, axis=0)), taps), _conv_dw_call(x, du, taps)[:taps]


short_conv.defvjp(_short_conv_fwd, _short_conv_bwd)


def _to_heads(t, h, d):
    b_, t_, _ = t.shape
    return t.reshape(b_, t_, h, d).transpose(0, 2, 1, 3).reshape(b_ * h, t_, d)


def _from_heads(t, b_):
    g, t_, d = t.shape
    return t.reshape(b_, g // b_, t_, d).transpose(0, 2, 1, 3).reshape(b_, t_, (g // b_) * d)


def _lane_scalar(vals):
    return jnp.broadcast_to(vals[:, None, None], (vals.shape[0], 1, 128))


def swa_group(q, k, v, qc, kc, vc, sink, cos, sin, with_ctx_out):
    b_, s_, _ = q.shape
    l_ = kc.shape[1]
    grp = SWA_HEADS // SWA_KV_HEADS
    d = SWA_HEAD_DIM
    w_ = SWA_BLOCK
    scale = d ** -0.5
    qh = apply_rope(q.reshape(b_, s_, SWA_HEADS, d), cos, sin).transpose(0, 2, 1, 3).reshape(b_ * SWA_HEADS, s_, d)
    kh = apply_rope(k.reshape(b_, s_, SWA_KV_HEADS, d), cos, sin).transpose(0, 2, 1, 3).reshape(b_ * SWA_KV_HEADS, s_, d)
    vh = _to_heads(v, SWA_KV_HEADS, d)
    kch = _to_heads(kc, SWA_KV_HEADS, d)
    vch = _to_heads(vc, SWA_KV_HEADS, d)
    padk = lambda t: jnp.pad(t, ((0, 0), (w_, w_), (0, 0)))
    sink_g = _lane_scalar(jnp.tile(sink, b_))
    y = _from_heads(swa_attn(qh, padk(kh), padk(vh), kch, vch, sink_g, scale), b_)
    yc = None
    if with_ctx_out:
        qch = _to_heads(qc, SWA_HEADS, d)
        rep = lambda t: jnp.repeat(t.reshape(b_, SWA_KV_HEADS, l_, d), grp, axis=1).reshape(b_ * SWA_HEADS, l_, d)
        yc = _from_heads(attn_full(qch, rep(kch), rep(vch), sink_g, scale, True), b_)
    return y, yc


def gated_delta_chunked(q, k, v, log_g, beta, n_ctx):
    g_, t_, dk = k.shape
    dv = v.shape[-1]
    c_ = DN_CHUNK
    n = t_ // c_
    assert dk == c_ and dv == c_
    lg = log_g.reshape(2, g_, n, c_)
    g_cum = jnp.concatenate([jnp.cumsum(lg[0], axis=-1), jnp.flip(jnp.cumsum(jnp.flip(lg[1], axis=-1), axis=-1), axis=-1)],
                            axis=0)
    o_f, o_b = dn_chunked(q.reshape(g_, n, c_, dk), k.reshape(g_, n, c_, dk), v.reshape(g_, n, c_, dv), g_cum,
                          beta.reshape(2 * g_, n, c_), n_ctx)
    return (o_f + o_b).reshape(g_, t_, dv)


def deltanet_group(qkv, z, ab, qkv_c, z_c, ab_c, conv_w, a_log, dt_bias, norm_g, with_ctx_out):
    def prep(qkv_, ab_):
        b_, t_, _ = qkv_.shape
        y = jax.nn.silu(short_conv(qkv_, conv_w))
        q, k, v = [t.reshape(b_, t_, DN_HEADS, DN_HEAD_DIM).transpose(0, 2, 1, 3) for t in jnp.split(y, 3, axis=-1)]
        q = l2norm(q) * DN_HEAD_DIM ** -0.5
        k = l2norm(k)
        ab_ = ab_.reshape(b_, t_, 2, 2, DN_HEADS)
        log_g = -jnp.exp(a_log) * jax.nn.softplus(ab_[:, :, :, 0] + dt_bias)
        beta = jax.nn.sigmoid(ab_[:, :, :, 1])
        return q, k, v, log_g.transpose(2, 0, 3, 1), beta.transpose(2, 0, 3, 1)

    def out(o, z_):
        b_, t_, _ = z_.shape
        o = rms_norm(o.transpose(0, 2, 1, 3), norm_g) * jax.nn.silu(z_).reshape(b_, t_, DN_HEADS, DN_HEAD_DIM)
        return o.reshape(b_, t_, DN_W)

    qc, kc, vc, lgc, bc = prep(qkv_c, ab_c)
    q, k, v, lg, bt = prep(qkv, ab)
    b_, l_, s_ = qkv.shape[0], qkv_c.shape[1], qkv.shape[1]
    seq = lambda tc, tl: jnp.concatenate([tc, tl], axis=2).reshape((b_ * DN_HEADS, l_ + s_) + tc.shape[3:])
    seq_g = lambda tc, tl: jnp.concatenate([tc, tl], axis=3).reshape(2, b_ * DN_HEADS, l_ + s_)
    o = gated_delta_chunked(seq(qc, q), seq(kc, k), seq(vc, v), seq_g(lgc, lg), seq_g(bc, bt), l_ // DN_CHUNK)
    o = o.reshape(b_, DN_HEADS, l_ + s_, DN_HEAD_DIM)
    y = out(o[:, :, l_:], z)
    yc = out(o[:, :, :l_], z_c) if with_ctx_out else None
    return y, yc


def retention_core(q, k, v, log_gamma, n_ctx):
    b_, h_, t_, dk = q.shape
    dv = v.shape[-1]
    c_ = RET_CHUNK
    n = t_ // c_
    gh = b_ * h_
    fwd_rank = jnp.arange(c_, dtype=F32)
    rank = jnp.stack([fwd_rank, c_ - 1 - fwd_rank])[:, None, :]
    lg = log_gamma[..., None]
    per_g = lambda t: jnp.broadcast_to(t[:, None], (2, b_) + t.shape[1:]).reshape((2 * gh,) + t.shape[2:])
    zeta = jnp.exp((c_ - 1 - rank) * lg)
    xi = jnp.exp((rank + 1.0) * lg)
    rel = rank[..., :, None] - rank[..., None, :]
    dmat = jnp.where(rel >= 0, jnp.exp(jnp.maximum(rel, 0.0) * lg[..., None]), 0.0)
    gm = jnp.exp(c_ * log_gamma)
    o_f, o_b = ret_chunked(q.reshape(gh, n, c_, dk), k.reshape(gh, n, c_, dk), v.reshape(gh, n, c_, dv),
                           per_g(dmat), per_g(xi), per_g(zeta), per_g(gm), n_ctx)
    return (o_f + o_b).reshape(b_, h_, t_, dv)


def retention_group(q, k, v, g, qc, kc, vc, gc, log1m_gamma, norm_g, cos, sin, with_ctx_out):
    log_gamma = jnp.log1p(-jnp.exp(log1m_gamma))
    heads = lambda t, dh: t.reshape(t.shape[0], t.shape[1], RET_HEADS, dh)
    bhtd = lambda t: t.transpose(0, 2, 1, 3)
    sc = RET_QK_DIM ** -0.5
    l_ = kc.shape[1]
    q = bhtd(apply_rope(heads(q, RET_QK_DIM), cos, sin)) * sc
    k = bhtd(apply_rope(heads(k, RET_QK_DIM), cos, sin))
    v = bhtd(heads(v, RET_V_DIM))
    kc = bhtd(heads(kc, RET_QK_DIM))
    vc = bhtd(heads(vc, RET_V_DIM))
    qcs = bhtd(heads(qc, RET_QK_DIM)) * sc

    def out(o, g_):
        return head_layer_norm(o.transpose(0, 2, 1, 3), norm_g) * jax.nn.silu(g_)

    seq = lambda tc, tl: jnp.concatenate([tc, tl], axis=2)
    o = retention_core(seq(qcs, q), seq(kc, k), seq(vc, v), log_gamma, l_ // RET_CHUNK)
    y = out(o[:, :, l_:], g)
    yc = out(o[:, :, :l_], gc) if with_ctx_out else None
    return y, yc


def mla_group(cq, ckv, kr, cq_c, ckv_c, kr_c, q_norm, w_uq, kv_norm, w_ukv, cos, sin, with_ctx_out):
    b_, s_, _ = cq.shape
    l_ = cq_c.shape[1]
    dqk = MLA_NOPE_DIM + MLA_ROPE_DIM
    rows = lambda tl, tc: jnp.concatenate([tl.reshape(b_ * s_, -1), tc.reshape(b_ * l_, -1)], axis=0)
    qa = matmul(rms_norm(rows(cq, cq_c), q_norm), w_uq)
    kva = matmul(rms_norm(rows(ckv, ckv_c), kv_norm), w_ukv)
    q = qa[:b_ * s_].reshape(b_, s_, MLA_HEADS, dqk)
    qc = qa[b_ * s_:].reshape(b_, l_, MLA_HEADS, dqk)
    q = jnp.concatenate([q[..., :MLA_NOPE_DIM], apply_rope(q[..., MLA_NOPE_DIM:], cos, sin)], axis=-1)
    kv = kva[:b_ * s_].reshape(b_, s_, MLA_HEADS, MLA_NOPE_DIM + MLA_V_DIM)
    kvc = kva[b_ * s_:].reshape(b_, l_, MLA_HEADS, MLA_NOPE_DIM + MLA_V_DIM)
    kr = apply_rope(kr[:, :, None, :], cos, sin)
    k = jnp.concatenate([kv[..., :MLA_NOPE_DIM], jnp.broadcast_to(kr, (b_, s_, MLA_HEADS, MLA_ROPE_DIM))], axis=-1)
    kc = jnp.concatenate([kvc[..., :MLA_NOPE_DIM],
                          jnp.broadcast_to(kr_c[:, :, None, :], (b_, l_, MLA_HEADS, MLA_ROPE_DIM))], axis=-1)
    v, vc = kv[..., MLA_NOPE_DIM:], kvc[..., MLA_NOPE_DIM:]
    hd = lambda t: t.transpose(0, 2, 1, 3).reshape(b_ * MLA_HEADS, t.shape[1], t.shape[3])
    scale = dqk ** -0.5
    no_sink = jnp.zeros((b_ * MLA_HEADS, 1, 128), F32)
    kch, vch = hd(kc), hd(vc)
    y = attn_full(hd(q), jnp.concatenate([hd(k), kch], axis=1), jnp.concatenate([hd(v), vch], axis=1), no_sink, scale, False)
    y = _from_heads(y, b_)
    yc = _from_heads(attn_full(hd(qc), kch, vch, no_sink, scale, False), b_) if with_ctx_out else None
    return y, yc


def token_mixers(zl, zc, p, layer, rope, with_ctx_out):
    (a_q, a_k, a_v, b_qkv, b_z, b_ab, c_q, c_k, c_v, c_g, d_cq, d_ckv, d_kr) = _split_columns(zl)
    (a_qc, a_kc, a_vc, b_qkvc, b_zc, b_abc, c_qc, c_kc, c_vc, c_gc, d_cqc, d_ckvc, d_krc) = _split_columns(zc)
    swa_cos, swa_sin, ret_cos, ret_sin, mla_cos, mla_sin = rope
    ya, yac = swa_group(a_q, a_k, a_v, a_qc, a_kc, a_vc, p['swa_sink'][layer], swa_cos, swa_sin, with_ctx_out)
    yb, ybc = deltanet_group(b_qkv, b_z, b_ab, b_qkvc, b_zc, b_abc, p['dn_conv_w'][layer], p['dn_a_log'][layer],
                             p['dn_dt_bias'][layer], p['dn_norm_g'][layer], with_ctx_out)
    yr, yrc = retention_group(c_q, c_k, c_v, c_g, c_qc, c_kc, c_vc, c_gc, p['ret_log1m_gamma'][layer],
                              p['ret_norm_g'][layer], ret_cos, ret_sin, with_ctx_out)
    yd, ydc = mla_group(d_cq, d_ckv, d_kr, d_cqc, d_ckvc, d_krc, p['mla_q_norm'][layer], p['mla_w_uq'][layer],
                        p['mla_kv_norm'][layer], p['mla_w_ukv'][layer], mla_cos, mla_sin, with_ctx_out)
    y = jnp.concatenate([ya, yb, yr, yd], axis=-1)
    yc = jnp.concatenate([yac, ybc, yrc, ydc], axis=-1) if with_ctx_out else None
    return y, yc


def local_loss(p, x, ctx, loss_target):
    b_, n_tok, d_ = x.shape
    l_ = ctx.shape[1]
    rows = n_tok // GRID_W
    rope = (*axial_rope(rows, SWA_HEAD_DIM), *sequence_rope(n_tok, RET_QK_DIM), *axial_rope(rows, MLA_ROPE_DIM))
    rl, rc = b_ * n_tok, b_ * l_
    mods = [jnp.concatenate([p['mod'][layer], p['cmod'][layer][None]], axis=0) for layer in range(DEPTH)]
    part = lambda layer, j: mods[layer][:, j * d_:(j + 1) * d_][:, None, :]
    vec = lambda name, layer: p[name][layer][None, :]
    xr = jnp.concatenate([x.reshape(rl, d_), ctx.reshape(rc, d_)], axis=0)
    sh1, sc1 = part(0, 0), part(0, 1)
    h = jnp.concatenate([(x * (1 + sc1[:b_]) + sh1[:b_]).reshape(rl, d_), (ctx * (1 + sc1[b_]) + sh1[b_]).reshape(rc, d_)],
                        axis=0)
    for layer in range(DEPTH):
        with_ctx_out = layer < DEPTH - 1
        g1, sh2, sc2, g2 = part(layer, 2), part(layer, 3), part(layer, 4), part(layer, 5)
        z = matmul(h, p['w_in'][layer])
        zl = z[:rl, :IN_WIDTH].reshape(b_, n_tok, IN_WIDTH)
        zc = z[rl:, :IN_WIDTH].reshape(b_, l_, IN_WIDTH)
        y, yc = token_mixers(zl, zc, p, layer, rope, with_ctx_out)
        if with_ctx_out:
            yo = matmul(jnp.concatenate([y.reshape(rl, d_), yc.reshape(rc, d_)], axis=0), p['w_out'][layer])
            xr, h2 = ln_mod(xr, yo, g1, vec('ln1_g', layer), vec('ln1_b', layer), sc2, sh2, n_tok)
            f = matmul_relu2(matmul(h2, p['w_ff1'][layer]), p['w_ff2'][layer])
            xr, h = ln_mod(xr, f, g2, vec('ln2_g', layer), vec('ln2_b', layer), part(layer + 1, 1), part(layer + 1, 0), n_tok)
        else:
            lat = lambda t: t[:b_]
            yo = matmul(y.reshape(rl, d_), p['w_out'][layer])
            xl, h2 = ln_mod(xr[:rl], yo, lat(g1), vec('ln1_g', layer), vec('ln1_b', layer), lat(sc2), lat(sh2), n_tok)
            f = matmul_relu2(matmul(h2, p['w_ff1'][layer]), p['w_ff2'][layer])
            none = jnp.zeros((b_, 1, d_), F32)
            xl, _ = ln_mod(xl, f, lat(g2), vec('ln2_g', layer), vec('ln2_b', layer), none, none, n_tok)
    err = jnp.square(xl - loss_target.reshape(rl, d_))
    return 0.5 * jnp.sum(jnp.mean(err, axis=-1))


def _shard_shape(shape, axis):
    s = list(shape)
    s[axis] //= N_DEV
    return tuple(s)


def _join_shards(pieces, axis):
    _, _, r, c = pieces.shape
    if axis == 0:
        full = pieces.transpose(1, 0, 2, 3).reshape(DEPTH, N_DEV * r, c)
    else:
        full = pieces.transpose(1, 2, 0, 3).reshape(DEPTH, r, N_DEV * c)
    return full.astype(F32)


def _split_shards(g, shape, axis):
    r, c = _shard_shape(shape, axis)
    if axis == 0:
        pieces = g.reshape(DEPTH, N_DEV, r, c).transpose(1, 0, 2, 3)
    else:
        pieces = g.reshape(DEPTH, r, N_DEV, c).transpose(2, 0, 1, 3)
    return pieces.astype(BF16)


def _pad_vec(vec, rows_multiple=8):
    n = vec.shape[0]
    rows = -(-n // (128 * rows_multiple)) * rows_multiple
    return jnp.pad(vec, (0, rows * 128 - n)).reshape(rows, 128)


def _adamw(w, g, m, v, name):
    shape = w.shape
    if w.ndim >= 2 and shape[-1] >= 128:
        as2 = lambda t: t.reshape(-1, shape[-1])
        d, nm, nv = _adamw_call(as2(w), as2(g), as2(m), as2(v), name)
        return d.reshape(shape), nm.reshape(shape), nv.reshape(shape)
    n = int(np.prod(shape))
    as2 = lambda t: _pad_vec(t.reshape(-1))
    d, nm, nv = _adamw_call(as2(w), as2(g), as2(m), as2(v), name)
    un = lambda t: t.reshape(-1)[:n].reshape(shape)
    return un(d), un(nm), un(nv)


def kernel(x, c, ctx, c_ctx, ada_w, ada_b, w_in, swa_sink, dn_conv_w, dn_a_log, dn_dt_bias, dn_norm_g, ret_log1m_gamma, ret_norm_g, mla_q_norm, mla_w_uq, mla_kv_norm, mla_w_ukv, w_out, ln1_g, ln1_b, w_ff1, w_ff2, ln2_g, ln2_b, loss_target, m_c_ctx, m_ada_w, m_ada_b, m_w_in, m_swa_sink, m_dn_conv_w, m_dn_a_log, m_dn_dt_bias, m_dn_norm_g, m_ret_log1m_gamma, m_ret_norm_g, m_mla_q_norm, m_mla_w_uq, m_mla_kv_norm, m_mla_w_ukv, m_w_out, m_ln1_g, m_ln1_b, m_w_ff1, m_w_ff2, m_ln2_g, m_ln2_b, v_c_ctx, v_ada_w, v_ada_b, v_w_in, v_swa_sink, v_dn_conv_w, v_dn_a_log, v_dn_dt_bias, v_dn_norm_g, v_ret_log1m_gamma, v_ret_norm_g, v_mla_q_norm, v_mla_w_uq, v_mla_kv_norm, v_mla_w_ukv, v_w_out, v_ln1_g, v_ln1_b, v_w_ff1, v_w_ff2, v_ln2_g, v_ln2_b):
    a = dict(zip(ARG_NAMES, (x, c, ctx, c_ctx, ada_w, ada_b, w_in, swa_sink, dn_conv_w, dn_a_log, dn_dt_bias, dn_norm_g, ret_log1m_gamma, ret_norm_g, mla_q_norm, mla_w_uq, mla_kv_norm, mla_w_ukv, w_out, ln1_g, ln1_b, w_ff1, w_ff2, ln2_g, ln2_b, loss_target, m_c_ctx, m_ada_w, m_ada_b, m_w_in, m_swa_sink, m_dn_conv_w, m_dn_a_log, m_dn_dt_bias, m_dn_norm_g, m_ret_log1m_gamma, m_ret_norm_g, m_mla_q_norm, m_mla_w_uq, m_mla_kv_norm, m_mla_w_ukv, m_w_out, m_ln1_g, m_ln1_b, m_w_ff1, m_w_ff2, m_ln2_g, m_ln2_b, v_c_ctx, v_ada_w, v_ada_b, v_w_in, v_swa_sink, v_dn_conv_w, v_dn_a_log, v_dn_dt_bias, v_dn_norm_g, v_ret_log1m_gamma, v_ret_norm_g, v_mla_q_norm, v_mla_w_uq, v_mla_kv_norm, v_mla_w_ukv, v_w_out, v_ln1_g, v_ln1_b, v_w_ff1, v_w_ff2, v_ln2_g, v_ln2_b)))
    me = 4 * lax.axis_index("x") + 2 * lax.axis_index("y") + lax.axis_index("c")
    b_loc = x.shape[0]
    n_ex = N_DEV * b_loc
    conv_k, conv_c = dn_conv_w.shape[1], dn_conv_w.shape[2]
    ada_cols = ada_w.shape[2]

    small_in = jnp.concatenate([c.reshape(-1), dn_conv_w.reshape(-1)])
    gathered = _gather_call([_pad_vec(small_in)] + [a[name].astype(BF16) for name, _, _ in BIG], "gather_weights")
    small_all = gathered[0].reshape(N_DEV, -1)
    c_all = small_all[:, :b_loc * D_MODEL].reshape(n_ex, D_MODEL)
    conv_all = small_all[:, b_loc * D_MODEL:b_loc * D_MODEL + DEPTH * conv_k * conv_c].reshape(N_DEV, DEPTH, conv_k, conv_c)
    conv_full = conv_all.transpose(1, 2, 0, 3).reshape(DEPTH, conv_k, N_DEV * conv_c)
    big = {name: _join_shards(pieces, axis) for (name, _, axis), pieces in zip(BIG, gathered[1:])}
    big['w_in'] = jnp.pad(big['w_in'], ((0, 0), (0, 0), (0, IN_WIDTH_PAD - IN_WIDTH)))

    n_rows = -(-(n_ex + 1) // 16) * 16
    silu_cc = jax.nn.silu(c_ctx)
    a_rows = jnp.concatenate([jax.nn.silu(c_all), silu_cc[None], jnp.zeros((n_rows - n_ex - 1, D_MODEL), F32)], axis=0)
    m_loc = jnp.concatenate([_mm_call(a_rows, ada_w[l], False, "ada_fwd") for l in range(DEPTH)], axis=0)
    m_all = _gather_call([m_loc], "gather_mod")[0].reshape(N_DEV, DEPTH, n_rows, ada_cols)
    mod_full = m_all.transpose(1, 2, 0, 3).reshape(DEPTH, n_rows, N_DEV * ada_cols) + ada_b[:, None, :]
    mod = lax.dynamic_slice_in_dim(mod_full, me * b_loc, b_loc, axis=1)
    cmod = mod_full[:, n_ex]

    p = dict(big)
    p.update(mod=mod, cmod=cmod, dn_conv_w=conv_full)
    for name in SMALL:
        p[name] = a[name]
    loss_loc, (gp, gx) = jax.value_and_grad(local_loss, argnums=(0, 1))(p, x, ctx, loss_target)
    loss = lax.psum(loss_loc, MESH_AXES)

    gp['w_in'] = gp['w_in'][:, :, :IN_WIDTH]
    my_core = lax.axis_index("c")
    parts = [_split_shards(gp[name], shape, axis) for name, shape, axis in BIG]
    from_sibling = _transfer_call(parts, "scatter_sibling", _sibling_plan, N_CHIP, N_CHIP)
    chip_parts = []
    for (name, shape, axis), part, other in zip(BIG, parts, from_sibling):
        r, c_ = _shard_shape(shape, axis)
        mine = lax.dynamic_index_in_dim(part.reshape(N_CHIP, 2, DEPTH * r, c_), my_core, axis=1, keepdims=False)
        chip_parts.append(_pair_add_call(mine, other.reshape(N_CHIP, DEPTH * r, c_), "pair_" + name))
    arrived = _transfer_call(chip_parts, "scatter_chips", _chip_plan, N_CHIP - 1, N_CHIP)
    g_big = {}
    for (name, shape, axis), part in zip(BIG, arrived):
        r, c_ = _shard_shape(shape, axis)
        g_big[name] = _sum_slots_call(part, "sum_" + name).reshape(DEPTH, r, c_)

    d_loc = jnp.concatenate([gp['mod'], gp['cmod'][:, None, :]], axis=1).reshape(DEPTH * (b_loc + 1), -1)
    d_loc = jnp.pad(d_loc, ((0, 8 - DEPTH * (b_loc + 1)), (0, 0)))
    d_all = _gather_call([d_loc], "gather_dmod")[0][:, :DEPTH * (b_loc + 1)].reshape(N_DEV, DEPTH, b_loc + 1, -1)
    d_rows = d_all[:, :, :b_loc].transpose(1, 0, 2, 3).reshape(DEPTH, n_ex, -1)
    d_crow = d_all[0, :, b_loc]
    for d in range(1, N_DEV):
        d_crow = d_crow + d_all[d, :, b_loc]
    dm_full = jnp.concatenate([d_rows, d_crow[:, None, :], jnp.zeros((DEPTH, n_rows - n_ex - 1, d_rows.shape[-1]), F32)], axis=1)
    g_ada_b = jnp.sum(dm_full, axis=1)
    dm_mine = lax.dynamic_slice_in_dim(dm_full, me * ada_cols, ada_cols, axis=2)
    g_ada_w = jnp.stack([_mm_call(a_rows, dm_mine[l], True, "ada_bwd_w") for l in range(DEPTH)])
    crow8 = jnp.concatenate([dm_mine[:, n_ex:n_ex + 1], jnp.zeros((DEPTH, 15, ada_cols), F32)], axis=1)
    dsilu_part = sum(_mm_call(crow8[l], jnp.transpose(ada_w[l]), False, "ada_bwd_c")[0] for l in range(DEPTH))

    small_g = jnp.concatenate([gp[name].reshape(-1) for name in SMALL] + [gp['dn_conv_w'].reshape(-1), dsilu_part])
    small_sum = _sum_slots_call(_gather_call([_pad_vec(small_g)], "gather_small_grads")[0], "sum_small_grads").reshape(-1)
    g_all, off = {}, 0
    for name in SMALL:
        n = int(np.prod(a[name].shape))
        g_all[name] = small_sum[off:off + n].reshape(a[name].shape)
        off += n
    n = DEPTH * conv_k * N_DEV * conv_c
    g_conv_full = small_sum[off:off + n].reshape(DEPTH, conv_k, N_DEV * conv_c)
    g_all['dn_conv_w'] = lax.dynamic_slice_in_dim(g_conv_full, me * conv_c, conv_c, axis=2)
    off += n
    dsilu = small_sum[off:off + D_MODEL]
    sig = jax.nn.sigmoid(c_ctx)
    g_all['c_ctx'] = dsilu * (sig * (1 + c_ctx * (1 - sig)))
    g_all['ada_w'] = g_ada_w
    g_all['ada_b'] = g_ada_b
    g_all.update(g_big)

    delta, new_m, new_v = {}, {}, {}
    for name in WEIGHTS:
        delta[name], new_m[name], new_v[name] = _adamw(a[name], g_all[name], a['m_' + name], a['v_' + name], "adamw_" + name)
    return (loss, gx, *[g_all[n] for n in WEIGHTS], *[delta[n] for n in WEIGHTS],
            *[new_m[n] for n in WEIGHTS], *[new_v[n] for n in WEIGHTS])
```

```python
import functools

import jax
import jax.numpy as jnp
import numpy as np
from jax import lax
from jax.experimental import pallas as pl
from jax.experimental.pallas import tpu as pltpu

F32 = jnp.float32
BF16 = jnp.bfloat16
N_DEV = 8
MESH_AXES = ("x", "y", "c")

D_MODEL = 1024
DEPTH = 2
GRID_W = 64
SWA_HEADS, SWA_KV_HEADS, SWA_HEAD_DIM, SWA_WINDOW, SWA_BLOCK = 4, 2, 64, 128, 128
DN_HEADS, DN_HEAD_DIM, DN_CHUNK = 4, 64, 64
RET_HEADS, RET_QK_DIM, RET_V_DIM, RET_CHUNK = 4, 32, 64, 64
MLA_HEADS, MLA_Q_RANK, MLA_KV_RANK, MLA_NOPE_DIM, MLA_ROPE_DIM, MLA_V_DIM = 4, 256, 128, 64, 32, 64
D_FF = 4 * D_MODEL
ROPE_BASE = 10000.0
NORM_EPS = 1e-6
LN_EPS = 1e-5
DEEPNORM_ALPHA = (2 * DEPTH) ** 0.25
SWA_Q = SWA_HEADS * SWA_HEAD_DIM
SWA_KV = SWA_KV_HEADS * SWA_HEAD_DIM
DN_W = DN_HEADS * DN_HEAD_DIM
RET_QK = RET_HEADS * RET_QK_DIM
RET_V = RET_HEADS * RET_V_DIM
IN_SPLITS = (SWA_Q, SWA_KV, SWA_KV, 3 * DN_W, DN_W, 4 * DN_HEADS, RET_QK, RET_QK, RET_V, RET_V,
             MLA_Q_RANK, MLA_KV_RANK, MLA_ROPE_DIM)
IN_WIDTH = sum(IN_SPLITS)
IN_WIDTH_PAD = -(-IN_WIDTH // 128) * 128

ADAM_LR, ADAM_B1, ADAM_B2, ADAM_EPS, ADAM_WD, ADAM_STEP = 0.001, 0.9, 0.999, 1e-08, 0.01, 10

WEIGHTS = ['c_ctx', 'ada_w', 'ada_b', 'w_in', 'swa_sink', 'dn_conv_w', 'dn_a_log', 'dn_dt_bias', 'dn_norm_g',
           'ret_log1m_gamma', 'ret_norm_g', 'mla_q_norm', 'mla_w_uq', 'mla_kv_norm', 'mla_w_ukv', 'w_out', 'ln1_g',
           'ln1_b', 'w_ff1', 'w_ff2', 'ln2_g', 'ln2_b']
FWD_INPUTS = ['x', 'c', 'ctx'] + WEIGHTS
ARG_NAMES = FWD_INPUTS + ['loss_target'] + ['m_' + n for n in WEIGHTS] + ['v_' + n for n in WEIGHTS]

BIG = (('w_in', (D_MODEL, IN_WIDTH), 1), ('w_out', (D_MODEL, D_MODEL), 0), ('w_ff1', (D_MODEL, D_FF), 1),
       ('w_ff2', (D_FF, D_MODEL), 0), ('mla_w_uq', (MLA_Q_RANK, MLA_HEADS * (MLA_NOPE_DIM + MLA_ROPE_DIM)), 1),
       ('mla_w_ukv', (MLA_KV_RANK, MLA_HEADS * (MLA_NOPE_DIM + MLA_V_DIM)), 1))
SMALL = ('swa_sink', 'dn_a_log', 'dn_dt_bias', 'dn_norm_g', 'ret_log1m_gamma', 'ret_norm_g', 'mla_q_norm',
         'mla_kv_norm', 'ln1_g', 'ln1_b', 'ln2_g', 'ln2_b')


def _pcall(body, **kw):
    return pl.pallas_call(body, **kw)


def _pick(n, cands):
    for cand in cands:
        if n % cand == 0:
            return cand
    return n


def _bdot(a, b, dims):
    return lax.dot_general(a.astype(BF16), b.astype(BF16), dims, preferred_element_type=F32)


def _lane0(t):
    return jnp.where(lax.broadcasted_iota(jnp.int32, t.shape, t.ndim - 1) == 0, t, 0.0)


NN = (((1,), (0,)), ((), ()))
NT = (((1,), (1,)), ((), ()))
TN = (((0,), (0,)), ((), ()))
BNN = (((2,), (1,)), ((0,), (0,)))
BNT = (((2,), (2,)), ((0,), (0,)))


MM_ROW_TILE_MAX = 1088
MM_COL_TILE_MAX = 1408
MM_TOKEN_TILE_MAX = 1088
VMEM_LIMIT_MAX = 60 * 1024 * 1024


def _tile(n, cap, align):
    best = None
    for t in range(align, min(n, cap) + 1, align):
        if n % t == 0:
            best = t
    return best or n


def _relu2(t):
    return jnp.square(jnp.maximum(t, 0.0))


def _mm_call(a, b, trans_a, name, act_a=False, epi=None):
    if trans_a:
        kdim, m = a.shape
        tk = _tile(kdim, MM_TOKEN_TILE_MAX, 8)
        tm = _tile(m, 1024, 128)
    else:
        m, kdim = a.shape
        tk = _tile(kdim, MM_COL_TILE_MAX, 128)
        tm = _tile(m, MM_ROW_TILE_MAX, 8)
    n = b.shape[1]
    assert b.shape[0] == kdim
    tn = _tile(n, MM_COL_TILE_MAX, 128)
    nk = kdim // tk

    def body(*refs):
        a_ref, b_ref = refs[0], refs[1]
        e_ref = refs[2] if epi is not None else None
        o_ref = refs[-1]
        k = pl.program_id(2)
        av = a_ref[...]
        if act_a:
            av = _relu2(av)
        part = _bdot(av, b_ref[...], TN if trans_a else NN)

        def finish(t):
            return t * (2.0 * jnp.maximum(e_ref[...], 0.0)) if epi is not None else t

        if nk == 1:
            o_ref[...] = finish(part)
        else:
            @pl.when(k == 0)
            def _():
                o_ref[...] = part

            @pl.when((k > 0) & (k < nk - 1))
            def _():
                o_ref[...] += part

            @pl.when(k == nk - 1)
            def _():
                o_ref[...] = finish(o_ref[...] + part)

    if trans_a:
        a_spec = pl.BlockSpec((tk, tm), lambda i, j, k: (k, i))
    else:
        a_spec = pl.BlockSpec((tm, tk), lambda i, j, k: (i, k))
    o_spec = pl.BlockSpec((tm, tn), lambda i, j, k: (i, j))
    in_specs = [a_spec, pl.BlockSpec((tk, tn), lambda i, j, k: (k, j))] + ([o_spec] if epi is not None else [])
    tiles = tm * tk * a.dtype.itemsize + tk * tn * b.dtype.itemsize + tm * tn * 4 * (2 if epi is not None else 1)
    temps = tm * tk * (2 + (4 if act_a else 0)) + tk * tn * 2 + 2 * tm * tn * 4
    return _pcall(
        body, name=name, grid=(m // tm, n // tn, nk), in_specs=in_specs, out_specs=o_spec,
        out_shape=jax.ShapeDtypeStruct((m, n), F32),
        compiler_params=pltpu.CompilerParams(dimension_semantics=("parallel", "parallel", "arbitrary"),
                                             vmem_limit_bytes=min(2 * tiles + temps + (4 << 20), VMEM_LIMIT_MAX)),
    )(*((a, b) + ((epi,) if epi is not None else ())))


@jax.custom_vjp
def matmul(a, b):
    return _mm_call(a, b.astype(BF16), False, "mm_fwd")


def _matmul_fwd(a, b):
    bb = b.astype(BF16)
    return _mm_call(a, bb, False, "mm_fwd"), (a, bb)


def _matmul_bwd(res, g):
    a, bb = res
    da = _mm_call(g, jnp.transpose(bb), False, "mm_bwd_da")
    db = _mm_call(a, g, True, "mm_bwd_db")
    return da, db


matmul.defvjp(_matmul_fwd, _matmul_bwd)


@jax.custom_vjp
def matmul_relu2(a, b):
    return _mm_call(a, b.astype(BF16), False, "mm_act_fwd", act_a=True)


def _matmul_relu2_fwd(a, b):
    bb = b.astype(BF16)
    return _mm_call(a, bb, False, "mm_act_fwd", act_a=True), (a, bb)


def _matmul_relu2_bwd(res, g):
    a, bb = res
    da = _mm_call(g, jnp.transpose(bb), False, "mm_act_bwd_da", epi=a)
    db = _mm_call(a, g, True, "mm_act_bwd_db", act_a=True)
    return da, db


matmul_relu2.defvjp(_matmul_relu2_fwd, _matmul_relu2_bwd)


LN_ROW_TILE = 256


def _ln_group_map(group_rows, n_groups):
    per = group_rows // LN_ROW_TILE
    return lambda i: (jnp.minimum(i // per, n_groups - 1), 0, 0)


def _ln_stats(x, y, gate):
    pre = DEEPNORM_ALPHA * x + gate * y
    mu = jnp.mean(pre, axis=-1, keepdims=True)
    cen = pre - mu
    rstd = lax.rsqrt(jnp.mean(jnp.square(cen), axis=-1, keepdims=True) + LN_EPS)
    return cen * rstd, rstd


def _ln_mod_fwd_call(x, y, gate, gamma, beta, sc, sh, group_rows):
    r, d = x.shape
    ng = gate.shape[0]
    gmap = _ln_group_map(group_rows, ng)

    def body(x_ref, y_ref, gate_ref, gamma_ref, beta_ref, sc_ref, sh_ref, xn_ref, h_ref):
        xh, _ = _ln_stats(x_ref[...], y_ref[...], gate_ref[0])
        xn = xh * gamma_ref[...] + beta_ref[...]
        xn_ref[...] = xn
        h_ref[...] = xn * (1.0 + sc_ref[0]) + sh_ref[0]

    row = pl.BlockSpec((LN_ROW_TILE, d), lambda i: (i, 0))
    grp = pl.BlockSpec((1, 1, d), gmap)
    vec = pl.BlockSpec((1, d), lambda i: (0, 0))
    return _pcall(
        body, name="ln_mod_fwd", grid=(r // LN_ROW_TILE,),
        in_specs=[row, row, grp, vec, vec, grp, grp], out_specs=[row, row],
        out_shape=[jax.ShapeDtypeStruct((r, d), F32)] * 2,
        compiler_params=pltpu.CompilerParams(dimension_semantics=("parallel",)),
    )(x, y, gate, gamma, beta, sc, sh)


def _ln_mod_bwd_call(x, y, gate, gamma, beta, sc, dxn, dh, group_rows):
    r, d = x.shape
    ng = gate.shape[0]
    gmap = _ln_group_map(group_rows, ng)
    per = group_rows // LN_ROW_TILE

    def body(x_ref, y_ref, gate_ref, gamma_ref, beta_ref, sc_ref, dxn_ref, dh_ref,
             dx_ref, dy_ref, dgate_ref, dgamma_ref, dbeta_ref, dsc_ref, dsh_ref):
        i = pl.program_id(0)
        yv, gate_v, gamma_v = y_ref[...], gate_ref[0], gamma_ref[...]
        xh, rstd = _ln_stats(x_ref[...], yv, gate_v)
        dhv = dh_ref[...]
        dtot = dxn_ref[...] + dhv * (1.0 + sc_ref[0])
        dxh = dtot * gamma_v
        dpre = rstd * (dxh - jnp.mean(dxh, axis=-1, keepdims=True) - xh * jnp.mean(dxh * xh, axis=-1, keepdims=True))
        dx_ref[...] = DEEPNORM_ALPHA * dpre
        dy_ref[...] = gate_v * dpre
        col = lambda t: jnp.sum(t, axis=0, keepdims=True)

        @pl.when(i == 0)
        def _():
            dgamma_ref[...] = jnp.zeros_like(dgamma_ref)
            dbeta_ref[...] = jnp.zeros_like(dbeta_ref)

        first_of_group = (i % per == 0) | (i == (ng - 1) * per)

        @pl.when(first_of_group & (i <= (ng - 1) * per))
        def _():
            dgate_ref[...] = jnp.zeros_like(dgate_ref)
            dsc_ref[...] = jnp.zeros_like(dsc_ref)
            dsh_ref[...] = jnp.zeros_like(dsh_ref)

        dgamma_ref[...] += col(dtot * xh)
        dbeta_ref[...] += col(dtot)
        dgate_ref[0] += col(dpre * yv)
        dsc_ref[0] += col(dhv * (xh * gamma_v + beta_ref[...]))
        dsh_ref[0] += col(dhv)

    row = pl.BlockSpec((LN_ROW_TILE, d), lambda i: (i, 0))
    grp = pl.BlockSpec((1, 1, d), gmap)
    vec = pl.BlockSpec((1, d), lambda i: (0, 0))
    big = jax.ShapeDtypeStruct((r, d), F32)
    gs = jax.ShapeDtypeStruct((ng, 1, d), F32)
    vs = jax.ShapeDtypeStruct((1, d), F32)
    return _pcall(
        body, name="ln_mod_bwd", grid=(r // LN_ROW_TILE,),
        in_specs=[row, row, grp, vec, vec, grp, row, row],
        out_specs=[row, row, grp, vec, vec, grp, grp],
        out_shape=[big, big, gs, vs, vs, gs, gs],
        compiler_params=pltpu.CompilerParams(dimension_semantics=("arbitrary",)),
    )(x, y, gate, gamma, beta, sc, dxn, dh)


@functools.partial(jax.custom_vjp, nondiff_argnums=(7,))
def ln_mod(x, y, gate, gamma, beta, sc, sh, group_rows):
    return tuple(_ln_mod_fwd_call(x, y, gate, gamma, beta, sc, sh, group_rows))


def _ln_mod_fwd(x, y, gate, gamma, beta, sc, sh, group_rows):
    xn, h = _ln_mod_fwd_call(x, y, gate, gamma, beta, sc, sh, group_rows)
    return (xn, h), (x, y, gate, gamma, beta, sc)


def _ln_mod_bwd(group_rows, res, cts):
    x, y, gate, gamma, beta, sc = res
    dxn, dh = cts
    return tuple(_ln_mod_bwd_call(x, y, gate, gamma, beta, sc, dxn, dh, group_rows))


ln_mod.defvjp(_ln_mod_fwd, _ln_mod_bwd)


def _attn_probs(q, k, sink, scale, has_sink):
    s = _bdot(q, k, NT) * scale
    m = jnp.max(s, axis=-1, keepdims=True)
    if has_sink:
        m = jnp.maximum(m, sink)
    p = jnp.exp(s - m)
    den = jnp.sum(p, axis=-1, keepdims=True)
    p_sink = None
    if has_sink:
        p_sink = jnp.exp(sink - m)
        den = den + p_sink
    inv = 1.0 / den
    if has_sink:
        p_sink = p_sink * inv
    return p * inv, p_sink


def _attn_full_fwd_call(q, k, v, sink, scale, has_sink):
    g, sq, dq = q.shape
    nk, dv = k.shape[1], v.shape[2]
    bq = _pick(sq, (256, 128))

    def body(q_ref, k_ref, v_ref, sink_ref, o_ref):
        p, _ = _attn_probs(q_ref[0], k_ref[0], sink_ref[0, :, 0:1], scale, has_sink)
        o_ref[0] = _bdot(p, v_ref[0], NN)

    return _pcall(
        body, name="attn_full_fwd", grid=(g, sq // bq),
        in_specs=[pl.BlockSpec((1, bq, dq), lambda b, i: (b, i, 0)), pl.BlockSpec((1, nk, dq), lambda b, i: (b, 0, 0)),
                  pl.BlockSpec((1, nk, dv), lambda b, i: (b, 0, 0)), pl.BlockSpec((1, 1, 128), lambda b, i: (b, 0, 0))],
        out_specs=pl.BlockSpec((1, bq, dv), lambda b, i: (b, i, 0)),
        out_shape=jax.ShapeDtypeStruct((g, sq, dv), F32),
        compiler_params=pltpu.CompilerParams(dimension_semantics=("parallel", "arbitrary")),
    )(q, k, v, sink)


def _attn_full_bwd_call(q, k, v, sink, o, do, scale, has_sink):
    g, sq, dq = q.shape
    nk, dv = k.shape[1], v.shape[2]
    bq = _pick(sq, (256, 128))

    def body(q_ref, k_ref, v_ref, sink_ref, o_ref, do_ref, dq_ref, dk_ref, dv_ref, dsink_ref):
        i = pl.program_id(1)
        qv, kv, vv, dov = q_ref[0], k_ref[0], v_ref[0], do_ref[0]
        p, p_sink = _attn_probs(qv, kv, sink_ref[0, :, 0:1], scale, has_sink)
        delta = jnp.sum(dov * o_ref[0], axis=-1, keepdims=True)
        dv_part = _bdot(p, dov, TN)
        dp = _bdot(dov, vv, NT)
        ds = p * (dp - delta) * scale
        dq_ref[0] = _bdot(ds, kv, NN)
        dk_part = _bdot(ds, qv, TN)
        if has_sink:
            dsk = jnp.broadcast_to(-jnp.sum(p_sink * delta, axis=0, keepdims=True), (1, 128))
        else:
            dsk = jnp.zeros((1, 128), F32)

        @pl.when(i == 0)
        def _():
            dk_ref[0] = dk_part
            dv_ref[0] = dv_part
            dsink_ref[0] = dsk

        @pl.when(i > 0)
        def _():
            dk_ref[0] += dk_part
            dv_ref[0] += dv_part
            dsink_ref[0] += dsk

    qspec = pl.BlockSpec((1, bq, dq), lambda b, i: (b, i, 0))
    kspec = pl.BlockSpec((1, nk, dq), lambda b, i: (b, 0, 0))
    vspec = pl.BlockSpec((1, nk, dv), lambda b, i: (b, 0, 0))
    ospec = pl.BlockSpec((1, bq, dv), lambda b, i: (b, i, 0))
    sspec = pl.BlockSpec((1, 1, 128), lambda b, i: (b, 0, 0))
    return _pcall(
        body, name="attn_full_bwd", grid=(g, sq // bq),
        in_specs=[qspec, kspec, vspec, sspec, ospec, ospec],
        out_specs=[qspec, kspec, vspec, sspec],
        out_shape=[jax.ShapeDtypeStruct(q.shape, F32), jax.ShapeDtypeStruct(k.shape, F32),
                   jax.ShapeDtypeStruct(v.shape, F32), jax.ShapeDtypeStruct(sink.shape, F32)],
        compiler_params=pltpu.CompilerParams(dimension_semantics=("parallel", "arbitrary")),
    )(q, k, v, sink, o, do)


@functools.partial(jax.custom_vjp, nondiff_argnums=(4, 5))
def attn_full(q, k, v, sink, scale, has_sink):
    return _attn_full_fwd_call(q.astype(BF16), k.astype(BF16), v.astype(BF16), sink, scale, has_sink)


def _attn_full_fwd(q, k, v, sink, scale, has_sink):
    q, k, v = q.astype(BF16), k.astype(BF16), v.astype(BF16)
    o = _attn_full_fwd_call(q, k, v, sink, scale, has_sink)
    return o, (q, k, v, sink, o)


def _attn_full_bwd(scale, has_sink, res, do):
    q, k, v, sink, o = res
    dq, dk, dv, dsink = _attn_full_bwd_call(q, k, v, sink, o, do, scale, has_sink)
    return dq, dk, dv, _lane0(dsink)


attn_full.defvjp(_attn_full_fwd, _attn_full_bwd)


SWA_GROUP = SWA_HEADS // SWA_KV_HEADS


def _swa_rows(ref):
    return ref[...].reshape(SWA_GROUP * SWA_BLOCK, ref.shape[-1])


def _swa_sink_rows(sink_ref):
    head = lax.broadcasted_iota(jnp.int32, (SWA_GROUP * SWA_BLOCK, 1), 0) // SWA_BLOCK
    out = jnp.zeros((SWA_GROUP * SWA_BLOCK, 1), F32)
    for j in range(SWA_GROUP):
        out = jnp.where(head == j, sink_ref[j, :, 0:1], out)
    return out


def _swa_probs(q, kw, kc, sink, i, s_len, scale):
    w = SWA_BLOCK
    rows = q.shape[0]
    s_loc = _bdot(q, kw, NT) * scale
    qpos = i * w + lax.broadcasted_iota(jnp.int32, (rows, 3 * w), 0) % w
    kpos = (i - 1) * w + lax.broadcasted_iota(jnp.int32, (rows, 3 * w), 1)
    valid = (jnp.abs(kpos - qpos) <= SWA_WINDOW) & (kpos >= 0) & (kpos < s_len)
    s_loc = jnp.where(valid, s_loc, -jnp.inf)
    s_ctx = _bdot(q, kc, NT) * scale
    m = jnp.maximum(jnp.maximum(jnp.max(s_loc, axis=-1, keepdims=True), jnp.max(s_ctx, axis=-1, keepdims=True)), sink)
    p_loc = jnp.exp(s_loc - m)
    p_ctx = jnp.exp(s_ctx - m)
    p_sink = jnp.exp(sink - m)
    inv = 1.0 / (jnp.sum(p_loc, axis=-1, keepdims=True) + jnp.sum(p_ctx, axis=-1, keepdims=True) + p_sink)
    return p_loc * inv, p_ctx * inv, p_sink * inv


def _swa_fwd_call(q, kp, vp, kc, vc, sink, scale):
    g, s_len, d = q.shape
    l_ctx = kc.shape[1]
    w = SWA_BLOCK
    grp = SWA_GROUP

    def body(q_ref, kp_ref, vp_ref, kc_ref, vc_ref, sink_ref, o_ref):
        i = pl.program_id(1)
        start = pl.multiple_of(i * w, w)
        kw = kp_ref[0, pl.ds(start, 3 * w), :]
        vw = vp_ref[0, pl.ds(start, 3 * w), :]
        p_loc, p_ctx, _ = _swa_probs(_swa_rows(q_ref), kw, kc_ref[0], _swa_sink_rows(sink_ref), i, s_len, scale)
        o_ref[...] = (_bdot(p_loc, vw, NN) + _bdot(p_ctx, vc_ref[0], NN)).reshape(grp, w, d)

    return _pcall(
        body, name="swa_fwd", grid=(g // grp, s_len // w),
        in_specs=[pl.BlockSpec((grp, w, d), lambda b, i: (b, i, 0)),
                  pl.BlockSpec((1, s_len + 2 * w, d), lambda b, i: (b, 0, 0)),
                  pl.BlockSpec((1, s_len + 2 * w, d), lambda b, i: (b, 0, 0)),
                  pl.BlockSpec((1, l_ctx, d), lambda b, i: (b, 0, 0)),
                  pl.BlockSpec((1, l_ctx, d), lambda b, i: (b, 0, 0)),
                  pl.BlockSpec((grp, 1, 128), lambda b, i: (b, 0, 0))],
        out_specs=pl.BlockSpec((grp, w, d), lambda b, i: (b, i, 0)),
        out_shape=jax.ShapeDtypeStruct(q.shape, F32),
        compiler_params=pltpu.CompilerParams(dimension_semantics=("parallel", "arbitrary")),
    )(q, kp, vp, kc, vc, sink)


def _swa_bwd_call(q, kp, vp, kc, vc, sink, o, do, scale):
    g, s_len, d = q.shape
    l_ctx = kc.shape[1]
    w = SWA_BLOCK
    grp = SWA_GROUP
    sp = s_len + 2 * w

    def body(q_ref, kp_ref, vp_ref, kc_ref, vc_ref, sink_ref, o_ref, do_ref,
             dq_ref, dkp_ref, dvp_ref, dkc_ref, dvc_ref, dsink_ref):
        i = pl.program_id(1)
        start = pl.multiple_of(i * w, w)
        qv, dov = _swa_rows(q_ref), _swa_rows(do_ref)
        kw = kp_ref[0, pl.ds(start, 3 * w), :]
        vw = vp_ref[0, pl.ds(start, 3 * w), :]
        kcv, vcv = kc_ref[0], vc_ref[0]
        p_loc, p_ctx, p_sink = _swa_probs(qv, kw, kcv, _swa_sink_rows(sink_ref), i, s_len, scale)
        delta = jnp.sum(dov * _swa_rows(o_ref), axis=-1, keepdims=True)
        ds_loc = p_loc * (_bdot(dov, vw, NT) - delta) * scale
        ds_ctx = p_ctx * (_bdot(dov, vcv, NT) - delta) * scale
        dq_ref[...] = (_bdot(ds_loc, kw, NN) + _bdot(ds_ctx, kcv, NN)).reshape(grp, w, d)
        dsk = jnp.broadcast_to(-jnp.sum((p_sink * delta).reshape(grp, w, 1), axis=1, keepdims=True), (grp, 1, 128))

        @pl.when(i == 0)
        def _():
            dkp_ref[...] = jnp.zeros_like(dkp_ref)
            dvp_ref[...] = jnp.zeros_like(dvp_ref)
            dkc_ref[...] = jnp.zeros_like(dkc_ref)
            dvc_ref[...] = jnp.zeros_like(dvc_ref)
            dsink_ref[...] = jnp.zeros_like(dsink_ref)

        dkp_ref[0, pl.ds(start, 3 * w), :] += _bdot(ds_loc, qv, TN)
        dvp_ref[0, pl.ds(start, 3 * w), :] += _bdot(p_loc, dov, TN)
        dkc_ref[0] += _bdot(ds_ctx, qv, TN)
        dvc_ref[0] += _bdot(p_ctx, dov, TN)
        dsink_ref[...] += dsk

    qspec = pl.BlockSpec((grp, w, d), lambda b, i: (b, i, 0))
    kspec = pl.BlockSpec((1, sp, d), lambda b, i: (b, 0, 0))
    cspec = pl.BlockSpec((1, l_ctx, d), lambda b, i: (b, 0, 0))
    sspec = pl.BlockSpec((grp, 1, 128), lambda b, i: (b, 0, 0))
    return _pcall(
        body, name="swa_bwd", grid=(g // grp, s_len // w),
        in_specs=[qspec, kspec, kspec, cspec, cspec, sspec, qspec, qspec],
        out_specs=[qspec, kspec, kspec, cspec, cspec, sspec],
        out_shape=[jax.ShapeDtypeStruct(q.shape, F32), jax.ShapeDtypeStruct(kp.shape, F32),
                   jax.ShapeDtypeStruct(vp.shape, F32), jax.ShapeDtypeStruct(kc.shape, F32),
                   jax.ShapeDtypeStruct(vc.shape, F32), jax.ShapeDtypeStruct(sink.shape, F32)],
        compiler_params=pltpu.CompilerParams(dimension_semantics=("parallel", "arbitrary")),
    )(q, kp, vp, kc, vc, sink, o, do)


@functools.partial(jax.custom_vjp, nondiff_argnums=(6,))
def swa_attn(q, kp, vp, kc, vc, sink, scale):
    return _swa_fwd_call(*(t.astype(BF16) for t in (q, kp, vp, kc, vc)), sink, scale)


def _swa_attn_fwd(q, kp, vp, kc, vc, sink, scale):
    q, kp, vp, kc, vc = (t.astype(BF16) for t in (q, kp, vp, kc, vc))
    o = _swa_fwd_call(q, kp, vp, kc, vc, sink, scale)
    return o, (q, kp, vp, kc, vc, sink, o)


def _swa_attn_bwd(scale, res, do):
    q, kp, vp, kc, vc, sink, o = res
    dq, dkp, dvp, dkc, dvc, dsink = _swa_bwd_call(q, kp, vp, kc, vc, sink, o, do, scale)
    return dq, dkp, dvp, dkc, dvc, _lane0(dsink)


swa_attn.defvjp(_swa_attn_fwd, _swa_attn_bwd)


def _f32dot(a, b, dims):
    return lax.dot_general(a, b, dims, precision=lax.Precision.HIGHEST, preferred_element_type=F32)


DN_SOLVE_BLOCK = 16


def _unit_lower_inverse(a, a_t, transposed):
    g, c, _ = a.shape
    nb = DN_SOLVE_BLOCK
    row = lax.broadcasted_iota(jnp.int32, (1, c, c), 1)
    col = lax.broadcasted_iota(jnp.int32, (1, c, c), 2)
    src, off = (a, a_t) if transposed else (a_t, a)
    coef = jnp.zeros((g, c, nb), F32)
    for b in range(c // nb):
        in_block = (lax.broadcasted_iota(jnp.int32, (1, c, nb), 1) // nb) == b
        coef = coef + jnp.where(in_block, src[:, :, b * nb:(b + 1) * nb], 0.0)
    sub = lax.broadcasted_iota(jnp.int32, (1, c // nb, nb, c), 2)
    x = jnp.broadcast_to((row == col).astype(F32), a.shape)
    for i in (range(nb - 2, -1, -1) if transposed else range(1, nb)):
        prod = (coef[:, :, i:i + 1] * x).reshape(g, c // nb, nb, c)
        new_rows = -jnp.sum(prod, axis=2, keepdims=True)
        x = x + jnp.where(sub == i, new_rows, 0.0).reshape(g, c, c)
    width = nb
    while width < c:
        joins = ((row // (2 * width)) == (col // (2 * width))) & ((row // width) != (col // width))
        x = x - _f32dot(x, _f32dot(jnp.where(joins, off, 0.0), x, BNN), BNN)
        width *= 2
    return x


def _dn_masks(c):
    row = lax.broadcasted_iota(jnp.int32, (1, c, c), 1)
    col = lax.broadcasted_iota(jnp.int32, (1, c, c), 2)
    return row, col


def _sweep_chunks(n, n_ctx):
    return (lambda i: i), (lambda i: jnp.where(i < n_ctx, n_ctx - 1 - i, n + n_ctx - 1 - i))


def _half_spec(gh, tail, half, chunk_of):
    return pl.BlockSpec((gh, 1) + tail, lambda i: (half, chunk_of(i), 0, 0))


def _both(ref_f, ref_b):
    return jnp.concatenate([ref_f[:, 0], ref_b[:, 0]], axis=0)


def _dn_direction_masks(g, c):
    backward = lax.broadcasted_iota(jnp.int32, (g, 1, 1), 0) >= g // 2
    row, col = _dn_masks(c)
    return backward, jnp.where(backward, c - 1 - row, row), jnp.where(backward, c - 1 - col, col)


def _dn_inverse(a_mat, a_t, transposed):
    h = a_mat.shape[0] // 2
    return jnp.concatenate([_unit_lower_inverse(a_mat[:h], a_t[:h], transposed),
                            _unit_lower_inverse(a_t[h:], a_mat[h:], not transposed)], axis=0)


def _dn_fwd_call(q, k, k_t, v, gc, bb, gr, n_ctx):
    gh, n, c, _ = q.shape
    g = 2 * gh

    def body(qf, qb, kf, kb_, ktf, ktb, vf, vb, gcf, gcb, bbf, bbb, grf, grb,
             of_ref, ob_ref, vn_ref, sall_ref, w_ref, u_ref, t_ref, s_scr):
        i = pl.program_id(0)

        @pl.when(i == 0)
        def _():
            s_scr[...] = jnp.zeros_like(s_scr)

        qv, kv, ktv, vv, gcv, bv, grv = (_both(qf, qb), _both(kf, kb_), _both(ktf, ktb), _both(vf, vb), _both(gcf, gcb),
                                          _both(bbf, bbb), _both(grf, grb))
        backward, row, col = _dn_direction_masks(g, c)
        e = jnp.exp(gcv)
        kb = kv * bv
        decay = jnp.exp(jnp.where(row >= col, gcv - grv, -jnp.inf))
        decay_ts = jnp.exp(jnp.where(row < col, grv - gcv, -jnp.inf))
        a_mat = _bdot(kb, kv, BNT) * jnp.where(row > col, decay, 0.0)
        t = _dn_inverse(a_mat, _bdot(kv, kb, BNT) * decay_ts, False)
        w = _f32dot(t, kb * e, BNN)
        u = _f32dot(t, vv * bv, BNN)
        glast = jnp.where(backward, grv[:, :, 0:1], grv[:, :, c - 1:c])
        s = s_scr[...]
        sall_ref[:, 0] = s
        vnew = u - _bdot(w, s, BNN)
        o = _bdot(qv * e, s, BNN) + _bdot(_bdot(qv, kv, BNT) * decay, vnew, BNN)
        of_ref[:, 0] = o[:gh]
        ob_ref[:, 0] = o[gh:]
        vn_ref[:, 0] = vnew
        w_ref[:, 0] = w
        u_ref[:, 0] = u
        t_ref[:, 0] = t
        s_scr[...] = s * jnp.exp(glast) + _bdot(ktv * jnp.exp(glast - grv), vnew, BNN)

    cf, cb = _sweep_chunks(n, n_ctx)
    tok = lambda half, chunk_of: _half_spec(gh, (c, c), half, chunk_of)
    rowv = lambda half, chunk_of: _half_spec(gh, (1, c), half, chunk_of)
    step = pl.BlockSpec((g, 1, c, c), lambda i: (0, i, 0, 0))
    shared = [tok(0, cf), tok(0, cb)]
    split = [tok(0, cf), tok(1, cb)]
    return _pcall(
        body, name="dn_fwd", grid=(n,),
        in_specs=shared * 4 + split * 2 + [rowv(0, cf), rowv(1, cb)],
        out_specs=[tok(0, cf), tok(0, cb)] + [step] * 5,
        out_shape=[jax.ShapeDtypeStruct((gh, n, c, c), F32)] * 2 + [jax.ShapeDtypeStruct((g, n, c, c), F32)] * 5,
        scratch_shapes=[pltpu.VMEM((g, c, c), F32)],
        compiler_params=pltpu.CompilerParams(dimension_semantics=("arbitrary",)),
    )(q, q, k, k, k_t, k_t, v, v, gc, gc, bb, bb, gr, gr)


def _dn_bwd_call(q, k, q_t, k_t, v, gc, bb, gr, br, sall, vn, w, u, t_t, do_f, do_b, n_ctx):
    gh, n, c, _ = q.shape
    g = 2 * gh

    def body(qf, qb, kf, kb_, qtf, qtb, ktf, ktb, vf, vb, gcf, gcb, bbf, bbb, grf, grb, brf, brb,
             sall_ref, vn_ref, w_ref, u_ref, tt_ref, dof, dob,
             dqf, dqb, dkf, dkb_, dvf, dvb, dgcf, dgcb, dbbf, dbbb, dgrf, dgrb, ds_scr):
        i = pl.program_id(0)

        @pl.when(i == 0)
        def _():
            ds_scr[...] = jnp.zeros_like(ds_scr)

        qv, kv, qtv, ktv, vv = _both(qf, qb), _both(kf, kb_), _both(qtf, qtb), _both(ktf, ktb), _both(vf, vb)
        gcv, bv, grv, brv, dov = _both(gcf, gcb), _both(bbf, bbb), _both(grf, grb), _both(brf, brb), _both(dof, dob)
        s, vnew, w, u = sall_ref[:, 0], vn_ref[:, 0], w_ref[:, 0], u_ref[:, 0]
        dsn = ds_scr[...]
        backward, row, col = _dn_direction_masks(g, c)
        e = jnp.exp(gcv)
        er = jnp.exp(grv)
        kb = kv * bv
        decay = jnp.exp(jnp.where(row >= col, gcv - grv, -jnp.inf))
        decay_s = jnp.where(row > col, decay, 0.0)
        decay_t = jnp.exp(jnp.where(row <= col, grv - gcv, -jnp.inf))
        decay_ts = jnp.where(row < col, decay_t, 0.0)
        kk = _bdot(kb, kv, BNT)
        tt = tt_ref[:, 0]
        glast = jnp.where(backward, grv[:, :, 0:1], grv[:, :, c - 1:c])
        eg = jnp.exp(glast)
        x = jnp.exp(glast - gcv)
        kt = kv * x
        qk_raw = _bdot(qv, kv, BNT)
        w_t = _f32dot(ktv * (brv * er), tt, BNN)
        dvn = _bdot(_bdot(kv, qv, BNT) * decay_t, dov, BNN) + _bdot(kt, dsn, BNN)
        dqk = _bdot(dov, vnew, BNT)
        dqk_t = _bdot(vnew, dov, BNT)
        dqd = _bdot(dov, s, BNT)
        dkt = _bdot(vnew, dsn, BNT)
        deg = jnp.sum(jnp.sum(dsn * s, axis=2, keepdims=True), axis=1, keepdims=True)
        dw = -_bdot(dvn, s, BNT)
        ds_scr[...] = dsn * eg + _bdot(qtv * er, dov, BNN) - _bdot(w_t, dvn, BNN)
        dwp = _f32dot(tt, dw, BNN)
        dup = _f32dot(tt, dvn, BNN)
        d_a = -(_bdot(dwp, w, BNT) + _bdot(dup, u, BNT))
        d_at = -(_bdot(w, dwp, BNT) + _bdot(u, dup, BNT))
        dkb = _bdot(d_a * decay_s, kv, BNN) + dwp * e
        dkx = dkt * kv * x
        ddiff = dqk * qk_raw * decay + d_a * kk * decay_s
        dglast = jnp.sum(jnp.sum(dkx, axis=2, keepdims=True), axis=1, keepdims=True) + deg * eg
        lane = lax.broadcasted_iota(jnp.int32, (1, 1, c), 2)
        last_lane = jnp.where(backward, 0, c - 1)
        results = (
            (dqf, dqb, dqd * e + _bdot(dqk * decay, kv, BNN)),
            (dkf, dkb_, _bdot(d_at * decay_ts, kb, BNN) + dkb * bv + dkt * x + _bdot(dqk_t * decay_t, qv, BNN)),
            (dvf, dvb, dup * bv),
            (dgcf, dgcb, ddiff + (dwp * kb + dqd * qv) * e - dkx),
            (dbbf, dbbb, dkb * kv + dup * vv),
            (dgrf, dgrb, jnp.where(lane == last_lane, dglast, 0.0) - jnp.sum(ddiff, axis=1, keepdims=True)),
        )
        for ref_f, ref_b, val in results:
            ref_f[:, 0] = val[:gh]
            ref_b[:, 0] = val[gh:]

    cf, cb = _sweep_chunks(n, n_ctx)
    rf, rb = (lambda i: cf(n - 1 - i)), (lambda i: cb(n - 1 - i))
    tok = lambda half, chunk_of: _half_spec(gh, (c, c), half, chunk_of)
    rowv = lambda half, chunk_of: _half_spec(gh, (1, c), half, chunk_of)
    step = pl.BlockSpec((g, 1, c, c), lambda i: (0, n - 1 - i, 0, 0))
    shared = [tok(0, rf), tok(0, rb)]
    split = [tok(0, rf), tok(1, rb)]
    split_row = [rowv(0, rf), rowv(1, rb)]
    big = jax.ShapeDtypeStruct((gh, n, c, c), F32)
    return _pcall(
        body, name="dn_bwd", grid=(n,),
        in_specs=shared * 5 + split * 2 + split_row * 2 + [step] * 5 + shared,
        out_specs=shared * 5 + [rowv(0, rf), rowv(0, rb)],
        out_shape=[big] * 10 + [jax.ShapeDtypeStruct((gh, n, 1, c), F32)] * 2,
        scratch_shapes=[pltpu.VMEM((g, c, c), F32)],
        compiler_params=pltpu.CompilerParams(dimension_semantics=("arbitrary",)),
    )(q, q, k, k, q_t, q_t, k_t, k_t, v, v, gc, gc, bb, bb, gr, gr, br, br, sall, vn, w, u, t_t, do_f, do_b)


_t = lambda a: jnp.swapaxes(a, -1, -2)


def _dn_forms(gcum, beta, d):
    lanes = lambda t: jnp.broadcast_to(t[..., None], t.shape + (d,))
    return lanes(gcum), lanes(beta), gcum[:, :, None, :], beta[:, :, None, :]


@functools.partial(jax.custom_vjp, nondiff_argnums=(5,))
def dn_chunked(q, k, v, gcum, beta, n_ctx):
    gc, bb, gr, _ = _dn_forms(gcum, beta, q.shape[-1])
    return tuple(_dn_fwd_call(q, k, _t(k), v, gc, bb, gr, n_ctx)[:2])


def _dn_chunked_fwd(q, k, v, gcum, beta, n_ctx):
    gc, bb, gr, _ = _dn_forms(gcum, beta, q.shape[-1])
    o_f, o_b, vn, sall, w, u, t = _dn_fwd_call(q, k, _t(k), v, gc, bb, gr, n_ctx)
    return (o_f, o_b), (q, k, v, gcum, beta, vn, sall, w, u, t)


def _dn_chunked_bwd(n_ctx, res, cts):
    q, k, v, gcum, beta, vn, sall, w, u, t = res
    gc, bb, gr, br = _dn_forms(gcum, beta, q.shape[-1])
    (dq_f, dq_b, dk_f, dk_b, dv_f, dv_b, dgc_f, dgc_b, dbb_f, dbb_b, dgr_f, dgr_b) = _dn_bwd_call(
        q, k, _t(q), _t(k), v, gc, bb, gr, br, sall, vn, w, u, _t(t), cts[0], cts[1], n_ctx)
    dgcum = jnp.concatenate([jnp.sum(dgc_f, axis=-1) + dgr_f[:, :, 0, :], jnp.sum(dgc_b, axis=-1) + dgr_b[:, :, 0, :]], axis=0)
    dbeta = jnp.concatenate([jnp.sum(dbb_f, axis=-1), jnp.sum(dbb_b, axis=-1)], axis=0)
    return dq_f + dq_b, dk_f + dk_b, dv_f + dv_b, dgcum, dbeta


dn_chunked.defvjp(_dn_chunked_fwd, _dn_chunked_bwd)


def _ret_fwd_call(q, k, k_t, v, dmat, xi_b, zeta_r, gm, n_ctx):
    gh, n, c, dk = q.shape
    dv = v.shape[-1]
    g = 2 * gh

    def body(qf, qb, kf, kb_, ktf, ktb, vf, vb, d_ref, xib_ref, zr_ref, gm_ref, of_ref, ob_ref, starts_ref, s_scr):
        i = pl.program_id(0)

        @pl.when(i == 0)
        def _():
            s_scr[...] = jnp.zeros_like(s_scr)

        qv, kv, ktv, vv = _both(qf, qb), _both(kf, kb_), _both(ktf, ktb), _both(vf, vb)
        s = s_scr[...]
        starts_ref[:, 0] = s
        o = _bdot(_bdot(qv, kv, BNT) * d_ref[...], vv, BNN) + _bdot(qv * xib_ref[...], s, BNN)
        of_ref[:, 0] = o[:gh]
        ob_ref[:, 0] = o[gh:]
        s_scr[...] = s * gm_ref[...] + _bdot(ktv * zr_ref[...], vv, BNN)

    cf, cb = _sweep_chunks(n, n_ctx)
    pair = lambda tail: [_half_spec(gh, tail, 0, cf), _half_spec(gh, tail, 0, cb)]
    const = lambda a, b: pl.BlockSpec((g, a, b), lambda i: (0, 0, 0))
    return _pcall(
        body, name="ret_fwd", grid=(n,),
        in_specs=pair((c, dk)) * 2 + pair((dk, c)) + pair((c, dv)) + [const(c, c), const(c, dk), const(1, c), const(dk, dv)],
        out_specs=pair((c, dv)) + [pl.BlockSpec((g, 1, dk, dv), lambda i: (0, i, 0, 0))],
        out_shape=[jax.ShapeDtypeStruct((gh, n, c, dv), F32)] * 2 + [jax.ShapeDtypeStruct((g, n, dk, dv), F32)],
        scratch_shapes=[pltpu.VMEM((g, dk, dv), F32)],
        compiler_params=pltpu.CompilerParams(dimension_semantics=("arbitrary",)),
    )(q, q, k, k, k_t, k_t, v, v, dmat, xi_b, zeta_r, gm)


def _ret_bwd_call(q, k, q_t, k_t, v, dmat, dmat_t, xi_b, xi_r, zeta_b, gm, starts, do_f, do_b, n_ctx):
    gh, n, c, dk = q.shape
    dv = v.shape[-1]
    g = 2 * gh

    def body(qf, qb, kf, kb_, qtf, qtb, ktf, ktb, vf, vb, d_ref, dt_ref, xib_ref, xr_ref, zb_ref, gm_ref, starts_ref,
             dof, dob, dqf, dqb, dkf, dkb_, dvf, dvb, dd_ref, dxib_ref, dzb_ref, dgm_ref, ds_scr):
        i = pl.program_id(0)

        @pl.when(i == 0)
        def _():
            ds_scr[...] = jnp.zeros_like(ds_scr)
            dd_ref[...] = jnp.zeros_like(dd_ref)
            dxib_ref[...] = jnp.zeros_like(dxib_ref)
            dzb_ref[...] = jnp.zeros_like(dzb_ref)
            dgm_ref[...] = jnp.zeros_like(dgm_ref)

        qv, kv, qtv, vv, dov = _both(qf, qb), _both(kf, kb_), _both(qtf, qtb), _both(vf, vb), _both(dof, dob)
        s, dsn = starts_ref[:, 0], ds_scr[...]
        dm, dmt, zb = d_ref[...], dt_ref[...], zb_ref[...]
        qk_raw = _bdot(qv, kv, BNT)
        dqkd = _bdot(dov, vv, BNT)
        do_s = _bdot(dov, s, BNT)
        dkz = _bdot(vv, dsn, BNT)
        results = ((dqf, dqb, _bdot(dqkd * dm, kv, BNN) + do_s * xib_ref[...]),
                   (dkf, dkb_, _bdot(_bdot(vv, dov, BNT) * dmt, qv, BNN) + dkz * zb),
                   (dvf, dvb, _bdot(_bdot(kv, qv, BNT) * dmt, dov, BNN) + _bdot(kv * zb, dsn, BNN)))
        for ref_f, ref_b, val in results:
            ref_f[:, 0] = val[:gh]
            ref_b[:, 0] = val[gh:]
        dd_ref[...] += dqkd * qk_raw
        dxib_ref[...] += do_s * qv
        dzb_ref[...] += dkz * kv
        dgm_ref[...] += dsn * s
        ds_scr[...] = dsn * gm_ref[...] + _bdot(qtv * xr_ref[...], dov, BNN)

    cf, cb = _sweep_chunks(n, n_ctx)
    rf, rb = (lambda i: cf(n - 1 - i)), (lambda i: cb(n - 1 - i))
    pair = lambda tail: [_half_spec(gh, tail, 0, rf), _half_spec(gh, tail, 0, rb)]
    const = lambda a, b: pl.BlockSpec((g, a, b), lambda i: (0, 0, 0))
    sds = lambda *s: jax.ShapeDtypeStruct(s, F32)
    return _pcall(
        body, name="ret_bwd", grid=(n,),
        in_specs=pair((c, dk)) * 2 + pair((dk, c)) * 2 + pair((c, dv))
        + [const(c, c), const(c, c), const(c, dk), const(1, c), const(c, dk), const(dk, dv),
           pl.BlockSpec((g, 1, dk, dv), lambda i: (0, n - 1 - i, 0, 0))] + pair((c, dv)),
        out_specs=pair((c, dk)) * 2 + pair((c, dv)) + [const(c, c), const(c, dk), const(c, dk), const(dk, dv)],
        out_shape=[sds(gh, n, c, dk)] * 4 + [sds(gh, n, c, dv)] * 2 + [sds(g, c, c), sds(g, c, dk), sds(g, c, dk), sds(g, dk, dv)],
        scratch_shapes=[pltpu.VMEM((g, dk, dv), F32)],
        compiler_params=pltpu.CompilerParams(dimension_semantics=("arbitrary",)),
    )(q, q, k, k, q_t, q_t, k_t, k_t, v, v, dmat, dmat_t, xi_b, xi_r, zeta_b, gm, starts, do_f, do_b)


def _ret_forms(xi, zeta, gm, dk, dv):
    lanes = lambda t: jnp.broadcast_to(t[..., None], t.shape + (dk,))
    return lanes(xi), xi[:, None, :], lanes(zeta), zeta[:, None, :], jnp.broadcast_to(gm[:, None, None], gm.shape + (dk, dv))


@functools.partial(jax.custom_vjp, nondiff_argnums=(7,))
def ret_chunked(q, k, v, dmat, xi, zeta, gm, n_ctx):
    xi_b, _, _, zeta_r, gm_f = _ret_forms(xi, zeta, gm, q.shape[-1], v.shape[-1])
    return tuple(_ret_fwd_call(q, k, _t(k), v, dmat, xi_b, zeta_r, gm_f, n_ctx)[:2])


def _ret_chunked_fwd(q, k, v, dmat, xi, zeta, gm, n_ctx):
    xi_b, _, _, zeta_r, gm_f = _ret_forms(xi, zeta, gm, q.shape[-1], v.shape[-1])
    o_f, o_b, starts = _ret_fwd_call(q, k, _t(k), v, dmat, xi_b, zeta_r, gm_f, n_ctx)
    return (o_f, o_b), (q, k, v, dmat, xi, zeta, gm, starts)


def _ret_chunked_bwd(n_ctx, res, cts):
    q, k, v, dmat, xi, zeta, gm, starts = res
    xi_b, xi_r, zeta_b, _, gm_f = _ret_forms(xi, zeta, gm, q.shape[-1], v.shape[-1])
    dq_f, dq_b, dk_f, dk_b, dv_f, dv_b, dd, dxib, dzb, dgm = _ret_bwd_call(
        q, k, _t(q), _t(k), v, dmat, _t(dmat), xi_b, xi_r, zeta_b, gm_f, starts, cts[0], cts[1], n_ctx)
    return (dq_f + dq_b, dk_f + dk_b, dv_f + dv_b, dd, jnp.sum(dxib, axis=-1), jnp.sum(dzb, axis=-1),
            jnp.sum(dgm, axis=(1, 2)))


ret_chunked.defvjp(_ret_chunked_fwd, _ret_chunked_bwd)


def _peer(k):
    mx, my, mc = lax.axis_index("x"), lax.axis_index("y"), lax.axis_index("c")
    px = 1 - mx if k & 4 else mx
    py = 1 - my if k & 2 else my
    pc = 1 - mc if k & 1 else mc
    return (px, py, pc), 4 * px + 2 * py + pc


N_CHIP = N_DEV // 2


def _transfer_call(xs, name, plan, n_transfers, n_out):
    n_arr = len(xs)

    def body(*refs):
        x_refs, out_refs = refs[:n_arr], refs[n_arr:2 * n_arr]
        send_sems, recv_sems, local_sems = refs[2 * n_arr:]
        transfers, local = plan()
        n_tr = n_transfers
        assert len(transfers) == n_tr

        def copy(j, s, dst_slot):
            flip, src_slot, _, _ = transfers[s]
            return pltpu.make_async_remote_copy(
                src_ref=x_refs[j].at[src_slot], dst_ref=out_refs[j].at[dst_slot],
                send_sem=send_sems.at[j * n_tr + s], recv_sem=recv_sems.at[j * n_tr + s],
                device_id=_peer(flip)[0], device_id_type=pl.DeviceIdType.MESH)

        mine = []
        if local is not None:
            mine = [pltpu.make_async_copy(x_refs[j].at[local[0]], out_refs[j].at[local[1]], local_sems.at[j])
                    for j in range(n_arr)]
        for cp in mine:
            cp.start()
        sends = [copy(j, s, transfers[s][2]) for j in range(n_arr) for s in range(n_tr)]
        for cp in sends:
            cp.start()
        for j in range(n_arr):
            for s in range(n_tr):
                copy(j, s, transfers[s][3]).wait_recv()
        for cp in sends:
            cp.wait_send()
        for cp in mine:
            cp.wait()

    n_sem = n_arr * n_transfers
    return _pcall(
        body, name=name,
        in_specs=[pl.BlockSpec(memory_space=pl.ANY)] * n_arr, out_specs=[pl.BlockSpec(memory_space=pl.ANY)] * n_arr,
        out_shape=[jax.ShapeDtypeStruct((n_out,) + tuple(x.shape[1:]), x.dtype) for x in xs],
        scratch_shapes=[pltpu.SemaphoreType.DMA((n_sem,)), pltpu.SemaphoreType.DMA((n_sem,)),
                        pltpu.SemaphoreType.DMA((n_arr,))],
    )(*xs)


def _sibling_plan():
    mc = lax.axis_index("c")
    return [(1, 2 * t + (1 - mc), t, t) for t in range(N_CHIP)], None


def _chip_plan():
    my_chip = 2 * lax.axis_index("x") + lax.axis_index("y")
    transfers = []
    for flip in (2, 4, 6):
        peer_chip = _peer(flip)[1] // 2
        transfers.append((flip, peer_chip, my_chip, peer_chip))
    return transfers, (my_chip, my_chip)


def _gather_call(xs, name):
    n_arr = len(xs)
    per = N_DEV - 1
    chips = (2, 4, 6)

    def body(*refs):
        x_refs, out_refs = refs[:n_arr], refs[n_arr:2 * n_arr]
        send_sems, recv_sems, local_sems = refs[2 * n_arr:]
        me = 4 * lax.axis_index("x") + 2 * lax.axis_index("y") + lax.axis_index("c")
        sib_dev, sib_idx = _peer(1)

        def copy(j, s, src, slot, dev):
            return pltpu.make_async_remote_copy(
                src_ref=src, dst_ref=out_refs[j].at[slot],
                send_sem=send_sems.at[j * per + s], recv_sem=recv_sems.at[j * per + s],
                device_id=dev, device_id_type=pl.DeviceIdType.MESH)

        mine = [pltpu.make_async_copy(x_refs[j], out_refs[j].at[me], local_sems.at[j]) for j in range(n_arr)]
        for cp in mine:
            cp.start()
        sends = []
        for j in range(n_arr):
            sends.append(copy(j, 0, x_refs[j], me, sib_dev))
            for t, k in enumerate(chips):
                sends.append(copy(j, 1 + t, x_refs[j], me, _peer(k)[0]))
        for cp in sends:
            cp.start()
        for j in range(n_arr):
            for t, k in enumerate(chips):
                dev, idx = _peer(k)
                copy(j, 1 + t, x_refs[j], idx, dev).wait_recv()
                forward = copy(j, 4 + t, out_refs[j].at[idx], idx, sib_dev)
                forward.start()
                sends.append(forward)
        for j in range(n_arr):
            copy(j, 0, x_refs[j], sib_idx, sib_dev).wait_recv()
            for t, k in enumerate(chips):
                idx = _peer(k + 1)[1]
                copy(j, 4 + t, out_refs[j].at[idx], idx, sib_dev).wait_recv()
        for cp in sends:
            cp.wait_send()
        for cp in mine:
            cp.wait()

    return _pcall(
        body, name=name,
        in_specs=[pl.BlockSpec(memory_space=pl.ANY)] * n_arr, out_specs=[pl.BlockSpec(memory_space=pl.ANY)] * n_arr,
        out_shape=[jax.ShapeDtypeStruct((N_DEV,) + tuple(x.shape), x.dtype) for x in xs],
        scratch_shapes=[pltpu.SemaphoreType.DMA((n_arr * per,)), pltpu.SemaphoreType.DMA((n_arr * per,)),
                        pltpu.SemaphoreType.DMA((n_arr,))],
    )(*xs)


def _sum_slots_call(x, name):
    n_slots, r, c = x.shape
    tr = _pick(r, (256, 160, 128, 72, 64, 32, 16, 8))

    def body(x_ref, o_ref):
        acc = x_ref[0].astype(F32)
        for d in range(1, n_slots):
            acc = acc + x_ref[d].astype(F32)
        o_ref[...] = acc

    return _pcall(
        body, name=name, grid=(r // tr,),
        in_specs=[pl.BlockSpec((n_slots, tr, c), lambda i: (0, i, 0))],
        out_specs=pl.BlockSpec((tr, c), lambda i: (i, 0)),
        out_shape=jax.ShapeDtypeStruct((r, c), F32),
        compiler_params=pltpu.CompilerParams(dimension_semantics=("parallel",)),
    )(x)


def _pair_add_call(a, b, name):
    n_slots, r, c = a.shape
    tr = _pick(r, (256, 160, 128, 64, 32, 16))

    def body(a_ref, b_ref, o_ref):
        o_ref[...] = (a_ref[...].astype(F32) + b_ref[...].astype(F32)).astype(BF16)

    spec = pl.BlockSpec((n_slots, tr, c), lambda i: (0, i, 0))
    return _pcall(
        body, name=name, grid=(r // tr,), in_specs=[spec, spec], out_specs=spec,
        out_shape=jax.ShapeDtypeStruct(a.shape, BF16),
        compiler_params=pltpu.CompilerParams(dimension_semantics=("parallel",)),
    )(a, b)


def _adamw_call(w, g, m, v, name):
    r, c = w.shape
    tr = _pick(r, (256, 128, 64, 32, 16, 8))
    bc1 = 1.0 - ADAM_B1 ** ADAM_STEP
    bc2 = 1.0 - ADAM_B2 ** ADAM_STEP

    def body(w_ref, g_ref, m_ref, v_ref, d_ref, nm_ref, nv_ref):
        gv = g_ref[...]
        nm = ADAM_B1 * m_ref[...] + (1.0 - ADAM_B1) * gv
        nv = ADAM_B2 * v_ref[...] + (1.0 - ADAM_B2) * jnp.square(gv)
        d_ref[...] = -ADAM_LR * ((nm / bc1) / (jnp.sqrt(nv / bc2) + ADAM_EPS) + ADAM_WD * w_ref[...])
        nm_ref[...] = nm
        nv_ref[...] = nv

    spec = pl.BlockSpec((tr, c), lambda i: (i, 0))
    sds = jax.ShapeDtypeStruct((r, c), F32)
    return _pcall(
        body, name=name, grid=(r // tr,), in_specs=[spec] * 4, out_specs=[spec] * 3, out_shape=[sds] * 3,
        compiler_params=pltpu.CompilerParams(dimension_semantics=("parallel",)),
    )(w, g, m, v)


def rms_norm(x, g):
    return x * lax.rsqrt(jnp.mean(x * x, axis=-1, keepdims=True) + NORM_EPS) * g


def head_layer_norm(o, g):
    b_, t_, h_, d_ = o.shape
    mu = jnp.mean(o, axis=-1, keepdims=True)
    var = jnp.mean(jnp.square(o - mu), axis=-1, keepdims=True)
    return ((o - mu) * lax.rsqrt(var + NORM_EPS)).reshape(b_, t_, h_ * d_) * g


def l2norm(t):
    return t * lax.rsqrt(jnp.sum(t * t, axis=-1, keepdims=True) + NORM_EPS)


def rope_freqs(dim):
    return ROPE_BASE ** (-jnp.arange(0, dim, 2, dtype=F32) / dim)


def axial_rope(rows, rot_dim):
    row = jnp.broadcast_to(jnp.arange(rows, dtype=F32)[:, None], (rows, GRID_W)).reshape(-1)
    col = jnp.broadcast_to(jnp.arange(GRID_W, dtype=F32)[None, :], (rows, GRID_W)).reshape(-1)
    inv = rope_freqs(rot_dim // 2)
    ang = jnp.concatenate([row[:, None] * inv, col[:, None] * inv], axis=-1)
    return jnp.cos(ang), jnp.sin(ang)


def sequence_rope(n_tok, rot_dim):
    ang = jnp.arange(n_tok, dtype=F32)[:, None] * rope_freqs(rot_dim)
    return jnp.cos(ang), jnp.sin(ang)


def apply_rope(x, cos, sin):
    x1, x2 = jnp.split(x, 2, axis=-1)
    c = cos[:, None, :]
    s = sin[:, None, :]
    return jnp.concatenate([x1 * c - x2 * s, x1 * s + x2 * c], axis=-1)


_AB_START = sum(IN_SPLITS[:5])
_AB_END = _AB_START + IN_SPLITS[5]
MOVED_SPLITS = IN_SPLITS[:5] + IN_SPLITS[6:] + IN_SPLITS[5:6]


def _gates_last(w):
    return jnp.concatenate([w[..., :_AB_START], w[..., _AB_END:IN_WIDTH], w[..., _AB_START:_AB_END]], axis=-1)


def _gates_back(w):
    moved = IN_WIDTH - (_AB_END - _AB_START)
    return jnp.concatenate([w[..., :_AB_START], w[..., moved:IN_WIDTH], w[..., _AB_START:moved]], axis=-1)


def _split_columns(z):
    idx = np.cumsum(np.array(MOVED_SPLITS))[:-1].tolist()
    parts = jnp.split(z, idx, axis=-1)
    return parts[:5] + parts[12:] + parts[5:12]


CONV_ROW_TILE = 512
CONV_HALO = 8


def _conv_windows(prev_ref, x_ref, next_ref, n_tiles, taps):
    i = pl.program_id(1)
    tr = x_ref.shape[1]
    prev = jnp.where(i == 0, 0.0, prev_ref[0])
    nxt = jnp.where(i == n_tiles - 1, 0.0, next_ref[0])
    xx = jnp.concatenate([prev, x_ref[0], nxt], axis=0)
    rows = tr + 2 * CONV_HALO
    pad = taps // 2
    return [pltpu.roll(xx, (pad - k) % rows, 0)[CONV_HALO:CONV_HALO + tr] for k in range(taps)]


def _conv_specs(t_len, tr, ch):
    per = tr // CONV_HALO
    last = t_len // CONV_HALO - 1
    return [pl.BlockSpec((1, CONV_HALO, ch), lambda b, i: (b, jnp.maximum(i * per - 1, 0), 0)),
            pl.BlockSpec((1, tr, ch), lambda b, i: (b, i, 0)),
            pl.BlockSpec((1, CONV_HALO, ch), lambda b, i: (b, jnp.minimum((i + 1) * per, last), 0))]


def _conv_fwd_call(x, w, taps):
    b_, t_len, ch = x.shape
    tr = min(CONV_ROW_TILE, t_len)
    n_tiles = t_len // tr

    def body(prev_ref, x_ref, next_ref, w_ref, o_ref):
        wins = _conv_windows(prev_ref, x_ref, next_ref, n_tiles, taps)
        acc = wins[0] * w_ref[0:1, :]
        for k in range(1, taps):
            acc = acc + wins[k] * w_ref[k:k + 1, :]
        o_ref[0] = acc

    return _pcall(
        body, name="conv_fwd", grid=(b_, n_tiles),
        in_specs=_conv_specs(t_len, tr, ch) + [pl.BlockSpec((8, ch), lambda b, i: (0, 0))],
        out_specs=pl.BlockSpec((1, tr, ch), lambda b, i: (b, i, 0)),
        out_shape=jax.ShapeDtypeStruct(x.shape, F32),
        compiler_params=pltpu.CompilerParams(dimension_semantics=("parallel", "parallel")),
    )(x, x, x, w)


def _conv_dw_call(x, du, taps):
    b_, t_len, ch = x.shape
    tr = min(CONV_ROW_TILE, t_len)
    n_tiles = t_len // tr

    def body(prev_ref, x_ref, next_ref, du_ref, dw_ref):
        wins = _conv_windows(prev_ref, x_ref, next_ref, n_tiles, taps)
        duv = du_ref[0]
        rows = [jnp.sum(duv * wins[k], axis=0, keepdims=True) for k in range(taps)]
        part = jnp.concatenate(rows + [jnp.zeros((8 - taps, ch), F32)], axis=0)

        @pl.when((pl.program_id(0) == 0) & (pl.program_id(1) == 0))
        def _():
            dw_ref[...] = jnp.zeros_like(dw_ref)

        dw_ref[...] += part

    return _pcall(
        body, name="conv_dw", grid=(b_, n_tiles),
        in_specs=_conv_specs(t_len, tr, ch) + [pl.BlockSpec((1, tr, ch), lambda b, i: (b, i, 0))],
        out_specs=pl.BlockSpec((8, ch), lambda b, i: (0, 0)),
        out_shape=jax.ShapeDtypeStruct((8, ch), F32),
        compiler_params=pltpu.CompilerParams(dimension_semantics=("arbitrary", "arbitrary")),
    )(x, x, x, du)


def _pad_taps(w):
    return jnp.concatenate([w, jnp.zeros((8 - w.shape[0], w.shape[1]), F32)], axis=0)


@jax.custom_vjp
def short_conv(x, w):
    return _conv_fwd_call(x, _pad_taps(w), w.shape[0])


def _short_conv_fwd(x, w):
    return _conv_fwd_call(x, _pad_taps(w), w.shape[0]), (x, w)


def _short_conv_bwd(res, du):
    x, w = res
    taps = w.shape[0]
    return _conv_fwd_call(du, _pad_taps(jnp.flip(w, axis=0)), taps), _conv_dw_call(x, du, taps)[:taps]


short_conv.defvjp(_short_conv_fwd, _short_conv_bwd)


def _to_heads(t, h, d):
    b_, t_, _ = t.shape
    return t.reshape(b_, t_, h, d).transpose(0, 2, 1, 3).reshape(b_ * h, t_, d)


def _from_heads(t, b_):
    g, t_, d = t.shape
    return t.reshape(b_, g // b_, t_, d).transpose(0, 2, 1, 3).reshape(b_, t_, (g // b_) * d)


def _lane_scalar(vals):
    return jnp.broadcast_to(vals[:, None, None], (vals.shape[0], 1, 128))


def swa_group(q, k, v, qc, kc, vc, sink, cos, sin, with_ctx_out):
    b_, s_, _ = q.shape
    l_ = kc.shape[1]
    grp = SWA_HEADS // SWA_KV_HEADS
    d = SWA_HEAD_DIM
    w_ = SWA_BLOCK
    scale = d ** -0.5
    qh = apply_rope(q.reshape(b_, s_, SWA_HEADS, d), cos, sin).transpose(0, 2, 1, 3).reshape(b_ * SWA_HEADS, s_, d)
    kh = apply_rope(k.reshape(b_, s_, SWA_KV_HEADS, d), cos, sin).transpose(0, 2, 1, 3).reshape(b_ * SWA_KV_HEADS, s_, d)
    vh = _to_heads(v, SWA_KV_HEADS, d)
    kch = _to_heads(kc, SWA_KV_HEADS, d)
    vch = _to_heads(vc, SWA_KV_HEADS, d)
    padk = lambda t: jnp.pad(t, ((0, 0), (w_, w_), (0, 0)))
    sink_g = _lane_scalar(jnp.tile(sink, b_))
    y = _from_heads(swa_attn(qh, padk(kh), padk(vh), kch, vch, sink_g, scale), b_)
    yc = None
    if with_ctx_out:
        qch = _to_heads(qc, SWA_HEADS, d)
        rep = lambda t: jnp.repeat(t.reshape(b_, SWA_KV_HEADS, l_, d), grp, axis=1).reshape(b_ * SWA_HEADS, l_, d)
        yc = _from_heads(attn_full(qch, rep(kch), rep(vch), sink_g, scale, True), b_)
    return y, yc


def gated_delta_chunked(q, k, v, log_g, beta, n_ctx):
    g_, t_, dk = k.shape
    dv = v.shape[-1]
    c_ = DN_CHUNK
    n = t_ // c_
    assert dk == c_ and dv == c_
    lg = log_g.reshape(2, g_, n, c_)
    g_cum = jnp.concatenate([jnp.cumsum(lg[0], axis=-1), jnp.flip(jnp.cumsum(jnp.flip(lg[1], axis=-1), axis=-1), axis=-1)],
                            axis=0)
    o_f, o_b = dn_chunked(q.reshape(g_, n, c_, dk), k.reshape(g_, n, c_, dk), v.reshape(g_, n, c_, dv), g_cum,
                          beta.reshape(2 * g_, n, c_), n_ctx)
    return (o_f + o_b).reshape(g_, t_, dv)


def deltanet_group(qkv, z, ab, qkv_c, z_c, ab_c, conv_w, a_log, dt_bias, norm_g, with_ctx_out):
    def prep(qkv_, ab_):
        b_, t_, _ = qkv_.shape
        y = jax.nn.silu(short_conv(qkv_, conv_w))
        q, k, v = [t.reshape(b_, t_, DN_HEADS, DN_HEAD_DIM).transpose(0, 2, 1, 3) for t in jnp.split(y, 3, axis=-1)]
        q = l2norm(q) * DN_HEAD_DIM ** -0.5
        k = l2norm(k)
        ab_ = ab_.reshape(b_, t_, 2, 2, DN_HEADS)
        log_g = -jnp.exp(a_log) * jax.nn.softplus(ab_[:, :, :, 0] + dt_bias)
        beta = jax.nn.sigmoid(ab_[:, :, :, 1])
        return q, k, v, log_g.transpose(2, 0, 3, 1), beta.transpose(2, 0, 3, 1)

    def out(o, z_):
        b_, t_, _ = z_.shape
        o = rms_norm(o.transpose(0, 2, 1, 3), norm_g) * jax.nn.silu(z_).reshape(b_, t_, DN_HEADS, DN_HEAD_DIM)
        return o.reshape(b_, t_, DN_W)

    qc, kc, vc, lgc, bc = prep(qkv_c, ab_c)
    q, k, v, lg, bt = prep(qkv, ab)
    b_, l_, s_ = qkv.shape[0], qkv_c.shape[1], qkv.shape[1]
    seq = lambda tc, tl: jnp.concatenate([tc, tl], axis=2).reshape((b_ * DN_HEADS, l_ + s_) + tc.shape[3:])
    seq_g = lambda tc, tl: jnp.concatenate([tc, tl], axis=3).reshape(2, b_ * DN_HEADS, l_ + s_)
    o = gated_delta_chunked(seq(qc, q), seq(kc, k), seq(vc, v), seq_g(lgc, lg), seq_g(bc, bt), l_ // DN_CHUNK)
    o = o.reshape(b_, DN_HEADS, l_ + s_, DN_HEAD_DIM)
    y = out(o[:, :, l_:], z)
    yc = out(o[:, :, :l_], z_c) if with_ctx_out else None
    return y, yc


def retention_core(q, k, v, log_gamma, n_ctx):
    b_, h_, t_, dk = q.shape
    dv = v.shape[-1]
    c_ = RET_CHUNK
    n = t_ // c_
    gh = b_ * h_
    fwd_rank = jnp.arange(c_, dtype=F32)
    rank = jnp.stack([fwd_rank, c_ - 1 - fwd_rank])[:, None, :]
    lg = log_gamma[..., None]
    per_g = lambda t: jnp.broadcast_to(t[:, None], (2, b_) + t.shape[1:]).reshape((2 * gh,) + t.shape[2:])
    zeta = jnp.exp((c_ - 1 - rank) * lg)
    xi = jnp.exp((rank + 1.0) * lg)
    rel = rank[..., :, None] - rank[..., None, :]
    dmat = jnp.where(rel >= 0, jnp.exp(jnp.maximum(rel, 0.0) * lg[..., None]), 0.0)
    gm = jnp.exp(c_ * log_gamma)
    o_f, o_b = ret_chunked(q.reshape(gh, n, c_, dk), k.reshape(gh, n, c_, dk), v.reshape(gh, n, c_, dv),
                           per_g(dmat), per_g(xi), per_g(zeta), per_g(gm), n_ctx)
    return (o_f + o_b).reshape(b_, h_, t_, dv)


def retention_group(q, k, v, g, qc, kc, vc, gc, log1m_gamma, norm_g, cos, sin, with_ctx_out):
    log_gamma = jnp.log1p(-jnp.exp(log1m_gamma))
    heads = lambda t, dh: t.reshape(t.shape[0], t.shape[1], RET_HEADS, dh)
    bhtd = lambda t: t.transpose(0, 2, 1, 3)
    sc = RET_QK_DIM ** -0.5
    l_ = kc.shape[1]
    q = bhtd(apply_rope(heads(q, RET_QK_DIM), cos, sin)) * sc
    k = bhtd(apply_rope(heads(k, RET_QK_DIM), cos, sin))
    v = bhtd(heads(v, RET_V_DIM))
    kc = bhtd(heads(kc, RET_QK_DIM))
    vc = bhtd(heads(vc, RET_V_DIM))
    qcs = bhtd(heads(qc, RET_QK_DIM)) * sc

    def out(o, g_):
        return head_layer_norm(o.transpose(0, 2, 1, 3), norm_g) * jax.nn.silu(g_)

    seq = lambda tc, tl: jnp.concatenate([tc, tl], axis=2)
    o = retention_core(seq(qcs, q), seq(kc, k), seq(vc, v), log_gamma, l_ // RET_CHUNK)
    y = out(o[:, :, l_:], g)
    yc = out(o[:, :, :l_], gc) if with_ctx_out else None
    return y, yc


def mla_group(cq, ckv, kr, cq_c, ckv_c, kr_c, q_norm, w_uq, kv_norm, w_ukv, cos, sin, with_ctx_out):
    b_, s_, _ = cq.shape
    l_ = cq_c.shape[1]
    dqk = MLA_NOPE_DIM + MLA_ROPE_DIM
    rows = lambda tl, tc: jnp.concatenate([tl.reshape(b_ * s_, -1), tc.reshape(b_ * l_, -1)], axis=0)
    qa = matmul(rms_norm(rows(cq, cq_c), q_norm), w_uq)
    kva = matmul(rms_norm(rows(ckv, ckv_c), kv_norm), w_ukv)
    q = qa[:b_ * s_].reshape(b_, s_, MLA_HEADS, dqk)
    qc = qa[b_ * s_:].reshape(b_, l_, MLA_HEADS, dqk)
    q = jnp.concatenate([q[..., :MLA_NOPE_DIM], apply_rope(q[..., MLA_NOPE_DIM:], cos, sin)], axis=-1)
    kv = kva[:b_ * s_].reshape(b_, s_, MLA_HEADS, MLA_NOPE_DIM + MLA_V_DIM)
    kvc = kva[b_ * s_:].reshape(b_, l_, MLA_HEADS, MLA_NOPE_DIM + MLA_V_DIM)
    kr = apply_rope(kr[:, :, None, :], cos, sin)
    k = jnp.concatenate([kv[..., :MLA_NOPE_DIM], jnp.broadcast_to(kr, (b_, s_, MLA_HEADS, MLA_ROPE_DIM))], axis=-1)
    kc = jnp.concatenate([kvc[..., :MLA_NOPE_DIM],
                          jnp.broadcast_to(kr_c[:, :, None, :], (b_, l_, MLA_HEADS, MLA_ROPE_DIM))], axis=-1)
    v, vc = kv[..., MLA_NOPE_DIM:], kvc[..., MLA_NOPE_DIM:]
    hd = lambda t: t.transpose(0, 2, 1, 3).reshape(b_ * MLA_HEADS, t.shape[1], t.shape[3])
    scale = dqk ** -0.5
    no_sink = jnp.zeros((b_ * MLA_HEADS, 1, 128), F32)
    kch, vch = hd(kc), hd(vc)
    y = attn_full(hd(q), jnp.concatenate([hd(k), kch], axis=1), jnp.concatenate([hd(v), vch], axis=1), no_sink, scale, False)
    y = _from_heads(y, b_)
    yc = _from_heads(attn_full(hd(qc), kch, vch, no_sink, scale, False), b_) if with_ctx_out else None
    return y, yc


def token_mixers(zl, zc, p, layer, rope, with_ctx_out):
    (a_q, a_k, a_v, b_qkv, b_z, b_ab, c_q, c_k, c_v, c_g, d_cq, d_ckv, d_kr) = _split_columns(zl)
    (a_qc, a_kc, a_vc, b_qkvc, b_zc, b_abc, c_qc, c_kc, c_vc, c_gc, d_cqc, d_ckvc, d_krc) = _split_columns(zc)
    swa_cos, swa_sin, ret_cos, ret_sin, mla_cos, mla_sin = rope
    ya, yac = swa_group(a_q, a_k, a_v, a_qc, a_kc, a_vc, p['swa_sink'][layer], swa_cos, swa_sin, with_ctx_out)
    yb, ybc = deltanet_group(b_qkv, b_z, b_ab, b_qkvc, b_zc, b_abc, p['dn_conv_w'][layer], p['dn_a_log'][layer],
                             p['dn_dt_bias'][layer], p['dn_norm_g'][layer], with_ctx_out)
    yr, yrc = retention_group(c_q, c_k, c_v, c_g, c_qc, c_kc, c_vc, c_gc, p['ret_log1m_gamma'][layer],
                              p['ret_norm_g'][layer], ret_cos, ret_sin, with_ctx_out)
    yd, ydc = mla_group(d_cq, d_ckv, d_kr, d_cqc, d_ckvc, d_krc, p['mla_q_norm'][layer], p['mla_w_uq'][layer],
                        p['mla_kv_norm'][layer], p['mla_w_ukv'][layer], mla_cos, mla_sin, with_ctx_out)
    y = jnp.concatenate([ya, yb, yr, yd], axis=-1)
    yc = jnp.concatenate([yac, ybc, yrc, ydc], axis=-1) if with_ctx_out else None
    return y, yc


def local_loss(p, x, ctx, loss_target):
    b_, n_tok, d_ = x.shape
    l_ = ctx.shape[1]
    rows = n_tok // GRID_W
    rope = (*axial_rope(rows, SWA_HEAD_DIM), *sequence_rope(n_tok, RET_QK_DIM), *axial_rope(rows, MLA_ROPE_DIM))
    rl, rc = b_ * n_tok, b_ * l_
    mods = [jnp.concatenate([p['mod'][layer], p['cmod'][layer][None]], axis=0) for layer in range(DEPTH)]
    part = lambda layer, j: mods[layer][:, j * d_:(j + 1) * d_][:, None, :]
    vec = lambda name, layer: p[name][layer][None, :]
    xr = jnp.concatenate([x.reshape(rl, d_), ctx.reshape(rc, d_)], axis=0)
    sh1, sc1 = part(0, 0), part(0, 1)
    h = jnp.concatenate([(x * (1 + sc1[:b_]) + sh1[:b_]).reshape(rl, d_), (ctx * (1 + sc1[b_]) + sh1[b_]).reshape(rc, d_)],
                        axis=0)
    for layer in range(DEPTH):
        with_ctx_out = layer < DEPTH - 1
        g1, sh2, sc2, g2 = part(layer, 2), part(layer, 3), part(layer, 4), part(layer, 5)
        z = matmul(h, p['w_in'][layer])
        zl = z[:rl, :IN_WIDTH].reshape(b_, n_tok, IN_WIDTH)
        zc = z[rl:, :IN_WIDTH].reshape(b_, l_, IN_WIDTH)
        y, yc = token_mixers(zl, zc, p, layer, rope, with_ctx_out)
        if with_ctx_out:
            yo = matmul(jnp.concatenate([y.reshape(rl, d_), yc.reshape(rc, d_)], axis=0), p['w_out'][layer])
            xr, h2 = ln_mod(xr, yo, g1, vec('ln1_g', layer), vec('ln1_b', layer), sc2, sh2, n_tok)
            f = matmul_relu2(matmul(h2, p['w_ff1'][layer]), p['w_ff2'][layer])
            xr, h = ln_mod(xr, f, g2, vec('ln2_g', layer), vec('ln2_b', layer), part(layer + 1, 1), part(layer + 1, 0), n_tok)
        else:
            lat = lambda t: t[:b_]
            yo = matmul(y.reshape(rl, d_), p['w_out'][layer])
            xl, h2 = ln_mod(xr[:rl], yo, lat(g1), vec('ln1_g', layer), vec('ln1_b', layer), lat(sc2), lat(sh2), n_tok)
            f = matmul_relu2(matmul(h2, p['w_ff1'][layer]), p['w_ff2'][layer])
            none = jnp.zeros((b_, 1, d_), F32)
            xl, _ = ln_mod(xl, f, lat(g2), vec('ln2_g', layer), vec('ln2_b', layer), none, none, n_tok)
    err = jnp.square(xl - loss_target.reshape(rl, d_))
    return 0.5 * jnp.sum(jnp.mean(err, axis=-1))


def _shard_shape(shape, axis):
    s = list(shape)
    s[axis] //= N_DEV
    return tuple(s)


def _join_shards(pieces, axis):
    _, _, r, c = pieces.shape
    if axis == 0:
        full = pieces.transpose(1, 0, 2, 3).reshape(DEPTH, N_DEV * r, c)
    else:
        full = pieces.transpose(1, 2, 0, 3).reshape(DEPTH, r, N_DEV * c)
    return full.astype(F32)


def _split_shards(g, shape, axis):
    r, c = _shard_shape(shape, axis)
    if axis == 0:
        pieces = g.reshape(DEPTH, N_DEV, r, c).transpose(1, 0, 2, 3)
    else:
        pieces = g.reshape(DEPTH, r, N_DEV, c).transpose(2, 0, 1, 3)
    return pieces.astype(BF16)


def _pad_vec(vec, rows_multiple=8):
    n = vec.shape[0]
    rows = -(-n // (128 * rows_multiple)) * rows_multiple
    return jnp.pad(vec, (0, rows * 128 - n)).reshape(rows, 128)


def _adamw(w, g, m, v, name):
    shape = w.shape
    if w.ndim >= 2 and shape[-1] >= 128:
        as2 = lambda t: t.reshape(-1, shape[-1])
        d, nm, nv = _adamw_call(as2(w), as2(g), as2(m), as2(v), name)
        return d.reshape(shape), nm.reshape(shape), nv.reshape(shape)
    n = int(np.prod(shape))
    as2 = lambda t: _pad_vec(t.reshape(-1))
    d, nm, nv = _adamw_call(as2(w), as2(g), as2(m), as2(v), name)
    un = lambda t: t.reshape(-1)[:n].reshape(shape)
    return un(d), un(nm), un(nv)


def kernel(x, c, ctx, c_ctx, ada_w, ada_b, w_in, swa_sink, dn_conv_w, dn_a_log, dn_dt_bias, dn_norm_g, ret_log1m_gamma, ret_norm_g, mla_q_norm, mla_w_uq, mla_kv_norm, mla_w_ukv, w_out, ln1_g, ln1_b, w_ff1, w_ff2, ln2_g, ln2_b, loss_target, m_c_ctx, m_ada_w, m_ada_b, m_w_in, m_swa_sink, m_dn_conv_w, m_dn_a_log, m_dn_dt_bias, m_dn_norm_g, m_ret_log1m_gamma, m_ret_norm_g, m_mla_q_norm, m_mla_w_uq, m_mla_kv_norm, m_mla_w_ukv, m_w_out, m_ln1_g, m_ln1_b, m_w_ff1, m_w_ff2, m_ln2_g, m_ln2_b, v_c_ctx, v_ada_w, v_ada_b, v_w_in, v_swa_sink, v_dn_conv_w, v_dn_a_log, v_dn_dt_bias, v_dn_norm_g, v_ret_log1m_gamma, v_ret_norm_g, v_mla_q_norm, v_mla_w_uq, v_mla_kv_norm, v_mla_w_ukv, v_w_out, v_ln1_g, v_ln1_b, v_w_ff1, v_w_ff2, v_ln2_g, v_ln2_b):
    a = dict(zip(ARG_NAMES, (x, c, ctx, c_ctx, ada_w, ada_b, w_in, swa_sink, dn_conv_w, dn_a_log, dn_dt_bias, dn_norm_g, ret_log1m_gamma, ret_norm_g, mla_q_norm, mla_w_uq, mla_kv_norm, mla_w_ukv, w_out, ln1_g, ln1_b, w_ff1, w_ff2, ln2_g, ln2_b, loss_target, m_c_ctx, m_ada_w, m_ada_b, m_w_in, m_swa_sink, m_dn_conv_w, m_dn_a_log, m_dn_dt_bias, m_dn_norm_g, m_ret_log1m_gamma, m_ret_norm_g, m_mla_q_norm, m_mla_w_uq, m_mla_kv_norm, m_mla_w_ukv, m_w_out, m_ln1_g, m_ln1_b, m_w_ff1, m_w_ff2, m_ln2_g, m_ln2_b, v_c_ctx, v_ada_w, v_ada_b, v_w_in, v_swa_sink, v_dn_conv_w, v_dn_a_log, v_dn_dt_bias, v_dn_norm_g, v_ret_log1m_gamma, v_ret_norm_g, v_mla_q_norm, v_mla_w_uq, v_mla_kv_norm, v_mla_w_ukv, v_w_out, v_ln1_g, v_ln1_b, v_w_ff1, v_w_ff2, v_ln2_g, v_ln2_b)))
    me = 4 * lax.axis_index("x") + 2 * lax.axis_index("y") + lax.axis_index("c")
    b_loc = x.shape[0]
    n_ex = N_DEV * b_loc
    conv_k, conv_c = dn_conv_w.shape[1], dn_conv_w.shape[2]
    ada_cols = ada_w.shape[2]

    small_in = jnp.concatenate([c.reshape(-1), dn_conv_w.reshape(-1)])
    gathered = _gather_call([_pad_vec(small_in)] + [a[name].astype(BF16) for name, _, _ in BIG], "gather_weights")
    small_all = gathered[0].reshape(N_DEV, -1)
    c_all = small_all[:, :b_loc * D_MODEL].reshape(n_ex, D_MODEL)
    conv_all = small_all[:, b_loc * D_MODEL:b_loc * D_MODEL + DEPTH * conv_k * conv_c].reshape(N_DEV, DEPTH, conv_k, conv_c)
    conv_full = conv_all.transpose(1, 2, 0, 3).reshape(DEPTH, conv_k, N_DEV * conv_c)
    big = {name: _join_shards(pieces, axis) for (name, _, axis), pieces in zip(BIG, gathered[1:])}
    big['w_in'] = jnp.pad(_gates_last(big['w_in']), ((0, 0), (0, 0), (0, IN_WIDTH_PAD - IN_WIDTH)))

    n_rows = -(-(n_ex + 1) // 16) * 16
    silu_cc = jax.nn.silu(c_ctx)
    a_rows = jnp.concatenate([jax.nn.silu(c_all), silu_cc[None], jnp.zeros((n_rows - n_ex - 1, D_MODEL), F32)], axis=0)
    m_loc = jnp.concatenate([_mm_call(a_rows, ada_w[l], False, "ada_fwd") for l in range(DEPTH)], axis=0)
    m_all = _gather_call([m_loc], "gather_mod")[0].reshape(N_DEV, DEPTH, n_rows, ada_cols)
    mod_full = m_all.transpose(1, 2, 0, 3).reshape(DEPTH, n_rows, N_DEV * ada_cols) + ada_b[:, None, :]
    mod = lax.dynamic_slice_in_dim(mod_full, me * b_loc, b_loc, axis=1)
    cmod = mod_full[:, n_ex]

    p = dict(big)
    p.update(mod=mod, cmod=cmod, dn_conv_w=conv_full)
    for name in SMALL:
        p[name] = a[name]
    loss_loc, (gp, gx) = jax.value_and_grad(local_loss, argnums=(0, 1))(p, x, ctx, loss_target)
    loss = lax.psum(loss_loc, MESH_AXES)

    gp['w_in'] = _gates_back(gp['w_in'][:, :, :IN_WIDTH])
    my_core = lax.axis_index("c")
    parts = [_split_shards(gp[name], shape, axis) for name, shape, axis in BIG]
    from_sibling = _transfer_call(parts, "scatter_sibling", _sibling_plan, N_CHIP, N_CHIP)
    chip_parts = []
    for (name, shape, axis), part, other in zip(BIG, parts, from_sibling):
        r, c_ = _shard_shape(shape, axis)
        mine = lax.dynamic_index_in_dim(part.reshape(N_CHIP, 2, DEPTH * r, c_), my_core, axis=1, keepdims=False)
        chip_parts.append(_pair_add_call(mine, other.reshape(N_CHIP, DEPTH * r, c_), "pair_" + name))
    arrived = _transfer_call(chip_parts, "scatter_chips", _chip_plan, N_CHIP - 1, N_CHIP)
    g_big = {}
    for (name, shape, axis), part in zip(BIG, arrived):
        r, c_ = _shard_shape(shape, axis)
        g_big[name] = _sum_slots_call(part, "sum_" + name).reshape(DEPTH, r, c_)

    d_loc = jnp.concatenate([gp['mod'], gp['cmod'][:, None, :]], axis=1).reshape(DEPTH * (b_loc + 1), -1)
    d_loc = jnp.pad(d_loc, ((0, 8 - DEPTH * (b_loc + 1)), (0, 0)))
    d_all = _gather_call([d_loc], "gather_dmod")[0][:, :DEPTH * (b_loc + 1)].reshape(N_DEV, DEPTH, b_loc + 1, -1)
    d_rows = d_all[:, :, :b_loc].transpose(1, 0, 2, 3).reshape(DEPTH, n_ex, -1)
    d_crow = d_all[0, :, b_loc]
    for d in range(1, N_DEV):
        d_crow = d_crow + d_all[d, :, b_loc]
    dm_full = jnp.concatenate([d_rows, d_crow[:, None, :], jnp.zeros((DEPTH, n_rows - n_ex - 1, d_rows.shape[-1]), F32)], axis=1)
    g_ada_b = jnp.sum(dm_full, axis=1)
    dm_mine = lax.dynamic_slice_in_dim(dm_full, me * ada_cols, ada_cols, axis=2)
    g_ada_w = jnp.stack([_mm_call(a_rows, dm_mine[l], True, "ada_bwd_w") for l in range(DEPTH)])
    crow8 = jnp.concatenate([dm_mine[:, n_ex:n_ex + 1], jnp.zeros((DEPTH, 15, ada_cols), F32)], axis=1)
    dsilu_part = sum(_mm_call(crow8[l], jnp.transpose(ada_w[l]), False, "ada_bwd_c")[0] for l in range(DEPTH))

    small_g = jnp.concatenate([gp[name].reshape(-1) for name in SMALL] + [gp['dn_conv_w'].reshape(-1), dsilu_part])
    small_sum = _sum_slots_call(_gather_call([_pad_vec(small_g)], "gather_small_grads")[0], "sum_small_grads").reshape(-1)
    g_all, off = {}, 0
    for name in SMALL:
        n = int(np.prod(a[name].shape))
        g_all[name] = small_sum[off:off + n].reshape(a[name].shape)
        off += n
    n = DEPTH * conv_k * N_DEV * conv_c
    g_conv_full = small_sum[off:off + n].reshape(DEPTH, conv_k, N_DEV * conv_c)
    g_all['dn_conv_w'] = lax.dynamic_slice_in_dim(g_conv_full, me * conv_c, conv_c, axis=2)
    off += n
    dsilu = small_sum[off:off + D_MODEL]
    sig = jax.nn.sigmoid(c_ctx)
    g_all['c_ctx'] = dsilu * (sig * (1 + c_ctx * (1 - sig)))
    g_all['ada_w'] = g_ada_w
    g_all['ada_b'] = g_ada_b
    g_all.update(g_big)

    delta, new_m, new_v = {}, {}, {}
    for name in WEIGHTS:
        delta[name], new_m[name], new_v[name] = _adamw(a[name], g_all[name], a['m_' + name], a['v_' + name], "adamw_" + name)
    return (loss, gx, *[g_all[n] for n in WEIGHTS], *[delta[n] for n in WEIGHTS],
            *[new_m[n] for n in WEIGHTS], *[new_v[n] for n in WEIGHTS])
```

```python
import functools

import jax
import jax.numpy as jnp
import numpy as np
from jax import lax
from jax.experimental import pallas as pl
from jax.experimental.pallas import tpu as pltpu

F32 = jnp.float32
BF16 = jnp.bfloat16
N_DEV = 8
MESH_AXES = ("x", "y", "c")

D_MODEL = 1024
DEPTH = 2
GRID_W = 64
SWA_HEADS, SWA_KV_HEADS, SWA_HEAD_DIM, SWA_WINDOW, SWA_BLOCK = 4, 2, 64, 128, 128
DN_HEADS, DN_HEAD_DIM, DN_CHUNK = 4, 64, 64
RET_HEADS, RET_QK_DIM, RET_V_DIM, RET_CHUNK = 4, 32, 64, 64
MLA_HEADS, MLA_Q_RANK, MLA_KV_RANK, MLA_NOPE_DIM, MLA_ROPE_DIM, MLA_V_DIM = 4, 256, 128, 64, 32, 64
D_FF = 4 * D_MODEL
ROPE_BASE = 10000.0
NORM_EPS = 1e-6
LN_EPS = 1e-5
DEEPNORM_ALPHA = (2 * DEPTH) ** 0.25
SWA_Q = SWA_HEADS * SWA_HEAD_DIM
SWA_KV = SWA_KV_HEADS * SWA_HEAD_DIM
DN_W = DN_HEADS * DN_HEAD_DIM
RET_QK = RET_HEADS * RET_QK_DIM
RET_V = RET_HEADS * RET_V_DIM
IN_SPLITS = (SWA_Q, SWA_KV, SWA_KV, 3 * DN_W, DN_W, 4 * DN_HEADS, RET_QK, RET_QK, RET_V, RET_V,
             MLA_Q_RANK, MLA_KV_RANK, MLA_ROPE_DIM)
IN_WIDTH = sum(IN_SPLITS)
IN_WIDTH_PAD = -(-IN_WIDTH // 128) * 128

ADAM_LR, ADAM_B1, ADAM_B2, ADAM_EPS, ADAM_WD, ADAM_STEP = 0.001, 0.9, 0.999, 1e-08, 0.01, 10

WEIGHTS = ['c_ctx', 'ada_w', 'ada_b', 'w_in', 'swa_sink', 'dn_conv_w', 'dn_a_log', 'dn_dt_bias', 'dn_norm_g',
           'ret_log1m_gamma', 'ret_norm_g', 'mla_q_norm', 'mla_w_uq', 'mla_kv_norm', 'mla_w_ukv', 'w_out', 'ln1_g',
           'ln1_b', 'w_ff1', 'w_ff2', 'ln2_g', 'ln2_b']
FWD_INPUTS = ['x', 'c', 'ctx'] + WEIGHTS
ARG_NAMES = FWD_INPUTS + ['loss_target'] + ['m_' + n for n in WEIGHTS] + ['v_' + n for n in WEIGHTS]

BIG = (('w_in', (D_MODEL, IN_WIDTH), 1), ('w_out', (D_MODEL, D_MODEL), 0), ('w_ff1', (D_MODEL, D_FF), 1),
       ('w_ff2', (D_FF, D_MODEL), 0), ('mla_w_uq', (MLA_Q_RANK, MLA_HEADS * (MLA_NOPE_DIM + MLA_ROPE_DIM)), 1),
       ('mla_w_ukv', (MLA_KV_RANK, MLA_HEADS * (MLA_NOPE_DIM + MLA_V_DIM)), 1))
SMALL = ('swa_sink', 'dn_a_log', 'dn_dt_bias', 'dn_norm_g', 'ret_log1m_gamma', 'ret_norm_g', 'mla_q_norm',
         'mla_kv_norm', 'ln1_g', 'ln1_b', 'ln2_g', 'ln2_b')


def _pcall(body, **kw):
    return pl.pallas_call(body, **kw)


def _pick(n, cands):
    for cand in cands:
        if n % cand == 0:
            return cand
    return n


def _bdot(a, b, dims):
    return lax.dot_general(a.astype(BF16), b.astype(BF16), dims, preferred_element_type=F32)


def _lane0(t):
    return jnp.where(lax.broadcasted_iota(jnp.int32, t.shape, t.ndim - 1) == 0, t, 0.0)


NN = (((1,), (0,)), ((), ()))
NT = (((1,), (1,)), ((), ()))
TN = (((0,), (0,)), ((), ()))
BNN = (((2,), (1,)), ((0,), (0,)))
BNT = (((2,), (2,)), ((0,), (0,)))


MM_ROW_TILE_MAX = 1088
MM_COL_TILE_MAX = 1408
MM_TOKEN_TILE_MAX = 1088
VMEM_LIMIT_MAX = 60 * 1024 * 1024


def _tile(n, cap, align):
    best = None
    for t in range(align, min(n, cap) + 1, align):
        if n % t == 0:
            best = t
    return best or n


def _relu2(t):
    return jnp.square(jnp.maximum(t, 0.0))


def _mm_call(a, b, trans_a, name, act_a=False, epi=None):
    if trans_a:
        kdim, m = a.shape
        tk = _tile(kdim, MM_TOKEN_TILE_MAX, 8)
        tm = _tile(m, 1024, 128)
    else:
        m, kdim = a.shape
        tk = _tile(kdim, MM_COL_TILE_MAX, 128)
        tm = _tile(m, MM_ROW_TILE_MAX, 8)
    n = b.shape[1]
    assert b.shape[0] == kdim
    tn = _tile(n, MM_COL_TILE_MAX, 128)
    nk = kdim // tk

    def body(*refs):
        a_ref, b_ref = refs[0], refs[1]
        e_ref = refs[2] if epi is not None else None
        o_ref = refs[-1]
        k = pl.program_id(2)
        av = a_ref[...]
        if act_a:
            av = _relu2(av)
        part = _bdot(av, b_ref[...], TN if trans_a else NN)

        def finish(t):
            return t * (2.0 * jnp.maximum(e_ref[...], 0.0)) if epi is not None else t

        if nk == 1:
            o_ref[...] = finish(part)
        else:
            @pl.when(k == 0)
            def _():
                o_ref[...] = part

            @pl.when((k > 0) & (k < nk - 1))
            def _():
                o_ref[...] += part

            @pl.when(k == nk - 1)
            def _():
                o_ref[...] = finish(o_ref[...] + part)

    if trans_a:
        a_spec = pl.BlockSpec((tk, tm), lambda i, j, k: (k, i))
    else:
        a_spec = pl.BlockSpec((tm, tk), lambda i, j, k: (i, k))
    o_spec = pl.BlockSpec((tm, tn), lambda i, j, k: (i, j))
    in_specs = [a_spec, pl.BlockSpec((tk, tn), lambda i, j, k: (k, j))] + ([o_spec] if epi is not None else [])
    tiles = tm * tk * a.dtype.itemsize + tk * tn * b.dtype.itemsize + tm * tn * 4 * (2 if epi is not None else 1)
    temps = tm * tk * (2 + (4 if act_a else 0)) + tk * tn * 2 + 2 * tm * tn * 4
    return _pcall(
        body, name=name, grid=(m // tm, n // tn, nk), in_specs=in_specs, out_specs=o_spec,
        out_shape=jax.ShapeDtypeStruct((m, n), F32),
        compiler_params=pltpu.CompilerParams(dimension_semantics=("parallel", "parallel", "arbitrary"),
                                             vmem_limit_bytes=min(2 * tiles + temps + (4 << 20), VMEM_LIMIT_MAX)),
    )(*((a, b) + ((epi,) if epi is not None else ())))


@jax.custom_vjp
def matmul(a, b):
    return _mm_call(a, b.astype(BF16), False, "mm_fwd")


def _matmul_fwd(a, b):
    bb = b.astype(BF16)
    return _mm_call(a, bb, False, "mm_fwd"), (a, bb)


def _matmul_bwd(res, g):
    a, bb = res
    da = _mm_call(g, jnp.transpose(bb), False, "mm_bwd_da")
    db = _mm_call(a, g, True, "mm_bwd_db")
    return da, db


matmul.defvjp(_matmul_fwd, _matmul_bwd)


@jax.custom_vjp
def matmul_relu2(a, b):
    return _mm_call(a, b.astype(BF16), False, "mm_act_fwd", act_a=True)


def _matmul_relu2_fwd(a, b):
    bb = b.astype(BF16)
    return _mm_call(a, bb, False, "mm_act_fwd", act_a=True), (a, bb)


def _matmul_relu2_bwd(res, g):
    a, bb = res
    da = _mm_call(g, jnp.transpose(bb), False, "mm_act_bwd_da", epi=a)
    db = _mm_call(a, g, True, "mm_act_bwd_db", act_a=True)
    return da, db


matmul_relu2.defvjp(_matmul_relu2_fwd, _matmul_relu2_bwd)


LN_ROW_TILE = 256


def _ln_group_map(group_rows, n_groups):
    per = group_rows // LN_ROW_TILE
    return lambda i: (jnp.minimum(i // per, n_groups - 1), 0, 0)


def _ln_stats(x, y, gate):
    pre = DEEPNORM_ALPHA * x + gate * y
    mu = jnp.mean(pre, axis=-1, keepdims=True)
    cen = pre - mu
    rstd = lax.rsqrt(jnp.mean(jnp.square(cen), axis=-1, keepdims=True) + LN_EPS)
    return cen * rstd, rstd


def _ln_mod_fwd_call(x, y, gate, gamma, beta, sc, sh, group_rows):
    r, d = x.shape
    ng = gate.shape[0]
    gmap = _ln_group_map(group_rows, ng)

    def body(x_ref, y_ref, gate_ref, gamma_ref, beta_ref, sc_ref, sh_ref, xn_ref, h_ref):
        xh, _ = _ln_stats(x_ref[...], y_ref[...], gate_ref[0])
        xn = xh * gamma_ref[...] + beta_ref[...]
        xn_ref[...] = xn
        h_ref[...] = xn * (1.0 + sc_ref[0]) + sh_ref[0]

    row = pl.BlockSpec((LN_ROW_TILE, d), lambda i: (i, 0))
    grp = pl.BlockSpec((1, 1, d), gmap)
    vec = pl.BlockSpec((1, d), lambda i: (0, 0))
    return _pcall(
        body, name="ln_mod_fwd", grid=(r // LN_ROW_TILE,),
        in_specs=[row, row, grp, vec, vec, grp, grp], out_specs=[row, row],
        out_shape=[jax.ShapeDtypeStruct((r, d), F32)] * 2,
        compiler_params=pltpu.CompilerParams(dimension_semantics=("parallel",)),
    )(x, y, gate, gamma, beta, sc, sh)


def _ln_mod_bwd_call(x, y, gate, gamma, beta, sc, dxn, dh, group_rows):
    r, d = x.shape
    ng = gate.shape[0]
    gmap = _ln_group_map(group_rows, ng)
    per = group_rows // LN_ROW_TILE

    def body(x_ref, y_ref, gate_ref, gamma_ref, beta_ref, sc_ref, dxn_ref, dh_ref,
             dx_ref, dy_ref, dgate_ref, dgamma_ref, dbeta_ref, dsc_ref, dsh_ref):
        i = pl.program_id(0)
        yv, gate_v, gamma_v = y_ref[...], gate_ref[0], gamma_ref[...]
        xh, rstd = _ln_stats(x_ref[...], yv, gate_v)
        dhv = dh_ref[...]
        dtot = dxn_ref[...] + dhv * (1.0 + sc_ref[0])
        dxh = dtot * gamma_v
        dpre = rstd * (dxh - jnp.mean(dxh, axis=-1, keepdims=True) - xh * jnp.mean(dxh * xh, axis=-1, keepdims=True))
        dx_ref[...] = DEEPNORM_ALPHA * dpre
        dy_ref[...] = gate_v * dpre
        col = lambda t: jnp.sum(t, axis=0, keepdims=True)

        @pl.when(i == 0)
        def _():
            dgamma_ref[...] = jnp.zeros_like(dgamma_ref)
            dbeta_ref[...] = jnp.zeros_like(dbeta_ref)

        first_of_group = (i % per == 0) | (i == (ng - 1) * per)

        @pl.when(first_of_group & (i <= (ng - 1) * per))
        def _():
            dgate_ref[...] = jnp.zeros_like(dgate_ref)
            dsc_ref[...] = jnp.zeros_like(dsc_ref)
            dsh_ref[...] = jnp.zeros_like(dsh_ref)

        dgamma_ref[...] += col(dtot * xh)
        dbeta_ref[...] += col(dtot)
        dgate_ref[0] += col(dpre * yv)
        dsc_ref[0] += col(dhv * (xh * gamma_v + beta_ref[...]))
        dsh_ref[0] += col(dhv)

    row = pl.BlockSpec((LN_ROW_TILE, d), lambda i: (i, 0))
    grp = pl.BlockSpec((1, 1, d), gmap)
    vec = pl.BlockSpec((1, d), lambda i: (0, 0))
    big = jax.ShapeDtypeStruct((r, d), F32)
    gs = jax.ShapeDtypeStruct((ng, 1, d), F32)
    vs = jax.ShapeDtypeStruct((1, d), F32)
    return _pcall(
        body, name="ln_mod_bwd", grid=(r // LN_ROW_TILE,),
        in_specs=[row, row, grp, vec, vec, grp, row, row],
        out_specs=[row, row, grp, vec, vec, grp, grp],
        out_shape=[big, big, gs, vs, vs, gs, gs],
        compiler_params=pltpu.CompilerParams(dimension_semantics=("arbitrary",)),
    )(x, y, gate, gamma, beta, sc, dxn, dh)


@functools.partial(jax.custom_vjp, nondiff_argnums=(7,))
def ln_mod(x, y, gate, gamma, beta, sc, sh, group_rows):
    return tuple(_ln_mod_fwd_call(x, y, gate, gamma, beta, sc, sh, group_rows))


def _ln_mod_fwd(x, y, gate, gamma, beta, sc, sh, group_rows):
    xn, h = _ln_mod_fwd_call(x, y, gate, gamma, beta, sc, sh, group_rows)
    return (xn, h), (x, y, gate, gamma, beta, sc)


def _ln_mod_bwd(group_rows, res, cts):
    x, y, gate, gamma, beta, sc = res
    dxn, dh = cts
    return tuple(_ln_mod_bwd_call(x, y, gate, gamma, beta, sc, dxn, dh, group_rows))


ln_mod.defvjp(_ln_mod_fwd, _ln_mod_bwd)


ATTN_BWD_QUERY_BLOCK = 512


def _attn_probs(q, k, sink, scale, has_sink):
    s = _bdot(q, k, NT) * scale
    m = jnp.max(s, axis=-1, keepdims=True)
    if has_sink:
        m = jnp.maximum(m, sink)
    p = jnp.exp(s - m)
    den = jnp.sum(p, axis=-1, keepdims=True)
    p_sink = None
    if has_sink:
        p_sink = jnp.exp(sink - m)
        den = den + p_sink
    inv = 1.0 / den
    if has_sink:
        p_sink = p_sink * inv
    return p * inv, p_sink


def _attn_full_fwd_call(q, k, v, sink, scale, has_sink):
    g, sq, dq = q.shape
    nk, dv = k.shape[1], v.shape[2]
    bq = _pick(sq, (256, 128))

    def body(q_ref, k_ref, v_ref, sink_ref, o_ref):
        p, _ = _attn_probs(q_ref[0], k_ref[0], sink_ref[0, :, 0:1], scale, has_sink)
        o_ref[0] = _bdot(p, v_ref[0], NN)

    return _pcall(
        body, name="attn_full_fwd", grid=(g, sq // bq),
        in_specs=[pl.BlockSpec((1, bq, dq), lambda b, i: (b, i, 0)), pl.BlockSpec((1, nk, dq), lambda b, i: (b, 0, 0)),
                  pl.BlockSpec((1, nk, dv), lambda b, i: (b, 0, 0)), pl.BlockSpec((1, 1, 128), lambda b, i: (b, 0, 0))],
        out_specs=pl.BlockSpec((1, bq, dv), lambda b, i: (b, i, 0)),
        out_shape=jax.ShapeDtypeStruct((g, sq, dv), F32),
        compiler_params=pltpu.CompilerParams(dimension_semantics=("parallel", "arbitrary")),
    )(q, k, v, sink)


def _attn_full_bwd_call(q, k, v, sink, o, do, scale, has_sink):
    g, sq, dq = q.shape
    nk, dv = k.shape[1], v.shape[2]
    bq = _pick(sq, (ATTN_BWD_QUERY_BLOCK, 256, 128))

    def body(q_ref, k_ref, v_ref, sink_ref, o_ref, do_ref, dq_ref, dk_ref, dv_ref, dsink_ref):
        i = pl.program_id(1)
        qv, kv, vv, dov = q_ref[0], k_ref[0], v_ref[0], do_ref[0]
        p, p_sink = _attn_probs(qv, kv, sink_ref[0, :, 0:1], scale, has_sink)
        delta = jnp.sum(dov * o_ref[0], axis=-1, keepdims=True)
        dv_part = _bdot(p, dov, TN)
        dp = _bdot(dov, vv, NT)
        ds = p * (dp - delta) * scale
        dq_ref[0] = _bdot(ds, kv, NN)
        dk_part = _bdot(ds, qv, TN)
        if has_sink:
            dsk = jnp.broadcast_to(-jnp.sum(p_sink * delta, axis=0, keepdims=True), (1, 128))
        else:
            dsk = jnp.zeros((1, 128), F32)

        @pl.when(i == 0)
        def _():
            dk_ref[0] = dk_part
            dv_ref[0] = dv_part
            dsink_ref[0] = dsk

        @pl.when(i > 0)
        def _():
            dk_ref[0] += dk_part
            dv_ref[0] += dv_part
            dsink_ref[0] += dsk

    qspec = pl.BlockSpec((1, bq, dq), lambda b, i: (b, i, 0))
    kspec = pl.BlockSpec((1, nk, dq), lambda b, i: (b, 0, 0))
    vspec = pl.BlockSpec((1, nk, dv), lambda b, i: (b, 0, 0))
    ospec = pl.BlockSpec((1, bq, dv), lambda b, i: (b, i, 0))
    sspec = pl.BlockSpec((1, 1, 128), lambda b, i: (b, 0, 0))
    return _pcall(
        body, name="attn_full_bwd", grid=(g, sq // bq),
        in_specs=[qspec, kspec, vspec, sspec, ospec, ospec],
        out_specs=[qspec, kspec, vspec, sspec],
        out_shape=[jax.ShapeDtypeStruct(q.shape, F32), jax.ShapeDtypeStruct(k.shape, F32),
                   jax.ShapeDtypeStruct(v.shape, F32), jax.ShapeDtypeStruct(sink.shape, F32)],
        compiler_params=pltpu.CompilerParams(dimension_semantics=("parallel", "arbitrary"),
                                             vmem_limit_bytes=VMEM_LIMIT_MAX),
    )(q, k, v, sink, o, do)


@functools.partial(jax.custom_vjp, nondiff_argnums=(4, 5))
def attn_full(q, k, v, sink, scale, has_sink):
    return _attn_full_fwd_call(q.astype(BF16), k.astype(BF16), v.astype(BF16), sink, scale, has_sink)


def _attn_full_fwd(q, k, v, sink, scale, has_sink):
    q, k, v = q.astype(BF16), k.astype(BF16), v.astype(BF16)
    o = _attn_full_fwd_call(q, k, v, sink, scale, has_sink)
    return o, (q, k, v, sink, o)


def _attn_full_bwd(scale, has_sink, res, do):
    q, k, v, sink, o = res
    dq, dk, dv, dsink = _attn_full_bwd_call(q, k, v, sink, o, do, scale, has_sink)
    return dq, dk, dv, _lane0(dsink)


attn_full.defvjp(_attn_full_fwd, _attn_full_bwd)


SWA_GROUP = SWA_HEADS // SWA_KV_HEADS


def _swa_rows(ref):
    return ref[...].reshape(SWA_GROUP * SWA_BLOCK, ref.shape[-1])


def _swa_sink_rows(sink_ref):
    head = lax.broadcasted_iota(jnp.int32, (SWA_GROUP * SWA_BLOCK, 1), 0) // SWA_BLOCK
    out = jnp.zeros((SWA_GROUP * SWA_BLOCK, 1), F32)
    for j in range(SWA_GROUP):
        out = jnp.where(head == j, sink_ref[j, :, 0:1], out)
    return out


def _swa_probs(q, kw, kc, sink, i, s_len, scale):
    w = SWA_BLOCK
    rows = q.shape[0]
    s_loc = _bdot(q, kw, NT) * scale
    qpos = i * w + lax.broadcasted_iota(jnp.int32, (rows, 3 * w), 0) % w
    kpos = (i - 1) * w + lax.broadcasted_iota(jnp.int32, (rows, 3 * w), 1)
    valid = (jnp.abs(kpos - qpos) <= SWA_WINDOW) & (kpos >= 0) & (kpos < s_len)
    s_loc = jnp.where(valid, s_loc, -jnp.inf)
    s_ctx = _bdot(q, kc, NT) * scale
    m = jnp.maximum(jnp.maximum(jnp.max(s_loc, axis=-1, keepdims=True), jnp.max(s_ctx, axis=-1, keepdims=True)), sink)
    p_loc = jnp.exp(s_loc - m)
    p_ctx = jnp.exp(s_ctx - m)
    p_sink = jnp.exp(sink - m)
    inv = 1.0 / (jnp.sum(p_loc, axis=-1, keepdims=True) + jnp.sum(p_ctx, axis=-1, keepdims=True) + p_sink)
    return p_loc * inv, p_ctx * inv, p_sink * inv


def _swa_fwd_call(q, kp, vp, kc, vc, sink, scale):
    g, s_len, d = q.shape
    l_ctx = kc.shape[1]
    w = SWA_BLOCK
    grp = SWA_GROUP

    def body(q_ref, kp_ref, vp_ref, kc_ref, vc_ref, sink_ref, o_ref):
        i = pl.program_id(1)
        start = pl.multiple_of(i * w, w)
        kw = kp_ref[0, pl.ds(start, 3 * w), :]
        vw = vp_ref[0, pl.ds(start, 3 * w), :]
        p_loc, p_ctx, _ = _swa_probs(_swa_rows(q_ref), kw, kc_ref[0], _swa_sink_rows(sink_ref), i, s_len, scale)
        o_ref[...] = (_bdot(p_loc, vw, NN) + _bdot(p_ctx, vc_ref[0], NN)).reshape(grp, w, d)

    return _pcall(
        body, name="swa_fwd", grid=(g // grp, s_len // w),
        in_specs=[pl.BlockSpec((grp, w, d), lambda b, i: (b, i, 0)),
                  pl.BlockSpec((1, s_len + 2 * w, d), lambda b, i: (b, 0, 0)),
                  pl.BlockSpec((1, s_len + 2 * w, d), lambda b, i: (b, 0, 0)),
                  pl.BlockSpec((1, l_ctx, d), lambda b, i: (b, 0, 0)),
                  pl.BlockSpec((1, l_ctx, d), lambda b, i: (b, 0, 0)),
                  pl.BlockSpec((grp, 1, 128), lambda b, i: (b, 0, 0))],
        out_specs=pl.BlockSpec((grp, w, d), lambda b, i: (b, i, 0)),
        out_shape=jax.ShapeDtypeStruct(q.shape, F32),
        compiler_params=pltpu.CompilerParams(dimension_semantics=("parallel", "arbitrary")),
    )(q, kp, vp, kc, vc, sink)


def _swa_bwd_call(q, kp, vp, kc, vc, sink, o, do, scale):
    g, s_len, d = q.shape
    l_ctx = kc.shape[1]
    w = SWA_BLOCK
    grp = SWA_GROUP
    sp = s_len + 2 * w

    def body(q_ref, kp_ref, vp_ref, kc_ref, vc_ref, sink_ref, o_ref, do_ref,
             dq_ref, dkp_ref, dvp_ref, dkc_ref, dvc_ref, dsink_ref):
        i = pl.program_id(1)
        start = pl.multiple_of(i * w, w)
        qv, dov = _swa_rows(q_ref), _swa_rows(do_ref)
        kw = kp_ref[0, pl.ds(start, 3 * w), :]
        vw = vp_ref[0, pl.ds(start, 3 * w), :]
        kcv, vcv = kc_ref[0], vc_ref[0]
        p_loc, p_ctx, p_sink = _swa_probs(qv, kw, kcv, _swa_sink_rows(sink_ref), i, s_len, scale)
        delta = jnp.sum(dov * _swa_rows(o_ref), axis=-1, keepdims=True)
        ds_loc = p_loc * (_bdot(dov, vw, NT) - delta) * scale
        ds_ctx = p_ctx * (_bdot(dov, vcv, NT) - delta) * scale
        dq_ref[...] = (_bdot(ds_loc, kw, NN) + _bdot(ds_ctx, kcv, NN)).reshape(grp, w, d)
        dsk = jnp.broadcast_to(-jnp.sum((p_sink * delta).reshape(grp, w, 1), axis=1, keepdims=True), (grp, 1, 128))

        @pl.when(i == 0)
        def _():
            dkp_ref[...] = jnp.zeros_like(dkp_ref)
            dvp_ref[...] = jnp.zeros_like(dvp_ref)
            dkc_ref[...] = jnp.zeros_like(dkc_ref)
            dvc_ref[...] = jnp.zeros_like(dvc_ref)
            dsink_ref[...] = jnp.zeros_like(dsink_ref)

        dkp_ref[0, pl.ds(start, 3 * w), :] += _bdot(ds_loc, qv, TN)
        dvp_ref[0, pl.ds(start, 3 * w), :] += _bdot(p_loc, dov, TN)
        dkc_ref[0] += _bdot(ds_ctx, qv, TN)
        dvc_ref[0] += _bdot(p_ctx, dov, TN)
        dsink_ref[...] += dsk

    qspec = pl.BlockSpec((grp, w, d), lambda b, i: (b, i, 0))
    kspec = pl.BlockSpec((1, sp, d), lambda b, i: (b, 0, 0))
    cspec = pl.BlockSpec((1, l_ctx, d), lambda b, i: (b, 0, 0))
    sspec = pl.BlockSpec((grp, 1, 128), lambda b, i: (b, 0, 0))
    return _pcall(
        body, name="swa_bwd", grid=(g // grp, s_len // w),
        in_specs=[qspec, kspec, kspec, cspec, cspec, sspec, qspec, qspec],
        out_specs=[qspec, kspec, kspec, cspec, cspec, sspec],
        out_shape=[jax.ShapeDtypeStruct(q.shape, F32), jax.ShapeDtypeStruct(kp.shape, F32),
                   jax.ShapeDtypeStruct(vp.shape, F32), jax.ShapeDtypeStruct(kc.shape, F32),
                   jax.ShapeDtypeStruct(vc.shape, F32), jax.ShapeDtypeStruct(sink.shape, F32)],
        compiler_params=pltpu.CompilerParams(dimension_semantics=("parallel", "arbitrary")),
    )(q, kp, vp, kc, vc, sink, o, do)


@functools.partial(jax.custom_vjp, nondiff_argnums=(6,))
def swa_attn(q, kp, vp, kc, vc, sink, scale):
    return _swa_fwd_call(*(t.astype(BF16) for t in (q, kp, vp, kc, vc)), sink, scale)


def _swa_attn_fwd(q, kp, vp, kc, vc, sink, scale):
    q, kp, vp, kc, vc = (t.astype(BF16) for t in (q, kp, vp, kc, vc))
    o = _swa_fwd_call(q, kp, vp, kc, vc, sink, scale)
    return o, (q, kp, vp, kc, vc, sink, o)


def _swa_attn_bwd(scale, res, do):
    q, kp, vp, kc, vc, sink, o = res
    dq, dkp, dvp, dkc, dvc, dsink = _swa_bwd_call(q, kp, vp, kc, vc, sink, o, do, scale)
    return dq, dkp, dvp, dkc, dvc, _lane0(dsink)


swa_attn.defvjp(_swa_attn_fwd, _swa_attn_bwd)


def _f32dot(a, b, dims):
    return lax.dot_general(a, b, dims, precision=lax.Precision.HIGHEST, preferred_element_type=F32)


DN_SOLVE_BLOCK = 16


def _unit_lower_inverse(a, a_t, transposed):
    g, c, _ = a.shape
    nb = DN_SOLVE_BLOCK
    row = lax.broadcasted_iota(jnp.int32, (1, c, c), 1)
    col = lax.broadcasted_iota(jnp.int32, (1, c, c), 2)
    src, off = (a, a_t) if transposed else (a_t, a)
    coef = jnp.zeros((g, c, nb), F32)
    for b in range(c // nb):
        in_block = (lax.broadcasted_iota(jnp.int32, (1, c, nb), 1) // nb) == b
        coef = coef + jnp.where(in_block, src[:, :, b * nb:(b + 1) * nb], 0.0)
    sub = lax.broadcasted_iota(jnp.int32, (1, c // nb, nb, c), 2)
    x = jnp.broadcast_to((row == col).astype(F32), a.shape)
    for i in (range(nb - 2, -1, -1) if transposed else range(1, nb)):
        prod = (coef[:, :, i:i + 1] * x).reshape(g, c // nb, nb, c)
        new_rows = -jnp.sum(prod, axis=2, keepdims=True)
        x = x + jnp.where(sub == i, new_rows, 0.0).reshape(g, c, c)
    width = nb
    while width < c:
        joins = ((row // (2 * width)) == (col // (2 * width))) & ((row // width) != (col // width))
        x = x - _f32dot(x, _f32dot(jnp.where(joins, off, 0.0), x, BNN), BNN)
        width *= 2
    return x


def _dn_masks(c):
    row = lax.broadcasted_iota(jnp.int32, (1, c, c), 1)
    col = lax.broadcasted_iota(jnp.int32, (1, c, c), 2)
    return row, col


def _sweep_chunks(n, n_ctx):
    return (lambda i: i), (lambda i: jnp.where(i < n_ctx, n_ctx - 1 - i, n + n_ctx - 1 - i))


def _half_spec(gh, tail, half, chunk_of):
    return pl.BlockSpec((gh, 1) + tail, lambda i: (half, chunk_of(i), 0, 0))


def _both(ref_f, ref_b):
    return jnp.concatenate([ref_f[:, 0], ref_b[:, 0]], axis=0)


def _dn_direction_masks(g, c):
    backward = lax.broadcasted_iota(jnp.int32, (g, 1, 1), 0) >= g // 2
    row, col = _dn_masks(c)
    return backward, jnp.where(backward, c - 1 - row, row), jnp.where(backward, c - 1 - col, col)


def _dn_inverse(a_mat, a_t, transposed):
    h = a_mat.shape[0] // 2
    return jnp.concatenate([_unit_lower_inverse(a_mat[:h], a_t[:h], transposed),
                            _unit_lower_inverse(a_t[h:], a_mat[h:], not transposed)], axis=0)


def _dn_fwd_call(q, k, k_t, v, gc, bb, gr, n_ctx):
    gh, n, c, _ = q.shape
    g = 2 * gh

    def body(qf, qb, kf, kb_, ktf, ktb, vf, vb, gcf, gcb, bbf, bbb, grf, grb,
             of_ref, ob_ref, vn_ref, sall_ref, w_ref, u_ref, t_ref, s_scr):
        i = pl.program_id(0)

        @pl.when(i == 0)
        def _():
            s_scr[...] = jnp.zeros_like(s_scr)

        qv, kv, ktv, vv, gcv, bv, grv = (_both(qf, qb), _both(kf, kb_), _both(ktf, ktb), _both(vf, vb), _both(gcf, gcb),
                                          _both(bbf, bbb), _both(grf, grb))
        backward, row, col = _dn_direction_masks(g, c)
        e = jnp.exp(gcv)
        kb = kv * bv
        decay = jnp.exp(jnp.where(row >= col, gcv - grv, -jnp.inf))
        decay_ts = jnp.exp(jnp.where(row < col, grv - gcv, -jnp.inf))
        a_mat = _bdot(kb, kv, BNT) * jnp.where(row > col, decay, 0.0)
        t = _dn_inverse(a_mat, _bdot(kv, kb, BNT) * decay_ts, False)
        w = _f32dot(t, kb * e, BNN)
        u = _f32dot(t, vv * bv, BNN)
        glast = jnp.where(backward, grv[:, :, 0:1], grv[:, :, c - 1:c])
        s = s_scr[...]
        sall_ref[:, 0] = s
        vnew = u - _bdot(w, s, BNN)
        o = _bdot(qv * e, s, BNN) + _bdot(_bdot(qv, kv, BNT) * decay, vnew, BNN)
        of_ref[:, 0] = o[:gh]
        ob_ref[:, 0] = o[gh:]
        vn_ref[:, 0] = vnew
        w_ref[:, 0] = w
        u_ref[:, 0] = u
        t_ref[:, 0] = t
        s_scr[...] = s * jnp.exp(glast) + _bdot(ktv * jnp.exp(glast - grv), vnew, BNN)

    cf, cb = _sweep_chunks(n, n_ctx)
    tok = lambda half, chunk_of: _half_spec(gh, (c, c), half, chunk_of)
    rowv = lambda half, chunk_of: _half_spec(gh, (1, c), half, chunk_of)
    step = pl.BlockSpec((g, 1, c, c), lambda i: (0, i, 0, 0))
    shared = [tok(0, cf), tok(0, cb)]
    split = [tok(0, cf), tok(1, cb)]
    return _pcall(
        body, name="dn_fwd", grid=(n,),
        in_specs=shared * 4 + split * 2 + [rowv(0, cf), rowv(1, cb)],
        out_specs=[tok(0, cf), tok(0, cb)] + [step] * 5,
        out_shape=[jax.ShapeDtypeStruct((gh, n, c, c), F32)] * 2 + [jax.ShapeDtypeStruct((g, n, c, c), F32)] * 5,
        scratch_shapes=[pltpu.VMEM((g, c, c), F32)],
        compiler_params=pltpu.CompilerParams(dimension_semantics=("arbitrary",)),
    )(q, q, k, k, k_t, k_t, v, v, gc, gc, bb, bb, gr, gr)


def _dn_bwd_call(q, k, q_t, k_t, v, gc, bb, gr, br, sall, vn, w, u, t_t, do_f, do_b, n_ctx):
    gh, n, c, _ = q.shape
    g = 2 * gh

    def body(qf, qb, kf, kb_, qtf, qtb, ktf, ktb, vf, vb, gcf, gcb, bbf, bbb, grf, grb, brf, brb,
             sall_ref, vn_ref, w_ref, u_ref, tt_ref, dof, dob,
             dqf, dqb, dkf, dkb_, dvf, dvb, dgcf, dgcb, dbbf, dbbb, dgrf, dgrb, ds_scr):
        i = pl.program_id(0)

        @pl.when(i == 0)
        def _():
            ds_scr[...] = jnp.zeros_like(ds_scr)

        qv, kv, qtv, ktv, vv = _both(qf, qb), _both(kf, kb_), _both(qtf, qtb), _both(ktf, ktb), _both(vf, vb)
        gcv, bv, grv, brv, dov = _both(gcf, gcb), _both(bbf, bbb), _both(grf, grb), _both(brf, brb), _both(dof, dob)
        s, vnew, w, u = sall_ref[:, 0], vn_ref[:, 0], w_ref[:, 0], u_ref[:, 0]
        dsn = ds_scr[...]
        backward, row, col = _dn_direction_masks(g, c)
        e = jnp.exp(gcv)
        er = jnp.exp(grv)
        kb = kv * bv
        decay = jnp.exp(jnp.where(row >= col, gcv - grv, -jnp.inf))
        decay_s = jnp.where(row > col, decay, 0.0)
        decay_t = jnp.exp(jnp.where(row <= col, grv - gcv, -jnp.inf))
        decay_ts = jnp.where(row < col, decay_t, 0.0)
        kk = _bdot(kb, kv, BNT)
        tt = tt_ref[:, 0]
        glast = jnp.where(backward, grv[:, :, 0:1], grv[:, :, c - 1:c])
        eg = jnp.exp(glast)
        x = jnp.exp(glast - gcv)
        kt = kv * x
        qk_raw = _bdot(qv, kv, BNT)
        w_t = _f32dot(ktv * (brv * er), tt, BNN)
        dvn = _bdot(_bdot(kv, qv, BNT) * decay_t, dov, BNN) + _bdot(kt, dsn, BNN)
        dqk = _bdot(dov, vnew, BNT)
        dqk_t = _bdot(vnew, dov, BNT)
        dqd = _bdot(dov, s, BNT)
        dkt = _bdot(vnew, dsn, BNT)
        deg = jnp.sum(jnp.sum(dsn * s, axis=2, keepdims=True), axis=1, keepdims=True)
        dw = -_bdot(dvn, s, BNT)
        ds_scr[...] = dsn * eg + _bdot(qtv * er, dov, BNN) - _bdot(w_t, dvn, BNN)
        dwp = _f32dot(tt, dw, BNN)
        dup = _f32dot(tt, dvn, BNN)
        d_a = -(_bdot(dwp, w, BNT) + _bdot(dup, u, BNT))
        d_at = -(_bdot(w, dwp, BNT) + _bdot(u, dup, BNT))
        dkb = _bdot(d_a * decay_s, kv, BNN) + dwp * e
        dkx = dkt * kv * x
        ddiff = dqk * qk_raw * decay + d_a * kk * decay_s
        dglast = jnp.sum(jnp.sum(dkx, axis=2, keepdims=True), axis=1, keepdims=True) + deg * eg
        lane = lax.broadcasted_iota(jnp.int32, (1, 1, c), 2)
        last_lane = jnp.where(backward, 0, c - 1)
        results = (
            (dqf, dqb, dqd * e + _bdot(dqk * decay, kv, BNN)),
            (dkf, dkb_, _bdot(d_at * decay_ts, kb, BNN) + dkb * bv + dkt * x + _bdot(dqk_t * decay_t, qv, BNN)),
            (dvf, dvb, dup * bv),
            (dgcf, dgcb, ddiff + (dwp * kb + dqd * qv) * e - dkx),
            (dbbf, dbbb, dkb * kv + dup * vv),
            (dgrf, dgrb, jnp.where(lane == last_lane, dglast, 0.0) - jnp.sum(ddiff, axis=1, keepdims=True)),
        )
        for ref_f, ref_b, val in results:
            ref_f[:, 0] = val[:gh]
            ref_b[:, 0] = val[gh:]

    cf, cb = _sweep_chunks(n, n_ctx)
    rf, rb = (lambda i: cf(n - 1 - i)), (lambda i: cb(n - 1 - i))
    tok = lambda half, chunk_of: _half_spec(gh, (c, c), half, chunk_of)
    rowv = lambda half, chunk_of: _half_spec(gh, (1, c), half, chunk_of)
    step = pl.BlockSpec((g, 1, c, c), lambda i: (0, n - 1 - i, 0, 0))
    shared = [tok(0, rf), tok(0, rb)]
    split = [tok(0, rf), tok(1, rb)]
    split_row = [rowv(0, rf), rowv(1, rb)]
    big = jax.ShapeDtypeStruct((gh, n, c, c), F32)
    return _pcall(
        body, name="dn_bwd", grid=(n,),
        in_specs=shared * 5 + split * 2 + split_row * 2 + [step] * 5 + shared,
        out_specs=shared * 5 + [rowv(0, rf), rowv(0, rb)],
        out_shape=[big] * 10 + [jax.ShapeDtypeStruct((gh, n, 1, c), F32)] * 2,
        scratch_shapes=[pltpu.VMEM((g, c, c), F32)],
        compiler_params=pltpu.CompilerParams(dimension_semantics=("arbitrary",)),
    )(q, q, k, k, q_t, q_t, k_t, k_t, v, v, gc, gc, bb, bb, gr, gr, br, br, sall, vn, w, u, t_t, do_f, do_b)


_t = lambda a: jnp.swapaxes(a, -1, -2)


def _dn_forms(gcum, beta, d):
    lanes = lambda t: jnp.broadcast_to(t[..., None], t.shape + (d,))
    return lanes(gcum), lanes(beta), gcum[:, :, None, :], beta[:, :, None, :]


@functools.partial(jax.custom_vjp, nondiff_argnums=(5,))
def dn_chunked(q, k, v, gcum, beta, n_ctx):
    gc, bb, gr, _ = _dn_forms(gcum, beta, q.shape[-1])
    return tuple(_dn_fwd_call(q, k, _t(k), v, gc, bb, gr, n_ctx)[:2])


def _dn_chunked_fwd(q, k, v, gcum, beta, n_ctx):
    gc, bb, gr, _ = _dn_forms(gcum, beta, q.shape[-1])
    o_f, o_b, vn, sall, w, u, t = _dn_fwd_call(q, k, _t(k), v, gc, bb, gr, n_ctx)
    return (o_f, o_b), (q, k, v, gcum, beta, vn, sall, w, u, t)


def _dn_chunked_bwd(n_ctx, res, cts):
    q, k, v, gcum, beta, vn, sall, w, u, t = res
    gc, bb, gr, br = _dn_forms(gcum, beta, q.shape[-1])
    (dq_f, dq_b, dk_f, dk_b, dv_f, dv_b, dgc_f, dgc_b, dbb_f, dbb_b, dgr_f, dgr_b) = _dn_bwd_call(
        q, k, _t(q), _t(k), v, gc, bb, gr, br, sall, vn, w, u, _t(t), cts[0], cts[1], n_ctx)
    dgcum = jnp.concatenate([jnp.sum(dgc_f, axis=-1) + dgr_f[:, :, 0, :], jnp.sum(dgc_b, axis=-1) + dgr_b[:, :, 0, :]], axis=0)
    dbeta = jnp.concatenate([jnp.sum(dbb_f, axis=-1), jnp.sum(dbb_b, axis=-1)], axis=0)
    return dq_f + dq_b, dk_f + dk_b, dv_f + dv_b, dgcum, dbeta


dn_chunked.defvjp(_dn_chunked_fwd, _dn_chunked_bwd)


def _ret_fwd_call(q, k, k_t, v, dmat, xi_b, zeta_r, gm, n_ctx):
    gh, n, c, dk = q.shape
    dv = v.shape[-1]
    g = 2 * gh

    def body(qf, qb, kf, kb_, ktf, ktb, vf, vb, d_ref, xib_ref, zr_ref, gm_ref, of_ref, ob_ref, starts_ref, s_scr):
        i = pl.program_id(0)

        @pl.when(i == 0)
        def _():
            s_scr[...] = jnp.zeros_like(s_scr)

        qv, kv, ktv, vv = _both(qf, qb), _both(kf, kb_), _both(ktf, ktb), _both(vf, vb)
        s = s_scr[...]
        starts_ref[:, 0] = s
        o = _bdot(_bdot(qv, kv, BNT) * d_ref[...], vv, BNN) + _bdot(qv * xib_ref[...], s, BNN)
        of_ref[:, 0] = o[:gh]
        ob_ref[:, 0] = o[gh:]
        s_scr[...] = s * gm_ref[...] + _bdot(ktv * zr_ref[...], vv, BNN)

    cf, cb = _sweep_chunks(n, n_ctx)
    pair = lambda tail: [_half_spec(gh, tail, 0, cf), _half_spec(gh, tail, 0, cb)]
    const = lambda a, b: pl.BlockSpec((g, a, b), lambda i: (0, 0, 0))
    return _pcall(
        body, name="ret_fwd", grid=(n,),
        in_specs=pair((c, dk)) * 2 + pair((dk, c)) + pair((c, dv)) + [const(c, c), const(c, dk), const(1, c), const(dk, dv)],
        out_specs=pair((c, dv)) + [pl.BlockSpec((g, 1, dk, dv), lambda i: (0, i, 0, 0))],
        out_shape=[jax.ShapeDtypeStruct((gh, n, c, dv), F32)] * 2 + [jax.ShapeDtypeStruct((g, n, dk, dv), F32)],
        scratch_shapes=[pltpu.VMEM((g, dk, dv), F32)],
        compiler_params=pltpu.CompilerParams(dimension_semantics=("arbitrary",)),
    )(q, q, k, k, k_t, k_t, v, v, dmat, xi_b, zeta_r, gm)


def _ret_bwd_call(q, k, q_t, k_t, v, dmat, dmat_t, xi_b, xi_r, zeta_b, gm, starts, do_f, do_b, n_ctx):
    gh, n, c, dk = q.shape
    dv = v.shape[-1]
    g = 2 * gh

    def body(qf, qb, kf, kb_, qtf, qtb, ktf, ktb, vf, vb, d_ref, dt_ref, xib_ref, xr_ref, zb_ref, gm_ref, starts_ref,
             dof, dob, dqf, dqb, dkf, dkb_, dvf, dvb, dd_ref, dxib_ref, dzb_ref, dgm_ref, ds_scr):
        i = pl.program_id(0)

        @pl.when(i == 0)
        def _():
            ds_scr[...] = jnp.zeros_like(ds_scr)
            dd_ref[...] = jnp.zeros_like(dd_ref)
            dxib_ref[...] = jnp.zeros_like(dxib_ref)
            dzb_ref[...] = jnp.zeros_like(dzb_ref)
            dgm_ref[...] = jnp.zeros_like(dgm_ref)

        qv, kv, qtv, vv, dov = _both(qf, qb), _both(kf, kb_), _both(qtf, qtb), _both(vf, vb), _both(dof, dob)
        s, dsn = starts_ref[:, 0], ds_scr[...]
        dm, dmt, zb = d_ref[...], dt_ref[...], zb_ref[...]
        qk_raw = _bdot(qv, kv, BNT)
        dqkd = _bdot(dov, vv, BNT)
        do_s = _bdot(dov, s, BNT)
        dkz = _bdot(vv, dsn, BNT)
        results = ((dqf, dqb, _bdot(dqkd * dm, kv, BNN) + do_s * xib_ref[...]),
                   (dkf, dkb_, _bdot(_bdot(vv, dov, BNT) * dmt, qv, BNN) + dkz * zb),
                   (dvf, dvb, _bdot(_bdot(kv, qv, BNT) * dmt, dov, BNN) + _bdot(kv * zb, dsn, BNN)))
        for ref_f, ref_b, val in results:
            ref_f[:, 0] = val[:gh]
            ref_b[:, 0] = val[gh:]
        dd_ref[...] += dqkd * qk_raw
        dxib_ref[...] += do_s * qv
        dzb_ref[...] += dkz * kv
        dgm_ref[...] += dsn * s
        ds_scr[...] = dsn * gm_ref[...] + _bdot(qtv * xr_ref[...], dov, BNN)

    cf, cb = _sweep_chunks(n, n_ctx)
    rf, rb = (lambda i: cf(n - 1 - i)), (lambda i: cb(n - 1 - i))
    pair = lambda tail: [_half_spec(gh, tail, 0, rf), _half_spec(gh, tail, 0, rb)]
    const = lambda a, b: pl.BlockSpec((g, a, b), lambda i: (0, 0, 0))
    sds = lambda *s: jax.ShapeDtypeStruct(s, F32)
    return _pcall(
        body, name="ret_bwd", grid=(n,),
        in_specs=pair((c, dk)) * 2 + pair((dk, c)) * 2 + pair((c, dv))
        + [const(c, c), const(c, c), const(c, dk), const(1, c), const(c, dk), const(dk, dv),
           pl.BlockSpec((g, 1, dk, dv), lambda i: (0, n - 1 - i, 0, 0))] + pair((c, dv)),
        out_specs=pair((c, dk)) * 2 + pair((c, dv)) + [const(c, c), const(c, dk), const(c, dk), const(dk, dv)],
        out_shape=[sds(gh, n, c, dk)] * 4 + [sds(gh, n, c, dv)] * 2 + [sds(g, c, c), sds(g, c, dk), sds(g, c, dk), sds(g, dk, dv)],
        scratch_shapes=[pltpu.VMEM((g, dk, dv), F32)],
        compiler_params=pltpu.CompilerParams(dimension_semantics=("arbitrary",)),
    )(q, q, k, k, q_t, q_t, k_t, k_t, v, v, dmat, dmat_t, xi_b, xi_r, zeta_b, gm, starts, do_f, do_b)


def _ret_forms(xi, zeta, gm, dk, dv):
    lanes = lambda t: jnp.broadcast_to(t[..., None], t.shape + (dk,))
    return lanes(xi), xi[:, None, :], lanes(zeta), zeta[:, None, :], jnp.broadcast_to(gm[:, None, None], gm.shape + (dk, dv))


@functools.partial(jax.custom_vjp, nondiff_argnums=(7,))
def ret_chunked(q, k, v, dmat, xi, zeta, gm, n_ctx):
    xi_b, _, _, zeta_r, gm_f = _ret_forms(xi, zeta, gm, q.shape[-1], v.shape[-1])
    return tuple(_ret_fwd_call(q, k, _t(k), v, dmat, xi_b, zeta_r, gm_f, n_ctx)[:2])


def _ret_chunked_fwd(q, k, v, dmat, xi, zeta, gm, n_ctx):
    xi_b, _, _, zeta_r, gm_f = _ret_forms(xi, zeta, gm, q.shape[-1], v.shape[-1])
    o_f, o_b, starts = _ret_fwd_call(q, k, _t(k), v, dmat, xi_b, zeta_r, gm_f, n_ctx)
    return (o_f, o_b), (q, k, v, dmat, xi, zeta, gm, starts)


def _ret_chunked_bwd(n_ctx, res, cts):
    q, k, v, dmat, xi, zeta, gm, starts = res
    xi_b, xi_r, zeta_b, _, gm_f = _ret_forms(xi, zeta, gm, q.shape[-1], v.shape[-1])
    dq_f, dq_b, dk_f, dk_b, dv_f, dv_b, dd, dxib, dzb, dgm = _ret_bwd_call(
        q, k, _t(q), _t(k), v, dmat, _t(dmat), xi_b, xi_r, zeta_b, gm_f, starts, cts[0], cts[1], n_ctx)
    return (dq_f + dq_b, dk_f + dk_b, dv_f + dv_b, dd, jnp.sum(dxib, axis=-1), jnp.sum(dzb, axis=-1),
            jnp.sum(dgm, axis=(1, 2)))


ret_chunked.defvjp(_ret_chunked_fwd, _ret_chunked_bwd)


def _peer(k):
    mx, my, mc = lax.axis_index("x"), lax.axis_index("y"), lax.axis_index("c")
    px = 1 - mx if k & 4 else mx
    py = 1 - my if k & 2 else my
    pc = 1 - mc if k & 1 else mc
    return (px, py, pc), 4 * px + 2 * py + pc


N_CHIP = N_DEV // 2


def _transfer_call(xs, name, plan, n_transfers, n_out):
    n_arr = len(xs)

    def body(*refs):
        x_refs, out_refs = refs[:n_arr], refs[n_arr:2 * n_arr]
        send_sems, recv_sems, local_sems = refs[2 * n_arr:]
        transfers, local = plan()
        n_tr = n_transfers
        assert len(transfers) == n_tr

        def copy(j, s, dst_slot):
            flip, src_slot, _, _ = transfers[s]
            return pltpu.make_async_remote_copy(
                src_ref=x_refs[j].at[src_slot], dst_ref=out_refs[j].at[dst_slot],
                send_sem=send_sems.at[j * n_tr + s], recv_sem=recv_sems.at[j * n_tr + s],
                device_id=_peer(flip)[0], device_id_type=pl.DeviceIdType.MESH)

        mine = []
        if local is not None:
            mine = [pltpu.make_async_copy(x_refs[j].at[local[0]], out_refs[j].at[local[1]], local_sems.at[j])
                    for j in range(n_arr)]
        for cp in mine:
            cp.start()
        sends = [copy(j, s, transfers[s][2]) for j in range(n_arr) for s in range(n_tr)]
        for cp in sends:
            cp.start()
        for j in range(n_arr):
            for s in range(n_tr):
                copy(j, s, transfers[s][3]).wait_recv()
        for cp in sends:
            cp.wait_send()
        for cp in mine:
            cp.wait()

    n_sem = n_arr * n_transfers
    return _pcall(
        body, name=name,
        in_specs=[pl.BlockSpec(memory_space=pl.ANY)] * n_arr, out_specs=[pl.BlockSpec(memory_space=pl.ANY)] * n_arr,
        out_shape=[jax.ShapeDtypeStruct((n_out,) + tuple(x.shape[1:]), x.dtype) for x in xs],
        scratch_shapes=[pltpu.SemaphoreType.DMA((n_sem,)), pltpu.SemaphoreType.DMA((n_sem,)),
                        pltpu.SemaphoreType.DMA((n_arr,))],
    )(*xs)


def _sibling_plan():
    mc = lax.axis_index("c")
    return [(1, 2 * t + (1 - mc), t, t) for t in range(N_CHIP)], None


def _chip_plan():
    my_chip = 2 * lax.axis_index("x") + lax.axis_index("y")
    transfers = []
    for flip in (2, 4, 6):
        peer_chip = _peer(flip)[1] // 2
        transfers.append((flip, peer_chip, my_chip, peer_chip))
    return transfers, (my_chip, my_chip)


def _gather_call(xs, name):
    n_arr = len(xs)
    per = N_DEV - 1
    chips = (2, 4, 6)

    def body(*refs):
        x_refs, out_refs = refs[:n_arr], refs[n_arr:2 * n_arr]
        send_sems, recv_sems, local_sems = refs[2 * n_arr:]
        me = 4 * lax.axis_index("x") + 2 * lax.axis_index("y") + lax.axis_index("c")
        sib_dev, sib_idx = _peer(1)

        def copy(j, s, src, slot, dev):
            return pltpu.make_async_remote_copy(
                src_ref=src, dst_ref=out_refs[j].at[slot],
                send_sem=send_sems.at[j * per + s], recv_sem=recv_sems.at[j * per + s],
                device_id=dev, device_id_type=pl.DeviceIdType.MESH)

        mine = [pltpu.make_async_copy(x_refs[j], out_refs[j].at[me], local_sems.at[j]) for j in range(n_arr)]
        for cp in mine:
            cp.start()
        sends = []
        for j in range(n_arr):
            sends.append(copy(j, 0, x_refs[j], me, sib_dev))
            for t, k in enumerate(chips):
                sends.append(copy(j, 1 + t, x_refs[j], me, _peer(k)[0]))
        for cp in sends:
            cp.start()
        for j in range(n_arr):
            for t, k in enumerate(chips):
                dev, idx = _peer(k)
                copy(j, 1 + t, x_refs[j], idx, dev).wait_recv()
                forward = copy(j, 4 + t, out_refs[j].at[idx], idx, sib_dev)
                forward.start()
                sends.append(forward)
        for j in range(n_arr):
            copy(j, 0, x_refs[j], sib_idx, sib_dev).wait_recv()
            for t, k in enumerate(chips):
                idx = _peer(k + 1)[1]
                copy(j, 4 + t, out_refs[j].at[idx], idx, sib_dev).wait_recv()
        for cp in sends:
            cp.wait_send()
        for cp in mine:
            cp.wait()

    return _pcall(
        body, name=name,
        in_specs=[pl.BlockSpec(memory_space=pl.ANY)] * n_arr, out_specs=[pl.BlockSpec(memory_space=pl.ANY)] * n_arr,
        out_shape=[jax.ShapeDtypeStruct((N_DEV,) + tuple(x.shape), x.dtype) for x in xs],
        scratch_shapes=[pltpu.SemaphoreType.DMA((n_arr * per,)), pltpu.SemaphoreType.DMA((n_arr * per,)),
                        pltpu.SemaphoreType.DMA((n_arr,))],
    )(*xs)


def _sum_slots_call(x, name):
    n_slots, r, c = x.shape
    tr = _pick(r, (256, 160, 128, 72, 64, 32, 16, 8))

    def body(x_ref, o_ref):
        acc = x_ref[0].astype(F32)
        for d in range(1, n_slots):
            acc = acc + x_ref[d].astype(F32)
        o_ref[...] = acc

    return _pcall(
        body, name=name, grid=(r // tr,),
        in_specs=[pl.BlockSpec((n_slots, tr, c), lambda i: (0, i, 0))],
        out_specs=pl.BlockSpec((tr, c), lambda i: (i, 0)),
        out_shape=jax.ShapeDtypeStruct((r, c), F32),
        compiler_params=pltpu.CompilerParams(dimension_semantics=("parallel",)),
    )(x)


def _pair_add_call(a, b, name):
    n_slots, r, c = a.shape
    tr = _pick(r, (256, 160, 128, 64, 32, 16))

    def body(a_ref, b_ref, o_ref):
        o_ref[...] = (a_ref[...].astype(F32) + b_ref[...].astype(F32)).astype(BF16)

    spec = pl.BlockSpec((n_slots, tr, c), lambda i: (0, i, 0))
    return _pcall(
        body, name=name, grid=(r // tr,), in_specs=[spec, spec], out_specs=spec,
        out_shape=jax.ShapeDtypeStruct(a.shape, BF16),
        compiler_params=pltpu.CompilerParams(dimension_semantics=("parallel",)),
    )(a, b)


def _adamw_call(w, g, m, v, name):
    r, c = w.shape
    tr = _pick(r, (256, 128, 64, 32, 16, 8))
    bc1 = 1.0 - ADAM_B1 ** ADAM_STEP
    bc2 = 1.0 - ADAM_B2 ** ADAM_STEP

    def body(w_ref, g_ref, m_ref, v_ref, d_ref, nm_ref, nv_ref):
        gv = g_ref[...]
        nm = ADAM_B1 * m_ref[...] + (1.0 - ADAM_B1) * gv
        nv = ADAM_B2 * v_ref[...] + (1.0 - ADAM_B2) * jnp.square(gv)
        d_ref[...] = -ADAM_LR * ((nm / bc1) / (jnp.sqrt(nv / bc2) + ADAM_EPS) + ADAM_WD * w_ref[...])
        nm_ref[...] = nm
        nv_ref[...] = nv

    spec = pl.BlockSpec((tr, c), lambda i: (i, 0))
    sds = jax.ShapeDtypeStruct((r, c), F32)
    return _pcall(
        body, name=name, grid=(r // tr,), in_specs=[spec] * 4, out_specs=[spec] * 3, out_shape=[sds] * 3,
        compiler_params=pltpu.CompilerParams(dimension_semantics=("parallel",)),
    )(w, g, m, v)


def rms_norm(x, g):
    return x * lax.rsqrt(jnp.mean(x * x, axis=-1, keepdims=True) + NORM_EPS) * g


def head_layer_norm(o, g):
    b_, t_, h_, d_ = o.shape
    mu = jnp.mean(o, axis=-1, keepdims=True)
    var = jnp.mean(jnp.square(o - mu), axis=-1, keepdims=True)
    return ((o - mu) * lax.rsqrt(var + NORM_EPS)).reshape(b_, t_, h_ * d_) * g


def l2norm(t):
    return t * lax.rsqrt(jnp.sum(t * t, axis=-1, keepdims=True) + NORM_EPS)


def rope_freqs(dim):
    return ROPE_BASE ** (-jnp.arange(0, dim, 2, dtype=F32) / dim)


def axial_rope(rows, rot_dim):
    row = jnp.broadcast_to(jnp.arange(rows, dtype=F32)[:, None], (rows, GRID_W)).reshape(-1)
    col = jnp.broadcast_to(jnp.arange(GRID_W, dtype=F32)[None, :], (rows, GRID_W)).reshape(-1)
    inv = rope_freqs(rot_dim // 2)
    ang = jnp.concatenate([row[:, None] * inv, col[:, None] * inv], axis=-1)
    return jnp.cos(ang), jnp.sin(ang)


def sequence_rope(n_tok, rot_dim):
    ang = jnp.arange(n_tok, dtype=F32)[:, None] * rope_freqs(rot_dim)
    return jnp.cos(ang), jnp.sin(ang)


def apply_rope(x, cos, sin):
    x1, x2 = jnp.split(x, 2, axis=-1)
    c = cos[:, None, :]
    s = sin[:, None, :]
    return jnp.concatenate([x1 * c - x2 * s, x1 * s + x2 * c], axis=-1)


def _split_columns(z):
    idx = np.cumsum(np.array(IN_SPLITS))[:-1].tolist()
    return jnp.split(z, idx, axis=-1)


CONV_ROW_TILE = 512
CONV_HALO = 8


def _conv_windows(prev_ref, x_ref, next_ref, n_tiles, taps):
    i = pl.program_id(1)
    tr = x_ref.shape[1]
    prev = jnp.where(i == 0, 0.0, prev_ref[0])
    nxt = jnp.where(i == n_tiles - 1, 0.0, next_ref[0])
    xx = jnp.concatenate([prev, x_ref[0], nxt], axis=0)
    rows = tr + 2 * CONV_HALO
    pad = taps // 2
    return [pltpu.roll(xx, (pad - k) % rows, 0)[CONV_HALO:CONV_HALO + tr] for k in range(taps)]


def _conv_specs(t_len, tr, ch):
    per = tr // CONV_HALO
    last = t_len // CONV_HALO - 1
    return [pl.BlockSpec((1, CONV_HALO, ch), lambda b, i: (b, jnp.maximum(i * per - 1, 0), 0)),
            pl.BlockSpec((1, tr, ch), lambda b, i: (b, i, 0)),
            pl.BlockSpec((1, CONV_HALO, ch), lambda b, i: (b, jnp.minimum((i + 1) * per, last), 0))]


def _conv_fwd_call(x, w, taps):
    b_, t_len, ch = x.shape
    tr = min(CONV_ROW_TILE, t_len)
    n_tiles = t_len // tr

    def body(prev_ref, x_ref, next_ref, w_ref, o_ref):
        wins = _conv_windows(prev_ref, x_ref, next_ref, n_tiles, taps)
        acc = wins[0] * w_ref[0:1, :]
        for k in range(1, taps):
            acc = acc + wins[k] * w_ref[k:k + 1, :]
        o_ref[0] = acc

    return _pcall(
        body, name="conv_fwd", grid=(b_, n_tiles),
        in_specs=_conv_specs(t_len, tr, ch) + [pl.BlockSpec((8, ch), lambda b, i: (0, 0))],
        out_specs=pl.BlockSpec((1, tr, ch), lambda b, i: (b, i, 0)),
        out_shape=jax.ShapeDtypeStruct(x.shape, F32),
        compiler_params=pltpu.CompilerParams(dimension_semantics=("parallel", "parallel")),
    )(x, x, x, w)


def _conv_dw_call(x, du, taps):
    b_, t_len, ch = x.shape
    tr = min(CONV_ROW_TILE, t_len)
    n_tiles = t_len // tr

    def body(prev_ref, x_ref, next_ref, du_ref, dw_ref):
        wins = _conv_windows(prev_ref, x_ref, next_ref, n_tiles, taps)
        duv = du_ref[0]
        rows = [jnp.sum(duv * wins[k], axis=0, keepdims=True) for k in range(taps)]
        part = jnp.concatenate(rows + [jnp.zeros((8 - taps, ch), F32)], axis=0)

        @pl.when((pl.program_id(0) == 0) & (pl.program_id(1) == 0))
        def _():
            dw_ref[...] = jnp.zeros_like(dw_ref)

        dw_ref[...] += part

    return _pcall(
        body, name="conv_dw", grid=(b_, n_tiles),
        in_specs=_conv_specs(t_len, tr, ch) + [pl.BlockSpec((1, tr, ch), lambda b, i: (b, i, 0))],
        out_specs=pl.BlockSpec((8, ch), lambda b, i: (0, 0)),
        out_shape=jax.ShapeDtypeStruct((8, ch), F32),
        compiler_params=pltpu.CompilerParams(dimension_semantics=("arbitrary", "arbitrary")),
    )(x, x, x, du)


def _pad_taps(w):
    return jnp.concatenate([w, jnp.zeros((8 - w.shape[0], w.shape[1]), F32)], axis=0)


@jax.custom_vjp
def short_conv(x, w):
    return _conv_fwd_call(x, _pad_taps(w), w.shape[0])


def _short_conv_fwd(x, w):
    return _conv_fwd_call(x, _pad_taps(w), w.shape[0]), (x, w)


def _short_conv_bwd(res, du):
    x, w = res
    taps = w.shape[0]
    return _conv_fwd_call(du, _pad_taps(jnp.flip(w, axis=0)), taps), _conv_dw_call(x, du, taps)[:taps]


short_conv.defvjp(_short_conv_fwd, _short_conv_bwd)


def _to_heads(t, h, d):
    b_, t_, _ = t.shape
    return t.reshape(b_, t_, h, d).transpose(0, 2, 1, 3).reshape(b_ * h, t_, d)


def _from_heads(t, b_):
    g, t_, d = t.shape
    return t.reshape(b_, g // b_, t_, d).transpose(0, 2, 1, 3).reshape(b_, t_, (g // b_) * d)


def _lane_scalar(vals):
    return jnp.broadcast_to(vals[:, None, None], (vals.shape[0], 1, 128))


def swa_group(q, k, v, qc, kc, vc, sink, cos, sin, with_ctx_out):
    b_, s_, _ = q.shape
    l_ = kc.shape[1]
    grp = SWA_HEADS // SWA_KV_HEADS
    d = SWA_HEAD_DIM
    w_ = SWA_BLOCK
    scale = d ** -0.5
    qh = apply_rope(q.reshape(b_, s_, SWA_HEADS, d), cos, sin).transpose(0, 2, 1, 3).reshape(b_ * SWA_HEADS, s_, d)
    kh = apply_rope(k.reshape(b_, s_, SWA_KV_HEADS, d), cos, sin).transpose(0, 2, 1, 3).reshape(b_ * SWA_KV_HEADS, s_, d)
    vh = _to_heads(v, SWA_KV_HEADS, d)
    kch = _to_heads(kc, SWA_KV_HEADS, d)
    vch = _to_heads(vc, SWA_KV_HEADS, d)
    padk = lambda t: jnp.pad(t, ((0, 0), (w_, w_), (0, 0)))
    sink_g = _lane_scalar(jnp.tile(sink, b_))
    y = _from_heads(swa_attn(qh, padk(kh), padk(vh), kch, vch, sink_g, scale), b_)
    yc = None
    if with_ctx_out:
        qch = _to_heads(qc, SWA_HEADS, d)
        rep = lambda t: jnp.repeat(t.reshape(b_, SWA_KV_HEADS, l_, d), grp, axis=1).reshape(b_ * SWA_HEADS, l_, d)
        yc = _from_heads(attn_full(qch, rep(kch), rep(vch), sink_g, scale, True), b_)
    return y, yc


def gated_delta_chunked(q, k, v, log_g, beta, n_ctx):
    g_, t_, dk = k.shape
    dv = v.shape[-1]
    c_ = DN_CHUNK
    n = t_ // c_
    assert dk == c_ and dv == c_
    lg = log_g.reshape(2, g_, n, c_)
    g_cum = jnp.concatenate([jnp.cumsum(lg[0], axis=-1), jnp.flip(jnp.cumsum(jnp.flip(lg[1], axis=-1), axis=-1), axis=-1)],
                            axis=0)
    o_f, o_b = dn_chunked(q.reshape(g_, n, c_, dk), k.reshape(g_, n, c_, dk), v.reshape(g_, n, c_, dv), g_cum,
                          beta.reshape(2 * g_, n, c_), n_ctx)
    return (o_f + o_b).reshape(g_, t_, dv)


def deltanet_group(qkv, z, ab, qkv_c, z_c, ab_c, conv_w, a_log, dt_bias, norm_g, with_ctx_out):
    def prep(qkv_, ab_):
        b_, t_, _ = qkv_.shape
        y = jax.nn.silu(short_conv(qkv_, conv_w))
        q, k, v = [t.reshape(b_, t_, DN_HEADS, DN_HEAD_DIM).transpose(0, 2, 1, 3) for t in jnp.split(y, 3, axis=-1)]
        q = l2norm(q) * DN_HEAD_DIM ** -0.5
        k = l2norm(k)
        ab_ = ab_.reshape(b_, t_, 2, 2, DN_HEADS)
        log_g = -jnp.exp(a_log) * jax.nn.softplus(ab_[:, :, :, 0] + dt_bias)
        beta = jax.nn.sigmoid(ab_[:, :, :, 1])
        return q, k, v, log_g.transpose(2, 0, 3, 1), beta.transpose(2, 0, 3, 1)

    def out(o, z_):
        b_, t_, _ = z_.shape
        o = rms_norm(o.transpose(0, 2, 1, 3), norm_g) * jax.nn.silu(z_).reshape(b_, t_, DN_HEADS, DN_HEAD_DIM)
        return o.reshape(b_, t_, DN_W)

    qc, kc, vc, lgc, bc = prep(qkv_c, ab_c)
    q, k, v, lg, bt = prep(qkv, ab)
    b_, l_, s_ = qkv.shape[0], qkv_c.shape[1], qkv.shape[1]
    seq = lambda tc, tl: jnp.concatenate([tc, tl], axis=2).reshape((b_ * DN_HEADS, l_ + s_) + tc.shape[3:])
    seq_g = lambda tc, tl: jnp.concatenate([tc, tl], axis=3).reshape(2, b_ * DN_HEADS, l_ + s_)
    o = gated_delta_chunked(seq(qc, q), seq(kc, k), seq(vc, v), seq_g(lgc, lg), seq_g(bc, bt), l_ // DN_CHUNK)
    o = o.reshape(b_, DN_HEADS, l_ + s_, DN_HEAD_DIM)
    y = out(o[:, :, l_:], z)
    yc = out(o[:, :, :l_], z_c) if with_ctx_out else None
    return y, yc


def retention_core(q, k, v, log_gamma, n_ctx):
    b_, h_, t_, dk = q.shape
    dv = v.shape[-1]
    c_ = RET_CHUNK
    n = t_ // c_
    gh = b_ * h_
    fwd_rank = jnp.arange(c_, dtype=F32)
    rank = jnp.stack([fwd_rank, c_ - 1 - fwd_rank])[:, None, :]
    lg = log_gamma[..., None]
    per_g = lambda t: jnp.broadcast_to(t[:, None], (2, b_) + t.shape[1:]).reshape((2 * gh,) + t.shape[2:])
    zeta = jnp.exp((c_ - 1 - rank) * lg)
    xi = jnp.exp((rank + 1.0) * lg)
    rel = rank[..., :, None] - rank[..., None, :]
    dmat = jnp.where(rel >= 0, jnp.exp(jnp.maximum(rel, 0.0) * lg[..., None]), 0.0)
    gm = jnp.exp(c_ * log_gamma)
    o_f, o_b = ret_chunked(q.reshape(gh, n, c_, dk), k.reshape(gh, n, c_, dk), v.reshape(gh, n, c_, dv),
                           per_g(dmat), per_g(xi), per_g(zeta), per_g(gm), n_ctx)
    return (o_f + o_b).reshape(b_, h_, t_, dv)


def retention_group(q, k, v, g, qc, kc, vc, gc, log1m_gamma, norm_g, cos, sin, with_ctx_out):
    log_gamma = jnp.log1p(-jnp.exp(log1m_gamma))
    heads = lambda t, dh: t.reshape(t.shape[0], t.shape[1], RET_HEADS, dh)
    bhtd = lambda t: t.transpose(0, 2, 1, 3)
    sc = RET_QK_DIM ** -0.5
    l_ = kc.shape[1]
    q = bhtd(apply_rope(heads(q, RET_QK_DIM), cos, sin)) * sc
    k = bhtd(apply_rope(heads(k, RET_QK_DIM), cos, sin))
    v = bhtd(heads(v, RET_V_DIM))
    kc = bhtd(heads(kc, RET_QK_DIM))
    vc = bhtd(heads(vc, RET_V_DIM))
    qcs = bhtd(heads(qc, RET_QK_DIM)) * sc

    def out(o, g_):
        return head_layer_norm(o.transpose(0, 2, 1, 3), norm_g) * jax.nn.silu(g_)

    seq = lambda tc, tl: jnp.concatenate([tc, tl], axis=2)
    o = retention_core(seq(qcs, q), seq(kc, k), seq(vc, v), log_gamma, l_ // RET_CHUNK)
    y = out(o[:, :, l_:], g)
    yc = out(o[:, :, :l_], gc) if with_ctx_out else None
    return y, yc


def mla_group(cq, ckv, kr, cq_c, ckv_c, kr_c, q_norm, w_uq, kv_norm, w_ukv, cos, sin, with_ctx_out):
    b_, s_, _ = cq.shape
    l_ = cq_c.shape[1]
    dqk = MLA_NOPE_DIM + MLA_ROPE_DIM
    rows = lambda tl, tc: jnp.concatenate([tl.reshape(b_ * s_, -1), tc.reshape(b_ * l_, -1)], axis=0)
    qa = matmul(rms_norm(rows(cq, cq_c), q_norm), w_uq)
    kva = matmul(rms_norm(rows(ckv, ckv_c), kv_norm), w_ukv)
    q = qa[:b_ * s_].reshape(b_, s_, MLA_HEADS, dqk)
    qc = qa[b_ * s_:].reshape(b_, l_, MLA_HEADS, dqk)
    q = jnp.concatenate([q[..., :MLA_NOPE_DIM], apply_rope(q[..., MLA_NOPE_DIM:], cos, sin)], axis=-1)
    kv = kva[:b_ * s_].reshape(b_, s_, MLA_HEADS, MLA_NOPE_DIM + MLA_V_DIM)
    kvc = kva[b_ * s_:].reshape(b_, l_, MLA_HEADS, MLA_NOPE_DIM + MLA_V_DIM)
    kr = apply_rope(kr[:, :, None, :], cos, sin)
    k = jnp.concatenate([kv[..., :MLA_NOPE_DIM], jnp.broadcast_to(kr, (b_, s_, MLA_HEADS, MLA_ROPE_DIM))], axis=-1)
    kc = jnp.concatenate([kvc[..., :MLA_NOPE_DIM],
                          jnp.broadcast_to(kr_c[:, :, None, :], (b_, l_, MLA_HEADS, MLA_ROPE_DIM))], axis=-1)
    v, vc = kv[..., MLA_NOPE_DIM:], kvc[..., MLA_NOPE_DIM:]
    hd = lambda t: t.transpose(0, 2, 1, 3).reshape(b_ * MLA_HEADS, t.shape[1], t.shape[3])
    scale = dqk ** -0.5
    no_sink = jnp.zeros((b_ * MLA_HEADS, 1, 128), F32)
    kch, vch = hd(kc), hd(vc)
    y = attn_full(hd(q), jnp.concatenate([hd(k), kch], axis=1), jnp.concatenate([hd(v), vch], axis=1), no_sink, scale, False)
    y = _from_heads(y, b_)
    yc = _from_heads(attn_full(hd(qc), kch, vch, no_sink, scale, False), b_) if with_ctx_out else None
    return y, yc


def token_mixers(zl, zc, p, layer, rope, with_ctx_out):
    (a_q, a_k, a_v, b_qkv, b_z, b_ab, c_q, c_k, c_v, c_g, d_cq, d_ckv, d_kr) = _split_columns(zl)
    (a_qc, a_kc, a_vc, b_qkvc, b_zc, b_abc, c_qc, c_kc, c_vc, c_gc, d_cqc, d_ckvc, d_krc) = _split_columns(zc)
    swa_cos, swa_sin, ret_cos, ret_sin, mla_cos, mla_sin = rope
    ya, yac = swa_group(a_q, a_k, a_v, a_qc, a_kc, a_vc, p['swa_sink'][layer], swa_cos, swa_sin, with_ctx_out)
    yb, ybc = deltanet_group(b_qkv, b_z, b_ab, b_qkvc, b_zc, b_abc, p['dn_conv_w'][layer], p['dn_a_log'][layer],
                             p['dn_dt_bias'][layer], p['dn_norm_g'][layer], with_ctx_out)
    yr, yrc = retention_group(c_q, c_k, c_v, c_g, c_qc, c_kc, c_vc, c_gc, p['ret_log1m_gamma'][layer],
                              p['ret_norm_g'][layer], ret_cos, ret_sin, with_ctx_out)
    yd, ydc = mla_group(d_cq, d_ckv, d_kr, d_cqc, d_ckvc, d_krc, p['mla_q_norm'][layer], p['mla_w_uq'][layer],
                        p['mla_kv_norm'][layer], p['mla_w_ukv'][layer], mla_cos, mla_sin, with_ctx_out)
    y = jnp.concatenate([ya, yb, yr, yd], axis=-1)
    yc = jnp.concatenate([yac, ybc, yrc, ydc], axis=-1) if with_ctx_out else None
    return y, yc


def local_loss(p, x, ctx, loss_target):
    b_, n_tok, d_ = x.shape
    l_ = ctx.shape[1]
    rows = n_tok // GRID_W
    rope = (*axial_rope(rows, SWA_HEAD_DIM), *sequence_rope(n_tok, RET_QK_DIM), *axial_rope(rows, MLA_ROPE_DIM))
    rl, rc = b_ * n_tok, b_ * l_
    mods = [jnp.concatenate([p['mod'][layer], p['cmod'][layer][None]], axis=0) for layer in range(DEPTH)]
    part = lambda layer, j: mods[layer][:, j * d_:(j + 1) * d_][:, None, :]
    vec = lambda name, layer: p[name][layer][None, :]
    xr = jnp.concatenate([x.reshape(rl, d_), ctx.reshape(rc, d_)], axis=0)
    sh1, sc1 = part(0, 0), part(0, 1)
    h = jnp.concatenate([(x * (1 + sc1[:b_]) + sh1[:b_]).reshape(rl, d_), (ctx * (1 + sc1[b_]) + sh1[b_]).reshape(rc, d_)],
                        axis=0)
    for layer in range(DEPTH):
        with_ctx_out = layer < DEPTH - 1
        g1, sh2, sc2, g2 = part(layer, 2), part(layer, 3), part(layer, 4), part(layer, 5)
        z = matmul(h, p['w_in'][layer])
        zl = z[:rl, :IN_WIDTH].reshape(b_, n_tok, IN_WIDTH)
        zc = z[rl:, :IN_WIDTH].reshape(b_, l_, IN_WIDTH)
        y, yc = token_mixers(zl, zc, p, layer, rope, with_ctx_out)
        if with_ctx_out:
            yo = matmul(jnp.concatenate([y.reshape(rl, d_), yc.reshape(rc, d_)], axis=0), p['w_out'][layer])
            xr, h2 = ln_mod(xr, yo, g1, vec('ln1_g', layer), vec('ln1_b', layer), sc2, sh2, n_tok)
            f = matmul_relu2(matmul(h2, p['w_ff1'][layer]), p['w_ff2'][layer])
            xr, h = ln_mod(xr, f, g2, vec('ln2_g', layer), vec('ln2_b', layer), part(layer + 1, 1), part(layer + 1, 0), n_tok)
        else:
            lat = lambda t: t[:b_]
            yo = matmul(y.reshape(rl, d_), p['w_out'][layer])
            xl, h2 = ln_mod(xr[:rl], yo, lat(g1), vec('ln1_g', layer), vec('ln1_b', layer), lat(sc2), lat(sh2), n_tok)
            f = matmul_relu2(matmul(h2, p['w_ff1'][layer]), p['w_ff2'][layer])
            none = jnp.zeros((b_, 1, d_), F32)
            xl, _ = ln_mod(xl, f, lat(g2), vec('ln2_g', layer), vec('ln2_b', layer), none, none, n_tok)
    err = jnp.square(xl - loss_target.reshape(rl, d_))
    return 0.5 * jnp.sum(jnp.mean(err, axis=-1))


def _shard_shape(shape, axis):
    s = list(shape)
    s[axis] //= N_DEV
    return tuple(s)


def _join_shards(pieces, axis):
    _, _, r, c = pieces.shape
    if axis == 0:
        full = pieces.transpose(1, 0, 2, 3).reshape(DEPTH, N_DEV * r, c)
    else:
        full = pieces.transpose(1, 2, 0, 3).reshape(DEPTH, r, N_DEV * c)
    return full.astype(F32)


def _split_shards(g, shape, axis):
    r, c = _shard_shape(shape, axis)
    if axis == 0:
        pieces = g.reshape(DEPTH, N_DEV, r, c).transpose(1, 0, 2, 3)
    else:
        pieces = g.reshape(DEPTH, r, N_DEV, c).transpose(2, 0, 1, 3)
    return pieces.astype(BF16)


def _pad_vec(vec, rows_multiple=8):
    n = vec.shape[0]
    rows = -(-n // (128 * rows_multiple)) * rows_multiple
    return jnp.pad(vec, (0, rows * 128 - n)).reshape(rows, 128)


def _adamw(w, g, m, v, name):
    shape = w.shape
    if w.ndim >= 2 and shape[-1] >= 128:
        as2 = lambda t: t.reshape(-1, shape[-1])
        d, nm, nv = _adamw_call(as2(w), as2(g), as2(m), as2(v), name)
        return d.reshape(shape), nm.reshape(shape), nv.reshape(shape)
    n = int(np.prod(shape))
    as2 = lambda t: _pad_vec(t.reshape(-1))
    d, nm, nv = _adamw_call(as2(w), as2(g), as2(m), as2(v), name)
    un = lambda t: t.reshape(-1)[:n].reshape(shape)
    return un(d), un(nm), un(nv)


def kernel(x, c, ctx, c_ctx, ada_w, ada_b, w_in, swa_sink, dn_conv_w, dn_a_log, dn_dt_bias, dn_norm_g, ret_log1m_gamma, ret_norm_g, mla_q_norm, mla_w_uq, mla_kv_norm, mla_w_ukv, w_out, ln1_g, ln1_b, w_ff1, w_ff2, ln2_g, ln2_b, loss_target, m_c_ctx, m_ada_w, m_ada_b, m_w_in, m_swa_sink, m_dn_conv_w, m_dn_a_log, m_dn_dt_bias, m_dn_norm_g, m_ret_log1m_gamma, m_ret_norm_g, m_mla_q_norm, m_mla_w_uq, m_mla_kv_norm, m_mla_w_ukv, m_w_out, m_ln1_g, m_ln1_b, m_w_ff1, m_w_ff2, m_ln2_g, m_ln2_b, v_c_ctx, v_ada_w, v_ada_b, v_w_in, v_swa_sink, v_dn_conv_w, v_dn_a_log, v_dn_dt_bias, v_dn_norm_g, v_ret_log1m_gamma, v_ret_norm_g, v_mla_q_norm, v_mla_w_uq, v_mla_kv_norm, v_mla_w_ukv, v_w_out, v_ln1_g, v_ln1_b, v_w_ff1, v_w_ff2, v_ln2_g, v_ln2_b):
    a = dict(zip(ARG_NAMES, (x, c, ctx, c_ctx, ada_w, ada_b, w_in, swa_sink, dn_conv_w, dn_a_log, dn_dt_bias, dn_norm_g, ret_log1m_gamma, ret_norm_g, mla_q_norm, mla_w_uq, mla_kv_norm, mla_w_ukv, w_out, ln1_g, ln1_b, w_ff1, w_ff2, ln2_g, ln2_b, loss_target, m_c_ctx, m_ada_w, m_ada_b, m_w_in, m_swa_sink, m_dn_conv_w, m_dn_a_log, m_dn_dt_bias, m_dn_norm_g, m_ret_log1m_gamma, m_ret_norm_g, m_mla_q_norm, m_mla_w_uq, m_mla_kv_norm, m_mla_w_ukv, m_w_out, m_ln1_g, m_ln1_b, m_w_ff1, m_w_ff2, m_ln2_g, m_ln2_b, v_c_ctx, v_ada_w, v_ada_b, v_w_in, v_swa_sink, v_dn_conv_w, v_dn_a_log, v_dn_dt_bias, v_dn_norm_g, v_ret_log1m_gamma, v_ret_norm_g, v_mla_q_norm, v_mla_w_uq, v_mla_kv_norm, v_mla_w_ukv, v_w_out, v_ln1_g, v_ln1_b, v_w_ff1, v_w_ff2, v_ln2_g, v_ln2_b)))
    me = 4 * lax.axis_index("x") + 2 * lax.axis_index("y") + lax.axis_index("c")
    b_loc = x.shape[0]
    n_ex = N_DEV * b_loc
    conv_k, conv_c = dn_conv_w.shape[1], dn_conv_w.shape[2]
    ada_cols = ada_w.shape[2]

    small_in = jnp.concatenate([c.reshape(-1), dn_conv_w.reshape(-1)])
    gathered = _gather_call([_pad_vec(small_in)] + [a[name].astype(BF16) for name, _, _ in BIG], "gather_weights")
    small_all = gathered[0].reshape(N_DEV, -1)
    c_all = small_all[:, :b_loc * D_MODEL].reshape(n_ex, D_MODEL)
    conv_all = small_all[:, b_loc * D_MODEL:b_loc * D_MODEL + DEPTH * conv_k * conv_c].reshape(N_DEV, DEPTH, conv_k, conv_c)
    conv_full = conv_all.transpose(1, 2, 0, 3).reshape(DEPTH, conv_k, N_DEV * conv_c)
    big = {name: _join_shards(pieces, axis) for (name, _, axis), pieces in zip(BIG, gathered[1:])}
    big['w_in'] = jnp.pad(big['w_in'], ((0, 0), (0, 0), (0, IN_WIDTH_PAD - IN_WIDTH)))

    n_rows = -(-(n_ex + 1) // 16) * 16
    silu_cc = jax.nn.silu(c_ctx)
    a_rows = jnp.concatenate([jax.nn.silu(c_all), silu_cc[None], jnp.zeros((n_rows - n_ex - 1, D_MODEL), F32)], axis=0)
    m_loc = jnp.concatenate([_mm_call(a_rows, ada_w[l], False, "ada_fwd") for l in range(DEPTH)], axis=0)
    m_all = _gather_call([m_loc], "gather_mod")[0].reshape(N_DEV, DEPTH, n_rows, ada_cols)
    mod_full = m_all.transpose(1, 2, 0, 3).reshape(DEPTH, n_rows, N_DEV * ada_cols) + ada_b[:, None, :]
    mod = lax.dynamic_slice_in_dim(mod_full, me * b_loc, b_loc, axis=1)
    cmod = mod_full[:, n_ex]

    p = dict(big)
    p.update(mod=mod, cmod=cmod, dn_conv_w=conv_full)
    for name in SMALL:
        p[name] = a[name]
    loss_loc, (gp, gx) = jax.value_and_grad(local_loss, argnums=(0, 1))(p, x, ctx, loss_target)
    loss = lax.psum(loss_loc, MESH_AXES)

    gp['w_in'] = gp['w_in'][:, :, :IN_WIDTH]
    my_core = lax.axis_index("c")
    parts = [_split_shards(gp[name], shape, axis) for name, shape, axis in BIG]
    from_sibling = _transfer_call(parts, "scatter_sibling", _sibling_plan, N_CHIP, N_CHIP)
    chip_parts = []
    for (name, shape, axis), part, other in zip(BIG, parts, from_sibling):
        r, c_ = _shard_shape(shape, axis)
        mine = lax.dynamic_index_in_dim(part.reshape(N_CHIP, 2, DEPTH * r, c_), my_core, axis=1, keepdims=False)
        chip_parts.append(_pair_add_call(mine, other.reshape(N_CHIP, DEPTH * r, c_), "pair_" + name))
    arrived = _transfer_call(chip_parts, "scatter_chips", _chip_plan, N_CHIP - 1, N_CHIP)
    g_big = {}
    for (name, shape, axis), part in zip(BIG, arrived):
        r, c_ = _shard_shape(shape, axis)
        g_big[name] = _sum_slots_call(part, "sum_" + name).reshape(DEPTH, r, c_)

    d_loc = jnp.concatenate([gp['mod'], gp['cmod'][:, None, :]], axis=1).reshape(DEPTH * (b_loc + 1), -1)
    d_loc = jnp.pad(d_loc, ((0, 8 - DEPTH * (b_loc + 1)), (0, 0)))
    d_all = _gather_call([d_loc], "gather_dmod")[0][:, :DEPTH * (b_loc + 1)].reshape(N_DEV, DEPTH, b_loc + 1, -1)
    d_rows = d_all[:, :, :b_loc].transpose(1, 0, 2, 3).reshape(DEPTH, n_ex, -1)
    d_crow = d_all[0, :, b_loc]
    for d in range(1, N_DEV):
        d_crow = d_crow + d_all[d, :, b_loc]
    dm_full = jnp.concatenate([d_rows, d_crow[:, None, :], jnp.zeros((DEPTH, n_rows - n_ex - 1, d_rows.shape[-1]), F32)], axis=1)
    g_ada_b = jnp.sum(dm_full, axis=1)
    dm_mine = lax.dynamic_slice_in_dim(dm_full, me * ada_cols, ada_cols, axis=2)
    g_ada_w = jnp.stack([_mm_call(a_rows, dm_mine[l], True, "ada_bwd_w") for l in range(DEPTH)])
    crow8 = jnp.concatenate([dm_mine[:, n_ex:n_ex + 1], jnp.zeros((DEPTH, 15, ada_cols), F32)], axis=1)
    dsilu_part = sum(_mm_call(crow8[l], jnp.transpose(ada_w[l]), False, "ada_bwd_c")[0] for l in range(DEPTH))

    small_g = jnp.concatenate([gp[name].reshape(-1) for name in SMALL] + [gp['dn_conv_w'].reshape(-1), dsilu_part])
    small_sum = _sum_slots_call(_gather_call([_pad_vec(small_g)], "gather_small_grads")[0], "sum_small_grads").reshape(-1)
    g_all, off = {}, 0
    for name in SMALL:
        n = int(np.prod(a[name].shape))
        g_all[name] = small_sum[off:off + n].reshape(a[name].shape)
        off += n
    n = DEPTH * conv_k * N_DEV * conv_c
    g_conv_full = small_sum[off:off + n].reshape(DEPTH, conv_k, N_DEV * conv_c)
    g_all['dn_conv_w'] = lax.dynamic_slice_in_dim(g_conv_full, me * conv_c, conv_c, axis=2)
    off += n
    dsilu = small_sum[off:off + D_MODEL]
    sig = jax.nn.sigmoid(c_ctx)
    g_all['c_ctx'] = dsilu * (sig * (1 + c_ctx * (1 - sig)))
    g_all['ada_w'] = g_ada_w
    g_all['ada_b'] = g_ada_b
    g_all.update(g_big)

    delta, new_m, new_v = {}, {}, {}
    for name in WEIGHTS:
        delta[name], new_m[name], new_v[name] = _adamw(a[name], g_all[name], a['m_' + name], a['v_' + name], "adamw_" + name)
    return (loss, gx, *[g_all[n] for n in WEIGHTS], *[delta[n] for n in WEIGHTS],
            *[new_m[n] for n in WEIGHTS], *[new_v[n] for n in WEIGHTS])
```
